```python
import jax, jax.numpy as jnp
from jax import lax
import numpy as np

D_MODEL = 2048
BATCH = 2
SEQ = 8192
DEPTH = 2

CTX_LEN = 256
GRID_W = 64
EPS = 1e-6
ROPE_BASE = 10000.0

GROUP_WIDTH = D_MODEL // 4
HEAD_DIM = 64
NA_HEADS = GROUP_WIDTH // HEAD_DIM
NA_ROWS = 8
NA_COLS = 16
RET_HEADS = 4
RET_DK = GROUP_WIDTH // RET_HEADS
RET_DV = GROUP_WIDTH // RET_HEADS
GLA_HEADS = 4
GLA_DV = GROUP_WIDTH // GLA_HEADS
GLA_DK = GLA_DV // 2
GLA_RANK = 16
GLA_TAU = 16.0
SWA_HEADS = GROUP_WIDTH // HEAD_DIM
SWA_KV_HEADS = SWA_HEADS // 4
SWA_WINDOW = 128
SWA_BLOCK = 128
SCAN_CHUNK = 64
PEER_HEADS = 8
PEER_N_KEYS = 128
PEER_N_EXPERTS = PEER_N_KEYS * PEER_N_KEYS
PEER_DK = 256
PEER_TOPK = 16
PEER_TOKEN_BLOCK = 128

PROJ_WIDTHS = (
    NA_HEADS * HEAD_DIM, NA_HEADS * HEAD_DIM, NA_HEADS * HEAD_DIM,
    RET_HEADS * RET_DK, RET_HEADS * RET_DK, RET_HEADS * RET_DV, RET_HEADS * RET_DV,
    GLA_HEADS * GLA_DK, GLA_HEADS * GLA_DK, GLA_HEADS * GLA_DV, GLA_HEADS * GLA_DV,
    GLA_RANK, GLA_RANK,
    SWA_HEADS * HEAD_DIM, SWA_KV_HEADS * HEAD_DIM, SWA_KV_HEADS * HEAD_DIM,
)
D_IN = sum(PROJ_WIDTHS)

kernel_name = "hybrid_dit_na_ret_gla_swa_peer"


def rms_norm(x, g):
    xf = x.astype(jnp.float32)
    y = xf * lax.rsqrt(jnp.mean(xf * xf, axis=-1, keepdims=True) + EPS)
    return (y * g.astype(jnp.float32)).astype(x.dtype)


def modulate(x, g, shift, scale):
    return rms_norm(x, g) * (1.0 + scale) + shift


def split_heads(t, n):
    b, l, _ = t.shape
    return t.reshape(b, l, n, -1).transpose(0, 2, 1, 3)


def merge_heads(t):
    b, h, l, d = t.shape
    return t.transpose(0, 2, 1, 3).reshape(b, l, h * d)


def split_columns(p):
    offsets = np.cumsum(PROJ_WIDTHS)[:-1].tolist()
    return jnp.split(p, offsets, axis=-1)


def rope_2d_tables(length, dh):
    t = jnp.arange(length)
    pos = jnp.stack([t // GRID_W, t % GRID_W], axis=-1).astype(jnp.float32)
    quarter = dh // 4
    inv = ROPE_BASE ** (-jnp.arange(quarter, dtype=jnp.float32) / quarter)
    ang = pos[:, :, None] * inv
    return jnp.cos(ang), jnp.sin(ang)


def apply_rope_2d(x, cos, sin):
    shp = x.shape
    xs = x.astype(jnp.float32).reshape(*shp[:-1], 2, 2, shp[-1] // 4)
    x1, x2 = xs[..., 0, :], xs[..., 1, :]
    out = jnp.stack([x1 * cos - x2 * sin, x2 * cos + x1 * sin], axis=-2)
    return out.reshape(shp).astype(x.dtype)


def head_layer_norm(y):
    yf = y.astype(jnp.float32)
    mu = jnp.mean(yf, axis=-1, keepdims=True)
    var = jnp.mean(jnp.square(yf - mu), axis=-1, keepdims=True)
    return ((yf - mu) * lax.rsqrt(var + EPS)).astype(y.dtype)


def head_rms_norm(y, g):
    yf = y.astype(jnp.float32)
    out = yf * lax.rsqrt(jnp.mean(yf * yf, axis=-1, keepdims=True) + EPS) * g.astype(jnp.float32)
    return out.astype(y.dtype)


def context_attention(q, k, v, sink):
    lc = k.shape[2]
    s = jnp.einsum('bkgqd,bkcd->bkgqc', q, k).astype(jnp.float32) * (q.shape[-1] ** -0.5)
    if sink is not None:
        sink_col = jnp.broadcast_to(sink.astype(jnp.float32)[None, :, :, None, None], s.shape[:-1] + (1,))
        s = jnp.concatenate([s, sink_col], axis=-1)
    p = jax.nn.softmax(s, axis=-1)[..., :lc]
    return jnp.einsum('bkgqc,bkcd->bkgqd', p.astype(v.dtype), v)


def neighbourhood_attention(qx, kx, vx, qz, kz, vz, rpb, with_ctx_out):
    b, h, s_len, dh = qx.shape
    rows = s_len // GRID_W
    kh = min(NA_ROWS, rows)
    grid = lambda t: t.reshape(b, h, rows, GRID_W, dh)
    q = grid(qx) * (dh ** -0.5)
    r = jnp.arange(rows)
    row_idx = jnp.clip(r - kh // 2, 0, rows - kh)[:, None] + jnp.arange(kh)
    k_band = grid(kx)[:, :, row_idx]
    v_band = grid(vx)[:, :, row_idx]
    s_nb = jnp.einsum('bhrqd,bhrjwd->bhrqjw', q, k_band).astype(jnp.float32)
    col = jnp.arange(GRID_W)
    col_start = jnp.clip(col - NA_COLS // 2, 0, GRID_W - NA_COLS)
    col_ok = (col[None, :] >= col_start[:, None]) & (col[None, :] < col_start[:, None] + NA_COLS)
    d_row = row_idx - r[:, None] + NA_ROWS - 1
    d_col = jnp.clip(col[None, :] - col[:, None], -(NA_COLS - 1), NA_COLS - 1) + NA_COLS - 1
    bias = rpb.astype(jnp.float32)[:, d_row][..., d_col].transpose(0, 1, 3, 2, 4)
    s_nb = jnp.where(col_ok[:, None, :], s_nb + bias[None], -jnp.inf)
    n_nb = kh * GRID_W
    s_ctx = jnp.einsum('bhrqd,bhcd->bhrqc', q, kz).astype(jnp.float32)
    p = jax.nn.softmax(jnp.concatenate([s_nb.reshape(b, h, rows, GRID_W, n_nb), s_ctx], axis=-1), axis=-1)
    p_nb = p[..., :n_nb].reshape(b, h, rows, GRID_W, kh, GRID_W).astype(vx.dtype)
    o = (jnp.einsum('bhrqjw,bhrjwd->bhrqd', p_nb, v_band)
         + jnp.einsum('bhrqc,bhcd->bhrqd', p[..., n_nb:].astype(vx.dtype), vz))
    ox = o.reshape(b, h, s_len, dh)
    oz = context_attention(qz[:, :, None], kz, vz, None)[:, :, 0] if with_ctx_out else None
    return oz, ox


def sliding_window_attention(qx, kx, vx, qz, kz, vz, sink, with_ctx_out):
    b, hq, s_len, dh = qx.shape
    hk = kx.shape[1]
    g = hq // hk
    nb = s_len // SWA_BLOCK
    q = (qx * (dh ** -0.5)).reshape(b, hk, g, nb, SWA_BLOCK, dh)
    pad = lambda t: jnp.pad(t, ((0, 0), (0, 0), (SWA_BLOCK, SWA_BLOCK), (0, 0))).reshape(b, hk, nb + 2, SWA_BLOCK, dh)
    band = lambda t: jnp.concatenate([t[:, :, :-2], t[:, :, 1:-1], t[:, :, 2:]], axis=3)
    kb, vb = band(pad(kx)), band(pad(vx))
    blk = jnp.arange(nb)[:, None]
    qpos = blk * SWA_BLOCK + jnp.arange(SWA_BLOCK)
    kpos = (blk - 1) * SWA_BLOCK + jnp.arange(3 * SWA_BLOCK)
    ok = ((jnp.abs(qpos[:, :, None] - kpos[:, None, :]) <= SWA_WINDOW)
          & (kpos[:, None, :] >= 0) & (kpos[:, None, :] < s_len))
    s_win = jnp.where(ok, jnp.einsum('bkgnqd,bkncd->bkgnqc', q, kb).astype(jnp.float32), -jnp.inf)
    s_ctx = jnp.einsum('bkgnqd,bkcd->bkgnqc', q, kz).astype(jnp.float32)
    sink_hg = sink.reshape(hk, g)
    s_sink = jnp.broadcast_to(sink_hg.astype(jnp.float32)[None, :, :, None, None, None], s_win.shape[:-1] + (1,))
    p = jax.nn.softmax(jnp.concatenate([s_win, s_ctx, s_sink], axis=-1), axis=-1).astype(vx.dtype)
    nw = 3 * SWA_BLOCK
    lc = kz.shape[2]
    o = (jnp.einsum('bkgnqc,bkncd->bkgnqd', p[..., :nw], vb)
         + jnp.einsum('bkgnqc,bkcd->bkgnqd', p[..., nw:nw + lc], vz))
    ox = o.reshape(b, hq, s_len, dh)
    oz = None
    if with_ctx_out:
        lz = qz.shape[2]
        oz = context_attention(qz.reshape(b, hk, g, lz, dh), kz, vz, sink_hg).reshape(b, hq, lz, dh)
    return oz, ox


def gated_chunk_scan(q, k, v, log_a, s0, inclusive):
    b, h, l, dk = q.shape
    dv = v.shape[-1]
    n = l // SCAN_CHUNK
    f32 = jnp.float32
    qc = q.astype(f32).reshape(b, h, n, SCAN_CHUNK, dk)
    kc = k.astype(f32).reshape(b, h, n, SCAN_CHUNK, dk)
    vc = v.astype(f32).reshape(b, h, n, SCAN_CHUNK, dv)
    gcum = jnp.cumsum(log_a.astype(f32).reshape(b, h, n, SCAN_CHUNK, dk), axis=3)
    g_last = gcum[:, :, :, -1:, :]
    q_rel = qc * jnp.exp(gcum - g_last)
    k_rel = kc * jnp.exp(g_last - gcum)
    a = jnp.einsum('bhnid,bhnjd->bhnij', q_rel, k_rel)
    tri = jnp.tril(jnp.ones((SCAN_CHUNK, SCAN_CHUNK), dtype=bool), 0 if inclusive else -1)
    o_intra = jnp.einsum('bhnij,bhnjv->bhniv', jnp.where(tri, a, 0.0), vc)
    inc = jnp.einsum('bhnjd,bhnjv->nbhdv', k_rel, vc)
    dec = jnp.moveaxis(jnp.exp(g_last[:, :, :, 0, :]), 2, 0)[..., None]

    def step(s, xs):
        d, u = xs
        return d * s + u, s

    s_final, s_prev = lax.scan(step, s0.astype(f32), (dec, inc))
    o_inter = jnp.einsum('bhnid,nbhdv->bhniv', qc * jnp.exp(gcum), s_prev)
    o = (o_intra + o_inter).reshape(b, h, l, dv).astype(v.dtype)
    return o, s_final


def bidirectional_scan(qz, kz, vz, gz_f, gz_b, qx, kx, vx, gx_f, gx_b):
    b, h, _, dk = qz.shape
    dv = vz.shape[-1]
    flip = lambda t: jnp.flip(t, axis=2)
    s0 = jnp.zeros((b, h, dk, dv), jnp.float32)
    oz_f, sz_f = gated_chunk_scan(qz, kz, vz, gz_f, s0, True)
    oz_b, sz_b = gated_chunk_scan(flip(qz), flip(kz), flip(vz), flip(gz_b), s0, False)
    ox_f, _ = gated_chunk_scan(qx, kx, vx, gx_f, sz_f, True)
    ox_b, _ = gated_chunk_scan(flip(qx), flip(kx), flip(vx), flip(gx_b), sz_b, False)
    return oz_f + flip(oz_b), ox_f + flip(ox_b)


def token_mixers(hx, hz, w_in, w_out, na_rpb, ret_log_gamma, gla_w_gate_up, gla_b_gate, gla_norm_g,
                 swa_sink, rope_hd, rope_ret, with_ctx_out):
    (na_qx, na_kx, na_vx, rt_qx, rt_kx, rt_vx, rt_gx, gl_qx, gl_kx, gl_vx, gl_gx, gl_dfx, gl_dbx,
     sw_qx, sw_kx, sw_vx) = split_columns(hx @ w_in)
    (na_qz, na_kz, na_vz, rt_qz, rt_kz, rt_vz, rt_gz, gl_qz, gl_kz, gl_vz, gl_gz, gl_dfz, gl_dbz,
     sw_qz, sw_kz, sw_vz) = split_columns(hz @ w_in)
    cos_h, sin_h = rope_hd
    cos_r, sin_r = rope_ret

    na_z, na_x = neighbourhood_attention(
        split_heads(na_qx, NA_HEADS), split_heads(na_kx, NA_HEADS), split_heads(na_vx, NA_HEADS),
        split_heads(na_qz, NA_HEADS), split_heads(na_kz, NA_HEADS), split_heads(na_vz, NA_HEADS),
        na_rpb, with_ctx_out)

    rt_scale = RET_DK ** -0.5
    rq_x = apply_rope_2d(split_heads(rt_qx, RET_HEADS), cos_r, sin_r)
    rk_x = apply_rope_2d(split_heads(rt_kx, RET_HEADS), cos_r, sin_r) * rt_scale
    rq_z = split_heads(rt_qz, RET_HEADS)
    rk_z = split_heads(rt_kz, RET_HEADS) * rt_scale
    gam = lambda ref, d: jnp.broadcast_to(ret_log_gamma[d][None, :, None, None], ref.shape)
    ry_z, ry_x = bidirectional_scan(
        rq_z, rk_z, split_heads(rt_vz, RET_HEADS), gam(rq_z, 0), gam(rq_z, 1),
        rq_x, rk_x, split_heads(rt_vx, RET_HEADS), gam(rq_x, 0), gam(rq_x, 1))

    def gla_gate(d_low, direction):
        pre = d_low @ gla_w_gate_up[direction] + gla_b_gate[direction]
        return split_heads(jax.nn.log_sigmoid(pre.astype(jnp.float32)) / GLA_TAU, GLA_HEADS)

    gq = lambda t: split_heads(t, GLA_HEADS) * (GLA_DK ** -0.5)
    gy_z, gy_x = bidirectional_scan(
        gq(gl_qz), split_heads(gl_kz, GLA_HEADS), split_heads(gl_vz, GLA_HEADS),
        gla_gate(gl_dfz, 0), gla_gate(gl_dbz, 1),
        gq(gl_qx), split_heads(gl_kx, GLA_HEADS), split_heads(gl_vx, GLA_HEADS),
        gla_gate(gl_dfx, 0), gla_gate(gl_dbx, 1))

    sw_z, sw_x = sliding_window_attention(
        apply_rope_2d(split_heads(sw_qx, SWA_HEADS), cos_h, sin_h),
        apply_rope_2d(split_heads(sw_kx, SWA_KV_HEADS), cos_h, sin_h),
        split_heads(sw_vx, SWA_KV_HEADS),
        split_heads(sw_qz, SWA_HEADS), split_heads(sw_kz, SWA_KV_HEADS), split_heads(sw_vz, SWA_KV_HEADS),
        swa_sink, with_ctx_out)

    ret_x = merge_heads(head_layer_norm(ry_x)) * jax.nn.silu(rt_gx)
    gla_x = merge_heads(head_rms_norm(gy_x, gla_norm_g)) * jax.nn.silu(gl_gx)
    out_x = jnp.concatenate([merge_heads(na_x), ret_x, gla_x, merge_heads(sw_x)], axis=-1) @ w_out
    if not with_ctx_out:
        return out_x, None
    ret_z = merge_heads(head_layer_norm(ry_z)) * jax.nn.silu(rt_gz)
    gla_z = merge_heads(head_rms_norm(gy_z, gla_norm_g)) * jax.nn.silu(gl_gz)
    out_z = jnp.concatenate([merge_heads(na_z), ret_z, gla_z, merge_heads(sw_z)], axis=-1) @ w_out
    return out_x, out_z


def peer_ffn(h, w_q, sub_keys, expert_u, expert_v):
    b, l, d = h.shape
    n = PEER_TOKEN_BLOCK
    ht = h.reshape(b * l // n, n, d)

    def block(hb):
        q = (hb @ w_q).reshape(n, PEER_HEADS, 2, PEER_DK // 2)
        s = jnp.einsum('thpd,phkd->thpk', q, sub_keys).astype(jnp.float32)
        top_s, top_i = lax.top_k(s, PEER_TOPK)
        cand_s = (top_s[:, :, 0, :, None] + top_s[:, :, 1, None, :]).reshape(n, PEER_HEADS, -1)
        cand_i = (top_i[:, :, 0, :, None] * PEER_N_KEYS + top_i[:, :, 1, None, :]).reshape(n, PEER_HEADS, -1)
        best_s, best_pos = lax.top_k(cand_s, PEER_TOPK)
        idx = jnp.take_along_axis(cand_i, best_pos, axis=-1)
        gate = jax.nn.softmax(best_s, axis=-1)
        act = jax.nn.gelu(jnp.einsum('td,thkd->thk', hb, expert_u[idx]).astype(jnp.float32), approximate=False)
        return jnp.einsum('thk,thkd->td', (gate * act).astype(hb.dtype), expert_v[idx])

    return lax.map(block, ht).reshape(b, l, d)


def setup_inputs(seed: int = 0) -> dict:
    key = jax.random.key(seed)
    ks = jax.random.split(key, 24)
    f32 = jnp.float32
    nrm = lambda k, shape, scale: jax.random.normal(k, shape, f32) * scale
    L, D = DEPTH, D_MODEL
    ret_base = jnp.log1p(-(2.0 ** (-5.0 - jnp.arange(RET_HEADS, dtype=f32))))
    return {
        "x": nrm(ks[0], (BATCH, SEQ, D), 1.0),
        "c": nrm(ks[1], (BATCH, D), 1.0),
        "ctx": nrm(ks[2], (BATCH, CTX_LEN, D), 1.0),
        "c_ctx": nrm(ks[3], (D,), 1.0),
        "w_ada": nrm(ks[4], (L, D, 6 * D), 0.2 * D ** -0.5),
        "b_ada": nrm(ks[5], (L, 6 * D), 0.02),
        "norm_attn_g": 1.0 + nrm(ks[6], (L, D), 0.02),
        "norm_ffn_g": 1.0 + nrm(ks[7], (L, D), 0.02),
        "w_in": nrm(ks[8], (L, D, D_IN), D ** -0.5),
        "na_rpb": nrm(ks[9], (L, NA_HEADS, 2 * NA_ROWS - 1, 2 * NA_COLS - 1), 0.1),
        "ret_log_gamma": ret_base * (1.0 + nrm(ks[10], (L, 2, RET_HEADS), 0.05)),
        "gla_w_gate_up": nrm(ks[11], (L, 2, GLA_RANK, GLA_HEADS * GLA_DK), GLA_RANK ** -0.5),
        "gla_b_gate": nrm(ks[12], (L, 2, GLA_HEADS * GLA_DK), 0.1),
        "gla_norm_g": 1.0 + nrm(ks[13], (L, GLA_DV), 0.02),
        "swa_sink": nrm(ks[14], (L, SWA_HEADS), 0.5),
        "w_out": nrm(ks[15], (L, D, D), D ** -0.5),
        "peer_w_q": nrm(ks[16], (L, D, PEER_HEADS * PEER_DK), D ** -0.5),
        "peer_sub_keys": nrm(ks[17], (L, 2, PEER_HEADS, PEER_N_KEYS, PEER_DK // 2), (PEER_DK // 2) ** -0.5),
        "peer_u": nrm(ks[18], (L, PEER_N_EXPERTS, D), D ** -0.5),
        "peer_v": nrm(ks[19], (L, PEER_N_EXPERTS, D), PEER_HEADS ** -0.5),
        "final_g": 1.0 + nrm(ks[20], (D,), 0.02),
    }


def reference(x, c, ctx, c_ctx, w_ada, b_ada, norm_attn_g, norm_ffn_g, w_in, na_rpb, ret_log_gamma,
              gla_w_gate_up, gla_b_gate, gla_norm_g, swa_sink, w_out, peer_w_q, peer_sub_keys,
              peer_u, peer_v, final_g):
    s_len = x.shape[1]
    rope_hd = rope_2d_tables(s_len, HEAD_DIM)
    rope_ret = rope_2d_tables(s_len, RET_DK)
    z = ctx
    for layer in range(DEPTH):
        has_next = layer < DEPTH - 1
        mx = jnp.split((jax.nn.silu(c) @ w_ada[layer] + b_ada[layer])[:, None, :], 6, axis=-1)
        mz = jnp.split(jax.nn.silu(c_ctx) @ w_ada[layer] + b_ada[layer], 6, axis=-1)
        hx = modulate(x, norm_attn_g[layer], mx[0], mx[1])
        hz = modulate(z, norm_attn_g[layer], mz[0], mz[1])
        ox, oz = token_mixers(hx, hz, w_in[layer], w_out[layer], na_rpb[layer], ret_log_gamma[layer],
                              gla_w_gate_up[layer], gla_b_gate[layer], gla_norm_g[layer], swa_sink[layer],
                              rope_hd, rope_ret, has_next)
        x = x + mx[2] * ox
        hx = modulate(x, norm_ffn_g[layer], mx[3], mx[4])
        x = x + mx[5] * peer_ffn(hx, peer_w_q[layer], peer_sub_keys[layer], peer_u[layer], peer_v[layer])
        if has_next:
            z = z + mz[2] * oz
            hz = modulate(z, norm_ffn_g[layer], mz[3], mz[4])
            z = z + mz[5] * peer_ffn(hz, peer_w_q[layer], peer_sub_keys[layer], peer_u[layer], peer_v[layer])
    return rms_norm(x, final_g)
```

```python
import jax, jax.numpy as jnp
from jax import lax
import numpy as np
from jax.experimental import pallas as pl
from jax.experimental.pallas import tpu as pltpu

D_MODEL = 2048
BATCH = 2
SEQ = 8192
DEPTH = 2

CTX_LEN = 256
GRID_W = 64
EPS = 1e-6
ROPE_BASE = 10000.0

GROUP_WIDTH = D_MODEL // 4
HEAD_DIM = 64
NA_HEADS = GROUP_WIDTH // HEAD_DIM
NA_ROWS = 8
NA_COLS = 16
RET_HEADS = 4
RET_DK = GROUP_WIDTH // RET_HEADS
RET_DV = GROUP_WIDTH // RET_HEADS
GLA_HEADS = 4
GLA_DV = GROUP_WIDTH // GLA_HEADS
GLA_DK = GLA_DV // 2
GLA_RANK = 16
GLA_TAU = 16.0
SWA_HEADS = GROUP_WIDTH // HEAD_DIM
SWA_KV_HEADS = SWA_HEADS // 4
SWA_WINDOW = 128
SWA_BLOCK = 128
SCAN_CHUNK = 64
PEER_HEADS = 8
PEER_N_KEYS = 128
PEER_N_EXPERTS = PEER_N_KEYS * PEER_N_KEYS
PEER_DK = 256
PEER_TOPK = 16
PEER_TOKEN_BLOCK = 128

PROJ_WIDTHS = (
    NA_HEADS * HEAD_DIM, NA_HEADS * HEAD_DIM, NA_HEADS * HEAD_DIM,
    RET_HEADS * RET_DK, RET_HEADS * RET_DK, RET_HEADS * RET_DV, RET_HEADS * RET_DV,
    GLA_HEADS * GLA_DK, GLA_HEADS * GLA_DK, GLA_HEADS * GLA_DV, GLA_HEADS * GLA_DV,
    GLA_RANK, GLA_RANK,
    SWA_HEADS * HEAD_DIM, SWA_KV_HEADS * HEAD_DIM, SWA_KV_HEADS * HEAD_DIM,
)
D_IN = sum(PROJ_WIDTHS)


def rms_norm(x, g):
    xf = x.astype(jnp.float32)
    y = xf * lax.rsqrt(jnp.mean(xf * xf, axis=-1, keepdims=True) + EPS)
    return (y * g.astype(jnp.float32)).astype(x.dtype)


def modulate(x, g, shift, scale):
    return rms_norm(x, g) * (1.0 + scale) + shift


def split_heads(t, n):
    b, l, _ = t.shape
    return t.reshape(b, l, n, -1).transpose(0, 2, 1, 3)


def merge_heads(t):
    b, h, l, d = t.shape
    return t.transpose(0, 2, 1, 3).reshape(b, l, h * d)


def split_columns(p):
    offsets = np.cumsum(PROJ_WIDTHS)[:-1].tolist()
    return jnp.split(p, offsets, axis=-1)


def rope_2d_tables(length, dh):
    t = jnp.arange(length)
    pos = jnp.stack([t // GRID_W, t % GRID_W], axis=-1).astype(jnp.float32)
    quarter = dh // 4
    inv = ROPE_BASE ** (-jnp.arange(quarter, dtype=jnp.float32) / quarter)
    ang = pos[:, :, None] * inv
    return jnp.cos(ang), jnp.sin(ang)


def apply_rope_2d(x, cos, sin):
    shp = x.shape
    xs = x.astype(jnp.float32).reshape(*shp[:-1], 2, 2, shp[-1] // 4)
    x1, x2 = xs[..., 0, :], xs[..., 1, :]
    out = jnp.stack([x1 * cos - x2 * sin, x2 * cos + x1 * sin], axis=-2)
    return out.reshape(shp).astype(x.dtype)


def head_layer_norm(y):
    yf = y.astype(jnp.float32)
    mu = jnp.mean(yf, axis=-1, keepdims=True)
    var = jnp.mean(jnp.square(yf - mu), axis=-1, keepdims=True)
    return ((yf - mu) * lax.rsqrt(var + EPS)).astype(y.dtype)


def head_rms_norm(y, g):
    yf = y.astype(jnp.float32)
    out = yf * lax.rsqrt(jnp.mean(yf * yf, axis=-1, keepdims=True) + EPS) * g.astype(jnp.float32)
    return out.astype(y.dtype)


def context_attention(q, k, v, sink):
    lc = k.shape[2]
    s = jnp.einsum('bkgqd,bkcd->bkgqc', q, k).astype(jnp.float32) * (q.shape[-1] ** -0.5)
    if sink is not None:
        sink_col = jnp.broadcast_to(sink.astype(jnp.float32)[None, :, :, None, None], s.shape[:-1] + (1,))
        s = jnp.concatenate([s, sink_col], axis=-1)
    p = jax.nn.softmax(s, axis=-1)[..., :lc]
    return jnp.einsum('bkgqc,bkcd->bkgqd', p.astype(v.dtype), v)


def neighbourhood_attention(qx, kx, vx, qz, kz, vz, rpb, with_ctx_out):
    b, h, s_len, dh = qx.shape
    rows = s_len // GRID_W
    kh = min(NA_ROWS, rows)
    grid = lambda t: t.reshape(b, h, rows, GRID_W, dh)
    q = grid(qx) * (dh ** -0.5)
    r = jnp.arange(rows)
    row_idx = jnp.clip(r - kh // 2, 0, rows - kh)[:, None] + jnp.arange(kh)
    k_band = grid(kx)[:, :, row_idx]
    v_band = grid(vx)[:, :, row_idx]
    s_nb = jnp.einsum('bhrqd,bhrjwd->bhrqjw', q, k_band).astype(jnp.float32)
    col = jnp.arange(GRID_W)
    col_start = jnp.clip(col - NA_COLS // 2, 0, GRID_W - NA_COLS)
    col_ok = (col[None, :] >= col_start[:, None]) & (col[None, :] < col_start[:, None] + NA_COLS)
    d_row = row_idx - r[:, None] + NA_ROWS - 1
    d_col = jnp.clip(col[None, :] - col[:, None], -(NA_COLS - 1), NA_COLS - 1) + NA_COLS - 1
    bias = rpb.astype(jnp.float32)[:, d_row][..., d_col].transpose(0, 1, 3, 2, 4)
    s_nb = jnp.where(col_ok[:, None, :], s_nb + bias[None], -jnp.inf)
    n_nb = kh * GRID_W
    s_ctx = jnp.einsum('bhrqd,bhcd->bhrqc', q, kz).astype(jnp.float32)
    p = jax.nn.softmax(jnp.concatenate([s_nb.reshape(b, h, rows, GRID_W, n_nb), s_ctx], axis=-1), axis=-1)
    p_nb = p[..., :n_nb].reshape(b, h, rows, GRID_W, kh, GRID_W).astype(vx.dtype)
    o = (jnp.einsum('bhrqjw,bhrjwd->bhrqd', p_nb, v_band)
         + jnp.einsum('bhrqc,bhcd->bhrqd', p[..., n_nb:].astype(vx.dtype), vz))
    ox = o.reshape(b, h, s_len, dh)
    oz = context_attention(qz[:, :, None], kz, vz, None)[:, :, 0] if with_ctx_out else None
    return oz, ox


def sliding_window_attention(qx, kx, vx, qz, kz, vz, sink, with_ctx_out):
    b, hq, s_len, dh = qx.shape
    hk = kx.shape[1]
    g = hq // hk
    nb = s_len // SWA_BLOCK
    q = (qx * (dh ** -0.5)).reshape(b, hk, g, nb, SWA_BLOCK, dh)
    pad = lambda t: jnp.pad(t, ((0, 0), (0, 0), (SWA_BLOCK, SWA_BLOCK), (0, 0))).reshape(b, hk, nb + 2, SWA_BLOCK, dh)
    band = lambda t: jnp.concatenate([t[:, :, :-2], t[:, :, 1:-1], t[:, :, 2:]], axis=3)
    kb, vb = band(pad(kx)), band(pad(vx))
    blk = jnp.arange(nb)[:, None]
    qpos = blk * SWA_BLOCK + jnp.arange(SWA_BLOCK)
    kpos = (blk - 1) * SWA_BLOCK + jnp.arange(3 * SWA_BLOCK)
    ok = ((jnp.abs(qpos[:, :, None] - kpos[:, None, :]) <= SWA_WINDOW)
          & (kpos[:, None, :] >= 0) & (kpos[:, None, :] < s_len))
    s_win = jnp.where(ok, jnp.einsum('bkgnqd,bkncd->bkgnqc', q, kb).astype(jnp.float32), -jnp.inf)
    s_ctx = jnp.einsum('bkgnqd,bkcd->bkgnqc', q, kz).astype(jnp.float32)
    sink_hg = sink.reshape(hk, g)
    s_sink = jnp.broadcast_to(sink_hg.astype(jnp.float32)[None, :, :, None, None, None], s_win.shape[:-1] + (1,))
    p = jax.nn.softmax(jnp.concatenate([s_win, s_ctx, s_sink], axis=-1), axis=-1).astype(vx.dtype)
    nw = 3 * SWA_BLOCK
    lc = kz.shape[2]
    o = (jnp.einsum('bkgnqc,bkncd->bkgnqd', p[..., :nw], vb)
         + jnp.einsum('bkgnqc,bkcd->bkgnqd', p[..., nw:nw + lc], vz))
    ox = o.reshape(b, hq, s_len, dh)
    oz = None
    if with_ctx_out:
        lz = qz.shape[2]
        oz = context_attention(qz.reshape(b, hk, g, lz, dh), kz, vz, sink_hg).reshape(b, hq, lz, dh)
    return oz, ox


def gated_chunk_scan(q, k, v, log_a, s0, inclusive):
    b, h, l, dk = q.shape
    dv = v.shape[-1]
    n = l // SCAN_CHUNK
    f32 = jnp.float32
    qc = q.astype(f32).reshape(b, h, n, SCAN_CHUNK, dk)
    kc = k.astype(f32).reshape(b, h, n, SCAN_CHUNK, dk)
    vc = v.astype(f32).reshape(b, h, n, SCAN_CHUNK, dv)
    gcum = jnp.cumsum(log_a.astype(f32).reshape(b, h, n, SCAN_CHUNK, dk), axis=3)
    g_last = gcum[:, :, :, -1:, :]
    q_rel = qc * jnp.exp(gcum - g_last)
    k_rel = kc * jnp.exp(g_last - gcum)
    a = jnp.einsum('bhnid,bhnjd->bhnij', q_rel, k_rel)
    tri = jnp.tril(jnp.ones((SCAN_CHUNK, SCAN_CHUNK), dtype=bool), 0 if inclusive else -1)
    o_intra = jnp.einsum('bhnij,bhnjv->bhniv', jnp.where(tri, a, 0.0), vc)
    inc = jnp.einsum('bhnjd,bhnjv->nbhdv', k_rel, vc)
    dec = jnp.moveaxis(jnp.exp(g_last[:, :, :, 0, :]), 2, 0)[..., None]

    def step(s, xs):
        d, u = xs
        return d * s + u, s

    s_final, s_prev = lax.scan(step, s0.astype(f32), (dec, inc))
    o_inter = jnp.einsum('bhnid,nbhdv->bhniv', qc * jnp.exp(gcum), s_prev)
    o = (o_intra + o_inter).reshape(b, h, l, dv).astype(v.dtype)
    return o, s_final


def bidirectional_scan(qz, kz, vz, gz_f, gz_b, qx, kx, vx, gx_f, gx_b):
    b, h, _, dk = qz.shape
    dv = vz.shape[-1]
    flip = lambda t: jnp.flip(t, axis=2)
    s0 = jnp.zeros((b, h, dk, dv), jnp.float32)
    oz_f, sz_f = gated_chunk_scan(qz, kz, vz, gz_f, s0, True)
    oz_b, sz_b = gated_chunk_scan(flip(qz), flip(kz), flip(vz), flip(gz_b), s0, False)
    ox_f, _ = gated_chunk_scan(qx, kx, vx, gx_f, sz_f, True)
    ox_b, _ = gated_chunk_scan(flip(qx), flip(kx), flip(vx), flip(gx_b), sz_b, False)
    return oz_f + flip(oz_b), ox_f + flip(ox_b)


def token_mixers(hx, hz, w_in, w_out, na_rpb, ret_log_gamma, gla_w_gate_up, gla_b_gate, gla_norm_g,
                 swa_sink, rope_hd, rope_ret, with_ctx_out):
    (na_qx, na_kx, na_vx, rt_qx, rt_kx, rt_vx, rt_gx, gl_qx, gl_kx, gl_vx, gl_gx, gl_dfx, gl_dbx,
     sw_qx, sw_kx, sw_vx) = split_columns(hx @ w_in)
    (na_qz, na_kz, na_vz, rt_qz, rt_kz, rt_vz, rt_gz, gl_qz, gl_kz, gl_vz, gl_gz, gl_dfz, gl_dbz,
     sw_qz, sw_kz, sw_vz) = split_columns(hz @ w_in)
    cos_h, sin_h = rope_hd
    cos_r, sin_r = rope_ret

    na_z, na_x = neighbourhood_attention(
        split_heads(na_qx, NA_HEADS), split_heads(na_kx, NA_HEADS), split_heads(na_vx, NA_HEADS),
        split_heads(na_qz, NA_HEADS), split_heads(na_kz, NA_HEADS), split_heads(na_vz, NA_HEADS),
        na_rpb, with_ctx_out)

    rt_scale = RET_DK ** -0.5
    rq_x = apply_rope_2d(split_heads(rt_qx, RET_HEADS), cos_r, sin_r)
    rk_x = apply_rope_2d(split_heads(rt_kx, RET_HEADS), cos_r, sin_r) * rt_scale
    rq_z = split_heads(rt_qz, RET_HEADS)
    rk_z = split_heads(rt_kz, RET_HEADS) * rt_scale
    gam = lambda ref, d: jnp.broadcast_to(ret_log_gamma[d][None, :, None, None], ref.shape)
    ry_z, ry_x = bidirectional_scan(
        rq_z, rk_z, split_heads(rt_vz, RET_HEADS), gam(rq_z, 0), gam(rq_z, 1),
        rq_x, rk_x, split_heads(rt_vx, RET_HEADS), gam(rq_x, 0), gam(rq_x, 1))

    def gla_gate(d_low, direction):
        pre = d_low @ gla_w_gate_up[direction] + gla_b_gate[direction]
        return split_heads(jax.nn.log_sigmoid(pre.astype(jnp.float32)) / GLA_TAU, GLA_HEADS)

    gq = lambda t: split_heads(t, GLA_HEADS) * (GLA_DK ** -0.5)
    gy_z, gy_x = bidirectional_scan(
        gq(gl_qz), split_heads(gl_kz, GLA_HEADS), split_heads(gl_vz, GLA_HEADS),
        gla_gate(gl_dfz, 0), gla_gate(gl_dbz, 1),
        gq(gl_qx), split_heads(gl_kx, GLA_HEADS), split_heads(gl_vx, GLA_HEADS),
        gla_gate(gl_dfx, 0), gla_gate(gl_dbx, 1))

    sw_z, sw_x = sliding_window_attention(
        apply_rope_2d(split_heads(sw_qx, SWA_HEADS), cos_h, sin_h),
        apply_rope_2d(split_heads(sw_kx, SWA_KV_HEADS), cos_h, sin_h),
        split_heads(sw_vx, SWA_KV_HEADS),
        split_heads(sw_qz, SWA_HEADS), split_heads(sw_kz, SWA_KV_HEADS), split_heads(sw_vz, SWA_KV_HEADS),
        swa_sink, with_ctx_out)

    ret_x = merge_heads(head_layer_norm(ry_x)) * jax.nn.silu(rt_gx)
    gla_x = merge_heads(head_rms_norm(gy_x, gla_norm_g)) * jax.nn.silu(gl_gx)
    out_x = jnp.concatenate([merge_heads(na_x), ret_x, gla_x, merge_heads(sw_x)], axis=-1) @ w_out
    if not with_ctx_out:
        return out_x, None
    ret_z = merge_heads(head_layer_norm(ry_z)) * jax.nn.silu(rt_gz)
    gla_z = merge_heads(head_rms_norm(gy_z, gla_norm_g)) * jax.nn.silu(gl_gz)
    out_z = jnp.concatenate([merge_heads(na_z), ret_z, gla_z, merge_heads(sw_z)], axis=-1) @ w_out
    return out_x, out_z


def peer_ffn(h, w_q, sub_keys, expert_u, expert_v):
    b, l, d = h.shape
    n = PEER_TOKEN_BLOCK
    ht = h.reshape(b * l // n, n, d)

    def block(hb):
        q = (hb @ w_q).reshape(n, PEER_HEADS, 2, PEER_DK // 2)
        s = jnp.einsum('thpd,phkd->thpk', q, sub_keys).astype(jnp.float32)
        top_s, top_i = lax.top_k(s, PEER_TOPK)
        cand_s = (top_s[:, :, 0, :, None] + top_s[:, :, 1, None, :]).reshape(n, PEER_HEADS, -1)
        cand_i = (top_i[:, :, 0, :, None] * PEER_N_KEYS + top_i[:, :, 1, None, :]).reshape(n, PEER_HEADS, -1)
        best_s, best_pos = lax.top_k(cand_s, PEER_TOPK)
        idx = jnp.take_along_axis(cand_i, best_pos, axis=-1)
        gate = jax.nn.softmax(best_s, axis=-1)
        act = jax.nn.gelu(jnp.einsum('td,thkd->thk', hb, expert_u[idx]).astype(jnp.float32), approximate=False)
        return jnp.einsum('thk,thkd->td', (gate * act).astype(hb.dtype), expert_v[idx])

    return lax.map(block, ht).reshape(b, l, d)


def _final_norm_kernel(x_ref, g_ref, o_ref):
    x = x_ref[...]
    y = x * lax.rsqrt(jnp.mean(x * x, axis=-1, keepdims=True) + EPS)
    o_ref[...] = y * g_ref[...]


def final_norm_pallas(x, g):
    b, l, d = x.shape
    xf = x.reshape(b * l, d)
    tm = 512
    out = pl.pallas_call(
        _final_norm_kernel,
        grid=(b * l // tm,),
        in_specs=[pl.BlockSpec((tm, d), lambda i: (i, 0)), pl.BlockSpec((1, d), lambda i: (0, 0))],
        out_specs=pl.BlockSpec((tm, d), lambda i: (i, 0)),
        out_shape=jax.ShapeDtypeStruct((b * l, d), x.dtype),
    )(xf, g.reshape(1, d))
    return out.reshape(b, l, d)


def kernel(x, c, ctx, c_ctx, w_ada, b_ada, norm_attn_g, norm_ffn_g, w_in, na_rpb, ret_log_gamma,
           gla_w_gate_up, gla_b_gate, gla_norm_g, swa_sink, w_out, peer_w_q, peer_sub_keys,
           peer_u, peer_v, final_g):
    s_len = x.shape[1]
    rope_hd = rope_2d_tables(s_len, HEAD_DIM)
    rope_ret = rope_2d_tables(s_len, RET_DK)
    z = ctx
    for layer in range(DEPTH):
        has_next = layer < DEPTH - 1
        mx = jnp.split((jax.nn.silu(c) @ w_ada[layer] + b_ada[layer])[:, None, :], 6, axis=-1)
        mz = jnp.split(jax.nn.silu(c_ctx) @ w_ada[layer] + b_ada[layer], 6, axis=-1)
        hx = modulate(x, norm_attn_g[layer], mx[0], mx[1])
        hz = modulate(z, norm_attn_g[layer], mz[0], mz[1])
        ox, oz = token_mixers(hx, hz, w_in[layer], w_out[layer], na_rpb[layer], ret_log_gamma[layer],
                              gla_w_gate_up[layer], gla_b_gate[layer], gla_norm_g[layer], swa_sink[layer],
                              rope_hd, rope_ret, has_next)
        x = x + mx[2] * ox
        hx = modulate(x, norm_ffn_g[layer], mx[3], mx[4])
        x = x + mx[5] * peer_ffn(hx, peer_w_q[layer], peer_sub_keys[layer], peer_u[layer], peer_v[layer])
        if has_next:
            z = z + mz[2] * oz
            hz = modulate(z, norm_ffn_g[layer], mz[3], mz[4])
            z = z + mz[5] * peer_ffn(hz, peer_w_q[layer], peer_sub_keys[layer], peer_u[layer], peer_v[layer])
    return final_norm_pallas(x, final_g)
```

```python
import jax, jax.numpy as jnp
from jax import lax
import numpy as np
from jax.experimental import pallas as pl
from jax.experimental.pallas import tpu as pltpu

D_MODEL = 2048
BATCH = 2
SEQ = 8192
DEPTH = 2

CTX_LEN = 256
GRID_W = 64
EPS = 1e-6
ROPE_BASE = 10000.0

GROUP_WIDTH = D_MODEL // 4
HEAD_DIM = 64
NA_HEADS = GROUP_WIDTH // HEAD_DIM
NA_ROWS = 8
NA_COLS = 16
RET_HEADS = 4
RET_DK = GROUP_WIDTH // RET_HEADS
RET_DV = GROUP_WIDTH // RET_HEADS
GLA_HEADS = 4
GLA_DV = GROUP_WIDTH // GLA_HEADS
GLA_DK = GLA_DV // 2
GLA_RANK = 16
GLA_TAU = 16.0
SWA_HEADS = GROUP_WIDTH // HEAD_DIM
SWA_KV_HEADS = SWA_HEADS // 4
SWA_WINDOW = 128
SWA_BLOCK = 128
SCAN_CHUNK = 64
PEER_HEADS = 8
PEER_N_KEYS = 128
PEER_N_EXPERTS = PEER_N_KEYS * PEER_N_KEYS
PEER_DK = 256
PEER_TOPK = 16
PEER_TOKEN_BLOCK = 128

PROJ_WIDTHS = (
    NA_HEADS * HEAD_DIM, NA_HEADS * HEAD_DIM, NA_HEADS * HEAD_DIM,
    RET_HEADS * RET_DK, RET_HEADS * RET_DK, RET_HEADS * RET_DV, RET_HEADS * RET_DV,
    GLA_HEADS * GLA_DK, GLA_HEADS * GLA_DK, GLA_HEADS * GLA_DV, GLA_HEADS * GLA_DV,
    GLA_RANK, GLA_RANK,
    SWA_HEADS * HEAD_DIM, SWA_KV_HEADS * HEAD_DIM, SWA_KV_HEADS * HEAD_DIM,
)
D_IN = sum(PROJ_WIDTHS)


def rms_norm(x, g):
    xf = x.astype(jnp.float32)
    y = xf * lax.rsqrt(jnp.mean(xf * xf, axis=-1, keepdims=True) + EPS)
    return (y * g.astype(jnp.float32)).astype(x.dtype)


def modulate(x, g, shift, scale):
    return rms_norm(x, g) * (1.0 + scale) + shift


def split_heads(t, n):
    b, l, _ = t.shape
    return t.reshape(b, l, n, -1).transpose(0, 2, 1, 3)


def merge_heads(t):
    b, h, l, d = t.shape
    return t.transpose(0, 2, 1, 3).reshape(b, l, h * d)


def split_columns(p):
    offsets = np.cumsum(PROJ_WIDTHS)[:-1].tolist()
    return jnp.split(p, offsets, axis=-1)


def rope_2d_tables(length, dh):
    t = jnp.arange(length)
    pos = jnp.stack([t // GRID_W, t % GRID_W], axis=-1).astype(jnp.float32)
    quarter = dh // 4
    inv = ROPE_BASE ** (-jnp.arange(quarter, dtype=jnp.float32) / quarter)
    ang = pos[:, :, None] * inv
    return jnp.cos(ang), jnp.sin(ang)


def apply_rope_2d(x, cos, sin):
    shp = x.shape
    xs = x.astype(jnp.float32).reshape(*shp[:-1], 2, 2, shp[-1] // 4)
    x1, x2 = xs[..., 0, :], xs[..., 1, :]
    out = jnp.stack([x1 * cos - x2 * sin, x2 * cos + x1 * sin], axis=-2)
    return out.reshape(shp).astype(x.dtype)


def head_layer_norm(y):
    yf = y.astype(jnp.float32)
    mu = jnp.mean(yf, axis=-1, keepdims=True)
    var = jnp.mean(jnp.square(yf - mu), axis=-1, keepdims=True)
    return ((yf - mu) * lax.rsqrt(var + EPS)).astype(y.dtype)


def head_rms_norm(y, g):
    yf = y.astype(jnp.float32)
    out = yf * lax.rsqrt(jnp.mean(yf * yf, axis=-1, keepdims=True) + EPS) * g.astype(jnp.float32)
    return out.astype(y.dtype)


def context_attention(q, k, v, sink):
    lc = k.shape[2]
    s = jnp.einsum('bkgqd,bkcd->bkgqc', q, k).astype(jnp.float32) * (q.shape[-1] ** -0.5)
    if sink is not None:
        sink_col = jnp.broadcast_to(sink.astype(jnp.float32)[None, :, :, None, None], s.shape[:-1] + (1,))
        s = jnp.concatenate([s, sink_col], axis=-1)
    p = jax.nn.softmax(s, axis=-1)[..., :lc]
    return jnp.einsum('bkgqc,bkcd->bkgqd', p.astype(v.dtype), v)


def neighbourhood_attention(qx, kx, vx, qz, kz, vz, rpb, with_ctx_out):
    b, h, s_len, dh = qx.shape
    rows = s_len // GRID_W
    kh = min(NA_ROWS, rows)
    grid = lambda t: t.reshape(b, h, rows, GRID_W, dh)
    q = grid(qx) * (dh ** -0.5)
    r = jnp.arange(rows)
    row_idx = jnp.clip(r - kh // 2, 0, rows - kh)[:, None] + jnp.arange(kh)
    k_band = grid(kx)[:, :, row_idx]
    v_band = grid(vx)[:, :, row_idx]
    s_nb = jnp.einsum('bhrqd,bhrjwd->bhrqjw', q, k_band).astype(jnp.float32)
    col = jnp.arange(GRID_W)
    col_start = jnp.clip(col - NA_COLS // 2, 0, GRID_W - NA_COLS)
    col_ok = (col[None, :] >= col_start[:, None]) & (col[None, :] < col_start[:, None] + NA_COLS)
    d_row = row_idx - r[:, None] + NA_ROWS - 1
    d_col = jnp.clip(col[None, :] - col[:, None], -(NA_COLS - 1), NA_COLS - 1) + NA_COLS - 1
    bias = rpb.astype(jnp.float32)[:, d_row][..., d_col].transpose(0, 1, 3, 2, 4)
    s_nb = jnp.where(col_ok[:, None, :], s_nb + bias[None], -jnp.inf)
    n_nb = kh * GRID_W
    s_ctx = jnp.einsum('bhrqd,bhcd->bhrqc', q, kz).astype(jnp.float32)
    p = jax.nn.softmax(jnp.concatenate([s_nb.reshape(b, h, rows, GRID_W, n_nb), s_ctx], axis=-1), axis=-1)
    p_nb = p[..., :n_nb].reshape(b, h, rows, GRID_W, kh, GRID_W).astype(vx.dtype)
    o = (jnp.einsum('bhrqjw,bhrjwd->bhrqd', p_nb, v_band)
         + jnp.einsum('bhrqc,bhcd->bhrqd', p[..., n_nb:].astype(vx.dtype), vz))
    ox = o.reshape(b, h, s_len, dh)
    oz = context_attention(qz[:, :, None], kz, vz, None)[:, :, 0] if with_ctx_out else None
    return oz, ox


def sliding_window_attention(qx, kx, vx, qz, kz, vz, sink, with_ctx_out):
    b, hq, s_len, dh = qx.shape
    hk = kx.shape[1]
    g = hq // hk
    nb = s_len // SWA_BLOCK
    q = (qx * (dh ** -0.5)).reshape(b, hk, g, nb, SWA_BLOCK, dh)
    pad = lambda t: jnp.pad(t, ((0, 0), (0, 0), (SWA_BLOCK, SWA_BLOCK), (0, 0))).reshape(b, hk, nb + 2, SWA_BLOCK, dh)
    band = lambda t: jnp.concatenate([t[:, :, :-2], t[:, :, 1:-1], t[:, :, 2:]], axis=3)
    kb, vb = band(pad(kx)), band(pad(vx))
    blk = jnp.arange(nb)[:, None]
    qpos = blk * SWA_BLOCK + jnp.arange(SWA_BLOCK)
    kpos = (blk - 1) * SWA_BLOCK + jnp.arange(3 * SWA_BLOCK)
    ok = ((jnp.abs(qpos[:, :, None] - kpos[:, None, :]) <= SWA_WINDOW)
          & (kpos[:, None, :] >= 0) & (kpos[:, None, :] < s_len))
    s_win = jnp.where(ok, jnp.einsum('bkgnqd,bkncd->bkgnqc', q, kb).astype(jnp.float32), -jnp.inf)
    s_ctx = jnp.einsum('bkgnqd,bkcd->bkgnqc', q, kz).astype(jnp.float32)
    sink_hg = sink.reshape(hk, g)
    s_sink = jnp.broadcast_to(sink_hg.astype(jnp.float32)[None, :, :, None, None, None], s_win.shape[:-1] + (1,))
    p = jax.nn.softmax(jnp.concatenate([s_win, s_ctx, s_sink], axis=-1), axis=-1).astype(vx.dtype)
    nw = 3 * SWA_BLOCK
    lc = kz.shape[2]
    o = (jnp.einsum('bkgnqc,bkncd->bkgnqd', p[..., :nw], vb)
         + jnp.einsum('bkgnqc,bkcd->bkgnqd', p[..., nw:nw + lc], vz))
    ox = o.reshape(b, hq, s_len, dh)
    oz = None
    if with_ctx_out:
        lz = qz.shape[2]
        oz = context_attention(qz.reshape(b, hk, g, lz, dh), kz, vz, sink_hg).reshape(b, hq, lz, dh)
    return oz, ox


def gated_chunk_scan(q, k, v, log_a, s0, inclusive):
    b, h, l, dk = q.shape
    dv = v.shape[-1]
    n = l // SCAN_CHUNK
    f32 = jnp.float32
    qc = q.astype(f32).reshape(b, h, n, SCAN_CHUNK, dk)
    kc = k.astype(f32).reshape(b, h, n, SCAN_CHUNK, dk)
    vc = v.astype(f32).reshape(b, h, n, SCAN_CHUNK, dv)
    gcum = jnp.cumsum(log_a.astype(f32).reshape(b, h, n, SCAN_CHUNK, dk), axis=3)
    g_last = gcum[:, :, :, -1:, :]
    q_rel = qc * jnp.exp(gcum - g_last)
    k_rel = kc * jnp.exp(g_last - gcum)
    a = jnp.einsum('bhnid,bhnjd->bhnij', q_rel, k_rel)
    tri = jnp.tril(jnp.ones((SCAN_CHUNK, SCAN_CHUNK), dtype=bool), 0 if inclusive else -1)
    o_intra = jnp.einsum('bhnij,bhnjv->bhniv', jnp.where(tri, a, 0.0), vc)
    inc = jnp.einsum('bhnjd,bhnjv->nbhdv', k_rel, vc)
    dec = jnp.moveaxis(jnp.exp(g_last[:, :, :, 0, :]), 2, 0)[..., None]

    def step(s, xs):
        d, u = xs
        return d * s + u, s

    s_final, s_prev = lax.scan(step, s0.astype(f32), (dec, inc))
    o_inter = jnp.einsum('bhnid,nbhdv->bhniv', qc * jnp.exp(gcum), s_prev)
    o = (o_intra + o_inter).reshape(b, h, l, dv).astype(v.dtype)
    return o, s_final


def bidirectional_scan(qz, kz, vz, gz_f, gz_b, qx, kx, vx, gx_f, gx_b):
    b, h, _, dk = qz.shape
    dv = vz.shape[-1]
    flip = lambda t: jnp.flip(t, axis=2)
    s0 = jnp.zeros((b, h, dk, dv), jnp.float32)
    oz_f, sz_f = gated_chunk_scan(qz, kz, vz, gz_f, s0, True)
    oz_b, sz_b = gated_chunk_scan(flip(qz), flip(kz), flip(vz), flip(gz_b), s0, False)
    ox_f, _ = gated_chunk_scan(qx, kx, vx, gx_f, sz_f, True)
    ox_b, _ = gated_chunk_scan(flip(qx), flip(kx), flip(vx), flip(gx_b), sz_b, False)
    return oz_f + flip(oz_b), ox_f + flip(ox_b)


def token_mixers(hx, hz, w_in, w_out, na_rpb, ret_log_gamma, gla_w_gate_up, gla_b_gate, gla_norm_g,
                 swa_sink, rope_hd, rope_ret, with_ctx_out):
    (na_qx, na_kx, na_vx, rt_qx, rt_kx, rt_vx, rt_gx, gl_qx, gl_kx, gl_vx, gl_gx, gl_dfx, gl_dbx,
     sw_qx, sw_kx, sw_vx) = split_columns(hx @ w_in)
    (na_qz, na_kz, na_vz, rt_qz, rt_kz, rt_vz, rt_gz, gl_qz, gl_kz, gl_vz, gl_gz, gl_dfz, gl_dbz,
     sw_qz, sw_kz, sw_vz) = split_columns(hz @ w_in)
    cos_h, sin_h = rope_hd
    cos_r, sin_r = rope_ret

    na_z, na_x = neighbourhood_attention(
        split_heads(na_qx, NA_HEADS), split_heads(na_kx, NA_HEADS), split_heads(na_vx, NA_HEADS),
        split_heads(na_qz, NA_HEADS), split_heads(na_kz, NA_HEADS), split_heads(na_vz, NA_HEADS),
        na_rpb, with_ctx_out)

    rt_scale = RET_DK ** -0.5
    rq_x = apply_rope_2d(split_heads(rt_qx, RET_HEADS), cos_r, sin_r)
    rk_x = apply_rope_2d(split_heads(rt_kx, RET_HEADS), cos_r, sin_r) * rt_scale
    rq_z = split_heads(rt_qz, RET_HEADS)
    rk_z = split_heads(rt_kz, RET_HEADS) * rt_scale
    gam = lambda ref, d: jnp.broadcast_to(ret_log_gamma[d][None, :, None, None], ref.shape)
    ry_z, ry_x = bidirectional_scan(
        rq_z, rk_z, split_heads(rt_vz, RET_HEADS), gam(rq_z, 0), gam(rq_z, 1),
        rq_x, rk_x, split_heads(rt_vx, RET_HEADS), gam(rq_x, 0), gam(rq_x, 1))

    def gla_gate(d_low, direction):
        pre = d_low @ gla_w_gate_up[direction] + gla_b_gate[direction]
        return split_heads(jax.nn.log_sigmoid(pre.astype(jnp.float32)) / GLA_TAU, GLA_HEADS)

    gq = lambda t: split_heads(t, GLA_HEADS) * (GLA_DK ** -0.5)
    gy_z, gy_x = bidirectional_scan(
        gq(gl_qz), split_heads(gl_kz, GLA_HEADS), split_heads(gl_vz, GLA_HEADS),
        gla_gate(gl_dfz, 0), gla_gate(gl_dbz, 1),
        gq(gl_qx), split_heads(gl_kx, GLA_HEADS), split_heads(gl_vx, GLA_HEADS),
        gla_gate(gl_dfx, 0), gla_gate(gl_dbx, 1))

    sw_z, sw_x = sliding_window_attention(
        apply_rope_2d(split_heads(sw_qx, SWA_HEADS), cos_h, sin_h),
        apply_rope_2d(split_heads(sw_kx, SWA_KV_HEADS), cos_h, sin_h),
        split_heads(sw_vx, SWA_KV_HEADS),
        split_heads(sw_qz, SWA_HEADS), split_heads(sw_kz, SWA_KV_HEADS), split_heads(sw_vz, SWA_KV_HEADS),
        swa_sink, with_ctx_out)

    ret_x = merge_heads(head_layer_norm(ry_x)) * jax.nn.silu(rt_gx)
    gla_x = merge_heads(head_rms_norm(gy_x, gla_norm_g)) * jax.nn.silu(gl_gx)
    out_x = jnp.concatenate([merge_heads(na_x), ret_x, gla_x, merge_heads(sw_x)], axis=-1) @ w_out
    if not with_ctx_out:
        return out_x, None
    ret_z = merge_heads(head_layer_norm(ry_z)) * jax.nn.silu(rt_gz)
    gla_z = merge_heads(head_rms_norm(gy_z, gla_norm_g)) * jax.nn.silu(gl_gz)
    out_z = jnp.concatenate([merge_heads(na_z), ret_z, gla_z, merge_heads(sw_z)], axis=-1) @ w_out
    return out_x, out_z


def peer_ffn(h, w_q, sub_keys, expert_u, expert_v):
    b, l, d = h.shape
    n = PEER_TOKEN_BLOCK
    ht = h.reshape(b * l // n, n, d)

    def block(hb):
        q = (hb @ w_q).reshape(n, PEER_HEADS, 2, PEER_DK // 2)
        s = jnp.einsum('thpd,phkd->thpk', q, sub_keys).astype(jnp.float32)
        top_s, top_i = lax.top_k(s, PEER_TOPK)
        cand_s = (top_s[:, :, 0, :, None] + top_s[:, :, 1, None, :]).reshape(n, PEER_HEADS, -1)
        cand_i = (top_i[:, :, 0, :, None] * PEER_N_KEYS + top_i[:, :, 1, None, :]).reshape(n, PEER_HEADS, -1)
        best_s, best_pos = lax.top_k(cand_s, PEER_TOPK)
        idx = jnp.take_along_axis(cand_i, best_pos, axis=-1)
        gate = jax.nn.softmax(best_s, axis=-1)
        act = jax.nn.gelu(jnp.einsum('td,thkd->thk', hb, expert_u[idx]).astype(jnp.float32), approximate=False)
        return jnp.einsum('thk,thkd->td', (gate * act).astype(hb.dtype), expert_v[idx])

    return lax.map(block, ht).reshape(b, l, d)


PEER_PAIRS = PEER_HEADS * PEER_TOPK
PEER_ROUTE_TOKENS = 128
PEER_EXPERT_TOKENS = 8
INV_SQRT2 = 0.7071067811865476


def _topk_cols(s, payload=None):
    n_rows = s.shape[0]
    row = lax.broadcasted_iota(jnp.int32, s.shape, 0)
    vals, idxs = [], []
    for _ in range(PEER_TOPK):
        m = jnp.max(s, axis=0, keepdims=True)
        am = jnp.min(jnp.where(s == m, row, n_rows), axis=0, keepdims=True)
        sel = row == am
        vals.append(m)
        idxs.append(am if payload is None else jnp.max(jnp.where(sel, payload, -1), axis=0, keepdims=True))
        s = jnp.where(sel, -jnp.inf, s)
    return jnp.concatenate(vals, axis=0), jnp.concatenate(idxs, axis=0)


def _peer_route_kernel(x_ref, g_ref, shift_ref, scale_ref, wq_ref, sk_ref, h_ref, idx_ref, gate_ref):
    x = x_ref[...]
    y = x * lax.rsqrt(jnp.mean(x * x, axis=-1, keepdims=True) + EPS)
    h = (y * g_ref[...]) * (1.0 + scale_ref[...]) + shift_ref[...]
    h_ref[...] = h
    hb = h.astype(jnp.bfloat16)
    half = PEER_DK // 2

    def head_body(hd, carry):
        q = jnp.dot(hb, wq_ref[hd], preferred_element_type=jnp.float32)
        tops = []
        for p in range(2):
            qp = q[:, p * half:(p + 1) * half].astype(jnp.bfloat16)
            s_t = lax.dot_general(sk_ref[p, hd], qp, (((1,), (1,)), ((), ())),
                                  preferred_element_type=jnp.float32)
            tops.append(_topk_cols(s_t))
        (v0, i0), (v1, i1) = tops
        cand_s = jnp.concatenate([v0[a:a + 1, :] + v1 for a in range(PEER_TOPK)], axis=0)
        cand_i = jnp.concatenate([i0[a:a + 1, :] * PEER_N_KEYS + i1 for a in range(PEER_TOPK)], axis=0)
        best_s, best_e = _topk_cols(cand_s, cand_i)
        e = jnp.exp(best_s - best_s[0:1, :])
        gate_ref[hd] = e / jnp.sum(e, axis=0, keepdims=True)
        idx_ref[hd] = best_e
        return carry

    lax.fori_loop(0, PEER_HEADS, head_body, 0)


def peer_route(x2, g, shift, scale, rows_per_mod, w_q, sub_keys):
    n, d = x2.shape
    t = PEER_ROUTE_TOKENS
    m = shift.shape[0]
    blocks_per_mod = rows_per_mod // t
    wq_h = w_q.reshape(d, PEER_HEADS, PEER_DK).transpose(1, 0, 2).astype(jnp.bfloat16)
    sk = sub_keys.astype(jnp.bfloat16)
    mod_map = lambda i: (i // blocks_per_mod, 0, 0)
    return pl.pallas_call(
        _peer_route_kernel,
        grid=(n // t,),
        in_specs=[
            pl.BlockSpec((t, d), lambda i: (i, 0)),
            pl.BlockSpec((1, d), lambda i: (0, 0)),
            pl.BlockSpec((None, 1, d), mod_map),
            pl.BlockSpec((None, 1, d), mod_map),
            pl.BlockSpec((PEER_HEADS, d, PEER_DK), lambda i: (0, 0, 0)),
            pl.BlockSpec((2, PEER_HEADS, PEER_N_KEYS, PEER_DK // 2), lambda i: (0, 0, 0, 0)),
        ],
        out_specs=[
            pl.BlockSpec((t, d), lambda i: (i, 0)),
            pl.BlockSpec((PEER_HEADS, PEER_TOPK, t), lambda i: (0, 0, i)),
            pl.BlockSpec((PEER_HEADS, PEER_TOPK, t), lambda i: (0, 0, i)),
        ],
        out_shape=[
            jax.ShapeDtypeStruct((n, d), jnp.float32),
            jax.ShapeDtypeStruct((PEER_HEADS, PEER_TOPK, n), jnp.int32),
            jax.ShapeDtypeStruct((PEER_HEADS, PEER_TOPK, n), jnp.float32),
        ],
        compiler_params=pltpu.CompilerParams(dimension_semantics=("arbitrary",),
                                             vmem_limit_bytes=48 * 1024 * 1024),
    )(x2, g.reshape(1, d), shift.reshape(m, 1, d), scale.reshape(m, 1, d), wq_h, sk)


def _peer_expert_kernel(idx0_ref, idxn_ref, h_ref, gate_ref, x_ref, og_ref, uv_hbm, o_ref, buf, sem):
    i = pl.program_id(0)
    n = pl.num_programs(0)
    tb = PEER_EXPERT_TOKENS
    d = h_ref.shape[1]
    slot = i % 2

    def issue_token(idx_ref, j, to_slot):
        for r in range(PEER_PAIRS):
            e = idx_ref[0, 0, j * PEER_PAIRS + r]
            pltpu.make_async_copy(uv_hbm.at[pl.ds(e, 1)], buf.at[to_slot, j, pl.ds(r, 1)], sem.at[to_slot]).start()

    def wait_block(the_slot):
        for j in range(tb):
            pltpu.make_async_copy(uv_hbm.at[pl.ds(0, PEER_PAIRS)], buf.at[the_slot, j], sem.at[the_slot]).wait()

    def issue_block(idx_ref, to_slot):
        def body(j, carry):
            issue_token(idx_ref, j, to_slot)
            return carry
        lax.fori_loop(0, tb, body, 0)

    @pl.when(i == 0)
    def _():
        issue_block(idx0_ref, 0)

    @pl.when(i + 1 < n)
    def _():
        issue_block(idxn_ref, 1 - slot)

    wait_block(slot)

    lane = lax.broadcasted_iota(jnp.int32, (PEER_TOPK, PEER_ROUTE_TOKENS), 1)
    lane0 = (i % (PEER_ROUTE_TOKENS // tb)) * tb

    def token_body(j, carry):
        hb = jnp.broadcast_to(h_ref[pl.ds(j, 1), :], (PEER_TOPK, d))
        acc = jnp.zeros((PEER_TOPK, d), jnp.float32)
        for hd in range(PEER_HEADS):
            u = buf[slot, j, pl.ds(hd * PEER_TOPK, PEER_TOPK), pl.ds(0, d)]
            v = buf[slot, j, pl.ds(hd * PEER_TOPK, PEER_TOPK), pl.ds(d, d)]
            s = jnp.sum(u * hb, axis=1, keepdims=True)
            act = 0.5 * s * (1.0 + lax.erf(s * INV_SQRT2))
            g = jnp.sum(jnp.where(lane == lane0 + j, gate_ref[hd], 0.0), axis=1, keepdims=True)
            acc = acc + (g * act) * v
        yrow = jnp.sum(acc, axis=0, keepdims=True)
        o_ref[pl.ds(j, 1), :] = x_ref[pl.ds(j, 1), :] + og_ref[...] * yrow
        return carry

    lax.fori_loop(0, tb, token_body, 0)


def peer_expert(h, idx, gate, x2, out_gate, rows_per_mod, uv):
    n, d = h.shape
    tb = PEER_EXPERT_TOKENS
    nb = n // tb
    m = out_gate.shape[0]
    rows = tb * PEER_PAIRS
    idx_rows = idx.reshape(PEER_PAIRS, n).T.reshape(nb, 1, rows)
    gate_blocks = PEER_ROUTE_TOKENS // tb
    return pl.pallas_call(
        _peer_expert_kernel,
        grid=(nb,),
        in_specs=[
            pl.BlockSpec((1, 1, rows), lambda i: (0, 0, 0), memory_space=pltpu.SMEM),
            pl.BlockSpec((1, 1, rows), lambda i: (jnp.minimum(i + 1, nb - 1), 0, 0), memory_space=pltpu.SMEM),
            pl.BlockSpec((tb, d), lambda i: (i, 0)),
            pl.BlockSpec((PEER_HEADS, PEER_TOPK, PEER_ROUTE_TOKENS), lambda i: (0, 0, i // gate_blocks)),
            pl.BlockSpec((tb, d), lambda i: (i, 0)),
            pl.BlockSpec((None, 1, d), lambda i: (i // (rows_per_mod // tb), 0, 0)),
            pl.BlockSpec(memory_space=pl.ANY),
        ],
        out_specs=pl.BlockSpec((tb, d), lambda i: (i, 0)),
        out_shape=jax.ShapeDtypeStruct((n, d), jnp.float32),
        scratch_shapes=[pltpu.VMEM((2, tb, PEER_PAIRS, 2 * d), jnp.float32), pltpu.SemaphoreType.DMA((2,))],
        compiler_params=pltpu.CompilerParams(dimension_semantics=("arbitrary",),
                                             vmem_limit_bytes=48 * 1024 * 1024,
                                             disable_bounds_checks=True),
    )(idx_rows, idx_rows, h, gate, x2, out_gate.reshape(m, 1, d), uv)


def peer_residual(x2, g, shift, scale, out_gate, rows_per_mod, w_q, sub_keys, uv):
    h, idx, gate = peer_route(x2, g, shift, scale, rows_per_mod, w_q, sub_keys)
    return peer_expert(h, idx, gate, x2, out_gate, rows_per_mod, uv)


def _final_norm_kernel(x_ref, g_ref, o_ref):
    x = x_ref[...]
    y = x * lax.rsqrt(jnp.mean(x * x, axis=-1, keepdims=True) + EPS)
    o_ref[...] = y * g_ref[...]


def final_norm_pallas(x, g):
    b, l, d = x.shape
    xf = x.reshape(b * l, d)
    tm = 512
    out = pl.pallas_call(
        _final_norm_kernel,
        grid=(b * l // tm,),
        in_specs=[pl.BlockSpec((tm, d), lambda i: (i, 0)), pl.BlockSpec((1, d), lambda i: (0, 0))],
        out_specs=pl.BlockSpec((tm, d), lambda i: (i, 0)),
        out_shape=jax.ShapeDtypeStruct((b * l, d), x.dtype),
    )(xf, g.reshape(1, d))
    return out.reshape(b, l, d)


def kernel(x, c, ctx, c_ctx, w_ada, b_ada, norm_attn_g, norm_ffn_g, w_in, na_rpb, ret_log_gamma,
           gla_w_gate_up, gla_b_gate, gla_norm_g, swa_sink, w_out, peer_w_q, peer_sub_keys,
           peer_u, peer_v, final_g):
    s_len = x.shape[1]
    rope_hd = rope_2d_tables(s_len, HEAD_DIM)
    rope_ret = rope_2d_tables(s_len, RET_DK)
    z = ctx
    for layer in range(DEPTH):
        has_next = layer < DEPTH - 1
        mx = jnp.split((jax.nn.silu(c) @ w_ada[layer] + b_ada[layer])[:, None, :], 6, axis=-1)
        mz = jnp.split(jax.nn.silu(c_ctx) @ w_ada[layer] + b_ada[layer], 6, axis=-1)
        hx = modulate(x, norm_attn_g[layer], mx[0], mx[1])
        hz = modulate(z, norm_attn_g[layer], mz[0], mz[1])
        ox, oz = token_mixers(hx, hz, w_in[layer], w_out[layer], na_rpb[layer], ret_log_gamma[layer],
                              gla_w_gate_up[layer], gla_b_gate[layer], gla_norm_g[layer], swa_sink[layer],
                              rope_hd, rope_ret, has_next)
        x = x + mx[2] * ox
        uv = jnp.concatenate([peer_u[layer], peer_v[layer]], axis=1)
        bsz, slen, d = x.shape
        x = peer_residual(x.reshape(bsz * slen, d), norm_ffn_g[layer], mx[3][:, 0], mx[4][:, 0], mx[5][:, 0],
                          slen, peer_w_q[layer], peer_sub_keys[layer], uv).reshape(bsz, slen, d)
        if has_next:
            z = z + mz[2] * oz
            zlen = z.shape[1]
            z = peer_residual(z.reshape(bsz * zlen, d), norm_ffn_g[layer], mz[3][None], mz[4][None], mz[5][None],
                              bsz * zlen, peer_w_q[layer], peer_sub_keys[layer], uv).reshape(bsz, zlen, d)
    return final_norm_pallas(x, final_g)
```

```python
import functools

import jax
import jax.numpy as jnp
import numpy as np
from jax import lax
from jax.experimental import pallas as pl
from jax.experimental.pallas import tpu as pltpu

D_MODEL = 2048
DEPTH = 2
GRID_W = 64
EPS = 1e-6
ROPE_BASE = 10000.0

GROUP_WIDTH = D_MODEL // 4
HEAD_DIM = 64
NA_HEADS = GROUP_WIDTH // HEAD_DIM
NA_ROWS = 8
NA_COLS = 16
RET_HEADS = 4
RET_DK = GROUP_WIDTH // RET_HEADS
RET_DV = GROUP_WIDTH // RET_HEADS
GLA_HEADS = 4
GLA_DV = GROUP_WIDTH // GLA_HEADS
GLA_DK = GLA_DV // 2
GLA_RANK = 16
GLA_TAU = 16.0
SWA_HEADS = GROUP_WIDTH // HEAD_DIM
SWA_KV_HEADS = SWA_HEADS // 4
SWA_WINDOW = 128
SWA_BLOCK = 128
SCAN_CHUNK = 64
PEER_HEADS = 8
PEER_N_KEYS = 128
PEER_N_EXPERTS = PEER_N_KEYS * PEER_N_KEYS
PEER_DK = 256
PEER_TOPK = 16

LANES = 128
VMEM_LIMIT = 48 * 1024 * 1024
BF16 = jnp.bfloat16
F32 = jnp.float32
NEG_INF = float("-inf")

COL_NA_Q, COL_NA_K, COL_NA_V = 0, 512, 1024
COL_RET_Q, COL_RET_K, COL_RET_V, COL_RET_G = 1536, 2048, 2560, 3072
COL_GLA_Q, COL_GLA_K, COL_GLA_V, COL_GLA_G = 3584, 3840, 4096, 4608
COL_SWA_Q, COL_SWA_K, COL_SWA_V = 5120, 5632, 5760
COL_GLA_D = 5888
REF_COL_GLA_D, REF_COL_SWA_Q, REF_D_IN = 5120, 5152, 5920
PROJ_WIDTH = 6144


def _silu(x):
    return x / (1.0 + jnp.exp(-x))


def _dot_nt(a, b):
    return lax.dot_general(a, b, (((1,), (1,)), ((), ())), preferred_element_type=F32)


def _dot_tn(a, b):
    return lax.dot_general(a, b, (((0,), (0,)), ((), ())), preferred_element_type=F32)


def _params(n_axes):
    return pltpu.CompilerParams(dimension_semantics=("arbitrary",) * n_axes, vmem_limit_bytes=VMEM_LIMIT)


def _rope_lanes(x, cs, sn, quarter):
    n = x.shape[-1]
    lane = lax.broadcasted_iota(jnp.int32, x.shape, x.ndim - 1)
    first = (lane % (2 * quarter)) < quarter
    swapped = jnp.where(first, pltpu.roll(x, n - quarter, x.ndim - 1), pltpu.roll(x, quarter, x.ndim - 1))
    return x * cs + swapped * sn


def rope_lane_tables(length, dh, copies):
    t = jnp.arange(length)
    pos = jnp.stack([t // GRID_W, t % GRID_W], axis=-1).astype(F32)
    quarter = dh // 4
    inv = ROPE_BASE ** (-jnp.arange(quarter, dtype=F32) / quarter)
    ang = pos[:, :, None] * inv
    cos, sin = jnp.cos(ang), jnp.sin(ang)
    cl = jnp.concatenate([cos[:, 0], cos[:, 0], cos[:, 1], cos[:, 1]], axis=-1)
    sl = jnp.concatenate([-sin[:, 0], sin[:, 0], -sin[:, 1], sin[:, 1]], axis=-1)
    return jnp.tile(cl, (1, copies)), jnp.tile(sl, (1, copies))


def _adaln_kernel(c_ref, w_ref, b_ref, o_ref):
    a = _silu(c_ref[...]).astype(BF16)
    o_ref[...] = jnp.dot(a, w_ref[...].astype(BF16), preferred_element_type=F32) + b_ref[...]


def adaln(c_rows, w, b):
    r, d = c_rows.shape
    m = w.shape[1]
    tn = 1024
    return pl.pallas_call(
        _adaln_kernel,
        grid=(m // tn,),
        in_specs=[pl.BlockSpec((r, d), lambda j: (0, 0)), pl.BlockSpec((d, tn), lambda j: (0, j)),
                  pl.BlockSpec((1, tn), lambda j: (0, j))],
        out_specs=pl.BlockSpec((r, tn), lambda j: (0, j)),
        out_shape=jax.ShapeDtypeStruct((r, m), F32),
        compiler_params=_params(1),
    )(c_rows, w, b.reshape(1, m))


def _modproj_kernel(x_ref, g_ref, shift_ref, scale_ref, w_ref, o_ref, hb_ref):
    @pl.when(pl.program_id(1) == 0)
    def _():
        x = x_ref[...]
        y = x * lax.rsqrt(jnp.mean(x * x, axis=-1, keepdims=True) + EPS)
        hb_ref[...] = ((y * g_ref[...]) * (1.0 + scale_ref[...]) + shift_ref[...]).astype(BF16)

    o_ref[...] = jnp.dot(hb_ref[...], w_ref[...], preferred_element_type=F32)


def modproj(x2, g, shift, scale, rows_per_mod, w):
    n, d = x2.shape
    wid = w.shape[1]
    tm = min(512, rows_per_mod)
    tn = 2048
    m = shift.shape[0]
    mod_map = lambda i, j: (i // (rows_per_mod // tm), 0, 0)
    return pl.pallas_call(
        _modproj_kernel,
        grid=(n // tm, wid // tn),
        in_specs=[pl.BlockSpec((tm, d), lambda i, j: (i, 0)), pl.BlockSpec((1, d), lambda i, j: (0, 0)),
                  pl.BlockSpec((None, 1, d), mod_map), pl.BlockSpec((None, 1, d), mod_map),
                  pl.BlockSpec((d, tn), lambda i, j: (0, j))],
        out_specs=pl.BlockSpec((tm, tn), lambda i, j: (i, j)),
        out_shape=jax.ShapeDtypeStruct((n, wid), F32),
        scratch_shapes=[pltpu.VMEM((tm, d), BF16)],
        compiler_params=_params(2),
    )(x2, g.reshape(1, d), shift.reshape(m, 1, d), scale.reshape(m, 1, d), w)


NA_QROWS = 4


def _na_kernel(q_ref, k_ref, v_ref, kz_ref, vz_ref, bias_ref, o_ref, *, rows):
    step = pl.program_id(2)
    dh = HEAD_DIM
    band = NA_ROWS * GRID_W
    kz = kz_ref[...].astype(BF16)
    vz = vz_ref[...].astype(BF16)
    for qr in range(NA_QROWS):
        r = step * NA_QROWS + qr
        start = jnp.clip(r - NA_ROWS // 2, 0, rows - NA_ROWS)
        dr0 = start - r + NA_ROWS - 1
        tok0 = pl.multiple_of(start * GRID_W, GRID_W)
        kb = k_ref[pl.ds(tok0, band), :].astype(BF16)
        vb = v_ref[pl.ds(tok0, band), :].astype(BF16)
        q = (q_ref[pl.ds(qr * GRID_W, GRID_W), :] * (dh ** -0.5)).astype(BF16)
        outs = []
        for hh in range(LANES // dh):
            sl = slice(hh * dh, (hh + 1) * dh)
            s_nb = _dot_nt(q[:, sl], kb[:, sl]) + bias_ref[hh, dr0]
            s_cx = _dot_nt(q[:, sl], kz[:, sl])
            m = jnp.maximum(jnp.max(s_nb, axis=1, keepdims=True), jnp.max(s_cx, axis=1, keepdims=True))
            p_nb = jnp.exp(s_nb - m)
            p_cx = jnp.exp(s_cx - m)
            den = jnp.sum(p_nb, axis=1, keepdims=True) + jnp.sum(p_cx, axis=1, keepdims=True)
            o = (jnp.dot(p_nb.astype(BF16), vb[:, sl], preferred_element_type=F32)
                 + jnp.dot(p_cx.astype(BF16), vz[:, sl], preferred_element_type=F32))
            outs.append(o / den)
        o_ref[pl.ds(qr * GRID_W, GRID_W), :] = jnp.concatenate(outs, axis=1)


def na_band_bias(rpb):
    col = jnp.arange(GRID_W)
    col_start = jnp.clip(col - NA_COLS // 2, 0, GRID_W - NA_COLS)
    col_ok = (col[None, :] >= col_start[:, None]) & (col[None, :] < col_start[:, None] + NA_COLS)
    d_col = jnp.clip(col[None, :] - col[:, None], -(NA_COLS - 1), NA_COLS - 1) + NA_COLS - 1
    d_row = jnp.arange(NA_ROWS)[:, None] + jnp.arange(NA_ROWS)[None, :]
    b = rpb.astype(F32)[:, d_row][..., d_col]
    b = jnp.where(col_ok[None, None, None], b, NEG_INF)
    return b.transpose(0, 1, 3, 2, 4).reshape(rpb.shape[0], NA_ROWS, GRID_W, NA_ROWS * GRID_W)


def na_attention(px, pz, n_batch, seq, ctx_len, bias):
    rows = seq // GRID_W
    tq = NA_QROWS * GRID_W
    nsteps = rows // NA_QROWS
    heads_per_blk = LANES // HEAD_DIM
    return pl.pallas_call(
        functools.partial(_na_kernel, rows=rows),
        grid=(n_batch, NA_HEADS // heads_per_blk, nsteps),
        in_specs=[
            pl.BlockSpec((tq, LANES), lambda b, hp, s: (b * nsteps + s, COL_NA_Q // LANES + hp)),
            pl.BlockSpec((seq, LANES), lambda b, hp, s: (b, COL_NA_K // LANES + hp)),
            pl.BlockSpec((seq, LANES), lambda b, hp, s: (b, COL_NA_V // LANES + hp)),
            pl.BlockSpec((ctx_len, LANES), lambda b, hp, s: (b, COL_NA_K // LANES + hp)),
            pl.BlockSpec((ctx_len, LANES), lambda b, hp, s: (b, COL_NA_V // LANES + hp)),
            pl.BlockSpec((heads_per_blk, NA_ROWS, GRID_W, NA_ROWS * GRID_W), lambda b, hp, s: (hp, 0, 0, 0)),
        ],
        out_specs=pl.BlockSpec((tq, LANES), lambda b, hp, s: (b * nsteps + s, hp)),
        out_shape=jax.ShapeDtypeStruct((n_batch * seq, GROUP_WIDTH), F32),
        compiler_params=_params(3),
    )(px, px, px, pz, pz, bias)


def _swa_kernel(q_ref, kp_ref, kc_ref, kn_ref, vp_ref, vc_ref, vn_ref, kz_ref, vz_ref, sink_ref,
                cq_ref, sq_ref, ckp_ref, skp_ref, ckc_ref, skc_ref, ckn_ref, skn_ref, o_ref):
    n = pl.program_id(1)
    nb = pl.num_programs(1)
    dh = HEAD_DIM
    blk = SWA_BLOCK
    quarter = dh // 4
    group = SWA_HEADS // SWA_KV_HEADS
    q = _rope_lanes(q_ref[...], cq_ref[...], sq_ref[...], quarter) * (dh ** -0.5)
    kp = _rope_lanes(kp_ref[...], ckp_ref[...], skp_ref[...], quarter).astype(BF16)
    kc = _rope_lanes(kc_ref[...], ckc_ref[...], skc_ref[...], quarter).astype(BF16)
    kn = _rope_lanes(kn_ref[...], ckn_ref[...], skn_ref[...], quarter).astype(BF16)
    kz = kz_ref[...].astype(BF16)
    vp, vc, vn, vz = (r[...].astype(BF16) for r in (vp_ref, vc_ref, vn_ref, vz_ref))
    qi = lax.broadcasted_iota(jnp.int32, (group * blk, blk), 0) % blk
    kj = lax.broadcasted_iota(jnp.int32, (group * blk, blk), 1)
    ok_p = (kj >= qi) & (n > 0)
    ok_n = (kj <= qi) & (n < nb - 1)
    outs = []
    for hk in range(SWA_KV_HEADS):
        ks = slice(hk * dh, (hk + 1) * dh)
        qs = jnp.concatenate([q[:, (hk * group + g) * dh:(hk * group + g + 1) * dh] for g in range(group)],
                             axis=0).astype(BF16)
        sink = jnp.concatenate([jnp.full((blk, 1), 1.0, F32) * sink_ref[hk * group + g] for g in range(group)],
                               axis=0)
        s_p = jnp.where(ok_p, _dot_nt(qs, kp[:, ks]), NEG_INF)
        s_c = _dot_nt(qs, kc[:, ks])
        s_n = jnp.where(ok_n, _dot_nt(qs, kn[:, ks]), NEG_INF)
        s_z = _dot_nt(qs, kz[:, ks])
        m = jnp.maximum(jnp.maximum(jnp.max(s_p, axis=1, keepdims=True), jnp.max(s_c, axis=1, keepdims=True)),
                        jnp.maximum(jnp.max(s_n, axis=1, keepdims=True), jnp.max(s_z, axis=1, keepdims=True)))
        m = jnp.maximum(m, sink)
        e_p, e_c, e_n, e_z = (jnp.exp(s - m) for s in (s_p, s_c, s_n, s_z))
        den = (jnp.sum(e_p, axis=1, keepdims=True) + jnp.sum(e_c, axis=1, keepdims=True)
               + jnp.sum(e_n, axis=1, keepdims=True) + jnp.sum(e_z, axis=1, keepdims=True) + jnp.exp(sink - m))
        o = (jnp.dot(e_p.astype(BF16), vp[:, ks], preferred_element_type=F32)
             + jnp.dot(e_c.astype(BF16), vc[:, ks], preferred_element_type=F32)
             + jnp.dot(e_n.astype(BF16), vn[:, ks], preferred_element_type=F32)
             + jnp.dot(e_z.astype(BF16), vz[:, ks], preferred_element_type=F32)) / den
        outs += [o[g * blk:(g + 1) * blk, :] for g in range(group)]
    o_ref[...] = jnp.concatenate(outs, axis=1)


def swa_attention(px, pz, n_batch, seq, ctx_len, sink, rope_q, rope_k):
    blk = SWA_BLOCK
    nb = seq // blk
    prev = lambda b, n: b * nb + jnp.maximum(n - 1, 0)
    cur = lambda b, n: b * nb + n
    nxt = lambda b, n: b * nb + jnp.minimum(n + 1, nb - 1)
    kv = lambda off, f: pl.BlockSpec((blk, LANES), lambda b, n: (f(b, n), off // LANES))
    tab = lambda w, f: pl.BlockSpec((blk, w), lambda b, n: (f(0, n), 0))
    cq, sq = rope_q
    ck, sk = rope_k
    return pl.pallas_call(
        _swa_kernel,
        grid=(n_batch, nb),
        in_specs=[
            pl.BlockSpec((blk, GROUP_WIDTH), lambda b, n: (cur(b, n), COL_SWA_Q // GROUP_WIDTH)),
            kv(COL_SWA_K, prev), kv(COL_SWA_K, cur), kv(COL_SWA_K, nxt),
            kv(COL_SWA_V, prev), kv(COL_SWA_V, cur), kv(COL_SWA_V, nxt),
            pl.BlockSpec((ctx_len, LANES), lambda b, n: (b, COL_SWA_K // LANES)),
            pl.BlockSpec((ctx_len, LANES), lambda b, n: (b, COL_SWA_V // LANES)),
            pl.BlockSpec(memory_space=pltpu.SMEM),
            tab(GROUP_WIDTH, cur), tab(GROUP_WIDTH, cur), tab(LANES, prev), tab(LANES, prev),
            tab(LANES, cur), tab(LANES, cur), tab(LANES, nxt), tab(LANES, nxt),
        ],
        out_specs=pl.BlockSpec((blk, GROUP_WIDTH), lambda b, n: (cur(b, n), 0)),
        out_shape=jax.ShapeDtypeStruct((n_batch * seq, GROUP_WIDTH), F32),
        compiler_params=_params(2),
    )(px, px, px, px, px, px, px, pz, pz, sink, cq, sq, ck, sk, ck, sk, ck, sk)


def _ctx_attn_kernel(q_ref, k_ref, v_ref, sink_ref, o_ref, *, group, use_sink):
    dh = HEAD_DIM
    q = (q_ref[...] * (dh ** -0.5)).astype(BF16)
    k = k_ref[...].astype(BF16)
    v = v_ref[...].astype(BF16)
    outs = []
    for qh in range(q.shape[1] // dh):
        ks = slice((qh // group) * dh, (qh // group + 1) * dh)
        s = _dot_nt(q[:, qh * dh:(qh + 1) * dh], k[:, ks])
        m = jnp.max(s, axis=1, keepdims=True)
        if use_sink:
            m = jnp.maximum(m, sink_ref[qh])
        e = jnp.exp(s - m)
        den = jnp.sum(e, axis=1, keepdims=True)
        if use_sink:
            den = den + jnp.exp(sink_ref[qh] - m)
        outs.append(jnp.dot(e.astype(BF16), v[:, ks], preferred_element_type=F32) / den)
    o_ref[...] = jnp.concatenate(outs, axis=1)


def ctx_attention(pz, n_batch, ctx_len, cols, kv_width, sink):
    qc, kc, vc = cols
    use_sink = sink is not None
    if sink is None:
        sink = jnp.zeros((GROUP_WIDTH // HEAD_DIM,), F32)
    return pl.pallas_call(
        functools.partial(_ctx_attn_kernel, group=GROUP_WIDTH // kv_width, use_sink=use_sink),
        grid=(n_batch,),
        in_specs=[pl.BlockSpec((ctx_len, GROUP_WIDTH), lambda b: (b, qc // GROUP_WIDTH)),
                  pl.BlockSpec((ctx_len, kv_width), lambda b: (b, kc // kv_width)),
                  pl.BlockSpec((ctx_len, kv_width), lambda b: (b, vc // kv_width)),
                  pl.BlockSpec(memory_space=pltpu.SMEM)],
        out_specs=pl.BlockSpec((ctx_len, GROUP_WIDTH), lambda b: (b, 0)),
        out_shape=jax.ShapeDtypeStruct((n_batch * ctx_len, GROUP_WIDTH), F32),
        compiler_params=_params(1),
    )(pz, pz, pz, sink)


SCAN_ROWS = 256


def _split3(x):
    a = x.astype(BF16)
    r = x - a.astype(F32)
    b = r.astype(BF16)
    c = (r - b.astype(F32)).astype(BF16)
    return a, b, c


def _scan_kernel(*refs, heads, dk, dv, kind, rope, q_scale, k_scale):
    it = iter(refs)
    qf, kf, vf, qb, kb, vb = (next(it) for _ in range(6))
    if kind == "ret":
        lg = next(it)
    else:
        df, db, wup, bup = (next(it) for _ in range(4))
    if rope:
        cosf, sinf, cosb, sinb = (next(it) for _ in range(4))
    s0 = next(it)
    of, ob, sfin = next(it), next(it), next(it)
    st = next(it)

    s = pl.program_id(1)
    c = SCAN_CHUNK
    nch = SCAN_ROWS // c
    hk = heads * dk

    @pl.when(s == 0)
    def _():
        st[...] = s0[...]

    r_i = lax.broadcasted_iota(jnp.int32, (c, c), 0)
    c_i = lax.broadcasted_iota(jnp.int32, (c, c), 1)
    masks = (r_i >= c_i, c_i > r_i)

    if kind == "ret":
        pos = lax.broadcasted_iota(jnp.int32, (c, hk), 0).astype(F32)
        gcums = ((pos + 1.0) * lg[0:1, :], (float(c) - pos) * lg[1:2, :])
    else:
        rr = lax.broadcasted_iota(jnp.int32, (SCAN_ROWS, SCAN_ROWS), 0)
        cc = lax.broadcasted_iota(jnp.int32, (SCAN_ROWS, SCAN_ROWS), 1)
        same = (rr // c) == (cc // c)
        tris = (jnp.where(same & (rr >= cc), 1.0, 0.0).astype(BF16),
                jnp.where(same & (cc >= rr), 1.0, 0.0).astype(BF16))

        def gate_cum(d_ref, direction):
            pre = jnp.dot(d_ref[...].astype(BF16), wup[direction], preferred_element_type=F32) + bup[direction]
            g = -(jnp.maximum(-pre, 0.0) + jnp.log1p(jnp.exp(-jnp.abs(pre)))) / GLA_TAU
            return sum(jnp.dot(tris[direction], p, preferred_element_type=F32) for p in _split3(g))

        gcums = (gate_cum(df, 0), gate_cum(db, 1))

    def one(direction, q_ref, k_ref, v_ref, o_ref, cos_ref, sin_ref, ch):
        rows = pl.ds(ch * c, c)
        q = q_ref[rows, :]
        k = k_ref[rows, :]
        v = v_ref[rows, :]
        if rope:
            cs = jnp.concatenate([cos_ref[rows, :]] * heads, axis=1)
            sn = jnp.concatenate([sin_ref[rows, :]] * heads, axis=1)
            q = _rope_lanes(q, cs, sn, dk // 4)
            k = _rope_lanes(k, cs, sn, dk // 4)
        if q_scale != 1.0:
            q = q * q_scale
        if k_scale != 1.0:
            k = k * k_scale
        gcum = gcums[direction] if kind == "ret" else gcums[direction][ch * c:(ch + 1) * c, :]
        gtot = gcum[c - 1:c, :] if direction == 0 else gcum[0:1, :]
        q_rel = (q * jnp.exp(gcum - gtot)).astype(BF16)
        k_rel = (k * jnp.exp(gtot - gcum)).astype(BF16)
        q_dec = (q * jnp.exp(gcum)).astype(BF16)
        dec = jnp.exp(gtot)
        vb16 = v.astype(BF16)
        outs = []
        for hd in range(heads):
            ks = slice(hd * dk, (hd + 1) * dk)
            vs = slice(hd * dv, (hd + 1) * dv)
            a = jnp.where(masks[direction], _dot_nt(q_rel[:, ks], k_rel[:, ks]), 0.0)
            state = st[direction, hd]
            o = jnp.dot(a.astype(BF16), vb16[:, vs], preferred_element_type=F32)
            o = o + _dot_nt(q_dec[:, ks], state.astype(BF16))
            st[direction, hd] = dec[:, ks] * state + _dot_tn(vb16[:, vs], k_rel[:, ks])
            outs.append(o)
        o_ref[rows, :] = jnp.concatenate(outs, axis=1)

    for ch in range(nch):
        one(0, qf, kf, vf, of, cosf if rope else None, sinf if rope else None, ch)
        one(1, qb, kb, vb, ob, cosb if rope else None, sinb if rope else None, nch - 1 - ch)

    @pl.when(s == pl.num_programs(1) - 1)
    def _():
        sfin[...] = st[...]


def bidir_scan(p, n_batch, seq, cols, heads, dk, dv, kind, s0, *, lg=None, wup=None, bup=None,
               rope=None, q_scale=1.0, k_scale=1.0):
    t = SCAN_ROWS
    nblk = seq // t
    hk, hv = heads * dk, heads * dv
    qc, kc, vc = cols
    fwd = lambda w, off: pl.BlockSpec((t, w), lambda b, s: (b * nblk + s, off // w))
    bwd = lambda w, off: pl.BlockSpec((t, w), lambda b, s: (b * nblk + nblk - 1 - s, off // w))
    const = lambda shape: pl.BlockSpec(shape, lambda b, s: (0,) * len(shape))
    args = [p] * 6
    specs = [fwd(hk, qc), fwd(hk, kc), fwd(hv, vc), bwd(hk, qc), bwd(hk, kc), bwd(hv, vc)]
    if kind == "ret":
        args += [lg]
        specs += [const((2, hk))]
    else:
        args += [p, p, wup, bup]
        specs += [fwd(LANES, COL_GLA_D), bwd(LANES, COL_GLA_D), const(wup.shape), const(bup.shape)]
    if rope is not None:
        cos, sin = rope
        args += [cos, sin, cos, sin]
        specs += [pl.BlockSpec((t, dk), lambda b, s: (s, 0)), pl.BlockSpec((t, dk), lambda b, s: (s, 0)),
                  pl.BlockSpec((t, dk), lambda b, s: (nblk - 1 - s, 0)),
                  pl.BlockSpec((t, dk), lambda b, s: (nblk - 1 - s, 0))]
    args += [s0]
    state_spec = pl.BlockSpec((None, 2, heads, dv, dk), lambda b, s: (b, 0, 0, 0, 0))
    specs += [state_spec]
    n = n_batch * seq
    kern = functools.partial(_scan_kernel, heads=heads, dk=dk, dv=dv, kind=kind, rope=rope is not None,
                             q_scale=q_scale, k_scale=k_scale)
    return pl.pallas_call(
        kern,
        grid=(n_batch, nblk),
        in_specs=specs,
        out_specs=[pl.BlockSpec((t, hv), lambda b, s: (b * nblk + s, 0)),
                   pl.BlockSpec((t, hv), lambda b, s: (b * nblk + nblk - 1 - s, 0)),
                   state_spec],
        out_shape=[jax.ShapeDtypeStruct((n, hv), F32), jax.ShapeDtypeStruct((n, hv), F32),
                   jax.ShapeDtypeStruct((n_batch, 2, heads, dv, dk), F32)],
        scratch_shapes=[pltpu.VMEM((2, heads, dv, dk), F32)],
        compiler_params=_params(2),
    )(*args)


def _outproj_kernel(na_ref, rf_ref, rb_ref, rg_ref, gf_ref, gb_ref, gg_ref, sw_ref, gn_ref, w_ref, x_ref, mg_ref,
                    o_ref, *, head_w):
    ry = rf_ref[...] + rb_ref[...]
    gy = gf_ref[...] + gb_ref[...]
    r_out, g_out = [], []
    for hd in range(ry.shape[1] // head_w):
        sl = slice(hd * head_w, (hd + 1) * head_w)
        r = ry[:, sl]
        mu = jnp.mean(r, axis=-1, keepdims=True)
        var = jnp.mean(jnp.square(r - mu), axis=-1, keepdims=True)
        r_out.append((r - mu) * lax.rsqrt(var + EPS))
        gq = gy[:, sl]
        g_out.append(gq * lax.rsqrt(jnp.mean(gq * gq, axis=-1, keepdims=True) + EPS) * gn_ref[...])
    ret = jnp.concatenate(r_out, axis=1) * _silu(rg_ref[...])
    gla = jnp.concatenate(g_out, axis=1) * _silu(gg_ref[...])
    mix = jnp.concatenate([na_ref[...], ret, gla, sw_ref[...]], axis=1).astype(BF16)
    o_ref[...] = x_ref[...] + mg_ref[...] * jnp.dot(mix, w_ref[...], preferred_element_type=F32)


def outproj(na, rf, rb, gf, gb, sw, p, gla_norm_g, w_out, x2, mg, rows_per_mod):
    n, d = x2.shape
    gw = GROUP_WIDTH
    tm = 256
    m = mg.shape[0]
    row = lambda w_: pl.BlockSpec((tm, w_), lambda i: (i, 0))
    return pl.pallas_call(
        functools.partial(_outproj_kernel, head_w=RET_DV),
        grid=(n // tm,),
        in_specs=[row(gw), row(gw), row(gw), pl.BlockSpec((tm, gw), lambda i: (i, COL_RET_G // gw)),
                  row(gw), row(gw), pl.BlockSpec((tm, gw), lambda i: (i, COL_GLA_G // gw)), row(gw),
                  pl.BlockSpec((1, GLA_DV), lambda i: (0, 0)),
                  pl.BlockSpec((d, d), lambda i: (0, 0)), row(d),
                  pl.BlockSpec((None, 1, d), lambda i: (i // (rows_per_mod // tm), 0, 0))],
        out_specs=row(d),
        out_shape=jax.ShapeDtypeStruct((n, d), F32),
        compiler_params=_params(1),
    )(na, rf, rb, p, gf, gb, p, sw, gla_norm_g.reshape(1, GLA_DV), w_out, x2, mg.reshape(m, 1, d))


PEER_PAIRS = PEER_HEADS * PEER_TOPK
PEER_ROUTE_TOKENS = 128
PEER_EXPERT_TOKENS = 8
INV_SQRT2 = 0.7071067811865476


def _topk_cols(s, payload=None):
    n_rows = s.shape[0]
    row = lax.broadcasted_iota(jnp.int32, s.shape, 0)
    vals, idxs = [], []
    for _ in range(PEER_TOPK):
        m = jnp.max(s, axis=0, keepdims=True)
        am = jnp.min(jnp.where(s == m, row, n_rows), axis=0, keepdims=True)
        sel = row == am
        vals.append(m)
        idxs.append(am if payload is None else jnp.max(jnp.where(sel, payload, -1), axis=0, keepdims=True))
        s = jnp.where(sel, -jnp.inf, s)
    return jnp.concatenate(vals, axis=0), jnp.concatenate(idxs, axis=0)


def _peer_route_kernel(x_ref, g_ref, shift_ref, scale_ref, wq_ref, sk_ref, h_ref, idx_ref, gate_ref):
    x = x_ref[...]
    y = x * lax.rsqrt(jnp.mean(x * x, axis=-1, keepdims=True) + EPS)
    h = (y * g_ref[...]) * (1.0 + scale_ref[...]) + shift_ref[...]
    h_ref[...] = h
    hb = h.astype(BF16)
    half = PEER_DK // 2

    def head_body(hd, carry):
        q = jnp.dot(hb, wq_ref[hd], preferred_element_type=F32)
        tops = []
        for p in range(2):
            qp = q[:, p * half:(p + 1) * half].astype(BF16)
            tops.append(_topk_cols(_dot_nt(sk_ref[p, hd], qp)))
        (v0, i0), (v1, i1) = tops
        cand_s = jnp.concatenate([v0[a:a + 1, :] + v1 for a in range(PEER_TOPK)], axis=0)
        cand_i = jnp.concatenate([i0[a:a + 1, :] * PEER_N_KEYS + i1 for a in range(PEER_TOPK)], axis=0)
        best_s, best_e = _topk_cols(cand_s, cand_i)
        e = jnp.exp(best_s - best_s[0:1, :])
        gate_ref[hd] = e / jnp.sum(e, axis=0, keepdims=True)
        idx_ref[hd] = best_e
        return carry

    lax.fori_loop(0, PEER_HEADS, head_body, 0)


def peer_route(x2, g, shift, scale, rows_per_mod, wq_h, sk):
    n, d = x2.shape
    t = PEER_ROUTE_TOKENS
    m = shift.shape[0]
    mod_map = lambda i: (i // (rows_per_mod // t), 0, 0)
    return pl.pallas_call(
        _peer_route_kernel,
        grid=(n // t,),
        in_specs=[
            pl.BlockSpec((t, d), lambda i: (i, 0)),
            pl.BlockSpec((1, d), lambda i: (0, 0)),
            pl.BlockSpec((None, 1, d), mod_map),
            pl.BlockSpec((None, 1, d), mod_map),
            pl.BlockSpec((PEER_HEADS, d, PEER_DK), lambda i: (0, 0, 0)),
            pl.BlockSpec((2, PEER_HEADS, PEER_N_KEYS, PEER_DK // 2), lambda i: (0, 0, 0, 0)),
        ],
        out_specs=[
            pl.BlockSpec((t, d), lambda i: (i, 0)),
            pl.BlockSpec((PEER_HEADS, PEER_TOPK, t), lambda i: (0, 0, i)),
            pl.BlockSpec((PEER_HEADS, PEER_TOPK, t), lambda i: (0, 0, i)),
        ],
        out_shape=[
            jax.ShapeDtypeStruct((n, d), F32),
            jax.ShapeDtypeStruct((PEER_HEADS, PEER_TOPK, n), jnp.int32),
            jax.ShapeDtypeStruct((PEER_HEADS, PEER_TOPK, n), F32),
        ],
        compiler_params=_params(1),
    )(x2, g.reshape(1, d), shift.reshape(m, 1, d), scale.reshape(m, 1, d), wq_h, sk)


def _peer_expert_kernel(idx0_ref, idxn_ref, h_ref, gate_ref, x_ref, og_ref, uv_hbm, o_ref, buf, sem):
    i = pl.program_id(0)
    n = pl.num_programs(0)
    tb = PEER_EXPERT_TOKENS
    d = h_ref.shape[1]
    slot = i % 2

    def issue_token(idx_ref, j, to_slot):
        for r in range(PEER_PAIRS):
            e = idx_ref[0, 0, j * PEER_PAIRS + r]
            pltpu.make_async_copy(uv_hbm.at[pl.ds(e, 1)], buf.at[to_slot, j, pl.ds(r, 1)], sem.at[to_slot]).start()

    def wait_block(the_slot):
        for j in range(tb):
            pltpu.make_async_copy(uv_hbm.at[pl.ds(0, PEER_PAIRS)], buf.at[the_slot, j], sem.at[the_slot]).wait()

    def issue_block(idx_ref, to_slot):
        def body(j, carry):
            issue_token(idx_ref, j, to_slot)
            return carry
        lax.fori_loop(0, tb, body, 0)

    @pl.when(i == 0)
    def _():
        issue_block(idx0_ref, 0)

    @pl.when(i + 1 < n)
    def _():
        issue_block(idxn_ref, 1 - slot)

    wait_block(slot)

    lane = lax.broadcasted_iota(jnp.int32, (PEER_TOPK, PEER_ROUTE_TOKENS), 1)
    lane0 = (i % (PEER_ROUTE_TOKENS // tb)) * tb

    def token_body(j, carry):
        hb = jnp.broadcast_to(h_ref[pl.ds(j, 1), :], (PEER_TOPK, d))
        acc = jnp.zeros((PEER_TOPK, d), F32)
        for hd in range(PEER_HEADS):
            u = buf[slot, j, pl.ds(hd * PEER_TOPK, PEER_TOPK), pl.ds(0, d)]
            v = buf[slot, j, pl.ds(hd * PEER_TOPK, PEER_TOPK), pl.ds(d, d)]
            s = jnp.sum(u * hb, axis=1, keepdims=True)
            act = 0.5 * s * (1.0 + lax.erf(s * INV_SQRT2))
            g = jnp.sum(jnp.where(lane == lane0 + j, gate_ref[hd], 0.0), axis=1, keepdims=True)
            acc = acc + (g * act) * v
        yrow = jnp.sum(acc, axis=0, keepdims=True)
        o_ref[pl.ds(j, 1), :] = x_ref[pl.ds(j, 1), :] + og_ref[...] * yrow
        return carry

    lax.fori_loop(0, tb, token_body, 0)


def peer_expert(h, idx, gate, x2, out_gate, rows_per_mod, uv):
    n, d = h.shape
    tb = PEER_EXPERT_TOKENS
    nb = n // tb
    m = out_gate.shape[0]
    rows = tb * PEER_PAIRS
    idx_rows = idx.reshape(PEER_PAIRS, n).T.reshape(nb, 1, rows)
    gate_blocks = PEER_ROUTE_TOKENS // tb
    return pl.pallas_call(
        _peer_expert_kernel,
        grid=(nb,),
        in_specs=[
            pl.BlockSpec((1, 1, rows), lambda i: (0, 0, 0), memory_space=pltpu.SMEM),
            pl.BlockSpec((1, 1, rows), lambda i: (jnp.minimum(i + 1, nb - 1), 0, 0), memory_space=pltpu.SMEM),
            pl.BlockSpec((tb, d), lambda i: (i, 0)),
            pl.BlockSpec((PEER_HEADS, PEER_TOPK, PEER_ROUTE_TOKENS), lambda i: (0, 0, i // gate_blocks)),
            pl.BlockSpec((tb, d), lambda i: (i, 0)),
            pl.BlockSpec((None, 1, d), lambda i: (i // (rows_per_mod // tb), 0, 0)),
            pl.BlockSpec(memory_space=pl.ANY),
        ],
        out_specs=pl.BlockSpec((tb, d), lambda i: (i, 0)),
        out_shape=jax.ShapeDtypeStruct((n, d), F32),
        scratch_shapes=[pltpu.VMEM((2, tb, PEER_PAIRS, 2 * d), F32), pltpu.SemaphoreType.DMA((2,))],
        compiler_params=pltpu.CompilerParams(dimension_semantics=("arbitrary",), vmem_limit_bytes=VMEM_LIMIT,
                                             disable_bounds_checks=True),
    )(idx_rows, idx_rows, h, gate, x2, out_gate.reshape(m, 1, d), uv)


def peer_residual(x2, g, shift, scale, out_gate, rows_per_mod, wq_h, sk, uv):
    h, idx, gate = peer_route(x2, g, shift, scale, rows_per_mod, wq_h, sk)
    return peer_expert(h, idx, gate, x2, out_gate, rows_per_mod, uv)


def _final_norm_kernel(x_ref, g_ref, o_ref):
    x = x_ref[...]
    y = x * lax.rsqrt(jnp.mean(x * x, axis=-1, keepdims=True) + EPS)
    o_ref[...] = y * g_ref[...]


def final_norm(x2, g):
    n, d = x2.shape
    tm = 512
    return pl.pallas_call(
        _final_norm_kernel,
        grid=(n // tm,),
        in_specs=[pl.BlockSpec((tm, d), lambda i: (i, 0)), pl.BlockSpec((1, d), lambda i: (0, 0))],
        out_specs=pl.BlockSpec((tm, d), lambda i: (i, 0)),
        out_shape=jax.ShapeDtypeStruct((n, d), x2.dtype),
        compiler_params=_params(1),
    )(x2, g.reshape(1, d))


def _mixers(p, pz, n_batch, seq, ctx_len, is_ctx, tables, prm):
    if is_ctx:
        na = ctx_attention(p, n_batch, seq, (COL_NA_Q, COL_NA_K, COL_NA_V), GROUP_WIDTH, None)
        sw = ctx_attention(p, n_batch, seq, (COL_SWA_Q, COL_SWA_K, COL_SWA_V), SWA_KV_HEADS * HEAD_DIM, prm["sink"])
    else:
        na = na_attention(p, pz, n_batch, seq, ctx_len, prm["na_bias"])
        sw = swa_attention(p, pz, n_batch, seq, ctx_len, prm["sink"], tables["swa_q"], tables["swa_k"])
    rf, rb, rs = bidir_scan(p, n_batch, seq, (COL_RET_Q, COL_RET_K, COL_RET_V), RET_HEADS, RET_DK, RET_DV, "ret",
                            prm["ret_s0"], lg=prm["ret_lg"], rope=None if is_ctx else tables["ret"],
                            k_scale=RET_DK ** -0.5)
    gf, gb, gs = bidir_scan(p, n_batch, seq, (COL_GLA_Q, COL_GLA_K, COL_GLA_V), GLA_HEADS, GLA_DK, GLA_DV, "gla",
                            prm["gla_s0"], wup=prm["gla_wup"], bup=prm["gla_bup"], q_scale=GLA_DK ** -0.5)
    return (na, rf, rb, gf, gb, sw), (rs, gs)


def kernel(x, c, ctx, c_ctx, w_ada, b_ada, norm_attn_g, norm_ffn_g, w_in, na_rpb, ret_log_gamma,
           gla_w_gate_up, gla_b_gate, gla_norm_g, swa_sink, w_out, peer_w_q, peer_sub_keys,
           peer_u, peer_v, final_g):
    bsz, slen, d = x.shape
    zlen = ctx.shape[1]
    x2 = x.reshape(bsz * slen, d)
    z2 = ctx.reshape(bsz * zlen, d)
    tables = {"ret": rope_lane_tables(slen, RET_DK, 1),
              "swa_q": rope_lane_tables(slen, HEAD_DIM, SWA_HEADS),
              "swa_k": rope_lane_tables(slen, HEAD_DIM, SWA_KV_HEADS)}
    c_rows = jnp.zeros((8, d), F32).at[:bsz].set(c).at[bsz].set(c_ctx)
    for layer in range(DEPTH):
        has_next = layer < DEPTH - 1
        mod = adaln(c_rows, w_ada[layer], b_ada[layer])
        mx = [mod[:bsz, k * d:(k + 1) * d] for k in range(6)]
        mz = [mod[bsz:bsz + 1, k * d:(k + 1) * d] for k in range(6)]

        wi = w_in[layer]
        wp = jnp.concatenate([wi[:, :REF_COL_GLA_D], wi[:, REF_COL_SWA_Q:], wi[:, REF_COL_GLA_D:REF_COL_SWA_Q],
                              jnp.zeros((d, PROJ_WIDTH - REF_D_IN), F32)], axis=1).astype(BF16)
        px = modproj(x2, norm_attn_g[layer], mx[0], mx[1], slen, wp)
        pz = modproj(z2, norm_attn_g[layer], mz[0], mz[1], bsz * zlen, wp)

        wup = (jnp.zeros((2, LANES, GLA_HEADS * GLA_DK), F32)
               .at[0, :GLA_RANK].set(gla_w_gate_up[layer, 0])
               .at[1, GLA_RANK:2 * GLA_RANK].set(gla_w_gate_up[layer, 1])).astype(BF16)
        prm = {"na_bias": na_band_bias(na_rpb[layer]), "sink": swa_sink[layer],
               "ret_lg": jnp.repeat(ret_log_gamma[layer], RET_DK, axis=1),
               "gla_wup": wup, "gla_bup": gla_b_gate[layer].reshape(2, 1, GLA_HEADS * GLA_DK),
               "ret_s0": jnp.zeros((bsz, 2, RET_HEADS, RET_DV, RET_DK), F32),
               "gla_s0": jnp.zeros((bsz, 2, GLA_HEADS, GLA_DV, GLA_DK), F32)}
        mix_z, (ret_s, gla_s) = _mixers(pz, pz, bsz, zlen, zlen, True, tables, prm)
        prm["ret_s0"], prm["gla_s0"] = ret_s, gla_s
        mix_x, _ = _mixers(px, pz, bsz, slen, zlen, False, tables, prm)

        wo = w_out[layer].astype(BF16)
        x2 = outproj(*mix_x, px, gla_norm_g[layer], wo, x2, mx[2], slen)

        uv = jnp.concatenate([peer_u[layer], peer_v[layer]], axis=1)
        wq_h = peer_w_q[layer].reshape(d, PEER_HEADS, PEER_DK).transpose(1, 0, 2).astype(BF16)
        sk = peer_sub_keys[layer].astype(BF16)
        x2 = peer_residual(x2, norm_ffn_g[layer], mx[3], mx[4], mx[5], slen, wq_h, sk, uv)
        if has_next:
            z2 = outproj(*mix_z, pz, gla_norm_g[layer], wo, z2, mz[2], bsz * zlen)
            z2 = peer_residual(z2, norm_ffn_g[layer], mz[3], mz[4], mz[5], bsz * zlen, wq_h, sk, uv)
    return final_norm(x2, final_g).reshape(bsz, slen, d)
```

```python
import functools

import jax
import jax.numpy as jnp
import numpy as np
from jax import lax
from jax.experimental import pallas as pl
from jax.experimental.pallas import tpu as pltpu

D_MODEL = 2048
DEPTH = 2
GRID_W = 64
EPS = 1e-6
ROPE_BASE = 10000.0

GROUP_WIDTH = D_MODEL // 4
HEAD_DIM = 64
NA_HEADS = GROUP_WIDTH // HEAD_DIM
NA_ROWS = 8
NA_COLS = 16
RET_HEADS = 4
RET_DK = GROUP_WIDTH // RET_HEADS
RET_DV = GROUP_WIDTH // RET_HEADS
GLA_HEADS = 4
GLA_DV = GROUP_WIDTH // GLA_HEADS
GLA_DK = GLA_DV // 2
GLA_RANK = 16
GLA_TAU = 16.0
SWA_HEADS = GROUP_WIDTH // HEAD_DIM
SWA_KV_HEADS = SWA_HEADS // 4
SWA_WINDOW = 128
SWA_BLOCK = 128
SCAN_CHUNK = 64
PEER_HEADS = 8
PEER_N_KEYS = 128
PEER_N_EXPERTS = PEER_N_KEYS * PEER_N_KEYS
PEER_DK = 256
PEER_TOPK = 16

LANES = 128
VMEM_LIMIT = 48 * 1024 * 1024
BF16 = jnp.bfloat16
F32 = jnp.float32
NEG_INF = float("-inf")

COL_NA_Q, COL_NA_K, COL_NA_V = 0, 512, 1024
COL_RET_Q, COL_RET_K, COL_RET_V, COL_RET_G = 1536, 2048, 2560, 3072
COL_GLA_Q, COL_GLA_K, COL_GLA_V, COL_GLA_G = 3584, 3840, 4096, 4608
COL_SWA_Q, COL_SWA_K, COL_SWA_V = 5120, 5632, 5760
COL_GLA_D = 5888
REF_COL_GLA_D, REF_COL_SWA_Q, REF_D_IN = 5120, 5152, 5920
PROJ_WIDTH = 6144


def _silu(x):
    return x / (1.0 + jnp.exp(-x))


def _dot_nt(a, b):
    return lax.dot_general(a, b, (((1,), (1,)), ((), ())), preferred_element_type=F32)


def _dot_tn(a, b):
    return lax.dot_general(a, b, (((0,), (0,)), ((), ())), preferred_element_type=F32)


def _params(n_axes):
    return pltpu.CompilerParams(dimension_semantics=("arbitrary",) * n_axes, vmem_limit_bytes=VMEM_LIMIT)


def _rope_lanes(x, cs, sn, quarter):
    n = x.shape[-1]
    lane = lax.broadcasted_iota(jnp.int32, x.shape, x.ndim - 1)
    first = (lane % (2 * quarter)) < quarter
    swapped = jnp.where(first, pltpu.roll(x, n - quarter, x.ndim - 1), pltpu.roll(x, quarter, x.ndim - 1))
    return x * cs + swapped * sn


def rope_lane_tables(length, dh, copies):
    t = jnp.arange(length)
    pos = jnp.stack([t // GRID_W, t % GRID_W], axis=-1).astype(F32)
    quarter = dh // 4
    inv = ROPE_BASE ** (-jnp.arange(quarter, dtype=F32) / quarter)
    ang = pos[:, :, None] * inv
    cos, sin = jnp.cos(ang), jnp.sin(ang)
    cl = jnp.concatenate([cos[:, 0], cos[:, 0], cos[:, 1], cos[:, 1]], axis=-1)
    sl = jnp.concatenate([-sin[:, 0], sin[:, 0], -sin[:, 1], sin[:, 1]], axis=-1)
    return jnp.tile(cl, (1, copies)), jnp.tile(sl, (1, copies))


def _adaln_kernel(c_ref, w_ref, b_ref, o_ref):
    a = _silu(c_ref[...]).astype(BF16)
    o_ref[...] = jnp.dot(a, w_ref[...].astype(BF16), preferred_element_type=F32) + b_ref[...]


def adaln(c_rows, w, b):
    r, d = c_rows.shape
    m = w.shape[1]
    tn = 1024
    return pl.pallas_call(
        _adaln_kernel,
        grid=(m // tn,),
        in_specs=[pl.BlockSpec((r, d), lambda j: (0, 0)), pl.BlockSpec((d, tn), lambda j: (0, j)),
                  pl.BlockSpec((1, tn), lambda j: (0, j))],
        out_specs=pl.BlockSpec((r, tn), lambda j: (0, j)),
        out_shape=jax.ShapeDtypeStruct((r, m), F32),
        compiler_params=_params(1),
    )(c_rows, w, b.reshape(1, m))


def _modproj_kernel(x_ref, g_ref, shift_ref, scale_ref, w_ref, o_ref, hb_ref):
    @pl.when(pl.program_id(1) == 0)
    def _():
        x = x_ref[...]
        y = x * lax.rsqrt(jnp.mean(x * x, axis=-1, keepdims=True) + EPS)
        hb_ref[...] = ((y * g_ref[...]) * (1.0 + scale_ref[...]) + shift_ref[...]).astype(BF16)

    o_ref[...] = jnp.dot(hb_ref[...], w_ref[...], preferred_element_type=F32)


def modproj(x2, g, shift, scale, rows_per_mod, w):
    n, d = x2.shape
    wid = w.shape[1]
    tm = min(512, rows_per_mod)
    tn = 2048
    m = shift.shape[0]
    mod_map = lambda i, j: (i // (rows_per_mod // tm), 0, 0)
    return pl.pallas_call(
        _modproj_kernel,
        grid=(n // tm, wid // tn),
        in_specs=[pl.BlockSpec((tm, d), lambda i, j: (i, 0)), pl.BlockSpec((1, d), lambda i, j: (0, 0)),
                  pl.BlockSpec((None, 1, d), mod_map), pl.BlockSpec((None, 1, d), mod_map),
                  pl.BlockSpec((d, tn), lambda i, j: (0, j))],
        out_specs=pl.BlockSpec((tm, tn), lambda i, j: (i, j)),
        out_shape=jax.ShapeDtypeStruct((n, wid), F32),
        scratch_shapes=[pltpu.VMEM((tm, d), BF16)],
        compiler_params=_params(2),
    )(x2, g.reshape(1, d), shift.reshape(m, 1, d), scale.reshape(m, 1, d), w)


NA_QROWS = 4


def _na_kernel(q_ref, k_ref, v_ref, kz_ref, vz_ref, bias_ref, o_ref, *, rows):
    step = pl.program_id(2)
    dh = HEAD_DIM
    band = NA_ROWS * GRID_W
    kz = kz_ref[...].astype(BF16)
    vz = vz_ref[...].astype(BF16)
    for qr in range(NA_QROWS):
        r = step * NA_QROWS + qr
        start = jnp.clip(r - NA_ROWS // 2, 0, rows - NA_ROWS)
        dr0 = start - r + NA_ROWS - 1
        tok0 = pl.multiple_of(start * GRID_W, GRID_W)
        kb = k_ref[pl.ds(tok0, band), :].astype(BF16)
        vb = v_ref[pl.ds(tok0, band), :].astype(BF16)
        q = (q_ref[pl.ds(qr * GRID_W, GRID_W), :] * (dh ** -0.5)).astype(BF16)
        outs = []
        for hh in range(LANES // dh):
            sl = slice(hh * dh, (hh + 1) * dh)
            s_nb = _dot_nt(q[:, sl], kb[:, sl]) + bias_ref[hh, dr0]
            s_cx = _dot_nt(q[:, sl], kz[:, sl])
            m = jnp.maximum(jnp.max(s_nb, axis=1, keepdims=True), jnp.max(s_cx, axis=1, keepdims=True))
            p_nb = jnp.exp(s_nb - m)
            p_cx = jnp.exp(s_cx - m)
            den = jnp.sum(p_nb, axis=1, keepdims=True) + jnp.sum(p_cx, axis=1, keepdims=True)
            o = (jnp.dot(p_nb.astype(BF16), vb[:, sl], preferred_element_type=F32)
                 + jnp.dot(p_cx.astype(BF16), vz[:, sl], preferred_element_type=F32))
            outs.append(o / den)
        o_ref[pl.ds(qr * GRID_W, GRID_W), :] = jnp.concatenate(outs, axis=1)


def na_band_bias(rpb):
    col = jnp.arange(GRID_W)
    col_start = jnp.clip(col - NA_COLS // 2, 0, GRID_W - NA_COLS)
    col_ok = (col[None, :] >= col_start[:, None]) & (col[None, :] < col_start[:, None] + NA_COLS)
    d_col = jnp.clip(col[None, :] - col[:, None], -(NA_COLS - 1), NA_COLS - 1) + NA_COLS - 1
    d_row = jnp.arange(NA_ROWS)[:, None] + jnp.arange(NA_ROWS)[None, :]
    b = rpb.astype(F32)[:, d_row][..., d_col]
    b = jnp.where(col_ok[None, None, None], b, NEG_INF)
    return b.transpose(0, 1, 3, 2, 4).reshape(rpb.shape[0], NA_ROWS, GRID_W, NA_ROWS * GRID_W)


def na_attention(px, pz, n_batch, seq, ctx_len, bias):
    rows = seq // GRID_W
    tq = NA_QROWS * GRID_W
    nsteps = rows // NA_QROWS
    heads_per_blk = LANES // HEAD_DIM
    return pl.pallas_call(
        functools.partial(_na_kernel, rows=rows),
        grid=(n_batch, NA_HEADS // heads_per_blk, nsteps),
        in_specs=[
            pl.BlockSpec((tq, LANES), lambda b, hp, s: (b * nsteps + s, COL_NA_Q // LANES + hp)),
            pl.BlockSpec((seq, LANES), lambda b, hp, s: (b, COL_NA_K // LANES + hp)),
            pl.BlockSpec((seq, LANES), lambda b, hp, s: (b, COL_NA_V // LANES + hp)),
            pl.BlockSpec((ctx_len, LANES), lambda b, hp, s: (b, COL_NA_K // LANES + hp)),
            pl.BlockSpec((ctx_len, LANES), lambda b, hp, s: (b, COL_NA_V // LANES + hp)),
            pl.BlockSpec((heads_per_blk, NA_ROWS, GRID_W, NA_ROWS * GRID_W), lambda b, hp, s: (hp, 0, 0, 0)),
        ],
        out_specs=pl.BlockSpec((tq, LANES), lambda b, hp, s: (b * nsteps + s, hp)),
        out_shape=jax.ShapeDtypeStruct((n_batch * seq, GROUP_WIDTH), F32),
        compiler_params=_params(3),
    )(px, px, px, pz, pz, bias)


def _swa_kernel(q_ref, kp_ref, kc_ref, kn_ref, vp_ref, vc_ref, vn_ref, kz_ref, vz_ref, sink_ref,
                cq_ref, sq_ref, ckp_ref, skp_ref, ckc_ref, skc_ref, ckn_ref, skn_ref, o_ref):
    n = pl.program_id(1)
    nb = pl.num_programs(1)
    dh = HEAD_DIM
    blk = SWA_BLOCK
    quarter = dh // 4
    group = SWA_HEADS // SWA_KV_HEADS
    q = _rope_lanes(q_ref[...], cq_ref[...], sq_ref[...], quarter) * (dh ** -0.5)
    kp = _rope_lanes(kp_ref[...], ckp_ref[...], skp_ref[...], quarter).astype(BF16)
    kc = _rope_lanes(kc_ref[...], ckc_ref[...], skc_ref[...], quarter).astype(BF16)
    kn = _rope_lanes(kn_ref[...], ckn_ref[...], skn_ref[...], quarter).astype(BF16)
    kz = kz_ref[...].astype(BF16)
    vp, vc, vn, vz = (r[...].astype(BF16) for r in (vp_ref, vc_ref, vn_ref, vz_ref))
    qi = lax.broadcasted_iota(jnp.int32, (group * blk, blk), 0) % blk
    kj = lax.broadcasted_iota(jnp.int32, (group * blk, blk), 1)
    ok_p = (kj >= qi) & (n > 0)
    ok_n = (kj <= qi) & (n < nb - 1)
    outs = []
    for hk in range(SWA_KV_HEADS):
        ks = slice(hk * dh, (hk + 1) * dh)
        qs = jnp.concatenate([q[:, (hk * group + g) * dh:(hk * group + g + 1) * dh] for g in range(group)],
                             axis=0).astype(BF16)
        sink = jnp.concatenate([jnp.full((blk, 1), 1.0, F32) * sink_ref[hk * group + g] for g in range(group)],
                               axis=0)
        s_p = jnp.where(ok_p, _dot_nt(qs, kp[:, ks]), NEG_INF)
        s_c = _dot_nt(qs, kc[:, ks])
        s_n = jnp.where(ok_n, _dot_nt(qs, kn[:, ks]), NEG_INF)
        s_z = _dot_nt(qs, kz[:, ks])
        m = jnp.maximum(jnp.maximum(jnp.max(s_p, axis=1, keepdims=True), jnp.max(s_c, axis=1, keepdims=True)),
                        jnp.maximum(jnp.max(s_n, axis=1, keepdims=True), jnp.max(s_z, axis=1, keepdims=True)))
        m = jnp.maximum(m, sink)
        e_p, e_c, e_n, e_z = (jnp.exp(s - m) for s in (s_p, s_c, s_n, s_z))
        den = (jnp.sum(e_p, axis=1, keepdims=True) + jnp.sum(e_c, axis=1, keepdims=True)
               + jnp.sum(e_n, axis=1, keepdims=True) + jnp.sum(e_z, axis=1, keepdims=True) + jnp.exp(sink - m))
        o = (jnp.dot(e_p.astype(BF16), vp[:, ks], preferred_element_type=F32)
             + jnp.dot(e_c.astype(BF16), vc[:, ks], preferred_element_type=F32)
             + jnp.dot(e_n.astype(BF16), vn[:, ks], preferred_element_type=F32)
             + jnp.dot(e_z.astype(BF16), vz[:, ks], preferred_element_type=F32)) / den
        outs += [o[g * blk:(g + 1) * blk, :] for g in range(group)]
    o_ref[...] = jnp.concatenate(outs, axis=1)


def swa_attention(px, pz, n_batch, seq, ctx_len, sink, rope_q, rope_k):
    blk = SWA_BLOCK
    nb = seq // blk
    prev = lambda b, n: b * nb + jnp.maximum(n - 1, 0)
    cur = lambda b, n: b * nb + n
    nxt = lambda b, n: b * nb + jnp.minimum(n + 1, nb - 1)
    kv = lambda off, f: pl.BlockSpec((blk, LANES), lambda b, n: (f(b, n), off // LANES))
    tab = lambda w, f: pl.BlockSpec((blk, w), lambda b, n: (f(0, n), 0))
    cq, sq = rope_q
    ck, sk = rope_k
    return pl.pallas_call(
        _swa_kernel,
        grid=(n_batch, nb),
        in_specs=[
            pl.BlockSpec((blk, GROUP_WIDTH), lambda b, n: (cur(b, n), COL_SWA_Q // GROUP_WIDTH)),
            kv(COL_SWA_K, prev), kv(COL_SWA_K, cur), kv(COL_SWA_K, nxt),
            kv(COL_SWA_V, prev), kv(COL_SWA_V, cur), kv(COL_SWA_V, nxt),
            pl.BlockSpec((ctx_len, LANES), lambda b, n: (b, COL_SWA_K // LANES)),
            pl.BlockSpec((ctx_len, LANES), lambda b, n: (b, COL_SWA_V // LANES)),
            pl.BlockSpec(memory_space=pltpu.SMEM),
            tab(GROUP_WIDTH, cur), tab(GROUP_WIDTH, cur), tab(LANES, prev), tab(LANES, prev),
            tab(LANES, cur), tab(LANES, cur), tab(LANES, nxt), tab(LANES, nxt),
        ],
        out_specs=pl.BlockSpec((blk, GROUP_WIDTH), lambda b, n: (cur(b, n), 0)),
        out_shape=jax.ShapeDtypeStruct((n_batch * seq, GROUP_WIDTH), F32),
        compiler_params=_params(2),
    )(px, px, px, px, px, px, px, pz, pz, sink, cq, sq, ck, sk, ck, sk, ck, sk)


def _ctx_attn_kernel(q_ref, k_ref, v_ref, sink_ref, o_ref, *, group, use_sink):
    dh = HEAD_DIM
    q = (q_ref[...] * (dh ** -0.5)).astype(BF16)
    k = k_ref[...].astype(BF16)
    v = v_ref[...].astype(BF16)
    outs = []
    for qh in range(q.shape[1] // dh):
        ks = slice((qh // group) * dh, (qh // group + 1) * dh)
        s = _dot_nt(q[:, qh * dh:(qh + 1) * dh], k[:, ks])
        m = jnp.max(s, axis=1, keepdims=True)
        if use_sink:
            m = jnp.maximum(m, sink_ref[qh])
        e = jnp.exp(s - m)
        den = jnp.sum(e, axis=1, keepdims=True)
        if use_sink:
            den = den + jnp.exp(sink_ref[qh] - m)
        outs.append(jnp.dot(e.astype(BF16), v[:, ks], preferred_element_type=F32) / den)
    o_ref[...] = jnp.concatenate(outs, axis=1)


def ctx_attention(pz, n_batch, ctx_len, cols, kv_width, sink):
    qc, kc, vc = cols
    use_sink = sink is not None
    if sink is None:
        sink = jnp.zeros((GROUP_WIDTH // HEAD_DIM,), F32)
    return pl.pallas_call(
        functools.partial(_ctx_attn_kernel, group=GROUP_WIDTH // kv_width, use_sink=use_sink),
        grid=(n_batch,),
        in_specs=[pl.BlockSpec((ctx_len, GROUP_WIDTH), lambda b: (b, qc // GROUP_WIDTH)),
                  pl.BlockSpec((ctx_len, kv_width), lambda b: (b, kc // kv_width)),
                  pl.BlockSpec((ctx_len, kv_width), lambda b: (b, vc // kv_width)),
                  pl.BlockSpec(memory_space=pltpu.SMEM)],
        out_specs=pl.BlockSpec((ctx_len, GROUP_WIDTH), lambda b: (b, 0)),
        out_shape=jax.ShapeDtypeStruct((n_batch * ctx_len, GROUP_WIDTH), F32),
        compiler_params=_params(1),
    )(pz, pz, pz, sink)


SCAN_ROWS = 256


def _split3(x):
    a = x.astype(BF16)
    r = x - a.astype(F32)
    b = r.astype(BF16)
    c = (r - b.astype(F32)).astype(BF16)
    return a, b, c


def _scan_kernel(*refs, heads, dk, dv, kind, rope, q_scale, k_scale):
    it = iter(refs)
    qf, kf, vf, qb, kb, vb = (next(it) for _ in range(6))
    if kind == "ret":
        lg = next(it)
    else:
        df, db, wup, bup = (next(it) for _ in range(4))
    if rope:
        cosf, sinf, cosb, sinb = (next(it) for _ in range(4))
    s0 = next(it)
    of, ob, sfin = next(it), next(it), next(it)
    st = next(it)

    s = pl.program_id(1)
    c = SCAN_CHUNK
    nch = SCAN_ROWS // c
    hk = heads * dk

    @pl.when(s == 0)
    def _():
        st[...] = s0[...]

    r_i = lax.broadcasted_iota(jnp.int32, (c, c), 0)
    c_i = lax.broadcasted_iota(jnp.int32, (c, c), 1)
    masks = (r_i >= c_i, c_i > r_i)

    if kind == "ret":
        pos = lax.broadcasted_iota(jnp.int32, (c, hk), 0).astype(F32)
        gcums = ((pos + 1.0) * lg[0:1, :], (float(c) - pos) * lg[1:2, :])
    else:
        rr = lax.broadcasted_iota(jnp.int32, (SCAN_ROWS, SCAN_ROWS), 0)
        cc = lax.broadcasted_iota(jnp.int32, (SCAN_ROWS, SCAN_ROWS), 1)
        same = (rr // c) == (cc // c)
        tris = (jnp.where(same & (rr >= cc), 1.0, 0.0).astype(BF16),
                jnp.where(same & (cc >= rr), 1.0, 0.0).astype(BF16))

        def gate_cum(d_ref, direction):
            pre = jnp.dot(d_ref[...].astype(BF16), wup[direction], preferred_element_type=F32) + bup[direction]
            g = -(jnp.maximum(-pre, 0.0) + jnp.log1p(jnp.exp(-jnp.abs(pre)))) / GLA_TAU
            return sum(jnp.dot(tris[direction], p, preferred_element_type=F32) for p in _split3(g))

        gcums = (gate_cum(df, 0), gate_cum(db, 1))

    def one(direction, q_ref, k_ref, v_ref, o_ref, cos_ref, sin_ref, ch):
        rows = pl.ds(ch * c, c)
        q = q_ref[rows, :]
        k = k_ref[rows, :]
        v = v_ref[rows, :]
        if rope:
            cs = jnp.concatenate([cos_ref[rows, :]] * heads, axis=1)
            sn = jnp.concatenate([sin_ref[rows, :]] * heads, axis=1)
            q = _rope_lanes(q, cs, sn, dk // 4)
            k = _rope_lanes(k, cs, sn, dk // 4)
        if q_scale != 1.0:
            q = q * q_scale
        if k_scale != 1.0:
            k = k * k_scale
        gcum = gcums[direction] if kind == "ret" else gcums[direction][ch * c:(ch + 1) * c, :]
        gtot = gcum[c - 1:c, :] if direction == 0 else gcum[0:1, :]
        q_rel = (q * jnp.exp(gcum - gtot)).astype(BF16)
        k_rel = (k * jnp.exp(gtot - gcum)).astype(BF16)
        q_dec = (q * jnp.exp(gcum)).astype(BF16)
        dec = jnp.exp(gtot)
        vb16 = v.astype(BF16)
        outs = []
        for hd in range(heads):
            ks = slice(hd * dk, (hd + 1) * dk)
            vs = slice(hd * dv, (hd + 1) * dv)
            a = jnp.where(masks[direction], _dot_nt(q_rel[:, ks], k_rel[:, ks]), 0.0)
            state = st[direction, hd]
            o = jnp.dot(a.astype(BF16), vb16[:, vs], preferred_element_type=F32)
            o = o + _dot_nt(q_dec[:, ks], state.astype(BF16))
            st[direction, hd] = dec[:, ks] * state + _dot_tn(vb16[:, vs], k_rel[:, ks])
            outs.append(o)
        o_ref[rows, :] = jnp.concatenate(outs, axis=1)

    for ch in range(nch):
        one(0, qf, kf, vf, of, cosf if rope else None, sinf if rope else None, ch)
        one(1, qb, kb, vb, ob, cosb if rope else None, sinb if rope else None, nch - 1 - ch)

    @pl.when(s == pl.num_programs(1) - 1)
    def _():
        sfin[...] = st[...]


def bidir_scan(p, n_batch, seq, cols, heads, dk, dv, kind, s0, *, lg=None, wup=None, bup=None,
               rope=None, q_scale=1.0, k_scale=1.0):
    t = SCAN_ROWS
    nblk = seq // t
    hk, hv = heads * dk, heads * dv
    qc, kc, vc = cols
    fwd = lambda w, off: pl.BlockSpec((t, w), lambda b, s: (b * nblk + s, off // w))
    bwd = lambda w, off: pl.BlockSpec((t, w), lambda b, s: (b * nblk + nblk - 1 - s, off // w))
    const = lambda shape: pl.BlockSpec(shape, lambda b, s: (0,) * len(shape))
    args = [p] * 6
    specs = [fwd(hk, qc), fwd(hk, kc), fwd(hv, vc), bwd(hk, qc), bwd(hk, kc), bwd(hv, vc)]
    if kind == "ret":
        args += [lg]
        specs += [const((2, hk))]
    else:
        args += [p, p, wup, bup]
        specs += [fwd(LANES, COL_GLA_D), bwd(LANES, COL_GLA_D), const(wup.shape), const(bup.shape)]
    if rope is not None:
        cos, sin = rope
        args += [cos, sin, cos, sin]
        specs += [pl.BlockSpec((t, dk), lambda b, s: (s, 0)), pl.BlockSpec((t, dk), lambda b, s: (s, 0)),
                  pl.BlockSpec((t, dk), lambda b, s: (nblk - 1 - s, 0)),
                  pl.BlockSpec((t, dk), lambda b, s: (nblk - 1 - s, 0))]
    args += [s0]
    state_spec = pl.BlockSpec((None, 2, heads, dv, dk), lambda b, s: (b, 0, 0, 0, 0))
    specs += [state_spec]
    n = n_batch * seq
    kern = functools.partial(_scan_kernel, heads=heads, dk=dk, dv=dv, kind=kind, rope=rope is not None,
                             q_scale=q_scale, k_scale=k_scale)
    return pl.pallas_call(
        kern,
        grid=(n_batch, nblk),
        in_specs=specs,
        out_specs=[pl.BlockSpec((t, hv), lambda b, s: (b * nblk + s, 0)),
                   pl.BlockSpec((t, hv), lambda b, s: (b * nblk + nblk - 1 - s, 0)),
                   state_spec],
        out_shape=[jax.ShapeDtypeStruct((n, hv), F32), jax.ShapeDtypeStruct((n, hv), F32),
                   jax.ShapeDtypeStruct((n_batch, 2, heads, dv, dk), F32)],
        scratch_shapes=[pltpu.VMEM((2, heads, dv, dk), F32)],
        compiler_params=_params(2),
    )(*args)


def _outproj_kernel(na_ref, rf_ref, rb_ref, rg_ref, gf_ref, gb_ref, gg_ref, sw_ref, gn_ref, w_ref, x_ref, mg_ref,
                    o_ref, *, head_w):
    ry = rf_ref[...] + rb_ref[...]
    gy = gf_ref[...] + gb_ref[...]
    r_out, g_out = [], []
    for hd in range(ry.shape[1] // head_w):
        sl = slice(hd * head_w, (hd + 1) * head_w)
        r = ry[:, sl]
        mu = jnp.mean(r, axis=-1, keepdims=True)
        var = jnp.mean(jnp.square(r - mu), axis=-1, keepdims=True)
        r_out.append((r - mu) * lax.rsqrt(var + EPS))
        gq = gy[:, sl]
        g_out.append(gq * lax.rsqrt(jnp.mean(gq * gq, axis=-1, keepdims=True) + EPS) * gn_ref[...])
    ret = jnp.concatenate(r_out, axis=1) * _silu(rg_ref[...])
    gla = jnp.concatenate(g_out, axis=1) * _silu(gg_ref[...])
    mix = jnp.concatenate([na_ref[...], ret, gla, sw_ref[...]], axis=1).astype(BF16)
    o_ref[...] = x_ref[...] + mg_ref[...] * jnp.dot(mix, w_ref[...], preferred_element_type=F32)


def outproj(na, rf, rb, gf, gb, sw, p, gla_norm_g, w_out, x2, mg, rows_per_mod):
    n, d = x2.shape
    gw = GROUP_WIDTH
    tm = 256
    m = mg.shape[0]
    row = lambda w_: pl.BlockSpec((tm, w_), lambda i: (i, 0))
    return pl.pallas_call(
        functools.partial(_outproj_kernel, head_w=RET_DV),
        grid=(n // tm,),
        in_specs=[row(gw), row(gw), row(gw), pl.BlockSpec((tm, gw), lambda i: (i, COL_RET_G // gw)),
                  row(gw), row(gw), pl.BlockSpec((tm, gw), lambda i: (i, COL_GLA_G // gw)), row(gw),
                  pl.BlockSpec((1, GLA_DV), lambda i: (0, 0)),
                  pl.BlockSpec((d, d), lambda i: (0, 0)), row(d),
                  pl.BlockSpec((None, 1, d), lambda i: (i // (rows_per_mod // tm), 0, 0))],
        out_specs=row(d),
        out_shape=jax.ShapeDtypeStruct((n, d), F32),
        compiler_params=_params(1),
    )(na, rf, rb, p, gf, gb, p, sw, gla_norm_g.reshape(1, GLA_DV), w_out, x2, mg.reshape(m, 1, d))


PEER_PAIRS = PEER_HEADS * PEER_TOPK
PEER_ROUTE_TOKENS = 128
PEER_EXPERT_TOKENS = 8
PEER_SUB = 8
PEER_FOLD = D_MODEL // PEER_SUB
INV_SQRT2 = 0.7071067811865476


def _topk_cols(s, payload=None):
    n_rows = s.shape[0]
    row = lax.broadcasted_iota(jnp.int32, s.shape, 0)
    vals, idxs = [], []
    for _ in range(PEER_TOPK):
        m = jnp.max(s, axis=0, keepdims=True)
        am = jnp.min(jnp.where(s == m, row, n_rows), axis=0, keepdims=True)
        sel = row == am
        vals.append(m)
        idxs.append(am if payload is None else jnp.max(jnp.where(sel, payload, -1), axis=0, keepdims=True))
        s = jnp.where(sel, -jnp.inf, s)
    return jnp.concatenate(vals, axis=0), jnp.concatenate(idxs, axis=0)


def _peer_route_kernel(x_ref, g_ref, shift_ref, scale_ref, wq_ref, sk_ref, h_ref, idx_ref, gate_ref):
    x = x_ref[...]
    y = x * lax.rsqrt(jnp.mean(x * x, axis=-1, keepdims=True) + EPS)
    h = (y * g_ref[...]) * (1.0 + scale_ref[...]) + shift_ref[...]
    h_ref[...] = h
    hb = h.astype(BF16)
    half = PEER_DK // 2

    def head_body(hd, carry):
        q = jnp.dot(hb, wq_ref[hd], preferred_element_type=F32)
        tops = []
        for p in range(2):
            qp = q[:, p * half:(p + 1) * half].astype(BF16)
            tops.append(_topk_cols(_dot_nt(sk_ref[p, hd], qp)))
        (v0, i0), (v1, i1) = tops
        cand_s = jnp.concatenate([v0[a:a + 1, :] + v1 for a in range(PEER_TOPK)], axis=0)
        cand_i = jnp.concatenate([i0[a:a + 1, :] * PEER_N_KEYS + i1 for a in range(PEER_TOPK)], axis=0)
        best_s, best_e = _topk_cols(cand_s, cand_i)
        e = jnp.exp(best_s - best_s[0:1, :])
        gate_ref[hd] = e / jnp.sum(e, axis=0, keepdims=True)
        idx_ref[hd] = best_e
        return carry

    lax.fori_loop(0, PEER_HEADS, head_body, 0)


def peer_route(x2, g, shift, scale, rows_per_mod, wq_h, sk):
    n, d = x2.shape
    t = PEER_ROUTE_TOKENS
    m = shift.shape[0]
    mod_map = lambda i: (i // (rows_per_mod // t), 0, 0)
    return pl.pallas_call(
        _peer_route_kernel,
        grid=(n // t,),
        in_specs=[
            pl.BlockSpec((t, d), lambda i: (i, 0)),
            pl.BlockSpec((1, d), lambda i: (0, 0)),
            pl.BlockSpec((None, 1, d), mod_map),
            pl.BlockSpec((None, 1, d), mod_map),
            pl.BlockSpec((PEER_HEADS, d, PEER_DK), lambda i: (0, 0, 0)),
            pl.BlockSpec((2, PEER_HEADS, PEER_N_KEYS, PEER_DK // 2), lambda i: (0, 0, 0, 0)),
        ],
        out_specs=[
            pl.BlockSpec((t, d), lambda i: (i, 0)),
            pl.BlockSpec((PEER_HEADS, PEER_TOPK, t), lambda i: (0, 0, i)),
            pl.BlockSpec((PEER_HEADS, PEER_TOPK, t), lambda i: (0, 0, i)),
        ],
        out_shape=[
            jax.ShapeDtypeStruct((n, d), F32),
            jax.ShapeDtypeStruct((PEER_HEADS, PEER_TOPK, n), jnp.int32),
            jax.ShapeDtypeStruct((PEER_HEADS, PEER_TOPK, n), F32),
        ],
        compiler_params=_params(1),
    )(x2, g.reshape(1, d), shift.reshape(m, 1, d), scale.reshape(m, 1, d), wq_h, sk)


def pack_experts(u, v):
    ub = lax.bitcast_convert_type(u.astype(BF16), jnp.uint16).astype(jnp.uint32)
    vb = lax.bitcast_convert_type(v.astype(BF16), jnp.uint16).astype(jnp.uint32)
    words = lax.bitcast_convert_type((ub << 16) | vb, jnp.int32)
    return words.reshape(u.shape[0], PEER_SUB, PEER_FOLD)


def _peer_expert_kernel(idx0_ref, idxn_ref, h_ref, gate_ref, x_ref, og_ref, uv_hbm, o_ref,
                        buf_even, buf_odd, sem, hbuf, pbuf, wbuf, ybuf):
    i = pl.program_id(0)
    n = pl.num_programs(0)
    tb = PEER_EXPERT_TOKENS
    fold = PEER_FOLD
    bufs = (buf_even, buf_odd)

    def slab_copy(idx_ref, j, r, parity):
        e = idx_ref[0, 0, j * PEER_PAIRS + r]
        return pltpu.make_async_copy(uv_hbm.at[e], bufs[parity].at[j, r], sem.at[parity, j])

    def wait_token(j, parity):
        pltpu.make_async_copy(uv_hbm.at[pl.ds(0, PEER_PAIRS)], bufs[parity].at[j], sem.at[parity, j]).wait()

    @pl.when(i == 0)
    def _():
        def prime(j, carry):
            for r in range(PEER_PAIRS):
                slab_copy(idx0_ref, j, r, 0).start(priority=r % 2)
            return carry
        lax.fori_loop(0, tb, prime, 0)

    lane = lax.broadcasted_iota(jnp.int32, (PEER_TOPK, PEER_ROUTE_TOKENS), 1)
    lane0 = (i % (PEER_ROUTE_TOKENS // tb)) * tb

    for s in range(PEER_SUB):
        for c in range(fold // LANES):
            hbuf[c, pl.ds(s * tb, tb), :] = h_ref[:, s * fold + c * LANES:s * fold + (c + 1) * LANES]

    def token_body(j, carry, parity, prefetch):
        buf = bufs[parity]
        wait_token(j, parity)
        hj = jnp.concatenate([hbuf[c, pl.ds(j, PEER_SUB, stride=tb), :] for c in range(fold // LANES)],
                             axis=1)
        for p in range(PEER_PAIRS):
            u = lax.bitcast_convert_type(buf[j, p] & jnp.int32(-65536), F32)
            prod = u * hj
            pbuf[pl.ds(p * PEER_SUB, PEER_SUB), :] = prod[:, :LANES] + prod[:, LANES:]
        part = pbuf[pl.ds(0, PEER_PAIRS, stride=PEER_SUB), :]
        for s in range(1, PEER_SUB):
            part = part + pbuf[pl.ds(s, PEER_PAIRS, stride=PEER_SUB), :]
        sc = jnp.sum(part, axis=1, keepdims=True)
        act = 0.5 * sc * (1.0 + lax.erf(sc * INV_SQRT2))
        g = jnp.concatenate([jnp.sum(jnp.where(lane == lane0 + j, gate_ref[hd], 0.0), axis=1, keepdims=True)
                             for hd in range(PEER_HEADS)], axis=0)
        wbuf[...] = jnp.broadcast_to(g * act, (PEER_PAIRS, LANES))
        accs = [jnp.zeros((PEER_SUB, fold), F32) for _ in range(4)]
        for p in range(PEER_PAIRS):
            if prefetch:
                slab_copy(idxn_ref, j, p, 1 - parity).start(priority=p % 2)
            v = lax.bitcast_convert_type(buf[j, p] << 16, F32)
            wp = jnp.broadcast_to(wbuf[p:p + 1, :], (PEER_SUB, LANES))
            accs[p % 4] = accs[p % 4] + v * jnp.concatenate([wp, wp], axis=1)
        yj = (accs[0] + accs[1]) + (accs[2] + accs[3])
        for c in range(fold // LANES):
            ybuf[c, pl.ds(j * PEER_SUB, PEER_SUB), :] = yj[:, c * LANES:(c + 1) * LANES]
        return carry

    for parity in range(2):
        for prefetch in (True, False):
            @pl.when((i % 2 == parity) & ((i + 1 < n) == prefetch))
            def _(parity=parity, prefetch=prefetch):
                lax.fori_loop(0, tb, functools.partial(token_body, parity=parity, prefetch=prefetch), 0)

    for s in range(PEER_SUB):
        for c in range(fold // LANES):
            sl = slice(s * fold + c * LANES, s * fold + (c + 1) * LANES)
            o_ref[:, sl] = x_ref[:, sl] + og_ref[:, sl] * ybuf[c, pl.ds(s, tb, stride=PEER_SUB), :]


def peer_expert(h, idx, gate, x2, out_gate, rows_per_mod, uv):
    n, d = h.shape
    tb = PEER_EXPERT_TOKENS
    nb = n // tb
    m = out_gate.shape[0]
    rows = tb * PEER_PAIRS
    idx_rows = idx.reshape(PEER_PAIRS, n).T.reshape(nb, 1, rows)
    gate_blocks = PEER_ROUTE_TOKENS // tb
    return pl.pallas_call(
        _peer_expert_kernel,
        grid=(nb,),
        in_specs=[
            pl.BlockSpec((1, 1, rows), lambda i: (0, 0, 0), memory_space=pltpu.SMEM),
            pl.BlockSpec((1, 1, rows), lambda i: (jnp.minimum(i + 1, nb - 1), 0, 0), memory_space=pltpu.SMEM),
            pl.BlockSpec((tb, d), lambda i: (i, 0)),
            pl.BlockSpec((PEER_HEADS, PEER_TOPK, PEER_ROUTE_TOKENS), lambda i: (0, 0, i // gate_blocks)),
            pl.BlockSpec((tb, d), lambda i: (i, 0)),
            pl.BlockSpec((None, 1, d), lambda i: (i // (rows_per_mod // tb), 0, 0)),
            pl.BlockSpec(memory_space=pl.ANY),
        ],
        out_specs=pl.BlockSpec((tb, d), lambda i: (i, 0)),
        out_shape=jax.ShapeDtypeStruct((n, d), F32),
        scratch_shapes=[pltpu.VMEM((tb, PEER_PAIRS, PEER_SUB, PEER_FOLD), jnp.int32),
                        pltpu.VMEM((tb, PEER_PAIRS, PEER_SUB, PEER_FOLD), jnp.int32),
                        pltpu.SemaphoreType.DMA((2, tb)),
                        pltpu.VMEM((PEER_FOLD // LANES, PEER_SUB * tb, LANES), F32),
                        pltpu.VMEM((PEER_PAIRS * PEER_SUB, LANES), F32),
                        pltpu.VMEM((PEER_PAIRS, LANES), F32),
                        pltpu.VMEM((PEER_FOLD // LANES, tb * PEER_SUB, LANES), F32)],
        compiler_params=pltpu.CompilerParams(dimension_semantics=("arbitrary",), vmem_limit_bytes=VMEM_LIMIT,
                                             disable_bounds_checks=True),
    )(idx_rows, idx_rows, h, gate, x2, out_gate.reshape(m, 1, d), uv)


def peer_residual(x2, g, shift, scale, out_gate, rows_per_mod, wq_h, sk, uv):
    h, idx, gate = peer_route(x2, g, shift, scale, rows_per_mod, wq_h, sk)
    return peer_expert(h, idx, gate, x2, out_gate, rows_per_mod, uv)


def _final_norm_kernel(x_ref, g_ref, o_ref):
    x = x_ref[...]
    y = x * lax.rsqrt(jnp.mean(x * x, axis=-1, keepdims=True) + EPS)
    o_ref[...] = y * g_ref[...]


def final_norm(x2, g):
    n, d = x2.shape
    tm = 512
    return pl.pallas_call(
        _final_norm_kernel,
        grid=(n // tm,),
        in_specs=[pl.BlockSpec((tm, d), lambda i: (i, 0)), pl.BlockSpec((1, d), lambda i: (0, 0))],
        out_specs=pl.BlockSpec((tm, d), lambda i: (i, 0)),
        out_shape=jax.ShapeDtypeStruct((n, d), x2.dtype),
        compiler_params=_params(1),
    )(x2, g.reshape(1, d))


def _mixers(p, pz, n_batch, seq, ctx_len, is_ctx, tables, prm):
    if is_ctx:
        na = ctx_attention(p, n_batch, seq, (COL_NA_Q, COL_NA_K, COL_NA_V), GROUP_WIDTH, None)
        sw = ctx_attention(p, n_batch, seq, (COL_SWA_Q, COL_SWA_K, COL_SWA_V), SWA_KV_HEADS * HEAD_DIM, prm["sink"])
    else:
        na = na_attention(p, pz, n_batch, seq, ctx_len, prm["na_bias"])
        sw = swa_attention(p, pz, n_batch, seq, ctx_len, prm["sink"], tables["swa_q"], tables["swa_k"])
    rf, rb, rs = bidir_scan(p, n_batch, seq, (COL_RET_Q, COL_RET_K, COL_RET_V), RET_HEADS, RET_DK, RET_DV, "ret",
                            prm["ret_s0"], lg=prm["ret_lg"], rope=None if is_ctx else tables["ret"],
                            k_scale=RET_DK ** -0.5)
    gf, gb, gs = bidir_scan(p, n_batch, seq, (COL_GLA_Q, COL_GLA_K, COL_GLA_V), GLA_HEADS, GLA_DK, GLA_DV, "gla",
                            prm["gla_s0"], wup=prm["gla_wup"], bup=prm["gla_bup"], q_scale=GLA_DK ** -0.5)
    return (na, rf, rb, gf, gb, sw), (rs, gs)


def kernel(x, c, ctx, c_ctx, w_ada, b_ada, norm_attn_g, norm_ffn_g, w_in, na_rpb, ret_log_gamma,
           gla_w_gate_up, gla_b_gate, gla_norm_g, swa_sink, w_out, peer_w_q, peer_sub_keys,
           peer_u, peer_v, final_g):
    bsz, slen, d = x.shape
    zlen = ctx.shape[1]
    x2 = x.reshape(bsz * slen, d)
    z2 = ctx.reshape(bsz * zlen, d)
    tables = {"ret": rope_lane_tables(slen, RET_DK, 1),
              "swa_q": rope_lane_tables(slen, HEAD_DIM, SWA_HEADS),
              "swa_k": rope_lane_tables(slen, HEAD_DIM, SWA_KV_HEADS)}
    c_rows = jnp.zeros((8, d), F32).at[:bsz].set(c).at[bsz].set(c_ctx)
    for layer in range(DEPTH):
        has_next = layer < DEPTH - 1
        mod = adaln(c_rows, w_ada[layer], b_ada[layer])
        mx = [mod[:bsz, k * d:(k + 1) * d] for k in range(6)]
        mz = [mod[bsz:bsz + 1, k * d:(k + 1) * d] for k in range(6)]

        wi = w_in[layer]
        wp = jnp.concatenate([wi[:, :REF_COL_GLA_D], wi[:, REF_COL_SWA_Q:], wi[:, REF_COL_GLA_D:REF_COL_SWA_Q],
                              jnp.zeros((d, PROJ_WIDTH - REF_D_IN), F32)], axis=1).astype(BF16)
        px = modproj(x2, norm_attn_g[layer], mx[0], mx[1], slen, wp)
        pz = modproj(z2, norm_attn_g[layer], mz[0], mz[1], bsz * zlen, wp)

        wup = (jnp.zeros((2, LANES, GLA_HEADS * GLA_DK), F32)
               .at[0, :GLA_RANK].set(gla_w_gate_up[layer, 0])
               .at[1, GLA_RANK:2 * GLA_RANK].set(gla_w_gate_up[layer, 1])).astype(BF16)
        prm = {"na_bias": na_band_bias(na_rpb[layer]), "sink": swa_sink[layer],
               "ret_lg": jnp.repeat(ret_log_gamma[layer], RET_DK, axis=1),
               "gla_wup": wup, "gla_bup": gla_b_gate[layer].reshape(2, 1, GLA_HEADS * GLA_DK),
               "ret_s0": jnp.zeros((bsz, 2, RET_HEADS, RET_DV, RET_DK), F32),
               "gla_s0": jnp.zeros((bsz, 2, GLA_HEADS, GLA_DV, GLA_DK), F32)}
        mix_z, (ret_s, gla_s) = _mixers(pz, pz, bsz, zlen, zlen, True, tables, prm)
        prm["ret_s0"], prm["gla_s0"] = ret_s, gla_s
        mix_x, _ = _mixers(px, pz, bsz, slen, zlen, False, tables, prm)

        wo = w_out[layer].astype(BF16)
        x2 = outproj(*mix_x, px, gla_norm_g[layer], wo, x2, mx[2], slen)

        uv = pack_experts(peer_u[layer], peer_v[layer])
        wq_h = peer_w_q[layer].reshape(d, PEER_HEADS, PEER_DK).transpose(1, 0, 2).astype(BF16)
        sk = peer_sub_keys[layer].astype(BF16)
        x2 = peer_residual(x2, norm_ffn_g[layer], mx[3], mx[4], mx[5], slen, wq_h, sk, uv)
        if has_next:
            z2 = outproj(*mix_z, pz, gla_norm_g[layer], wo, z2, mz[2], bsz * zlen)
            z2 = peer_residual(z2, norm_ffn_g[layer], mz[3], mz[4], mz[5], bsz * zlen, wq_h, sk, uv)
    return final_norm(x2, final_g).reshape(bsz, slen, d)
```

```python
import functools

import jax
import jax.numpy as jnp
import numpy as np
from jax import lax
from jax.experimental import pallas as pl
from jax.experimental.pallas import tpu as pltpu

D_MODEL = 2048
DEPTH = 2
GRID_W = 64
EPS = 1e-6
ROPE_BASE = 10000.0

GROUP_WIDTH = D_MODEL // 4
HEAD_DIM = 64
NA_HEADS = GROUP_WIDTH // HEAD_DIM
NA_ROWS = 8
NA_COLS = 16
RET_HEADS = 4
RET_DK = GROUP_WIDTH // RET_HEADS
RET_DV = GROUP_WIDTH // RET_HEADS
GLA_HEADS = 4
GLA_DV = GROUP_WIDTH // GLA_HEADS
GLA_DK = GLA_DV // 2
GLA_RANK = 16
GLA_TAU = 16.0
SWA_HEADS = GROUP_WIDTH // HEAD_DIM
SWA_KV_HEADS = SWA_HEADS // 4
SWA_WINDOW = 128
SWA_BLOCK = 128
SCAN_CHUNK = 64
PEER_HEADS = 8
PEER_N_KEYS = 128
PEER_N_EXPERTS = PEER_N_KEYS * PEER_N_KEYS
PEER_DK = 256
PEER_TOPK = 16

LANES = 128
VMEM_LIMIT = 48 * 1024 * 1024
BF16 = jnp.bfloat16
F32 = jnp.float32
NEG_INF = float("-inf")

COL_NA_Q, COL_NA_K, COL_NA_V = 0, 512, 1024
COL_RET_Q, COL_RET_K, COL_RET_V, COL_RET_G = 1536, 2048, 2560, 3072
COL_GLA_Q, COL_GLA_K, COL_GLA_V, COL_GLA_G = 3584, 3840, 4096, 4608
COL_SWA_Q, COL_SWA_K, COL_SWA_V = 5120, 5632, 5760
COL_GLA_D = 5888
REF_COL_GLA_D, REF_COL_SWA_Q, REF_D_IN = 5120, 5152, 5920
PROJ_WIDTH = 6144


def _silu(x):
    return x / (1.0 + jnp.exp(-x))


def _dot_nt(a, b):
    return lax.dot_general(a, b, (((1,), (1,)), ((), ())), preferred_element_type=F32)


def _dot_tn(a, b):
    return lax.dot_general(a, b, (((0,), (0,)), ((), ())), preferred_element_type=F32)


def _params(n_axes):
    return pltpu.CompilerParams(dimension_semantics=("arbitrary",) * n_axes, vmem_limit_bytes=VMEM_LIMIT)


def _rope_lanes(x, cs, sn, quarter):
    n = x.shape[-1]
    lane = lax.broadcasted_iota(jnp.int32, x.shape, x.ndim - 1)
    first = (lane % (2 * quarter)) < quarter
    swapped = jnp.where(first, pltpu.roll(x, n - quarter, x.ndim - 1), pltpu.roll(x, quarter, x.ndim - 1))
    return x * cs + swapped * sn


def rope_lane_tables(length, dh, copies):
    t = jnp.arange(length)
    pos = jnp.stack([t // GRID_W, t % GRID_W], axis=-1).astype(F32)
    quarter = dh // 4
    inv = ROPE_BASE ** (-jnp.arange(quarter, dtype=F32) / quarter)
    ang = pos[:, :, None] * inv
    cos, sin = jnp.cos(ang), jnp.sin(ang)
    cl = jnp.concatenate([cos[:, 0], cos[:, 0], cos[:, 1], cos[:, 1]], axis=-1)
    sl = jnp.concatenate([-sin[:, 0], sin[:, 0], -sin[:, 1], sin[:, 1]], axis=-1)
    return jnp.tile(cl, (1, copies)), jnp.tile(sl, (1, copies))


def _adaln_kernel(c_ref, w_ref, b_ref, o_ref):
    a = _silu(c_ref[...]).astype(BF16)
    o_ref[...] = jnp.dot(a, w_ref[...].astype(BF16), preferred_element_type=F32) + b_ref[...]


def adaln(c_rows, w, b):
    r, d = c_rows.shape
    m = w.shape[1]
    tn = 1024
    return pl.pallas_call(
        _adaln_kernel,
        grid=(m // tn,),
        in_specs=[pl.BlockSpec((r, d), lambda j: (0, 0)), pl.BlockSpec((d, tn), lambda j: (0, j)),
                  pl.BlockSpec((1, tn), lambda j: (0, j))],
        out_specs=pl.BlockSpec((r, tn), lambda j: (0, j)),
        out_shape=jax.ShapeDtypeStruct((r, m), F32),
        compiler_params=_params(1),
    )(c_rows, w, b.reshape(1, m))


def _modproj_kernel(x_ref, g_ref, shift_ref, scale_ref, w_ref, o_ref, hb_ref):
    @pl.when(pl.program_id(1) == 0)
    def _():
        x = x_ref[...]
        y = x * lax.rsqrt(jnp.mean(x * x, axis=-1, keepdims=True) + EPS)
        hb_ref[...] = ((y * g_ref[...]) * (1.0 + scale_ref[...]) + shift_ref[...]).astype(BF16)

    o_ref[...] = jnp.dot(hb_ref[...], w_ref[...], preferred_element_type=F32)


def modproj(x2, g, shift, scale, rows_per_mod, w):
    n, d = x2.shape
    wid = w.shape[1]
    tm = min(512, rows_per_mod)
    tn = 2048
    m = shift.shape[0]
    mod_map = lambda i, j: (i // (rows_per_mod // tm), 0, 0)
    return pl.pallas_call(
        _modproj_kernel,
        grid=(n // tm, wid // tn),
        in_specs=[pl.BlockSpec((tm, d), lambda i, j: (i, 0)), pl.BlockSpec((1, d), lambda i, j: (0, 0)),
                  pl.BlockSpec((None, 1, d), mod_map), pl.BlockSpec((None, 1, d), mod_map),
                  pl.BlockSpec((d, tn), lambda i, j: (0, j))],
        out_specs=pl.BlockSpec((tm, tn), lambda i, j: (i, j)),
        out_shape=jax.ShapeDtypeStruct((n, wid), F32),
        scratch_shapes=[pltpu.VMEM((tm, d), BF16)],
        compiler_params=_params(2),
    )(x2, g.reshape(1, d), shift.reshape(m, 1, d), scale.reshape(m, 1, d), w)


NA_QROWS = 4


def _na_kernel(q_ref, k_ref, v_ref, kz_ref, vz_ref, bias_ref, o_ref, *, rows):
    step = pl.program_id(2)
    dh = HEAD_DIM
    band = NA_ROWS * GRID_W
    kz = kz_ref[...].astype(BF16)
    vz = vz_ref[...].astype(BF16)
    for qr in range(NA_QROWS):
        r = step * NA_QROWS + qr
        start = jnp.clip(r - NA_ROWS // 2, 0, rows - NA_ROWS)
        dr0 = start - r + NA_ROWS - 1
        tok0 = pl.multiple_of(start * GRID_W, GRID_W)
        kb = k_ref[pl.ds(tok0, band), :].astype(BF16)
        vb = v_ref[pl.ds(tok0, band), :].astype(BF16)
        q = (q_ref[pl.ds(qr * GRID_W, GRID_W), :] * (dh ** -0.5)).astype(BF16)
        outs = []
        for hh in range(LANES // dh):
            sl = slice(hh * dh, (hh + 1) * dh)
            s_nb = _dot_nt(q[:, sl], kb[:, sl]) + bias_ref[hh, dr0]
            s_cx = _dot_nt(q[:, sl], kz[:, sl])
            m = jnp.maximum(jnp.max(s_nb, axis=1, keepdims=True), jnp.max(s_cx, axis=1, keepdims=True))
            p_nb = jnp.exp(s_nb - m)
            p_cx = jnp.exp(s_cx - m)
            den = jnp.sum(p_nb, axis=1, keepdims=True) + jnp.sum(p_cx, axis=1, keepdims=True)
            o = (jnp.dot(p_nb.astype(BF16), vb[:, sl], preferred_element_type=F32)
                 + jnp.dot(p_cx.astype(BF16), vz[:, sl], preferred_element_type=F32))
            outs.append(o / den)
        o_ref[pl.ds(qr * GRID_W, GRID_W), :] = jnp.concatenate(outs, axis=1)


def na_band_bias(rpb):
    col = jnp.arange(GRID_W)
    col_start = jnp.clip(col - NA_COLS // 2, 0, GRID_W - NA_COLS)
    col_ok = (col[None, :] >= col_start[:, None]) & (col[None, :] < col_start[:, None] + NA_COLS)
    d_col = jnp.clip(col[None, :] - col[:, None], -(NA_COLS - 1), NA_COLS - 1) + NA_COLS - 1
    d_row = jnp.arange(NA_ROWS)[:, None] + jnp.arange(NA_ROWS)[None, :]
    b = rpb.astype(F32)[:, d_row][..., d_col]
    b = jnp.where(col_ok[None, None, None], b, NEG_INF)
    return b.transpose(0, 1, 3, 2, 4).reshape(rpb.shape[0], NA_ROWS, GRID_W, NA_ROWS * GRID_W)


def na_attention(px, pz, n_batch, seq, ctx_len, bias):
    rows = seq // GRID_W
    tq = NA_QROWS * GRID_W
    nsteps = rows // NA_QROWS
    heads_per_blk = LANES // HEAD_DIM
    return pl.pallas_call(
        functools.partial(_na_kernel, rows=rows),
        grid=(n_batch, NA_HEADS // heads_per_blk, nsteps),
        in_specs=[
            pl.BlockSpec((tq, LANES), lambda b, hp, s: (b * nsteps + s, COL_NA_Q // LANES + hp)),
            pl.BlockSpec((seq, LANES), lambda b, hp, s: (b, COL_NA_K // LANES + hp)),
            pl.BlockSpec((seq, LANES), lambda b, hp, s: (b, COL_NA_V // LANES + hp)),
            pl.BlockSpec((ctx_len, LANES), lambda b, hp, s: (b, COL_NA_K // LANES + hp)),
            pl.BlockSpec((ctx_len, LANES), lambda b, hp, s: (b, COL_NA_V // LANES + hp)),
            pl.BlockSpec((heads_per_blk, NA_ROWS, GRID_W, NA_ROWS * GRID_W), lambda b, hp, s: (hp, 0, 0, 0)),
        ],
        out_specs=pl.BlockSpec((tq, LANES), lambda b, hp, s: (b * nsteps + s, hp)),
        out_shape=jax.ShapeDtypeStruct((n_batch * seq, GROUP_WIDTH), F32),
        compiler_params=_params(3),
    )(px, px, px, pz, pz, bias)


def _swa_kernel(q_ref, kp_ref, kc_ref, kn_ref, vp_ref, vc_ref, vn_ref, kz_ref, vz_ref, sink_ref,
                cq_ref, sq_ref, ckp_ref, skp_ref, ckc_ref, skc_ref, ckn_ref, skn_ref, o_ref):
    n = pl.program_id(1)
    nb = pl.num_programs(1)
    dh = HEAD_DIM
    blk = SWA_BLOCK
    quarter = dh // 4
    group = SWA_HEADS // SWA_KV_HEADS
    q = _rope_lanes(q_ref[...], cq_ref[...], sq_ref[...], quarter) * (dh ** -0.5)
    kp = _rope_lanes(kp_ref[...], ckp_ref[...], skp_ref[...], quarter).astype(BF16)
    kc = _rope_lanes(kc_ref[...], ckc_ref[...], skc_ref[...], quarter).astype(BF16)
    kn = _rope_lanes(kn_ref[...], ckn_ref[...], skn_ref[...], quarter).astype(BF16)
    kz = kz_ref[...].astype(BF16)
    vp, vc, vn, vz = (r[...].astype(BF16) for r in (vp_ref, vc_ref, vn_ref, vz_ref))
    qi = lax.broadcasted_iota(jnp.int32, (group * blk, blk), 0) % blk
    kj = lax.broadcasted_iota(jnp.int32, (group * blk, blk), 1)
    ok_p = (kj >= qi) & (n > 0)
    ok_n = (kj <= qi) & (n < nb - 1)
    outs = []
    for hk in range(SWA_KV_HEADS):
        ks = slice(hk * dh, (hk + 1) * dh)
        qs = jnp.concatenate([q[:, (hk * group + g) * dh:(hk * group + g + 1) * dh] for g in range(group)],
                             axis=0).astype(BF16)
        sink = jnp.concatenate([jnp.full((blk, 1), 1.0, F32) * sink_ref[hk * group + g] for g in range(group)],
                               axis=0)
        s_p = jnp.where(ok_p, _dot_nt(qs, kp[:, ks]), NEG_INF)
        s_c = _dot_nt(qs, kc[:, ks])
        s_n = jnp.where(ok_n, _dot_nt(qs, kn[:, ks]), NEG_INF)
        s_z = _dot_nt(qs, kz[:, ks])
        m = jnp.maximum(jnp.maximum(jnp.max(s_p, axis=1, keepdims=True), jnp.max(s_c, axis=1, keepdims=True)),
                        jnp.maximum(jnp.max(s_n, axis=1, keepdims=True), jnp.max(s_z, axis=1, keepdims=True)))
        m = jnp.maximum(m, sink)
        e_p, e_c, e_n, e_z = (jnp.exp(s - m) for s in (s_p, s_c, s_n, s_z))
        den = (jnp.sum(e_p, axis=1, keepdims=True) + jnp.sum(e_c, axis=1, keepdims=True)
               + jnp.sum(e_n, axis=1, keepdims=True) + jnp.sum(e_z, axis=1, keepdims=True) + jnp.exp(sink - m))
        o = (jnp.dot(e_p.astype(BF16), vp[:, ks], preferred_element_type=F32)
             + jnp.dot(e_c.astype(BF16), vc[:, ks], preferred_element_type=F32)
             + jnp.dot(e_n.astype(BF16), vn[:, ks], preferred_element_type=F32)
             + jnp.dot(e_z.astype(BF16), vz[:, ks], preferred_element_type=F32)) / den
        outs += [o[g * blk:(g + 1) * blk, :] for g in range(group)]
    o_ref[...] = jnp.concatenate(outs, axis=1)


def swa_attention(px, pz, n_batch, seq, ctx_len, sink, rope_q, rope_k):
    blk = SWA_BLOCK
    nb = seq // blk
    prev = lambda b, n: b * nb + jnp.maximum(n - 1, 0)
    cur = lambda b, n: b * nb + n
    nxt = lambda b, n: b * nb + jnp.minimum(n + 1, nb - 1)
    kv = lambda off, f: pl.BlockSpec((blk, LANES), lambda b, n: (f(b, n), off // LANES))
    tab = lambda w, f: pl.BlockSpec((blk, w), lambda b, n: (f(0, n), 0))
    cq, sq = rope_q
    ck, sk = rope_k
    return pl.pallas_call(
        _swa_kernel,
        grid=(n_batch, nb),
        in_specs=[
            pl.BlockSpec((blk, GROUP_WIDTH), lambda b, n: (cur(b, n), COL_SWA_Q // GROUP_WIDTH)),
            kv(COL_SWA_K, prev), kv(COL_SWA_K, cur), kv(COL_SWA_K, nxt),
            kv(COL_SWA_V, prev), kv(COL_SWA_V, cur), kv(COL_SWA_V, nxt),
            pl.BlockSpec((ctx_len, LANES), lambda b, n: (b, COL_SWA_K // LANES)),
            pl.BlockSpec((ctx_len, LANES), lambda b, n: (b, COL_SWA_V // LANES)),
            pl.BlockSpec(memory_space=pltpu.SMEM),
            tab(GROUP_WIDTH, cur), tab(GROUP_WIDTH, cur), tab(LANES, prev), tab(LANES, prev),
            tab(LANES, cur), tab(LANES, cur), tab(LANES, nxt), tab(LANES, nxt),
        ],
        out_specs=pl.BlockSpec((blk, GROUP_WIDTH), lambda b, n: (cur(b, n), 0)),
        out_shape=jax.ShapeDtypeStruct((n_batch * seq, GROUP_WIDTH), F32),
        compiler_params=_params(2),
    )(px, px, px, px, px, px, px, pz, pz, sink, cq, sq, ck, sk, ck, sk, ck, sk)


def _ctx_attn_kernel(q_ref, k_ref, v_ref, sink_ref, o_ref, *, group, use_sink):
    dh = HEAD_DIM
    q = (q_ref[...] * (dh ** -0.5)).astype(BF16)
    k = k_ref[...].astype(BF16)
    v = v_ref[...].astype(BF16)
    outs = []
    for qh in range(q.shape[1] // dh):
        ks = slice((qh // group) * dh, (qh // group + 1) * dh)
        s = _dot_nt(q[:, qh * dh:(qh + 1) * dh], k[:, ks])
        m = jnp.max(s, axis=1, keepdims=True)
        if use_sink:
            m = jnp.maximum(m, sink_ref[qh])
        e = jnp.exp(s - m)
        den = jnp.sum(e, axis=1, keepdims=True)
        if use_sink:
            den = den + jnp.exp(sink_ref[qh] - m)
        outs.append(jnp.dot(e.astype(BF16), v[:, ks], preferred_element_type=F32) / den)
    o_ref[...] = jnp.concatenate(outs, axis=1)


def ctx_attention(pz, n_batch, ctx_len, cols, kv_width, sink):
    qc, kc, vc = cols
    use_sink = sink is not None
    if sink is None:
        sink = jnp.zeros((GROUP_WIDTH // HEAD_DIM,), F32)
    return pl.pallas_call(
        functools.partial(_ctx_attn_kernel, group=GROUP_WIDTH // kv_width, use_sink=use_sink),
        grid=(n_batch,),
        in_specs=[pl.BlockSpec((ctx_len, GROUP_WIDTH), lambda b: (b, qc // GROUP_WIDTH)),
                  pl.BlockSpec((ctx_len, kv_width), lambda b: (b, kc // kv_width)),
                  pl.BlockSpec((ctx_len, kv_width), lambda b: (b, vc // kv_width)),
                  pl.BlockSpec(memory_space=pltpu.SMEM)],
        out_specs=pl.BlockSpec((ctx_len, GROUP_WIDTH), lambda b: (b, 0)),
        out_shape=jax.ShapeDtypeStruct((n_batch * ctx_len, GROUP_WIDTH), F32),
        compiler_params=_params(1),
    )(pz, pz, pz, sink)


SCAN_ROWS = 256


def _split3(x):
    a = x.astype(BF16)
    r = x - a.astype(F32)
    b = r.astype(BF16)
    c = (r - b.astype(F32)).astype(BF16)
    return a, b, c


def _scan_kernel(*refs, heads, dk, dv, kind, rope, q_scale, k_scale):
    it = iter(refs)
    qf, kf, vf, qb, kb, vb = (next(it) for _ in range(6))
    if kind == "ret":
        lg = next(it)
    else:
        df, db, wup, bup = (next(it) for _ in range(4))
    if rope:
        cosf, sinf, cosb, sinb = (next(it) for _ in range(4))
    s0 = next(it)
    of, ob, sfin = next(it), next(it), next(it)
    st = next(it)

    s = pl.program_id(1)
    c = SCAN_CHUNK
    nch = SCAN_ROWS // c
    hk = heads * dk

    @pl.when(s == 0)
    def _():
        st[...] = s0[...]

    r_i = lax.broadcasted_iota(jnp.int32, (c, c), 0)
    c_i = lax.broadcasted_iota(jnp.int32, (c, c), 1)
    masks = (r_i >= c_i, c_i > r_i)

    if kind == "ret":
        pos = lax.broadcasted_iota(jnp.int32, (c, hk), 0).astype(F32)
        gcums = ((pos + 1.0) * lg[0:1, :], (float(c) - pos) * lg[1:2, :])
    else:
        rr = lax.broadcasted_iota(jnp.int32, (SCAN_ROWS, SCAN_ROWS), 0)
        cc = lax.broadcasted_iota(jnp.int32, (SCAN_ROWS, SCAN_ROWS), 1)
        same = (rr // c) == (cc // c)
        tris = (jnp.where(same & (rr >= cc), 1.0, 0.0).astype(BF16),
                jnp.where(same & (cc >= rr), 1.0, 0.0).astype(BF16))

        def gate_cum(d_ref, direction):
            pre = jnp.dot(d_ref[...].astype(BF16), wup[direction], preferred_element_type=F32) + bup[direction]
            g = -(jnp.maximum(-pre, 0.0) + jnp.log1p(jnp.exp(-jnp.abs(pre)))) / GLA_TAU
            return sum(jnp.dot(tris[direction], p, preferred_element_type=F32) for p in _split3(g))

        gcums = (gate_cum(df, 0), gate_cum(db, 1))

    def one(direction, q_ref, k_ref, v_ref, o_ref, cos_ref, sin_ref, ch):
        rows = pl.ds(ch * c, c)
        q = q_ref[rows, :]
        k = k_ref[rows, :]
        v = v_ref[rows, :]
        if rope:
            cs = jnp.concatenate([cos_ref[rows, :]] * heads, axis=1)
            sn = jnp.concatenate([sin_ref[rows, :]] * heads, axis=1)
            q = _rope_lanes(q, cs, sn, dk // 4)
            k = _rope_lanes(k, cs, sn, dk // 4)
        if q_scale != 1.0:
            q = q * q_scale
        if k_scale != 1.0:
            k = k * k_scale
        gcum = gcums[direction] if kind == "ret" else gcums[direction][ch * c:(ch + 1) * c, :]
        gtot = gcum[c - 1:c, :] if direction == 0 else gcum[0:1, :]
        q_rel = (q * jnp.exp(gcum - gtot)).astype(BF16)
        k_rel = (k * jnp.exp(gtot - gcum)).astype(BF16)
        q_dec = (q * jnp.exp(gcum)).astype(BF16)
        dec = jnp.exp(gtot)
        vb16 = v.astype(BF16)
        outs = []
        for hd in range(heads):
            ks = slice(hd * dk, (hd + 1) * dk)
            vs = slice(hd * dv, (hd + 1) * dv)
            a = jnp.where(masks[direction], _dot_nt(q_rel[:, ks], k_rel[:, ks]), 0.0)
            state = st[direction, hd]
            o = jnp.dot(a.astype(BF16), vb16[:, vs], preferred_element_type=F32)
            o = o + _dot_nt(q_dec[:, ks], state.astype(BF16))
            st[direction, hd] = dec[:, ks] * state + _dot_tn(vb16[:, vs], k_rel[:, ks])
            outs.append(o)
        o_ref[rows, :] = jnp.concatenate(outs, axis=1)

    for ch in range(nch):
        one(0, qf, kf, vf, of, cosf if rope else None, sinf if rope else None, ch)
        one(1, qb, kb, vb, ob, cosb if rope else None, sinb if rope else None, nch - 1 - ch)

    @pl.when(s == pl.num_programs(1) - 1)
    def _():
        sfin[...] = st[...]


def bidir_scan(p, n_batch, seq, cols, heads, dk, dv, kind, s0, *, lg=None, wup=None, bup=None,
               rope=None, q_scale=1.0, k_scale=1.0):
    t = SCAN_ROWS
    nblk = seq // t
    hk, hv = heads * dk, heads * dv
    qc, kc, vc = cols
    fwd = lambda w, off: pl.BlockSpec((t, w), lambda b, s: (b * nblk + s, off // w))
    bwd = lambda w, off: pl.BlockSpec((t, w), lambda b, s: (b * nblk + nblk - 1 - s, off // w))
    const = lambda shape: pl.BlockSpec(shape, lambda b, s: (0,) * len(shape))
    args = [p] * 6
    specs = [fwd(hk, qc), fwd(hk, kc), fwd(hv, vc), bwd(hk, qc), bwd(hk, kc), bwd(hv, vc)]
    if kind == "ret":
        args += [lg]
        specs += [const((2, hk))]
    else:
        args += [p, p, wup, bup]
        specs += [fwd(LANES, COL_GLA_D), bwd(LANES, COL_GLA_D), const(wup.shape), const(bup.shape)]
    if rope is not None:
        cos, sin = rope
        args += [cos, sin, cos, sin]
        specs += [pl.BlockSpec((t, dk), lambda b, s: (s, 0)), pl.BlockSpec((t, dk), lambda b, s: (s, 0)),
                  pl.BlockSpec((t, dk), lambda b, s: (nblk - 1 - s, 0)),
                  pl.BlockSpec((t, dk), lambda b, s: (nblk - 1 - s, 0))]
    args += [s0]
    state_spec = pl.BlockSpec((None, 2, heads, dv, dk), lambda b, s: (b, 0, 0, 0, 0))
    specs += [state_spec]
    n = n_batch * seq
    kern = functools.partial(_scan_kernel, heads=heads, dk=dk, dv=dv, kind=kind, rope=rope is not None,
                             q_scale=q_scale, k_scale=k_scale)
    return pl.pallas_call(
        kern,
        grid=(n_batch, nblk),
        in_specs=specs,
        out_specs=[pl.BlockSpec((t, hv), lambda b, s: (b * nblk + s, 0)),
                   pl.BlockSpec((t, hv), lambda b, s: (b * nblk + nblk - 1 - s, 0)),
                   state_spec],
        out_shape=[jax.ShapeDtypeStruct((n, hv), F32), jax.ShapeDtypeStruct((n, hv), F32),
                   jax.ShapeDtypeStruct((n_batch, 2, heads, dv, dk), F32)],
        scratch_shapes=[pltpu.VMEM((2, heads, dv, dk), F32)],
        compiler_params=_params(2),
    )(*args)


def _outproj_kernel(na_ref, rf_ref, rb_ref, rg_ref, gf_ref, gb_ref, gg_ref, sw_ref, gn_ref, w_ref, x_ref, mg_ref,
                    o_ref, *, head_w):
    ry = rf_ref[...] + rb_ref[...]
    gy = gf_ref[...] + gb_ref[...]
    r_out, g_out = [], []
    for hd in range(ry.shape[1] // head_w):
        sl = slice(hd * head_w, (hd + 1) * head_w)
        r = ry[:, sl]
        mu = jnp.mean(r, axis=-1, keepdims=True)
        var = jnp.mean(jnp.square(r - mu), axis=-1, keepdims=True)
        r_out.append((r - mu) * lax.rsqrt(var + EPS))
        gq = gy[:, sl]
        g_out.append(gq * lax.rsqrt(jnp.mean(gq * gq, axis=-1, keepdims=True) + EPS) * gn_ref[...])
    ret = jnp.concatenate(r_out, axis=1) * _silu(rg_ref[...])
    gla = jnp.concatenate(g_out, axis=1) * _silu(gg_ref[...])
    mix = jnp.concatenate([na_ref[...], ret, gla, sw_ref[...]], axis=1).astype(BF16)
    o_ref[...] = x_ref[...] + mg_ref[...] * jnp.dot(mix, w_ref[...], preferred_element_type=F32)


def outproj(na, rf, rb, gf, gb, sw, p, gla_norm_g, w_out, x2, mg, rows_per_mod):
    n, d = x2.shape
    gw = GROUP_WIDTH
    tm = 256
    m = mg.shape[0]
    row = lambda w_: pl.BlockSpec((tm, w_), lambda i: (i, 0))
    return pl.pallas_call(
        functools.partial(_outproj_kernel, head_w=RET_DV),
        grid=(n // tm,),
        in_specs=[row(gw), row(gw), row(gw), pl.BlockSpec((tm, gw), lambda i: (i, COL_RET_G // gw)),
                  row(gw), row(gw), pl.BlockSpec((tm, gw), lambda i: (i, COL_GLA_G // gw)), row(gw),
                  pl.BlockSpec((1, GLA_DV), lambda i: (0, 0)),
                  pl.BlockSpec((d, d), lambda i: (0, 0)), row(d),
                  pl.BlockSpec((None, 1, d), lambda i: (i // (rows_per_mod // tm), 0, 0))],
        out_specs=row(d),
        out_shape=jax.ShapeDtypeStruct((n, d), F32),
        compiler_params=_params(1),
    )(na, rf, rb, p, gf, gb, p, sw, gla_norm_g.reshape(1, GLA_DV), w_out, x2, mg.reshape(m, 1, d))


PEER_PAIRS = PEER_HEADS * PEER_TOPK
PEER_ROUTE_TOKENS = 128
PEER_EXPERT_TOKENS = 16
PEER_SUB = 8
PEER_FOLD = D_MODEL // PEER_SUB
INV_SQRT2 = 0.7071067811865476


def _topk_cols(s, payload=None, order=None):
    row = lax.broadcasted_iota(jnp.int32, s.shape, 0) if order is None else order
    vals, idxs = [], []
    for _ in range(PEER_TOPK):
        m = jnp.max(s, axis=0, keepdims=True)
        am = jnp.min(jnp.where(s == m, row, jnp.iinfo(jnp.int32).max), axis=0, keepdims=True)
        sel = row == am
        vals.append(m)
        idxs.append(am if payload is None else jnp.max(jnp.where(sel, payload, -1), axis=0, keepdims=True))
        s = jnp.where(sel, -jnp.inf, s)
    return jnp.concatenate(vals, axis=0), jnp.concatenate(idxs, axis=0)


def _staircase_candidates(v0, i0, v1, i1):
    k = PEER_TOPK
    assert k == 16
    t = v0.shape[1]
    r8 = lax.broadcasted_iota(jnp.int32, (8, t), 0)
    r16 = lax.broadcasted_iota(jnp.int32, (k, t), 0)

    def piece(a_sl, b_sl):
        return v0[a_sl, :] + v1[b_sl, :], i0[a_sl, :] * PEER_N_KEYS + i1[b_sl, :]

    one = lambda j: slice(j, j + 1)
    lo = slice(0, 8)
    pieces = [
        (one(0), slice(0, k), None, r16),
        (one(1), lo, None, k + r8),
        (one(2), lo, r8 <= 4, 2 * k + r8),
        (one(3), lo, r8 <= 3, 3 * k + r8),
        (slice(8, k), one(0), None, (r8 + 8) * k),
        (lo, one(0), r8 >= 4, r8 * k),
        (lo, one(1), r8 >= 4, r8 * k + 1),
        (lo, one(2), r8 == 4, r8 * k + 2),
    ]
    sums, ids, orders = [], [], []
    for a_sl, b_sl, keep, order in pieces:
        s, e = piece(a_sl, b_sl)
        sums.append(s if keep is None else jnp.where(keep, s, NEG_INF))
        ids.append(e)
        orders.append(order)
    return jnp.concatenate(sums, axis=0), jnp.concatenate(ids, axis=0), jnp.concatenate(orders, axis=0)


def _peer_route_kernel(x_ref, g_ref, shift_ref, scale_ref, wq_ref, sk_ref, h_ref, idx_ref, gate_ref):
    x = x_ref[...]
    y = x * lax.rsqrt(jnp.mean(x * x, axis=-1, keepdims=True) + EPS)
    h = (y * g_ref[...]) * (1.0 + scale_ref[...]) + shift_ref[...]
    h_ref[...] = h
    hb = h.astype(BF16)
    half = PEER_DK // 2

    def head_body(hd, carry):
        q = jnp.dot(hb, wq_ref[hd], preferred_element_type=F32)
        tops = []
        for p in range(2):
            qp = q[:, p * half:(p + 1) * half].astype(BF16)
            tops.append(_topk_cols(_dot_nt(sk_ref[p, hd], qp)))
        (v0, i0), (v1, i1) = tops
        best_s, best_e = _topk_cols(*_staircase_candidates(v0, i0, v1, i1))
        e = jnp.exp(best_s - best_s[0:1, :])
        gate_ref[hd] = e / jnp.sum(e, axis=0, keepdims=True)
        idx_ref[hd] = best_e
        return carry

    lax.fori_loop(0, PEER_HEADS, head_body, 0, unroll=2)


def peer_route(x2, g, shift, scale, rows_per_mod, wq_h, sk):
    n, d = x2.shape
    t = PEER_ROUTE_TOKENS
    m = shift.shape[0]
    mod_map = lambda i: (i // (rows_per_mod // t), 0, 0)
    return pl.pallas_call(
        _peer_route_kernel,
        grid=(n // t,),
        in_specs=[
            pl.BlockSpec((t, d), lambda i: (i, 0)),
            pl.BlockSpec((1, d), lambda i: (0, 0)),
            pl.BlockSpec((None, 1, d), mod_map),
            pl.BlockSpec((None, 1, d), mod_map),
            pl.BlockSpec((PEER_HEADS, d, PEER_DK), lambda i: (0, 0, 0)),
            pl.BlockSpec((2, PEER_HEADS, PEER_N_KEYS, PEER_DK // 2), lambda i: (0, 0, 0, 0)),
        ],
        out_specs=[
            pl.BlockSpec((t, d), lambda i: (i, 0)),
            pl.BlockSpec((PEER_HEADS, PEER_TOPK, t), lambda i: (0, 0, i)),
            pl.BlockSpec((PEER_HEADS, PEER_TOPK, t), lambda i: (0, 0, i)),
        ],
        out_shape=[
            jax.ShapeDtypeStruct((n, d), F32),
            jax.ShapeDtypeStruct((PEER_HEADS, PEER_TOPK, n), jnp.int32),
            jax.ShapeDtypeStruct((PEER_HEADS, PEER_TOPK, n), F32),
        ],
        compiler_params=_params(1),
    )(x2, g.reshape(1, d), shift.reshape(m, 1, d), scale.reshape(m, 1, d), wq_h, sk)


def pack_experts(u, v):
    ub = lax.bitcast_convert_type(u.astype(BF16), jnp.uint16).astype(jnp.uint32)
    vb = lax.bitcast_convert_type(v.astype(BF16), jnp.uint16).astype(jnp.uint32)
    words = lax.bitcast_convert_type((ub << 16) | vb, jnp.int32)
    return words.reshape(u.shape[0], PEER_SUB, PEER_FOLD)


def _peer_expert_kernel(idx0_ref, idxn_ref, h_ref, gate_ref, x_ref, og_ref, uv_hbm, o_ref,
                        buf_even, buf_odd, sem, hbuf, pbuf, wbuf, ybuf):
    i = pl.program_id(0)
    n = pl.num_programs(0)
    tb = PEER_EXPERT_TOKENS
    fold = PEER_FOLD
    bufs = (buf_even, buf_odd)

    def slab_copy(idx_ref, j, r, parity):
        e = idx_ref[0, 0, j * PEER_PAIRS + r]
        return pltpu.make_async_copy(uv_hbm.at[e], bufs[parity].at[j, r], sem.at[parity, j])

    def wait_token(j, parity):
        pltpu.make_async_copy(uv_hbm.at[pl.ds(0, PEER_PAIRS)], bufs[parity].at[j], sem.at[parity, j]).wait()

    @pl.when(i == 0)
    def _():
        def prime(j, carry):
            for r in range(PEER_PAIRS):
                slab_copy(idx0_ref, j, r, 0).start(priority=r % 2)
            return carry
        lax.fori_loop(0, tb, prime, 0)

    lane = lax.broadcasted_iota(jnp.int32, (PEER_TOPK, PEER_ROUTE_TOKENS), 1)
    lane0 = (i % (PEER_ROUTE_TOKENS // tb)) * tb

    for s in range(PEER_SUB):
        for c in range(fold // LANES):
            hbuf[c, pl.ds(s * tb, tb), :] = h_ref[:, s * fold + c * LANES:s * fold + (c + 1) * LANES]

    def token_body(j, carry, parity, prefetch):
        buf = bufs[parity]
        wait_token(j, parity)
        hj = jnp.concatenate([hbuf[c, pl.ds(j, PEER_SUB, stride=tb), :] for c in range(fold // LANES)],
                             axis=1)
        for p in range(PEER_PAIRS):
            u = lax.bitcast_convert_type(buf[j, p] & jnp.int32(-65536), F32)
            prod = u * hj
            pbuf[pl.ds(p * PEER_SUB, PEER_SUB), :] = prod[:, :LANES] + prod[:, LANES:]
        part = pbuf[pl.ds(0, PEER_PAIRS, stride=PEER_SUB), :]
        for s in range(1, PEER_SUB):
            part = part + pbuf[pl.ds(s, PEER_PAIRS, stride=PEER_SUB), :]
        sc = jnp.sum(part, axis=1, keepdims=True)
        act = 0.5 * sc * (1.0 + lax.erf(sc * INV_SQRT2))
        g = jnp.concatenate([jnp.sum(jnp.where(lane == lane0 + j, gate_ref[hd], 0.0), axis=1, keepdims=True)
                             for hd in range(PEER_HEADS)], axis=0)
        wbuf[...] = jnp.broadcast_to(g * act, (PEER_PAIRS, LANES))
        accs = [jnp.zeros((PEER_SUB, fold), F32) for _ in range(4)]
        for p in range(PEER_PAIRS):
            if prefetch:
                slab_copy(idxn_ref, j, p, 1 - parity).start(priority=p % 2)
            v = lax.bitcast_convert_type(buf[j, p] << 16, F32)
            wp = jnp.broadcast_to(wbuf[p:p + 1, :], (PEER_SUB, LANES))
            accs[p % 4] = accs[p % 4] + v * jnp.concatenate([wp, wp], axis=1)
        yj = (accs[0] + accs[1]) + (accs[2] + accs[3])
        for c in range(fold // LANES):
            ybuf[c, pl.ds(j * PEER_SUB, PEER_SUB), :] = yj[:, c * LANES:(c + 1) * LANES]
        return carry

    for parity in range(2):
        for prefetch in (True, False):
            @pl.when((i % 2 == parity) & ((i + 1 < n) == prefetch))
            def _(parity=parity, prefetch=prefetch):
                lax.fori_loop(0, tb, functools.partial(token_body, parity=parity, prefetch=prefetch), 0)

    for s in range(PEER_SUB):
        for c in range(fold // LANES):
            sl = slice(s * fold + c * LANES, s * fold + (c + 1) * LANES)
            o_ref[:, sl] = x_ref[:, sl] + og_ref[:, sl] * ybuf[c, pl.ds(s, tb, stride=PEER_SUB), :]


def peer_expert(h, idx, gate, x2, out_gate, rows_per_mod, uv):
    n, d = h.shape
    tb = PEER_EXPERT_TOKENS
    nb = n // tb
    m = out_gate.shape[0]
    rows = tb * PEER_PAIRS
    idx_rows = idx.reshape(PEER_PAIRS, n).T.reshape(nb, 1, rows)
    gate_blocks = PEER_ROUTE_TOKENS // tb
    return pl.pallas_call(
        _peer_expert_kernel,
        grid=(nb,),
        in_specs=[
            pl.BlockSpec((1, 1, rows), lambda i: (0, 0, 0), memory_space=pltpu.SMEM),
            pl.BlockSpec((1, 1, rows), lambda i: (jnp.minimum(i + 1, nb - 1), 0, 0), memory_space=pltpu.SMEM),
            pl.BlockSpec((tb, d), lambda i: (i, 0)),
            pl.BlockSpec((PEER_HEADS, PEER_TOPK, PEER_ROUTE_TOKENS), lambda i: (0, 0, i // gate_blocks)),
            pl.BlockSpec((tb, d), lambda i: (i, 0)),
            pl.BlockSpec((None, 1, d), lambda i: (i // (rows_per_mod // tb), 0, 0)),
            pl.BlockSpec(memory_space=pl.ANY),
        ],
        out_specs=pl.BlockSpec((tb, d), lambda i: (i, 0)),
        out_shape=jax.ShapeDtypeStruct((n, d), F32),
        scratch_shapes=[pltpu.VMEM((tb, PEER_PAIRS, PEER_SUB, PEER_FOLD), jnp.int32),
                        pltpu.VMEM((tb, PEER_PAIRS, PEER_SUB, PEER_FOLD), jnp.int32),
                        pltpu.SemaphoreType.DMA((2, tb)),
                        pltpu.VMEM((PEER_FOLD // LANES, PEER_SUB * tb, LANES), F32),
                        pltpu.VMEM((PEER_PAIRS * PEER_SUB, LANES), F32),
                        pltpu.VMEM((PEER_PAIRS, LANES), F32),
                        pltpu.VMEM((PEER_FOLD // LANES, tb * PEER_SUB, LANES), F32)],
        compiler_params=pltpu.CompilerParams(dimension_semantics=("arbitrary",), vmem_limit_bytes=VMEM_LIMIT,
                                             disable_bounds_checks=True),
    )(idx_rows, idx_rows, h, gate, x2, out_gate.reshape(m, 1, d), uv)


def peer_residual(x2, g, shift, scale, out_gate, rows_per_mod, wq_h, sk, uv):
    h, idx, gate = peer_route(x2, g, shift, scale, rows_per_mod, wq_h, sk)
    return peer_expert(h, idx, gate, x2, out_gate, rows_per_mod, uv)


def _final_norm_kernel(x_ref, g_ref, o_ref):
    x = x_ref[...]
    y = x * lax.rsqrt(jnp.mean(x * x, axis=-1, keepdims=True) + EPS)
    o_ref[...] = y * g_ref[...]


def final_norm(x2, g):
    n, d = x2.shape
    tm = 512
    return pl.pallas_call(
        _final_norm_kernel,
        grid=(n // tm,),
        in_specs=[pl.BlockSpec((tm, d), lambda i: (i, 0)), pl.BlockSpec((1, d), lambda i: (0, 0))],
        out_specs=pl.BlockSpec((tm, d), lambda i: (i, 0)),
        out_shape=jax.ShapeDtypeStruct((n, d), x2.dtype),
        compiler_params=_params(1),
    )(x2, g.reshape(1, d))


def _mixers(p, pz, n_batch, seq, ctx_len, is_ctx, tables, prm):
    if is_ctx:
        na = ctx_attention(p, n_batch, seq, (COL_NA_Q, COL_NA_K, COL_NA_V), GROUP_WIDTH, None)
        sw = ctx_attention(p, n_batch, seq, (COL_SWA_Q, COL_SWA_K, COL_SWA_V), SWA_KV_HEADS * HEAD_DIM, prm["sink"])
    else:
        na = na_attention(p, pz, n_batch, seq, ctx_len, prm["na_bias"])
        sw = swa_attention(p, pz, n_batch, seq, ctx_len, prm["sink"], tables["swa_q"], tables["swa_k"])
    rf, rb, rs = bidir_scan(p, n_batch, seq, (COL_RET_Q, COL_RET_K, COL_RET_V), RET_HEADS, RET_DK, RET_DV, "ret",
                            prm["ret_s0"], lg=prm["ret_lg"], rope=None if is_ctx else tables["ret"],
                            k_scale=RET_DK ** -0.5)
    gf, gb, gs = bidir_scan(p, n_batch, seq, (COL_GLA_Q, COL_GLA_K, COL_GLA_V), GLA_HEADS, GLA_DK, GLA_DV, "gla",
                            prm["gla_s0"], wup=prm["gla_wup"], bup=prm["gla_bup"], q_scale=GLA_DK ** -0.5)
    return (na, rf, rb, gf, gb, sw), (rs, gs)


def kernel(x, c, ctx, c_ctx, w_ada, b_ada, norm_attn_g, norm_ffn_g, w_in, na_rpb, ret_log_gamma,
           gla_w_gate_up, gla_b_gate, gla_norm_g, swa_sink, w_out, peer_w_q, peer_sub_keys,
           peer_u, peer_v, final_g):
    bsz, slen, d = x.shape
    zlen = ctx.shape[1]
    x2 = x.reshape(bsz * slen, d)
    z2 = ctx.reshape(bsz * zlen, d)
    tables = {"ret": rope_lane_tables(slen, RET_DK, 1),
              "swa_q": rope_lane_tables(slen, HEAD_DIM, SWA_HEADS),
              "swa_k": rope_lane_tables(slen, HEAD_DIM, SWA_KV_HEADS)}
    c_rows = jnp.zeros((8, d), F32).at[:bsz].set(c).at[bsz].set(c_ctx)
    for layer in range(DEPTH):
        has_next = layer < DEPTH - 1
        mod = adaln(c_rows, w_ada[layer], b_ada[layer])
        mx = [mod[:bsz, k * d:(k + 1) * d] for k in range(6)]
        mz = [mod[bsz:bsz + 1, k * d:(k + 1) * d] for k in range(6)]

        wi = w_in[layer]
        wp = jnp.concatenate([wi[:, :REF_COL_GLA_D], wi[:, REF_COL_SWA_Q:], wi[:, REF_COL_GLA_D:REF_COL_SWA_Q],
                              jnp.zeros((d, PROJ_WIDTH - REF_D_IN), F32)], axis=1).astype(BF16)
        px = modproj(x2, norm_attn_g[layer], mx[0], mx[1], slen, wp)
        pz = modproj(z2, norm_attn_g[layer], mz[0], mz[1], bsz * zlen, wp)

        wup = (jnp.zeros((2, LANES, GLA_HEADS * GLA_DK), F32)
               .at[0, :GLA_RANK].set(gla_w_gate_up[layer, 0])
               .at[1, GLA_RANK:2 * GLA_RANK].set(gla_w_gate_up[layer, 1])).astype(BF16)
        prm = {"na_bias": na_band_bias(na_rpb[layer]), "sink": swa_sink[layer],
               "ret_lg": jnp.repeat(ret_log_gamma[layer], RET_DK, axis=1),
               "gla_wup": wup, "gla_bup": gla_b_gate[layer].reshape(2, 1, GLA_HEADS * GLA_DK),
               "ret_s0": jnp.zeros((bsz, 2, RET_HEADS, RET_DV, RET_DK), F32),
               "gla_s0": jnp.zeros((bsz, 2, GLA_HEADS, GLA_DV, GLA_DK), F32)}
        mix_z, (ret_s, gla_s) = _mixers(pz, pz, bsz, zlen, zlen, True, tables, prm)
        prm["ret_s0"], prm["gla_s0"] = ret_s, gla_s
        mix_x, _ = _mixers(px, pz, bsz, slen, zlen, False, tables, prm)

        wo = w_out[layer].astype(BF16)
        x2 = outproj(*mix_x, px, gla_norm_g[layer], wo, x2, mx[2], slen)

        uv = pack_experts(peer_u[layer], peer_v[layer])
        wq_h = peer_w_q[layer].reshape(d, PEER_HEADS, PEER_DK).transpose(1, 0, 2).astype(BF16)
        sk = peer_sub_keys[layer].astype(BF16)
        x2 = peer_residual(x2, norm_ffn_g[layer], mx[3], mx[4], mx[5], slen, wq_h, sk, uv)
        if has_next:
            z2 = outproj(*mix_z, pz, gla_norm_g[layer], wo, z2, mz[2], bsz * zlen)
            z2 = peer_residual(z2, norm_ffn_g[layer], mz[3], mz[4], mz[5], bsz * zlen, wq_h, sk, uv)
    return final_norm(x2, final_g).reshape(bsz, slen, d)
```

```python
import functools

import jax
import jax.numpy as jnp
import numpy as np
from jax import lax
from jax.experimental import pallas as pl
from jax.experimental.pallas import tpu as pltpu

D_MODEL = 2048
DEPTH = 2
GRID_W = 64
EPS = 1e-6
ROPE_BASE = 10000.0

GROUP_WIDTH = D_MODEL // 4
HEAD_DIM = 64
NA_HEADS = GROUP_WIDTH // HEAD_DIM
NA_ROWS = 8
NA_COLS = 16
RET_HEADS = 4
RET_DK = GROUP_WIDTH // RET_HEADS
RET_DV = GROUP_WIDTH // RET_HEADS
GLA_HEADS = 4
GLA_DV = GROUP_WIDTH // GLA_HEADS
GLA_DK = GLA_DV // 2
GLA_RANK = 16
GLA_TAU = 16.0
SWA_HEADS = GROUP_WIDTH // HEAD_DIM
SWA_KV_HEADS = SWA_HEADS // 4
SWA_WINDOW = 128
SWA_BLOCK = 128
SCAN_CHUNK = 64
PEER_HEADS = 8
PEER_N_KEYS = 128
PEER_N_EXPERTS = PEER_N_KEYS * PEER_N_KEYS
PEER_DK = 256
PEER_TOPK = 16

LANES = 128
VMEM_LIMIT = 48 * 1024 * 1024
BF16 = jnp.bfloat16
F32 = jnp.float32
NEG_INF = float("-inf")

COL_NA_Q, COL_NA_K, COL_NA_V = 0, 512, 1024
COL_RET_Q, COL_RET_K, COL_RET_V, COL_RET_G = 1536, 2048, 2560, 3072
COL_GLA_Q, COL_GLA_K, COL_GLA_V, COL_GLA_G = 3584, 3840, 4096, 4608
COL_SWA_Q, COL_SWA_K, COL_SWA_V = 5120, 5632, 5760
COL_GLA_D = 5888
REF_COL_GLA_D, REF_COL_SWA_Q, REF_D_IN = 5120, 5152, 5920
PROJ_WIDTH = 6144


def _silu(x):
    return x / (1.0 + jnp.exp(-x))


def _dot_nt(a, b):
    return lax.dot_general(a, b, (((1,), (1,)), ((), ())), preferred_element_type=F32)


def _dot_tn(a, b):
    return lax.dot_general(a, b, (((0,), (0,)), ((), ())), preferred_element_type=F32)


def _params(n_axes):
    return pltpu.CompilerParams(dimension_semantics=("arbitrary",) * n_axes, vmem_limit_bytes=VMEM_LIMIT)


def _rope_lanes(x, cs, sn, quarter):
    n = x.shape[-1]
    lane = lax.broadcasted_iota(jnp.int32, x.shape, x.ndim - 1)
    first = (lane % (2 * quarter)) < quarter
    swapped = jnp.where(first, pltpu.roll(x, n - quarter, x.ndim - 1), pltpu.roll(x, quarter, x.ndim - 1))
    return x * cs + swapped * sn


def rope_lane_tables(length, dh, copies):
    t = jnp.arange(length)
    pos = jnp.stack([t // GRID_W, t % GRID_W], axis=-1).astype(F32)
    quarter = dh // 4
    inv = ROPE_BASE ** (-jnp.arange(quarter, dtype=F32) / quarter)
    ang = pos[:, :, None] * inv
    cos, sin = jnp.cos(ang), jnp.sin(ang)
    cl = jnp.concatenate([cos[:, 0], cos[:, 0], cos[:, 1], cos[:, 1]], axis=-1)
    sl = jnp.concatenate([-sin[:, 0], sin[:, 0], -sin[:, 1], sin[:, 1]], axis=-1)
    return jnp.tile(cl, (1, copies)), jnp.tile(sl, (1, copies))


def _adaln_kernel(c_ref, w_ref, b_ref, o_ref):
    a = _silu(c_ref[...]).astype(BF16)
    o_ref[...] = jnp.dot(a, w_ref[...].astype(BF16), preferred_element_type=F32) + b_ref[...]


def adaln(c_rows, w, b):
    r, d = c_rows.shape
    m = w.shape[1]
    tn = 1024
    return pl.pallas_call(
        _adaln_kernel,
        grid=(m // tn,),
        in_specs=[pl.BlockSpec((r, d), lambda j: (0, 0)), pl.BlockSpec((d, tn), lambda j: (0, j)),
                  pl.BlockSpec((1, tn), lambda j: (0, j))],
        out_specs=pl.BlockSpec((r, tn), lambda j: (0, j)),
        out_shape=jax.ShapeDtypeStruct((r, m), F32),
        compiler_params=_params(1),
    )(c_rows, w, b.reshape(1, m))


def _modproj_kernel(x_ref, g_ref, shift_ref, scale_ref, w_ref, o_ref, hb_ref):
    @pl.when(pl.program_id(1) == 0)
    def _():
        x = x_ref[...]
        y = x * lax.rsqrt(jnp.mean(x * x, axis=-1, keepdims=True) + EPS)
        hb_ref[...] = ((y * g_ref[...]) * (1.0 + scale_ref[...]) + shift_ref[...]).astype(BF16)

    o_ref[...] = jnp.dot(hb_ref[...], w_ref[...], preferred_element_type=F32)


def modproj(x2, g, shift, scale, rows_per_mod, w):
    n, d = x2.shape
    wid = w.shape[1]
    tm = min(512, rows_per_mod)
    tn = 2048
    m = shift.shape[0]
    mod_map = lambda i, j: (i // (rows_per_mod // tm), 0, 0)
    return pl.pallas_call(
        _modproj_kernel,
        grid=(n // tm, wid // tn),
        in_specs=[pl.BlockSpec((tm, d), lambda i, j: (i, 0)), pl.BlockSpec((1, d), lambda i, j: (0, 0)),
                  pl.BlockSpec((None, 1, d), mod_map), pl.BlockSpec((None, 1, d), mod_map),
                  pl.BlockSpec((d, tn), lambda i, j: (0, j))],
        out_specs=pl.BlockSpec((tm, tn), lambda i, j: (i, j)),
        out_shape=jax.ShapeDtypeStruct((n, wid), F32),
        scratch_shapes=[pltpu.VMEM((tm, d), BF16)],
        compiler_params=_params(2),
    )(x2, g.reshape(1, d), shift.reshape(m, 1, d), scale.reshape(m, 1, d), w)


NA_QROWS = 4


def _na_kernel(q_ref, k_ref, v_ref, kz_ref, vz_ref, bias_ref, o_ref, *, rows):
    step = pl.program_id(2)
    dh = HEAD_DIM
    band = NA_ROWS * GRID_W
    kz = kz_ref[...].astype(BF16)
    vz = vz_ref[...].astype(BF16)
    heads = LANES // dh
    units = [(qr, hh) for qr in range(NA_QROWS) for hh in range(heads)]
    scores, vbands = {}, {}
    for qr in range(NA_QROWS):
        r = step * NA_QROWS + qr
        start = jnp.clip(r - NA_ROWS // 2, 0, rows - NA_ROWS)
        dr0 = start - r + NA_ROWS - 1
        tok0 = pl.multiple_of(start * GRID_W, GRID_W)
        kb = k_ref[pl.ds(tok0, band), :].astype(BF16)
        vbands[qr] = v_ref[pl.ds(tok0, band), :].astype(BF16)
        q = (q_ref[pl.ds(qr * GRID_W, GRID_W), :] * (dh ** -0.5)).astype(BF16)
        for hh in range(heads):
            sl = slice(hh * dh, (hh + 1) * dh)
            scores[qr, hh] = (_dot_nt(q[:, sl], kb[:, sl]) + bias_ref[hh, dr0],
                              _dot_nt(q[:, sl], kz[:, sl]))
    probs = {}
    for u in units:
        s_nb, s_cx = scores[u]
        m = jnp.maximum(jnp.max(s_nb, axis=1, keepdims=True), jnp.max(s_cx, axis=1, keepdims=True))
        p_nb = jnp.exp(s_nb - m)
        p_cx = jnp.exp(s_cx - m)
        den = jnp.sum(p_nb, axis=1, keepdims=True) + jnp.sum(p_cx, axis=1, keepdims=True)
        probs[u] = (p_nb.astype(BF16), p_cx.astype(BF16), den)
    for qr in range(NA_QROWS):
        outs = []
        for hh in range(heads):
            sl = slice(hh * dh, (hh + 1) * dh)
            p_nb, p_cx, den = probs[qr, hh]
            o = (jnp.dot(p_nb, vbands[qr][:, sl], preferred_element_type=F32)
                 + jnp.dot(p_cx, vz[:, sl], preferred_element_type=F32))
            outs.append(o / den)
        o_ref[pl.ds(qr * GRID_W, GRID_W), :] = jnp.concatenate(outs, axis=1)


def na_band_bias(rpb):
    col = jnp.arange(GRID_W)
    col_start = jnp.clip(col - NA_COLS // 2, 0, GRID_W - NA_COLS)
    col_ok = (col[None, :] >= col_start[:, None]) & (col[None, :] < col_start[:, None] + NA_COLS)
    d_col = jnp.clip(col[None, :] - col[:, None], -(NA_COLS - 1), NA_COLS - 1) + NA_COLS - 1
    d_row = jnp.arange(NA_ROWS)[:, None] + jnp.arange(NA_ROWS)[None, :]
    b = rpb.astype(F32)[:, d_row][..., d_col]
    b = jnp.where(col_ok[None, None, None], b, NEG_INF)
    return b.transpose(0, 1, 3, 2, 4).reshape(rpb.shape[0], NA_ROWS, GRID_W, NA_ROWS * GRID_W)


def na_attention(px, pz, n_batch, seq, ctx_len, bias):
    rows = seq // GRID_W
    tq = NA_QROWS * GRID_W
    nsteps = rows // NA_QROWS
    heads_per_blk = LANES // HEAD_DIM
    return pl.pallas_call(
        functools.partial(_na_kernel, rows=rows),
        grid=(n_batch, NA_HEADS // heads_per_blk, nsteps),
        in_specs=[
            pl.BlockSpec((tq, LANES), lambda b, hp, s: (b * nsteps + s, COL_NA_Q // LANES + hp)),
            pl.BlockSpec((seq, LANES), lambda b, hp, s: (b, COL_NA_K // LANES + hp)),
            pl.BlockSpec((seq, LANES), lambda b, hp, s: (b, COL_NA_V // LANES + hp)),
            pl.BlockSpec((ctx_len, LANES), lambda b, hp, s: (b, COL_NA_K // LANES + hp)),
            pl.BlockSpec((ctx_len, LANES), lambda b, hp, s: (b, COL_NA_V // LANES + hp)),
            pl.BlockSpec((heads_per_blk, NA_ROWS, GRID_W, NA_ROWS * GRID_W), lambda b, hp, s: (hp, 0, 0, 0)),
        ],
        out_specs=pl.BlockSpec((tq, LANES), lambda b, hp, s: (b * nsteps + s, hp)),
        out_shape=jax.ShapeDtypeStruct((n_batch * seq, GROUP_WIDTH), F32),
        compiler_params=_params(3),
    )(px, px, px, pz, pz, bias)


def _swa_kernel(q_ref, kp_ref, kc_ref, kn_ref, vp_ref, vc_ref, vn_ref, kz_ref, vz_ref, sink_ref,
                cq_ref, sq_ref, ckp_ref, skp_ref, ckc_ref, skc_ref, ckn_ref, skn_ref, o_ref):
    n = pl.program_id(1)
    nb = pl.num_programs(1)
    dh = HEAD_DIM
    blk = SWA_BLOCK
    quarter = dh // 4
    group = SWA_HEADS // SWA_KV_HEADS
    q = _rope_lanes(q_ref[...], cq_ref[...], sq_ref[...], quarter) * (dh ** -0.5)
    kp = _rope_lanes(kp_ref[...], ckp_ref[...], skp_ref[...], quarter).astype(BF16)
    kc = _rope_lanes(kc_ref[...], ckc_ref[...], skc_ref[...], quarter).astype(BF16)
    kn = _rope_lanes(kn_ref[...], ckn_ref[...], skn_ref[...], quarter).astype(BF16)
    kz = kz_ref[...].astype(BF16)
    vp, vc, vn, vz = (r[...].astype(BF16) for r in (vp_ref, vc_ref, vn_ref, vz_ref))
    qi = lax.broadcasted_iota(jnp.int32, (group * blk, blk), 0) % blk
    kj = lax.broadcasted_iota(jnp.int32, (group * blk, blk), 1)
    ok_p = (kj >= qi) & (n > 0)
    ok_n = (kj <= qi) & (n < nb - 1)
    outs = []
    for hk in range(SWA_KV_HEADS):
        ks = slice(hk * dh, (hk + 1) * dh)
        qs = jnp.concatenate([q[:, (hk * group + g) * dh:(hk * group + g + 1) * dh] for g in range(group)],
                             axis=0).astype(BF16)
        sink = jnp.concatenate([jnp.full((blk, 1), 1.0, F32) * sink_ref[hk * group + g] for g in range(group)],
                               axis=0)
        s_p = jnp.where(ok_p, _dot_nt(qs, kp[:, ks]), NEG_INF)
        s_c = _dot_nt(qs, kc[:, ks])
        s_n = jnp.where(ok_n, _dot_nt(qs, kn[:, ks]), NEG_INF)
        s_z = _dot_nt(qs, kz[:, ks])
        m = jnp.maximum(jnp.maximum(jnp.max(s_p, axis=1, keepdims=True), jnp.max(s_c, axis=1, keepdims=True)),
                        jnp.maximum(jnp.max(s_n, axis=1, keepdims=True), jnp.max(s_z, axis=1, keepdims=True)))
        m = jnp.maximum(m, sink)
        e_p, e_c, e_n, e_z = (jnp.exp(s - m) for s in (s_p, s_c, s_n, s_z))
        den = (jnp.sum(e_p, axis=1, keepdims=True) + jnp.sum(e_c, axis=1, keepdims=True)
               + jnp.sum(e_n, axis=1, keepdims=True) + jnp.sum(e_z, axis=1, keepdims=True) + jnp.exp(sink - m))
        o = (jnp.dot(e_p.astype(BF16), vp[:, ks], preferred_element_type=F32)
             + jnp.dot(e_c.astype(BF16), vc[:, ks], preferred_element_type=F32)
             + jnp.dot(e_n.astype(BF16), vn[:, ks], preferred_element_type=F32)
             + jnp.dot(e_z.astype(BF16), vz[:, ks], preferred_element_type=F32)) / den
        outs += [o[g * blk:(g + 1) * blk, :] for g in range(group)]
    o_ref[...] = jnp.concatenate(outs, axis=1)


def swa_attention(px, pz, n_batch, seq, ctx_len, sink, rope_q, rope_k):
    blk = SWA_BLOCK
    nb = seq // blk
    prev = lambda b, n: b * nb + jnp.maximum(n - 1, 0)
    cur = lambda b, n: b * nb + n
    nxt = lambda b, n: b * nb + jnp.minimum(n + 1, nb - 1)
    kv = lambda off, f: pl.BlockSpec((blk, LANES), lambda b, n: (f(b, n), off // LANES))
    tab = lambda w, f: pl.BlockSpec((blk, w), lambda b, n: (f(0, n), 0))
    cq, sq = rope_q
    ck, sk = rope_k
    return pl.pallas_call(
        _swa_kernel,
        grid=(n_batch, nb),
        in_specs=[
            pl.BlockSpec((blk, GROUP_WIDTH), lambda b, n: (cur(b, n), COL_SWA_Q // GROUP_WIDTH)),
            kv(COL_SWA_K, prev), kv(COL_SWA_K, cur), kv(COL_SWA_K, nxt),
            kv(COL_SWA_V, prev), kv(COL_SWA_V, cur), kv(COL_SWA_V, nxt),
            pl.BlockSpec((ctx_len, LANES), lambda b, n: (b, COL_SWA_K // LANES)),
            pl.BlockSpec((ctx_len, LANES), lambda b, n: (b, COL_SWA_V // LANES)),
            pl.BlockSpec(memory_space=pltpu.SMEM),
            tab(GROUP_WIDTH, cur), tab(GROUP_WIDTH, cur), tab(LANES, prev), tab(LANES, prev),
            tab(LANES, cur), tab(LANES, cur), tab(LANES, nxt), tab(LANES, nxt),
        ],
        out_specs=pl.BlockSpec((blk, GROUP_WIDTH), lambda b, n: (cur(b, n), 0)),
        out_shape=jax.ShapeDtypeStruct((n_batch * seq, GROUP_WIDTH), F32),
        compiler_params=_params(2),
    )(px, px, px, px, px, px, px, pz, pz, sink, cq, sq, ck, sk, ck, sk, ck, sk)


def _ctx_attn_kernel(q_ref, k_ref, v_ref, sink_ref, o_ref, *, group, use_sink):
    dh = HEAD_DIM
    q = (q_ref[...] * (dh ** -0.5)).astype(BF16)
    k = k_ref[...].astype(BF16)
    v = v_ref[...].astype(BF16)
    outs = []
    for qh in range(q.shape[1] // dh):
        ks = slice((qh // group) * dh, (qh // group + 1) * dh)
        s = _dot_nt(q[:, qh * dh:(qh + 1) * dh], k[:, ks])
        m = jnp.max(s, axis=1, keepdims=True)
        if use_sink:
            m = jnp.maximum(m, sink_ref[qh])
        e = jnp.exp(s - m)
        den = jnp.sum(e, axis=1, keepdims=True)
        if use_sink:
            den = den + jnp.exp(sink_ref[qh] - m)
        outs.append(jnp.dot(e.astype(BF16), v[:, ks], preferred_element_type=F32) / den)
    o_ref[...] = jnp.concatenate(outs, axis=1)


def ctx_attention(pz, n_batch, ctx_len, cols, kv_width, sink):
    qc, kc, vc = cols
    use_sink = sink is not None
    if sink is None:
        sink = jnp.zeros((GROUP_WIDTH // HEAD_DIM,), F32)
    return pl.pallas_call(
        functools.partial(_ctx_attn_kernel, group=GROUP_WIDTH // kv_width, use_sink=use_sink),
        grid=(n_batch,),
        in_specs=[pl.BlockSpec((ctx_len, GROUP_WIDTH), lambda b: (b, qc // GROUP_WIDTH)),
                  pl.BlockSpec((ctx_len, kv_width), lambda b: (b, kc // kv_width)),
                  pl.BlockSpec((ctx_len, kv_width), lambda b: (b, vc // kv_width)),
                  pl.BlockSpec(memory_space=pltpu.SMEM)],
        out_specs=pl.BlockSpec((ctx_len, GROUP_WIDTH), lambda b: (b, 0)),
        out_shape=jax.ShapeDtypeStruct((n_batch * ctx_len, GROUP_WIDTH), F32),
        compiler_params=_params(1),
    )(pz, pz, pz, sink)


SCAN_ROWS = 256


def _split3(x):
    a = x.astype(BF16)
    r = x - a.astype(F32)
    b = r.astype(BF16)
    c = (r - b.astype(F32)).astype(BF16)
    return a, b, c


def _scan_kernel(*refs, heads, dk, dv, kind, rope, q_scale, k_scale):
    it = iter(refs)
    qf, kf, vf, qb, kb, vb = (next(it) for _ in range(6))
    if kind == "ret":
        lg = next(it)
    else:
        df, db, wup, bup = (next(it) for _ in range(4))
    if rope:
        cosf, sinf, cosb, sinb = (next(it) for _ in range(4))
    s0 = next(it)
    of, ob, sfin = next(it), next(it), next(it)
    st = next(it)

    s = pl.program_id(1)
    c = SCAN_CHUNK
    nch = SCAN_ROWS // c
    hk = heads * dk

    @pl.when(s == 0)
    def _():
        st[...] = s0[...]

    r_i = lax.broadcasted_iota(jnp.int32, (c, c), 0)
    c_i = lax.broadcasted_iota(jnp.int32, (c, c), 1)
    masks = (r_i >= c_i, c_i > r_i)

    if kind == "ret":
        pos = lax.broadcasted_iota(jnp.int32, (c, hk), 0).astype(F32)
        gcums = ((pos + 1.0) * lg[0:1, :], (float(c) - pos) * lg[1:2, :])
    else:
        rr = lax.broadcasted_iota(jnp.int32, (SCAN_ROWS, SCAN_ROWS), 0)
        cc = lax.broadcasted_iota(jnp.int32, (SCAN_ROWS, SCAN_ROWS), 1)
        same = (rr // c) == (cc // c)
        tris = (jnp.where(same & (rr >= cc), 1.0, 0.0).astype(BF16),
                jnp.where(same & (cc >= rr), 1.0, 0.0).astype(BF16))

        def gate_cum(d_ref, direction):
            pre = jnp.dot(d_ref[...].astype(BF16), wup[direction], preferred_element_type=F32) + bup[direction]
            g = -(jnp.maximum(-pre, 0.0) + jnp.log1p(jnp.exp(-jnp.abs(pre)))) / GLA_TAU
            return sum(jnp.dot(tris[direction], p, preferred_element_type=F32) for p in _split3(g))

        gcums = (gate_cum(df, 0), gate_cum(db, 1))

    def one(direction, q_ref, k_ref, v_ref, o_ref, cos_ref, sin_ref, ch):
        rows = pl.ds(ch * c, c)
        q = q_ref[rows, :]
        k = k_ref[rows, :]
        v = v_ref[rows, :]
        if rope:
            cs = jnp.concatenate([cos_ref[rows, :]] * heads, axis=1)
            sn = jnp.concatenate([sin_ref[rows, :]] * heads, axis=1)
            q = _rope_lanes(q, cs, sn, dk // 4)
            k = _rope_lanes(k, cs, sn, dk // 4)
        if q_scale != 1.0:
            q = q * q_scale
        if k_scale != 1.0:
            k = k * k_scale
        gcum = gcums[direction] if kind == "ret" else gcums[direction][ch * c:(ch + 1) * c, :]
        gtot = gcum[c - 1:c, :] if direction == 0 else gcum[0:1, :]
        q_rel = (q * jnp.exp(gcum - gtot)).astype(BF16)
        k_rel = (k * jnp.exp(gtot - gcum)).astype(BF16)
        q_dec = (q * jnp.exp(gcum)).astype(BF16)
        dec = jnp.exp(gtot)
        vb16 = v.astype(BF16)
        outs = []
        for hd in range(heads):
            ks = slice(hd * dk, (hd + 1) * dk)
            vs = slice(hd * dv, (hd + 1) * dv)
            a = jnp.where(masks[direction], _dot_nt(q_rel[:, ks], k_rel[:, ks]), 0.0)
            state = st[direction, hd]
            o = jnp.dot(a.astype(BF16), vb16[:, vs], preferred_element_type=F32)
            o = o + _dot_nt(q_dec[:, ks], state.astype(BF16))
            st[direction, hd] = dec[:, ks] * state + _dot_tn(vb16[:, vs], k_rel[:, ks])
            outs.append(o)
        o_ref[rows, :] = jnp.concatenate(outs, axis=1)

    for ch in range(nch):
        one(0, qf, kf, vf, of, cosf if rope else None, sinf if rope else None, ch)
        one(1, qb, kb, vb, ob, cosb if rope else None, sinb if rope else None, nch - 1 - ch)

    @pl.when(s == pl.num_programs(1) - 1)
    def _():
        sfin[...] = st[...]


def bidir_scan(p, n_batch, seq, cols, heads, dk, dv, kind, s0, *, lg=None, wup=None, bup=None,
               rope=None, q_scale=1.0, k_scale=1.0):
    t = SCAN_ROWS
    nblk = seq // t
    hk, hv = heads * dk, heads * dv
    qc, kc, vc = cols
    fwd = lambda w, off: pl.BlockSpec((t, w), lambda b, s: (b * nblk + s, off // w))
    bwd = lambda w, off: pl.BlockSpec((t, w), lambda b, s: (b * nblk + nblk - 1 - s, off // w))
    const = lambda shape: pl.BlockSpec(shape, lambda b, s: (0,) * len(shape))
    args = [p] * 6
    specs = [fwd(hk, qc), fwd(hk, kc), fwd(hv, vc), bwd(hk, qc), bwd(hk, kc), bwd(hv, vc)]
    if kind == "ret":
        args += [lg]
        specs += [const((2, hk))]
    else:
        args += [p, p, wup, bup]
        specs += [fwd(LANES, COL_GLA_D), bwd(LANES, COL_GLA_D), const(wup.shape), const(bup.shape)]
    if rope is not None:
        cos, sin = rope
        args += [cos, sin, cos, sin]
        specs += [pl.BlockSpec((t, dk), lambda b, s: (s, 0)), pl.BlockSpec((t, dk), lambda b, s: (s, 0)),
                  pl.BlockSpec((t, dk), lambda b, s: (nblk - 1 - s, 0)),
                  pl.BlockSpec((t, dk), lambda b, s: (nblk - 1 - s, 0))]
    args += [s0]
    state_spec = pl.BlockSpec((None, 2, heads, dv, dk), lambda b, s: (b, 0, 0, 0, 0))
    specs += [state_spec]
    n = n_batch * seq
    kern = functools.partial(_scan_kernel, heads=heads, dk=dk, dv=dv, kind=kind, rope=rope is not None,
                             q_scale=q_scale, k_scale=k_scale)
    return pl.pallas_call(
        kern,
        grid=(n_batch, nblk),
        in_specs=specs,
        out_specs=[pl.BlockSpec((t, hv), lambda b, s: (b * nblk + s, 0)),
                   pl.BlockSpec((t, hv), lambda b, s: (b * nblk + nblk - 1 - s, 0)),
                   state_spec],
        out_shape=[jax.ShapeDtypeStruct((n, hv), F32), jax.ShapeDtypeStruct((n, hv), F32),
                   jax.ShapeDtypeStruct((n_batch, 2, heads, dv, dk), F32)],
        scratch_shapes=[pltpu.VMEM((2, heads, dv, dk), F32)],
        compiler_params=_params(2),
    )(*args)


def _outproj_kernel(na_ref, rf_ref, rb_ref, rg_ref, gf_ref, gb_ref, gg_ref, sw_ref, gn_ref, w_ref, x_ref, mg_ref,
                    o_ref, *, head_w):
    ry = rf_ref[...] + rb_ref[...]
    gy = gf_ref[...] + gb_ref[...]
    r_out, g_out = [], []
    for hd in range(ry.shape[1] // head_w):
        sl = slice(hd * head_w, (hd + 1) * head_w)
        r = ry[:, sl]
        mu = jnp.mean(r, axis=-1, keepdims=True)
        var = jnp.mean(jnp.square(r - mu), axis=-1, keepdims=True)
        r_out.append((r - mu) * lax.rsqrt(var + EPS))
        gq = gy[:, sl]
        g_out.append(gq * lax.rsqrt(jnp.mean(gq * gq, axis=-1, keepdims=True) + EPS) * gn_ref[...])
    ret = jnp.concatenate(r_out, axis=1) * _silu(rg_ref[...])
    gla = jnp.concatenate(g_out, axis=1) * _silu(gg_ref[...])
    mix = jnp.concatenate([na_ref[...], ret, gla, sw_ref[...]], axis=1).astype(BF16)
    o_ref[...] = x_ref[...] + mg_ref[...] * jnp.dot(mix, w_ref[...], preferred_element_type=F32)


def outproj(na, rf, rb, gf, gb, sw, p, gla_norm_g, w_out, x2, mg, rows_per_mod):
    n, d = x2.shape
    gw = GROUP_WIDTH
    tm = 256
    m = mg.shape[0]
    row = lambda w_: pl.BlockSpec((tm, w_), lambda i: (i, 0))
    return pl.pallas_call(
        functools.partial(_outproj_kernel, head_w=RET_DV),
        grid=(n // tm,),
        in_specs=[row(gw), row(gw), row(gw), pl.BlockSpec((tm, gw), lambda i: (i, COL_RET_G // gw)),
                  row(gw), row(gw), pl.BlockSpec((tm, gw), lambda i: (i, COL_GLA_G // gw)), row(gw),
                  pl.BlockSpec((1, GLA_DV), lambda i: (0, 0)),
                  pl.BlockSpec((d, d), lambda i: (0, 0)), row(d),
                  pl.BlockSpec((None, 1, d), lambda i: (i // (rows_per_mod // tm), 0, 0))],
        out_specs=row(d),
        out_shape=jax.ShapeDtypeStruct((n, d), F32),
        compiler_params=_params(1),
    )(na, rf, rb, p, gf, gb, p, sw, gla_norm_g.reshape(1, GLA_DV), w_out, x2, mg.reshape(m, 1, d))


PEER_PAIRS = PEER_HEADS * PEER_TOPK
PEER_ROUTE_TOKENS = 128
PEER_EXPERT_TOKENS = 16
PEER_SUB = 8
PEER_FOLD = D_MODEL // PEER_SUB
INV_SQRT2 = 0.7071067811865476


def _topk_cols(s, payload=None, order=None):
    row = lax.broadcasted_iota(jnp.int32, s.shape, 0) if order is None else order
    vals, idxs = [], []
    for _ in range(PEER_TOPK):
        m = jnp.max(s, axis=0, keepdims=True)
        am = jnp.min(jnp.where(s == m, row, jnp.iinfo(jnp.int32).max), axis=0, keepdims=True)
        sel = row == am
        vals.append(m)
        idxs.append(am if payload is None else jnp.max(jnp.where(sel, payload, -1), axis=0, keepdims=True))
        s = jnp.where(sel, -jnp.inf, s)
    return jnp.concatenate(vals, axis=0), jnp.concatenate(idxs, axis=0)


def _staircase_candidates(v0, i0, v1, i1):
    k = PEER_TOPK
    assert k == 16
    t = v0.shape[1]
    r8 = lax.broadcasted_iota(jnp.int32, (8, t), 0)
    r16 = lax.broadcasted_iota(jnp.int32, (k, t), 0)

    def piece(a_sl, b_sl):
        return v0[a_sl, :] + v1[b_sl, :], i0[a_sl, :] * PEER_N_KEYS + i1[b_sl, :]

    one = lambda j: slice(j, j + 1)
    lo = slice(0, 8)
    pieces = [
        (one(0), slice(0, k), None, r16),
        (one(1), lo, None, k + r8),
        (one(2), lo, r8 <= 4, 2 * k + r8),
        (one(3), lo, r8 <= 3, 3 * k + r8),
        (slice(8, k), one(0), None, (r8 + 8) * k),
        (lo, one(0), r8 >= 4, r8 * k),
        (lo, one(1), r8 >= 4, r8 * k + 1),
        (lo, one(2), r8 == 4, r8 * k + 2),
    ]
    sums, ids, orders = [], [], []
    for a_sl, b_sl, keep, order in pieces:
        s, e = piece(a_sl, b_sl)
        sums.append(s if keep is None else jnp.where(keep, s, NEG_INF))
        ids.append(e)
        orders.append(order)
    return jnp.concatenate(sums, axis=0), jnp.concatenate(ids, axis=0), jnp.concatenate(orders, axis=0)


def _peer_route_kernel(x_ref, g_ref, shift_ref, scale_ref, wq_ref, sk_ref, h_ref, idx_ref, gate_ref):
    x = x_ref[...]
    y = x * lax.rsqrt(jnp.mean(x * x, axis=-1, keepdims=True) + EPS)
    h = (y * g_ref[...]) * (1.0 + scale_ref[...]) + shift_ref[...]
    h_ref[...] = h
    hb = h.astype(BF16)
    half = PEER_DK // 2

    def head_body(hd, carry):
        q = jnp.dot(hb, wq_ref[hd], preferred_element_type=F32)
        tops = []
        for p in range(2):
            qp = q[:, p * half:(p + 1) * half].astype(BF16)
            tops.append(_topk_cols(_dot_nt(sk_ref[p, hd], qp)))
        (v0, i0), (v1, i1) = tops
        best_s, best_e = _topk_cols(*_staircase_candidates(v0, i0, v1, i1))
        e = jnp.exp(best_s - best_s[0:1, :])
        gate_ref[hd] = e / jnp.sum(e, axis=0, keepdims=True)
        idx_ref[hd] = best_e
        return carry

    lax.fori_loop(0, PEER_HEADS, head_body, 0, unroll=4)


def peer_route(x2, g, shift, scale, rows_per_mod, wq_h, sk):
    n, d = x2.shape
    t = PEER_ROUTE_TOKENS
    m = shift.shape[0]
    mod_map = lambda i: (i // (rows_per_mod // t), 0, 0)
    return pl.pallas_call(
        _peer_route_kernel,
        grid=(n // t,),
        in_specs=[
            pl.BlockSpec((t, d), lambda i: (i, 0)),
            pl.BlockSpec((1, d), lambda i: (0, 0)),
            pl.BlockSpec((None, 1, d), mod_map),
            pl.BlockSpec((None, 1, d), mod_map),
            pl.BlockSpec((PEER_HEADS, d, PEER_DK), lambda i: (0, 0, 0)),
            pl.BlockSpec((2, PEER_HEADS, PEER_N_KEYS, PEER_DK // 2), lambda i: (0, 0, 0, 0)),
        ],
        out_specs=[
            pl.BlockSpec((t, d), lambda i: (i, 0)),
            pl.BlockSpec((PEER_HEADS, PEER_TOPK, t), lambda i: (0, 0, i)),
            pl.BlockSpec((PEER_HEADS, PEER_TOPK, t), lambda i: (0, 0, i)),
        ],
        out_shape=[
            jax.ShapeDtypeStruct((n, d), F32),
            jax.ShapeDtypeStruct((PEER_HEADS, PEER_TOPK, n), jnp.int32),
            jax.ShapeDtypeStruct((PEER_HEADS, PEER_TOPK, n), F32),
        ],
        compiler_params=_params(1),
    )(x2, g.reshape(1, d), shift.reshape(m, 1, d), scale.reshape(m, 1, d), wq_h, sk)


def pack_experts(u, v):
    ub = lax.bitcast_convert_type(u.astype(BF16), jnp.uint16).astype(jnp.uint32)
    vb = lax.bitcast_convert_type(v.astype(BF16), jnp.uint16).astype(jnp.uint32)
    words = lax.bitcast_convert_type((ub << 16) | vb, jnp.int32)
    return words.reshape(u.shape[0], PEER_SUB, PEER_FOLD)


def _peer_expert_kernel(idx0_ref, idxn_ref, h_ref, gate_ref, x_ref, og_ref, uv_hbm, o_ref,
                        buf_even, buf_odd, sem, hbuf, pbuf, wbuf, ybuf):
    i = pl.program_id(0)
    n = pl.num_programs(0)
    tb = PEER_EXPERT_TOKENS
    fold = PEER_FOLD
    bufs = (buf_even, buf_odd)

    def slab_copy(idx_ref, j, r, parity):
        e = idx_ref[0, 0, j * PEER_PAIRS + r]
        return pltpu.make_async_copy(uv_hbm.at[e], bufs[parity].at[j, r], sem.at[parity, j])

    def wait_token(j, parity):
        pltpu.make_async_copy(uv_hbm.at[pl.ds(0, PEER_PAIRS)], bufs[parity].at[j], sem.at[parity, j]).wait()

    @pl.when(i == 0)
    def _():
        def prime(j, carry):
            for r in range(PEER_PAIRS):
                slab_copy(idx0_ref, j, r, 0).start(priority=r % 2)
            return carry
        lax.fori_loop(0, tb, prime, 0)

    lane = lax.broadcasted_iota(jnp.int32, (PEER_TOPK, PEER_ROUTE_TOKENS), 1)
    lane0 = (i % (PEER_ROUTE_TOKENS // tb)) * tb

    for s in range(PEER_SUB):
        for c in range(fold // LANES):
            hbuf[c, pl.ds(s * tb, tb), :] = h_ref[:, s * fold + c * LANES:s * fold + (c + 1) * LANES]

    def token_body(j, carry, parity, prefetch):
        buf = bufs[parity]
        wait_token(j, parity)
        hj = jnp.concatenate([hbuf[c, pl.ds(j, PEER_SUB, stride=tb), :] for c in range(fold // LANES)],
                             axis=1)
        for p in range(PEER_PAIRS):
            u = lax.bitcast_convert_type(buf[j, p] & jnp.int32(-65536), F32)
            prod = u * hj
            pbuf[pl.ds(p * PEER_SUB, PEER_SUB), :] = prod[:, :LANES] + prod[:, LANES:]
        part = pbuf[pl.ds(0, PEER_PAIRS, stride=PEER_SUB), :]
        for s in range(1, PEER_SUB):
            part = part + pbuf[pl.ds(s, PEER_PAIRS, stride=PEER_SUB), :]
        sc = jnp.sum(part, axis=1, keepdims=True)
        act = 0.5 * sc * (1.0 + lax.erf(sc * INV_SQRT2))
        g = jnp.concatenate([jnp.sum(jnp.where(lane == lane0 + j, gate_ref[hd], 0.0), axis=1, keepdims=True)
                             for hd in range(PEER_HEADS)], axis=0)
        wbuf[...] = jnp.broadcast_to(g * act, (PEER_PAIRS, LANES))
        accs = [jnp.zeros((PEER_SUB, fold), F32) for _ in range(4)]
        for p in range(PEER_PAIRS):
            if prefetch:
                slab_copy(idxn_ref, j, p, 1 - parity).start(priority=p % 2)
            v = lax.bitcast_convert_type(buf[j, p] << 16, F32)
            wp = jnp.broadcast_to(wbuf[p:p + 1, :], (PEER_SUB, LANES))
            accs[p % 4] = accs[p % 4] + v * jnp.concatenate([wp, wp], axis=1)
        yj = (accs[0] + accs[1]) + (accs[2] + accs[3])
        for c in range(fold // LANES):
            ybuf[c, pl.ds(j * PEER_SUB, PEER_SUB), :] = yj[:, c * LANES:(c + 1) * LANES]
        return carry

    for parity in range(2):
        for prefetch in (True, False):
            @pl.when((i % 2 == parity) & ((i + 1 < n) == prefetch))
            def _(parity=parity, prefetch=prefetch):
                lax.fori_loop(0, tb, functools.partial(token_body, parity=parity, prefetch=prefetch), 0)

    for s in range(PEER_SUB):
        for c in range(fold // LANES):
            sl = slice(s * fold + c * LANES, s * fold + (c + 1) * LANES)
            o_ref[:, sl] = x_ref[:, sl] + og_ref[:, sl] * ybuf[c, pl.ds(s, tb, stride=PEER_SUB), :]


def peer_expert(h, idx, gate, x2, out_gate, rows_per_mod, uv):
    n, d = h.shape
    tb = PEER_EXPERT_TOKENS
    nb = n // tb
    m = out_gate.shape[0]
    rows = tb * PEER_PAIRS
    idx_rows = idx.reshape(PEER_PAIRS, n).T.reshape(nb, 1, rows)
    gate_blocks = PEER_ROUTE_TOKENS // tb
    return pl.pallas_call(
        _peer_expert_kernel,
        grid=(nb,),
        in_specs=[
            pl.BlockSpec((1, 1, rows), lambda i: (0, 0, 0), memory_space=pltpu.SMEM),
            pl.BlockSpec((1, 1, rows), lambda i: (jnp.minimum(i + 1, nb - 1), 0, 0), memory_space=pltpu.SMEM),
            pl.BlockSpec((tb, d), lambda i: (i, 0)),
            pl.BlockSpec((PEER_HEADS, PEER_TOPK, PEER_ROUTE_TOKENS), lambda i: (0, 0, i // gate_blocks)),
            pl.BlockSpec((tb, d), lambda i: (i, 0)),
            pl.BlockSpec((None, 1, d), lambda i: (i // (rows_per_mod // tb), 0, 0)),
            pl.BlockSpec(memory_space=pl.ANY),
        ],
        out_specs=pl.BlockSpec((tb, d), lambda i: (i, 0)),
        out_shape=jax.ShapeDtypeStruct((n, d), F32),
        scratch_shapes=[pltpu.VMEM((tb, PEER_PAIRS, PEER_SUB, PEER_FOLD), jnp.int32),
                        pltpu.VMEM((tb, PEER_PAIRS, PEER_SUB, PEER_FOLD), jnp.int32),
                        pltpu.SemaphoreType.DMA((2, tb)),
                        pltpu.VMEM((PEER_FOLD // LANES, PEER_SUB * tb, LANES), F32),
                        pltpu.VMEM((PEER_PAIRS * PEER_SUB, LANES), F32),
                        pltpu.VMEM((PEER_PAIRS, LANES), F32),
                        pltpu.VMEM((PEER_FOLD // LANES, tb * PEER_SUB, LANES), F32)],
        compiler_params=pltpu.CompilerParams(dimension_semantics=("arbitrary",), vmem_limit_bytes=VMEM_LIMIT,
                                             disable_bounds_checks=True),
    )(idx_rows, idx_rows, h, gate, x2, out_gate.reshape(m, 1, d), uv)


def peer_residual(x2, g, shift, scale, out_gate, rows_per_mod, wq_h, sk, uv):
    h, idx, gate = peer_route(x2, g, shift, scale, rows_per_mod, wq_h, sk)
    return peer_expert(h, idx, gate, x2, out_gate, rows_per_mod, uv)


def _final_norm_kernel(x_ref, g_ref, o_ref):
    x = x_ref[...]
    y = x * lax.rsqrt(jnp.mean(x * x, axis=-1, keepdims=True) + EPS)
    o_ref[...] = y * g_ref[...]


def final_norm(x2, g):
    n, d = x2.shape
    tm = 512
    return pl.pallas_call(
        _final_norm_kernel,
        grid=(n // tm,),
        in_specs=[pl.BlockSpec((tm, d), lambda i: (i, 0)), pl.BlockSpec((1, d), lambda i: (0, 0))],
        out_specs=pl.BlockSpec((tm, d), lambda i: (i, 0)),
        out_shape=jax.ShapeDtypeStruct((n, d), x2.dtype),
        compiler_params=_params(1),
    )(x2, g.reshape(1, d))


def _mixers(p, pz, n_batch, seq, ctx_len, is_ctx, tables, prm):
    if is_ctx:
        na = ctx_attention(p, n_batch, seq, (COL_NA_Q, COL_NA_K, COL_NA_V), GROUP_WIDTH, None)
        sw = ctx_attention(p, n_batch, seq, (COL_SWA_Q, COL_SWA_K, COL_SWA_V), SWA_KV_HEADS * HEAD_DIM, prm["sink"])
    else:
        na = na_attention(p, pz, n_batch, seq, ctx_len, prm["na_bias"])
        sw = swa_attention(p, pz, n_batch, seq, ctx_len, prm["sink"], tables["swa_q"], tables["swa_k"])
    rf, rb, rs = bidir_scan(p, n_batch, seq, (COL_RET_Q, COL_RET_K, COL_RET_V), RET_HEADS, RET_DK, RET_DV, "ret",
                            prm["ret_s0"], lg=prm["ret_lg"], rope=None if is_ctx else tables["ret"],
                            k_scale=RET_DK ** -0.5)
    gf, gb, gs = bidir_scan(p, n_batch, seq, (COL_GLA_Q, COL_GLA_K, COL_GLA_V), GLA_HEADS, GLA_DK, GLA_DV, "gla",
                            prm["gla_s0"], wup=prm["gla_wup"], bup=prm["gla_bup"], q_scale=GLA_DK ** -0.5)
    return (na, rf, rb, gf, gb, sw), (rs, gs)


def kernel(x, c, ctx, c_ctx, w_ada, b_ada, norm_attn_g, norm_ffn_g, w_in, na_rpb, ret_log_gamma,
           gla_w_gate_up, gla_b_gate, gla_norm_g, swa_sink, w_out, peer_w_q, peer_sub_keys,
           peer_u, peer_v, final_g):
    bsz, slen, d = x.shape
    zlen = ctx.shape[1]
    x2 = x.reshape(bsz * slen, d)
    z2 = ctx.reshape(bsz * zlen, d)
    tables = {"ret": rope_lane_tables(slen, RET_DK, 1),
              "swa_q": rope_lane_tables(slen, HEAD_DIM, SWA_HEADS),
              "swa_k": rope_lane_tables(slen, HEAD_DIM, SWA_KV_HEADS)}
    c_rows = jnp.zeros((8, d), F32).at[:bsz].set(c).at[bsz].set(c_ctx)
    for layer in range(DEPTH):
        has_next = layer < DEPTH - 1
        mod = adaln(c_rows, w_ada[layer], b_ada[layer])
        mx = [mod[:bsz, k * d:(k + 1) * d] for k in range(6)]
        mz = [mod[bsz:bsz + 1, k * d:(k + 1) * d] for k in range(6)]

        wi = w_in[layer]
        wp = jnp.concatenate([wi[:, :REF_COL_GLA_D], wi[:, REF_COL_SWA_Q:], wi[:, REF_COL_GLA_D:REF_COL_SWA_Q],
                              jnp.zeros((d, PROJ_WIDTH - REF_D_IN), F32)], axis=1).astype(BF16)
        px = modproj(x2, norm_attn_g[layer], mx[0], mx[1], slen, wp)
        pz = modproj(z2, norm_attn_g[layer], mz[0], mz[1], bsz * zlen, wp)

        wup = (jnp.zeros((2, LANES, GLA_HEADS * GLA_DK), F32)
               .at[0, :GLA_RANK].set(gla_w_gate_up[layer, 0])
               .at[1, GLA_RANK:2 * GLA_RANK].set(gla_w_gate_up[layer, 1])).astype(BF16)
        prm = {"na_bias": na_band_bias(na_rpb[layer]), "sink": swa_sink[layer],
               "ret_lg": jnp.repeat(ret_log_gamma[layer], RET_DK, axis=1),
               "gla_wup": wup, "gla_bup": gla_b_gate[layer].reshape(2, 1, GLA_HEADS * GLA_DK),
               "ret_s0": jnp.zeros((bsz, 2, RET_HEADS, RET_DV, RET_DK), F32),
               "gla_s0": jnp.zeros((bsz, 2, GLA_HEADS, GLA_DV, GLA_DK), F32)}
        mix_z, (ret_s, gla_s) = _mixers(pz, pz, bsz, zlen, zlen, True, tables, prm)
        prm["ret_s0"], prm["gla_s0"] = ret_s, gla_s
        mix_x, _ = _mixers(px, pz, bsz, slen, zlen, False, tables, prm)

        wo = w_out[layer].astype(BF16)
        x2 = outproj(*mix_x, px, gla_norm_g[layer], wo, x2, mx[2], slen)

        uv = pack_experts(peer_u[layer], peer_v[layer])
        wq_h = peer_w_q[layer].reshape(d, PEER_HEADS, PEER_DK).transpose(1, 0, 2).astype(BF16)
        sk = peer_sub_keys[layer].astype(BF16)
        x2 = peer_residual(x2, norm_ffn_g[layer], mx[3], mx[4], mx[5], slen, wq_h, sk, uv)
        if has_next:
            z2 = outproj(*mix_z, pz, gla_norm_g[layer], wo, z2, mz[2], bsz * zlen)
            z2 = peer_residual(z2, norm_ffn_g[layer], mz[3], mz[4], mz[5], bsz * zlen, wq_h, sk, uv)
    return final_norm(x2, final_g).reshape(bsz, slen, d)
```

```python
import functools

import jax
import jax.numpy as jnp
import numpy as np
from jax import lax
from jax.experimental import pallas as pl
from jax.experimental.pallas import tpu as pltpu

D_MODEL = 2048
DEPTH = 2
GRID_W = 64
EPS = 1e-6
ROPE_BASE = 10000.0

GROUP_WIDTH = D_MODEL // 4
HEAD_DIM = 64
NA_HEADS = GROUP_WIDTH // HEAD_DIM
NA_ROWS = 8
NA_COLS = 16
RET_HEADS = 4
RET_DK = GROUP_WIDTH // RET_HEADS
RET_DV = GROUP_WIDTH // RET_HEADS
GLA_HEADS = 4
GLA_DV = GROUP_WIDTH // GLA_HEADS
GLA_DK = GLA_DV // 2
GLA_RANK = 16
GLA_TAU = 16.0
SWA_HEADS = GROUP_WIDTH // HEAD_DIM
SWA_KV_HEADS = SWA_HEADS // 4
SWA_WINDOW = 128
SWA_BLOCK = 128
SCAN_CHUNK = 64
PEER_HEADS = 8
PEER_N_KEYS = 128
PEER_N_EXPERTS = PEER_N_KEYS * PEER_N_KEYS
PEER_DK = 256
PEER_TOPK = 16

LANES = 128
VMEM_LIMIT = 48 * 1024 * 1024
PEER_FUSED_VMEM_LIMIT = 56 * 1024 * 1024
BF16 = jnp.bfloat16
F32 = jnp.float32
NEG_INF = float("-inf")

COL_NA_Q, COL_NA_K, COL_NA_V = 0, 512, 1024
COL_RET_Q, COL_RET_K, COL_RET_V, COL_RET_G = 1536, 2048, 2560, 3072
COL_GLA_Q, COL_GLA_K, COL_GLA_V, COL_GLA_G = 3584, 3840, 4096, 4608
COL_SWA_Q, COL_SWA_K, COL_SWA_V = 5120, 5632, 5760
COL_GLA_D = 5888
REF_COL_GLA_D, REF_COL_SWA_Q, REF_D_IN = 5120, 5152, 5920
PROJ_WIDTH = 6144


def _silu(x):
    return x / (1.0 + jnp.exp(-x))


def _dot_nt(a, b):
    return lax.dot_general(a, b, (((1,), (1,)), ((), ())), preferred_element_type=F32)


def _dot_tn(a, b):
    return lax.dot_general(a, b, (((0,), (0,)), ((), ())), preferred_element_type=F32)


def _params(n_axes):
    return pltpu.CompilerParams(dimension_semantics=("arbitrary",) * n_axes, vmem_limit_bytes=VMEM_LIMIT)


def _rope_lanes(x, cs, sn, quarter):
    n = x.shape[-1]
    lane = lax.broadcasted_iota(jnp.int32, x.shape, x.ndim - 1)
    first = (lane % (2 * quarter)) < quarter
    swapped = jnp.where(first, pltpu.roll(x, n - quarter, x.ndim - 1), pltpu.roll(x, quarter, x.ndim - 1))
    return x * cs + swapped * sn


def rope_lane_tables(length, dh, copies):
    t = jnp.arange(length)
    pos = jnp.stack([t // GRID_W, t % GRID_W], axis=-1).astype(F32)
    quarter = dh // 4
    inv = ROPE_BASE ** (-jnp.arange(quarter, dtype=F32) / quarter)
    ang = pos[:, :, None] * inv
    cos, sin = jnp.cos(ang), jnp.sin(ang)
    cl = jnp.concatenate([cos[:, 0], cos[:, 0], cos[:, 1], cos[:, 1]], axis=-1)
    sl = jnp.concatenate([-sin[:, 0], sin[:, 0], -sin[:, 1], sin[:, 1]], axis=-1)
    return jnp.tile(cl, (1, copies)), jnp.tile(sl, (1, copies))


def _adaln_kernel(c_ref, w_ref, b_ref, o_ref):
    a = _silu(c_ref[...]).astype(BF16)
    o_ref[...] = jnp.dot(a, w_ref[...].astype(BF16), preferred_element_type=F32) + b_ref[...]


def adaln(c_rows, w, b):
    r, d = c_rows.shape
    m = w.shape[1]
    tn = 1024
    return pl.pallas_call(
        _adaln_kernel,
        grid=(m // tn,),
        in_specs=[pl.BlockSpec((r, d), lambda j: (0, 0)), pl.BlockSpec((d, tn), lambda j: (0, j)),
                  pl.BlockSpec((1, tn), lambda j: (0, j))],
        out_specs=pl.BlockSpec((r, tn), lambda j: (0, j)),
        out_shape=jax.ShapeDtypeStruct((r, m), F32),
        compiler_params=_params(1),
    )(c_rows, w, b.reshape(1, m))


def _modproj_kernel(x_ref, g_ref, shift_ref, scale_ref, w_ref, o_ref, hb_ref):
    @pl.when(pl.program_id(1) == 0)
    def _():
        x = x_ref[...]
        y = x * lax.rsqrt(jnp.mean(x * x, axis=-1, keepdims=True) + EPS)
        hb_ref[...] = ((y * g_ref[...]) * (1.0 + scale_ref[...]) + shift_ref[...]).astype(BF16)

    o_ref[...] = jnp.dot(hb_ref[...], w_ref[...], preferred_element_type=F32)


def modproj(x2, g, shift, scale, rows_per_mod, w):
    n, d = x2.shape
    wid = w.shape[1]
    tm = min(512, rows_per_mod)
    tn = 2048
    m = shift.shape[0]
    mod_map = lambda i, j: (i // (rows_per_mod // tm), 0, 0)
    return pl.pallas_call(
        _modproj_kernel,
        grid=(n // tm, wid // tn),
        in_specs=[pl.BlockSpec((tm, d), lambda i, j: (i, 0)), pl.BlockSpec((1, d), lambda i, j: (0, 0)),
                  pl.BlockSpec((None, 1, d), mod_map), pl.BlockSpec((None, 1, d), mod_map),
                  pl.BlockSpec((d, tn), lambda i, j: (0, j))],
        out_specs=pl.BlockSpec((tm, tn), lambda i, j: (i, j)),
        out_shape=jax.ShapeDtypeStruct((n, wid), F32),
        scratch_shapes=[pltpu.VMEM((tm, d), BF16)],
        compiler_params=_params(2),
    )(x2, g.reshape(1, d), shift.reshape(m, 1, d), scale.reshape(m, 1, d), w)


NA_QROWS = 4


def _na_kernel(q_ref, k_ref, v_ref, kz_ref, vz_ref, bias_ref, o_ref, *, rows):
    step = pl.program_id(2)
    dh = HEAD_DIM
    band = NA_ROWS * GRID_W
    kz = kz_ref[...].astype(BF16)
    vz = vz_ref[...].astype(BF16)
    heads = LANES // dh
    units = [(qr, hh) for qr in range(NA_QROWS) for hh in range(heads)]
    scores, vbands = {}, {}
    for qr in range(NA_QROWS):
        r = step * NA_QROWS + qr
        start = jnp.clip(r - NA_ROWS // 2, 0, rows - NA_ROWS)
        dr0 = start - r + NA_ROWS - 1
        tok0 = pl.multiple_of(start * GRID_W, GRID_W)
        kb = k_ref[pl.ds(tok0, band), :].astype(BF16)
        vbands[qr] = v_ref[pl.ds(tok0, band), :].astype(BF16)
        q = (q_ref[pl.ds(qr * GRID_W, GRID_W), :] * (dh ** -0.5)).astype(BF16)
        for hh in range(heads):
            sl = slice(hh * dh, (hh + 1) * dh)
            scores[qr, hh] = (_dot_nt(q[:, sl], kb[:, sl]) + bias_ref[hh, dr0],
                              _dot_nt(q[:, sl], kz[:, sl]))
    probs = {}
    for u in units:
        s_nb, s_cx = scores[u]
        m = jnp.maximum(jnp.max(s_nb, axis=1, keepdims=True), jnp.max(s_cx, axis=1, keepdims=True))
        p_nb = jnp.exp(s_nb - m)
        p_cx = jnp.exp(s_cx - m)
        den = jnp.sum(p_nb, axis=1, keepdims=True) + jnp.sum(p_cx, axis=1, keepdims=True)
        probs[u] = (p_nb.astype(BF16), p_cx.astype(BF16), den)
    for qr in range(NA_QROWS):
        outs = []
        for hh in range(heads):
            sl = slice(hh * dh, (hh + 1) * dh)
            p_nb, p_cx, den = probs[qr, hh]
            o = (jnp.dot(p_nb, vbands[qr][:, sl], preferred_element_type=F32)
                 + jnp.dot(p_cx, vz[:, sl], preferred_element_type=F32))
            outs.append(o / den)
        o_ref[pl.ds(qr * GRID_W, GRID_W), :] = jnp.concatenate(outs, axis=1)


def na_band_bias(rpb):
    col = jnp.arange(GRID_W)
    col_start = jnp.clip(col - NA_COLS // 2, 0, GRID_W - NA_COLS)
    col_ok = (col[None, :] >= col_start[:, None]) & (col[None, :] < col_start[:, None] + NA_COLS)
    d_col = jnp.clip(col[None, :] - col[:, None], -(NA_COLS - 1), NA_COLS - 1) + NA_COLS - 1
    d_row = jnp.arange(NA_ROWS)[:, None] + jnp.arange(NA_ROWS)[None, :]
    b = rpb.astype(F32)[:, d_row][..., d_col]
    b = jnp.where(col_ok[None, None, None], b, NEG_INF)
    return b.transpose(0, 1, 3, 2, 4).reshape(rpb.shape[0], NA_ROWS, GRID_W, NA_ROWS * GRID_W)


def na_attention(px, pz, n_batch, seq, ctx_len, bias):
    rows = seq // GRID_W
    tq = NA_QROWS * GRID_W
    nsteps = rows // NA_QROWS
    heads_per_blk = LANES // HEAD_DIM
    return pl.pallas_call(
        functools.partial(_na_kernel, rows=rows),
        grid=(n_batch, NA_HEADS // heads_per_blk, nsteps),
        in_specs=[
            pl.BlockSpec((tq, LANES), lambda b, hp, s: (b * nsteps + s, COL_NA_Q // LANES + hp)),
            pl.BlockSpec((seq, LANES), lambda b, hp, s: (b, COL_NA_K // LANES + hp)),
            pl.BlockSpec((seq, LANES), lambda b, hp, s: (b, COL_NA_V // LANES + hp)),
            pl.BlockSpec((ctx_len, LANES), lambda b, hp, s: (b, COL_NA_K // LANES + hp)),
            pl.BlockSpec((ctx_len, LANES), lambda b, hp, s: (b, COL_NA_V // LANES + hp)),
            pl.BlockSpec((heads_per_blk, NA_ROWS, GRID_W, NA_ROWS * GRID_W), lambda b, hp, s: (hp, 0, 0, 0)),
        ],
        out_specs=pl.BlockSpec((tq, LANES), lambda b, hp, s: (b * nsteps + s, hp)),
        out_shape=jax.ShapeDtypeStruct((n_batch * seq, GROUP_WIDTH), F32),
        compiler_params=_params(3),
    )(px, px, px, pz, pz, bias)


def _swa_kernel(q_ref, kp_ref, kc_ref, kn_ref, vp_ref, vc_ref, vn_ref, kz_ref, vz_ref, sink_ref,
                cq_ref, sq_ref, ckp_ref, skp_ref, ckc_ref, skc_ref, ckn_ref, skn_ref, o_ref):
    n = pl.program_id(1)
    nb = pl.num_programs(1)
    dh = HEAD_DIM
    blk = SWA_BLOCK
    quarter = dh // 4
    group = SWA_HEADS // SWA_KV_HEADS
    q = _rope_lanes(q_ref[...], cq_ref[...], sq_ref[...], quarter) * (dh ** -0.5)
    kp = _rope_lanes(kp_ref[...], ckp_ref[...], skp_ref[...], quarter).astype(BF16)
    kc = _rope_lanes(kc_ref[...], ckc_ref[...], skc_ref[...], quarter).astype(BF16)
    kn = _rope_lanes(kn_ref[...], ckn_ref[...], skn_ref[...], quarter).astype(BF16)
    kz = kz_ref[...].astype(BF16)
    vp, vc, vn, vz = (r[...].astype(BF16) for r in (vp_ref, vc_ref, vn_ref, vz_ref))
    qi = lax.broadcasted_iota(jnp.int32, (group * blk, blk), 0) % blk
    kj = lax.broadcasted_iota(jnp.int32, (group * blk, blk), 1)
    ok_p = (kj >= qi) & (n > 0)
    ok_n = (kj <= qi) & (n < nb - 1)
    outs = []
    for hk in range(SWA_KV_HEADS):
        ks = slice(hk * dh, (hk + 1) * dh)
        qs = jnp.concatenate([q[:, (hk * group + g) * dh:(hk * group + g + 1) * dh] for g in range(group)],
                             axis=0).astype(BF16)
        sink = jnp.concatenate([jnp.full((blk, 1), 1.0, F32) * sink_ref[hk * group + g] for g in range(group)],
                               axis=0)
        s_p = jnp.where(ok_p, _dot_nt(qs, kp[:, ks]), NEG_INF)
        s_c = _dot_nt(qs, kc[:, ks])
        s_n = jnp.where(ok_n, _dot_nt(qs, kn[:, ks]), NEG_INF)
        s_z = _dot_nt(qs, kz[:, ks])
        m = jnp.maximum(jnp.maximum(jnp.max(s_p, axis=1, keepdims=True), jnp.max(s_c, axis=1, keepdims=True)),
                        jnp.maximum(jnp.max(s_n, axis=1, keepdims=True), jnp.max(s_z, axis=1, keepdims=True)))
        m = jnp.maximum(m, sink)
        e_p, e_c, e_n, e_z = (jnp.exp(s - m) for s in (s_p, s_c, s_n, s_z))
        den = (jnp.sum(e_p, axis=1, keepdims=True) + jnp.sum(e_c, axis=1, keepdims=True)
               + jnp.sum(e_n, axis=1, keepdims=True) + jnp.sum(e_z, axis=1, keepdims=True) + jnp.exp(sink - m))
        o = (jnp.dot(e_p.astype(BF16), vp[:, ks], preferred_element_type=F32)
             + jnp.dot(e_c.astype(BF16), vc[:, ks], preferred_element_type=F32)
             + jnp.dot(e_n.astype(BF16), vn[:, ks], preferred_element_type=F32)
             + jnp.dot(e_z.astype(BF16), vz[:, ks], preferred_element_type=F32)) / den
        outs += [o[g * blk:(g + 1) * blk, :] for g in range(group)]
    o_ref[...] = jnp.concatenate(outs, axis=1)


def swa_attention(px, pz, n_batch, seq, ctx_len, sink, rope_q, rope_k):
    blk = SWA_BLOCK
    nb = seq // blk
    prev = lambda b, n: b * nb + jnp.maximum(n - 1, 0)
    cur = lambda b, n: b * nb + n
    nxt = lambda b, n: b * nb + jnp.minimum(n + 1, nb - 1)
    kv = lambda off, f: pl.BlockSpec((blk, LANES), lambda b, n: (f(b, n), off // LANES))
    tab = lambda w, f: pl.BlockSpec((blk, w), lambda b, n: (f(0, n), 0))
    cq, sq = rope_q
    ck, sk = rope_k
    return pl.pallas_call(
        _swa_kernel,
        grid=(n_batch, nb),
        in_specs=[
            pl.BlockSpec((blk, GROUP_WIDTH), lambda b, n: (cur(b, n), COL_SWA_Q // GROUP_WIDTH)),
            kv(COL_SWA_K, prev), kv(COL_SWA_K, cur), kv(COL_SWA_K, nxt),
            kv(COL_SWA_V, prev), kv(COL_SWA_V, cur), kv(COL_SWA_V, nxt),
            pl.BlockSpec((ctx_len, LANES), lambda b, n: (b, COL_SWA_K // LANES)),
            pl.BlockSpec((ctx_len, LANES), lambda b, n: (b, COL_SWA_V // LANES)),
            pl.BlockSpec(memory_space=pltpu.SMEM),
            tab(GROUP_WIDTH, cur), tab(GROUP_WIDTH, cur), tab(LANES, prev), tab(LANES, prev),
            tab(LANES, cur), tab(LANES, cur), tab(LANES, nxt), tab(LANES, nxt),
        ],
        out_specs=pl.BlockSpec((blk, GROUP_WIDTH), lambda b, n: (cur(b, n), 0)),
        out_shape=jax.ShapeDtypeStruct((n_batch * seq, GROUP_WIDTH), F32),
        compiler_params=_params(2),
    )(px, px, px, px, px, px, px, pz, pz, sink, cq, sq, ck, sk, ck, sk, ck, sk)


def _ctx_attn_kernel(q_ref, k_ref, v_ref, sink_ref, o_ref, *, group, use_sink):
    dh = HEAD_DIM
    q = (q_ref[...] * (dh ** -0.5)).astype(BF16)
    k = k_ref[...].astype(BF16)
    v = v_ref[...].astype(BF16)
    outs = []
    for qh in range(q.shape[1] // dh):
        ks = slice((qh // group) * dh, (qh // group + 1) * dh)
        s = _dot_nt(q[:, qh * dh:(qh + 1) * dh], k[:, ks])
        m = jnp.max(s, axis=1, keepdims=True)
        if use_sink:
            m = jnp.maximum(m, sink_ref[qh])
        e = jnp.exp(s - m)
        den = jnp.sum(e, axis=1, keepdims=True)
        if use_sink:
            den = den + jnp.exp(sink_ref[qh] - m)
        outs.append(jnp.dot(e.astype(BF16), v[:, ks], preferred_element_type=F32) / den)
    o_ref[...] = jnp.concatenate(outs, axis=1)


def ctx_attention(pz, n_batch, ctx_len, cols, kv_width, sink):
    qc, kc, vc = cols
    use_sink = sink is not None
    if sink is None:
        sink = jnp.zeros((GROUP_WIDTH // HEAD_DIM,), F32)
    return pl.pallas_call(
        functools.partial(_ctx_attn_kernel, group=GROUP_WIDTH // kv_width, use_sink=use_sink),
        grid=(n_batch,),
        in_specs=[pl.BlockSpec((ctx_len, GROUP_WIDTH), lambda b: (b, qc // GROUP_WIDTH)),
                  pl.BlockSpec((ctx_len, kv_width), lambda b: (b, kc // kv_width)),
                  pl.BlockSpec((ctx_len, kv_width), lambda b: (b, vc // kv_width)),
                  pl.BlockSpec(memory_space=pltpu.SMEM)],
        out_specs=pl.BlockSpec((ctx_len, GROUP_WIDTH), lambda b: (b, 0)),
        out_shape=jax.ShapeDtypeStruct((n_batch * ctx_len, GROUP_WIDTH), F32),
        compiler_params=_params(1),
    )(pz, pz, pz, sink)


SCAN_ROWS = 256


def _split3(x):
    a = x.astype(BF16)
    r = x - a.astype(F32)
    b = r.astype(BF16)
    c = (r - b.astype(F32)).astype(BF16)
    return a, b, c


def _scan_kernel(*refs, heads, dk, dv, kind, rope, q_scale, k_scale):
    it = iter(refs)
    qf, kf, vf, qb, kb, vb = (next(it) for _ in range(6))
    if kind == "ret":
        lg = next(it)
    else:
        df, db, wup, bup = (next(it) for _ in range(4))
    if rope:
        cosf, sinf, cosb, sinb = (next(it) for _ in range(4))
    s0 = next(it)
    of, ob, sfin = next(it), next(it), next(it)
    st = next(it)

    s = pl.program_id(1)
    c = SCAN_CHUNK
    nch = SCAN_ROWS // c
    hk = heads * dk

    @pl.when(s == 0)
    def _():
        st[...] = s0[...]

    r_i = lax.broadcasted_iota(jnp.int32, (c, c), 0)
    c_i = lax.broadcasted_iota(jnp.int32, (c, c), 1)
    masks = (r_i >= c_i, c_i > r_i)

    if kind == "ret":
        pos = lax.broadcasted_iota(jnp.int32, (c, hk), 0).astype(F32)
        gcums = ((pos + 1.0) * lg[0:1, :], (float(c) - pos) * lg[1:2, :])
    else:
        rr = lax.broadcasted_iota(jnp.int32, (SCAN_ROWS, SCAN_ROWS), 0)
        cc = lax.broadcasted_iota(jnp.int32, (SCAN_ROWS, SCAN_ROWS), 1)
        same = (rr // c) == (cc // c)
        tris = (jnp.where(same & (rr >= cc), 1.0, 0.0).astype(BF16),
                jnp.where(same & (cc >= rr), 1.0, 0.0).astype(BF16))

        def gate_cum(d_ref, direction):
            pre = jnp.dot(d_ref[...].astype(BF16), wup[direction], preferred_element_type=F32) + bup[direction]
            g = -(jnp.maximum(-pre, 0.0) + jnp.log1p(jnp.exp(-jnp.abs(pre)))) / GLA_TAU
            return sum(jnp.dot(tris[direction], p, preferred_element_type=F32) for p in _split3(g))

        gcums = (gate_cum(df, 0), gate_cum(db, 1))

    def one(direction, q_ref, k_ref, v_ref, o_ref, cos_ref, sin_ref, ch):
        rows = pl.ds(ch * c, c)
        q = q_ref[rows, :]
        k = k_ref[rows, :]
        v = v_ref[rows, :]
        if rope:
            cs = jnp.concatenate([cos_ref[rows, :]] * heads, axis=1)
            sn = jnp.concatenate([sin_ref[rows, :]] * heads, axis=1)
            q = _rope_lanes(q, cs, sn, dk // 4)
            k = _rope_lanes(k, cs, sn, dk // 4)
        if q_scale != 1.0:
            q = q * q_scale
        if k_scale != 1.0:
            k = k * k_scale
        gcum = gcums[direction] if kind == "ret" else gcums[direction][ch * c:(ch + 1) * c, :]
        gtot = gcum[c - 1:c, :] if direction == 0 else gcum[0:1, :]
        q_rel = (q * jnp.exp(gcum - gtot)).astype(BF16)
        k_rel = (k * jnp.exp(gtot - gcum)).astype(BF16)
        q_dec = (q * jnp.exp(gcum)).astype(BF16)
        dec = jnp.exp(gtot)
        vb16 = v.astype(BF16)
        outs = []
        for hd in range(heads):
            ks = slice(hd * dk, (hd + 1) * dk)
            vs = slice(hd * dv, (hd + 1) * dv)
            a = jnp.where(masks[direction], _dot_nt(q_rel[:, ks], k_rel[:, ks]), 0.0)
            state = st[direction, hd]
            o = jnp.dot(a.astype(BF16), vb16[:, vs], preferred_element_type=F32)
            o = o + _dot_nt(q_dec[:, ks], state.astype(BF16))
            st[direction, hd] = dec[:, ks] * state + _dot_tn(vb16[:, vs], k_rel[:, ks])
            outs.append(o)
        o_ref[rows, :] = jnp.concatenate(outs, axis=1)

    for ch in range(nch):
        one(0, qf, kf, vf, of, cosf if rope else None, sinf if rope else None, ch)
        one(1, qb, kb, vb, ob, cosb if rope else None, sinb if rope else None, nch - 1 - ch)

    @pl.when(s == pl.num_programs(1) - 1)
    def _():
        sfin[...] = st[...]


def bidir_scan(p, n_batch, seq, cols, heads, dk, dv, kind, s0, *, lg=None, wup=None, bup=None,
               rope=None, q_scale=1.0, k_scale=1.0):
    t = SCAN_ROWS
    nblk = seq // t
    hk, hv = heads * dk, heads * dv
    qc, kc, vc = cols
    fwd = lambda w, off: pl.BlockSpec((t, w), lambda b, s: (b * nblk + s, off // w))
    bwd = lambda w, off: pl.BlockSpec((t, w), lambda b, s: (b * nblk + nblk - 1 - s, off // w))
    const = lambda shape: pl.BlockSpec(shape, lambda b, s: (0,) * len(shape))
    args = [p] * 6
    specs = [fwd(hk, qc), fwd(hk, kc), fwd(hv, vc), bwd(hk, qc), bwd(hk, kc), bwd(hv, vc)]
    if kind == "ret":
        args += [lg]
        specs += [const((2, hk))]
    else:
        args += [p, p, wup, bup]
        specs += [fwd(LANES, COL_GLA_D), bwd(LANES, COL_GLA_D), const(wup.shape), const(bup.shape)]
    if rope is not None:
        cos, sin = rope
        args += [cos, sin, cos, sin]
        specs += [pl.BlockSpec((t, dk), lambda b, s: (s, 0)), pl.BlockSpec((t, dk), lambda b, s: (s, 0)),
                  pl.BlockSpec((t, dk), lambda b, s: (nblk - 1 - s, 0)),
                  pl.BlockSpec((t, dk), lambda b, s: (nblk - 1 - s, 0))]
    args += [s0]
    state_spec = pl.BlockSpec((None, 2, heads, dv, dk), lambda b, s: (b, 0, 0, 0, 0))
    specs += [state_spec]
    n = n_batch * seq
    kern = functools.partial(_scan_kernel, heads=heads, dk=dk, dv=dv, kind=kind, rope=rope is not None,
                             q_scale=q_scale, k_scale=k_scale)
    return pl.pallas_call(
        kern,
        grid=(n_batch, nblk),
        in_specs=specs,
        out_specs=[pl.BlockSpec((t, hv), lambda b, s: (b * nblk + s, 0)),
                   pl.BlockSpec((t, hv), lambda b, s: (b * nblk + nblk - 1 - s, 0)),
                   state_spec],
        out_shape=[jax.ShapeDtypeStruct((n, hv), F32), jax.ShapeDtypeStruct((n, hv), F32),
                   jax.ShapeDtypeStruct((n_batch, 2, heads, dv, dk), F32)],
        scratch_shapes=[pltpu.VMEM((2, heads, dv, dk), F32)],
        compiler_params=_params(2),
    )(*args)


def _outproj_kernel(na_ref, rf_ref, rb_ref, rg_ref, gf_ref, gb_ref, gg_ref, sw_ref, gn_ref, w_ref, x_ref, mg_ref,
                    o_ref, *, head_w):
    ry = rf_ref[...] + rb_ref[...]
    gy = gf_ref[...] + gb_ref[...]
    r_out, g_out = [], []
    for hd in range(ry.shape[1] // head_w):
        sl = slice(hd * head_w, (hd + 1) * head_w)
        r = ry[:, sl]
        mu = jnp.mean(r, axis=-1, keepdims=True)
        var = jnp.mean(jnp.square(r - mu), axis=-1, keepdims=True)
        r_out.append((r - mu) * lax.rsqrt(var + EPS))
        gq = gy[:, sl]
        g_out.append(gq * lax.rsqrt(jnp.mean(gq * gq, axis=-1, keepdims=True) + EPS) * gn_ref[...])
    ret = jnp.concatenate(r_out, axis=1) * _silu(rg_ref[...])
    gla = jnp.concatenate(g_out, axis=1) * _silu(gg_ref[...])
    mix = jnp.concatenate([na_ref[...], ret, gla, sw_ref[...]], axis=1).astype(BF16)
    o_ref[...] = x_ref[...] + mg_ref[...] * jnp.dot(mix, w_ref[...], preferred_element_type=F32)


def outproj(na, rf, rb, gf, gb, sw, p, gla_norm_g, w_out, x2, mg, rows_per_mod):
    n, d = x2.shape
    gw = GROUP_WIDTH
    tm = 256
    m = mg.shape[0]
    row = lambda w_: pl.BlockSpec((tm, w_), lambda i: (i, 0))
    return pl.pallas_call(
        functools.partial(_outproj_kernel, head_w=RET_DV),
        grid=(n // tm,),
        in_specs=[row(gw), row(gw), row(gw), pl.BlockSpec((tm, gw), lambda i: (i, COL_RET_G // gw)),
                  row(gw), row(gw), pl.BlockSpec((tm, gw), lambda i: (i, COL_GLA_G // gw)), row(gw),
                  pl.BlockSpec((1, GLA_DV), lambda i: (0, 0)),
                  pl.BlockSpec((d, d), lambda i: (0, 0)), row(d),
                  pl.BlockSpec((None, 1, d), lambda i: (i // (rows_per_mod // tm), 0, 0))],
        out_specs=row(d),
        out_shape=jax.ShapeDtypeStruct((n, d), F32),
        compiler_params=_params(1),
    )(na, rf, rb, p, gf, gb, p, sw, gla_norm_g.reshape(1, GLA_DV), w_out, x2, mg.reshape(m, 1, d))


PEER_PAIRS = PEER_HEADS * PEER_TOPK
PEER_ROUTE_TOKENS = 128
PEER_EXPERT_TOKENS = 16
PEER_SUB = 8
PEER_FOLD = D_MODEL // PEER_SUB
INV_SQRT2 = 0.7071067811865476


def _topk_cols(s, payload=None, order=None):
    row = lax.broadcasted_iota(jnp.int32, s.shape, 0) if order is None else order
    vals, idxs = [], []
    for _ in range(PEER_TOPK):
        m = jnp.max(s, axis=0, keepdims=True)
        am = jnp.min(jnp.where(s == m, row, jnp.iinfo(jnp.int32).max), axis=0, keepdims=True)
        sel = row == am
        vals.append(m)
        idxs.append(am if payload is None else jnp.max(jnp.where(sel, payload, -1), axis=0, keepdims=True))
        s = jnp.where(sel, -jnp.inf, s)
    return jnp.concatenate(vals, axis=0), jnp.concatenate(idxs, axis=0)


def _staircase_candidates(v0, i0, v1, i1):
    k = PEER_TOPK
    assert k == 16
    t = v0.shape[1]
    r8 = lax.broadcasted_iota(jnp.int32, (8, t), 0)
    r16 = lax.broadcasted_iota(jnp.int32, (k, t), 0)

    def piece(a_sl, b_sl):
        return v0[a_sl, :] + v1[b_sl, :], i0[a_sl, :] * PEER_N_KEYS + i1[b_sl, :]

    one = lambda j: slice(j, j + 1)
    lo = slice(0, 8)
    pieces = [
        (one(0), slice(0, k), None, r16),
        (one(1), lo, None, k + r8),
        (one(2), lo, r8 <= 4, 2 * k + r8),
        (one(3), lo, r8 <= 3, 3 * k + r8),
        (slice(8, k), one(0), None, (r8 + 8) * k),
        (lo, one(0), r8 >= 4, r8 * k),
        (lo, one(1), r8 >= 4, r8 * k + 1),
        (lo, one(2), r8 == 4, r8 * k + 2),
    ]
    sums, ids, orders = [], [], []
    for a_sl, b_sl, keep, order in pieces:
        s, e = piece(a_sl, b_sl)
        sums.append(s if keep is None else jnp.where(keep, s, NEG_INF))
        ids.append(e)
        orders.append(order)
    return jnp.concatenate(sums, axis=0), jnp.concatenate(ids, axis=0), jnp.concatenate(orders, axis=0)


def _route_head(hb, wq_ref, sk_ref, hd):
    half = PEER_DK // 2
    q = jnp.dot(hb, wq_ref[hd], preferred_element_type=F32)
    tops = []
    for p in range(2):
        qp = q[:, p * half:(p + 1) * half].astype(BF16)
        tops.append(_topk_cols(_dot_nt(sk_ref[p, hd], qp)))
    (v0, i0), (v1, i1) = tops
    best_s, best_e = _topk_cols(*_staircase_candidates(v0, i0, v1, i1))
    e = jnp.exp(best_s - best_s[0:1, :])
    return e / jnp.sum(e, axis=0, keepdims=True), best_e


def _peer_route_kernel(x_ref, g_ref, shift_ref, scale_ref, wq_ref, sk_ref, h_ref, idx_ref, gate_ref):
    x = x_ref[...]
    y = x * lax.rsqrt(jnp.mean(x * x, axis=-1, keepdims=True) + EPS)
    h = (y * g_ref[...]) * (1.0 + scale_ref[...]) + shift_ref[...]
    h_ref[...] = h
    hb = h.astype(BF16)

    def head_body(hd, carry):
        gate_ref[hd], idx_ref[hd] = _route_head(hb, wq_ref, sk_ref, hd)
        return carry

    lax.fori_loop(0, PEER_HEADS, head_body, 0, unroll=4)


def peer_route(x2, g, shift, scale, rows_per_mod, wq_h, sk):
    n, d = x2.shape
    t = PEER_ROUTE_TOKENS
    m = shift.shape[0]
    mod_map = lambda i: (i // (rows_per_mod // t), 0, 0)
    return pl.pallas_call(
        _peer_route_kernel,
        grid=(n // t,),
        in_specs=[
            pl.BlockSpec((t, d), lambda i: (i, 0)),
            pl.BlockSpec((1, d), lambda i: (0, 0)),
            pl.BlockSpec((None, 1, d), mod_map),
            pl.BlockSpec((None, 1, d), mod_map),
            pl.BlockSpec((PEER_HEADS, d, PEER_DK), lambda i: (0, 0, 0)),
            pl.BlockSpec((2, PEER_HEADS, PEER_N_KEYS, PEER_DK // 2), lambda i: (0, 0, 0, 0)),
        ],
        out_specs=[
            pl.BlockSpec((t, d), lambda i: (i, 0)),
            pl.BlockSpec((PEER_HEADS, PEER_TOPK, t), lambda i: (0, 0, i)),
            pl.BlockSpec((PEER_HEADS, PEER_TOPK, t), lambda i: (0, 0, i)),
        ],
        out_shape=[
            jax.ShapeDtypeStruct((n, d), F32),
            jax.ShapeDtypeStruct((PEER_HEADS, PEER_TOPK, n), jnp.int32),
            jax.ShapeDtypeStruct((PEER_HEADS, PEER_TOPK, n), F32),
        ],
        compiler_params=_params(1),
    )(x2, g.reshape(1, d), shift.reshape(m, 1, d), scale.reshape(m, 1, d), wq_h, sk)


def pack_experts(u, v):
    ub = lax.bitcast_convert_type(u.astype(BF16), jnp.uint16).astype(jnp.uint32)
    vb = lax.bitcast_convert_type(v.astype(BF16), jnp.uint16).astype(jnp.uint32)
    words = lax.bitcast_convert_type((ub << 16) | vb, jnp.int32)
    return words.reshape(u.shape[0], PEER_SUB, PEER_FOLD)


def _fold_rows(hbuf, h_rows, tb):
    for s in range(PEER_SUB):
        for c in range(PEER_FOLD // LANES):
            lo = s * PEER_FOLD + c * LANES
            hbuf[c, pl.ds(s * tb, tb), :] = h_rows[:, lo:lo + LANES]


def _unfold_residual(o_ref, x_ref, og_ref, ybuf, tb):
    for s in range(PEER_SUB):
        for c in range(PEER_FOLD // LANES):
            sl = slice(s * PEER_FOLD + c * LANES, s * PEER_FOLD + (c + 1) * LANES)
            o_ref[:, sl] = x_ref[:, sl] + og_ref[:, sl] * ybuf[c, pl.ds(s, tb, stride=PEER_SUB), :]


def _expert_token(j, buf, hbuf, pbuf, wbuf, ybuf, tb, gate_column, start_fetch):
    fold = PEER_FOLD
    hj = jnp.concatenate([hbuf[c, pl.ds(j, PEER_SUB, stride=tb), :] for c in range(fold // LANES)], axis=1)
    for p in range(PEER_PAIRS):
        u = lax.bitcast_convert_type(buf[j, p] & jnp.int32(-65536), F32)
        prod = u * hj
        pbuf[pl.ds(p * PEER_SUB, PEER_SUB), :] = prod[:, :LANES] + prod[:, LANES:]
    part = pbuf[pl.ds(0, PEER_PAIRS, stride=PEER_SUB), :]
    for s in range(1, PEER_SUB):
        part = part + pbuf[pl.ds(s, PEER_PAIRS, stride=PEER_SUB), :]
    sc = jnp.sum(part, axis=1, keepdims=True)
    act = 0.5 * sc * (1.0 + lax.erf(sc * INV_SQRT2))
    g = jnp.concatenate([gate_column(hd) for hd in range(PEER_HEADS)], axis=0)
    wbuf[...] = jnp.broadcast_to(g * act, (PEER_PAIRS, LANES))
    accs = [jnp.zeros((PEER_SUB, fold), F32) for _ in range(4)]
    for p in range(PEER_PAIRS):
        if start_fetch is not None:
            start_fetch(p)
        v = lax.bitcast_convert_type(buf[j, p] << 16, F32)
        wp = jnp.broadcast_to(wbuf[p:p + 1, :], (PEER_SUB, LANES))
        accs[p % 4] = accs[p % 4] + v * jnp.concatenate([wp, wp], axis=1)
    yj = (accs[0] + accs[1]) + (accs[2] + accs[3])
    for c in range(fold // LANES):
        ybuf[c, pl.ds(j * PEER_SUB, PEER_SUB), :] = yj[:, c * LANES:(c + 1) * LANES]


def _peer_expert_kernel(idx0_ref, idxn_ref, h_ref, gate_ref, x_ref, og_ref, uv_hbm, o_ref,
                        buf_even, buf_odd, sem, hbuf, pbuf, wbuf, ybuf):
    i = pl.program_id(0)
    n = pl.num_programs(0)
    tb = PEER_EXPERT_TOKENS
    fold = PEER_FOLD
    bufs = (buf_even, buf_odd)

    def slab_copy(idx_ref, j, r, parity):
        e = idx_ref[0, 0, j * PEER_PAIRS + r]
        return pltpu.make_async_copy(uv_hbm.at[e], bufs[parity].at[j, r], sem.at[parity, j])

    def wait_token(j, parity):
        pltpu.make_async_copy(uv_hbm.at[pl.ds(0, PEER_PAIRS)], bufs[parity].at[j], sem.at[parity, j]).wait()

    @pl.when(i == 0)
    def _():
        def prime(j, carry):
            for r in range(PEER_PAIRS):
                slab_copy(idx0_ref, j, r, 0).start(priority=r % 2)
            return carry
        lax.fori_loop(0, tb, prime, 0)

    lane = lax.broadcasted_iota(jnp.int32, (PEER_TOPK, PEER_ROUTE_TOKENS), 1)
    lane0 = (i % (PEER_ROUTE_TOKENS // tb)) * tb

    _fold_rows(hbuf, h_ref[...], tb)

    def token_body(j, carry, parity, prefetch):
        wait_token(j, parity)
        gate_column = lambda hd: jnp.sum(jnp.where(lane == lane0 + j, gate_ref[hd], 0.0), axis=1, keepdims=True)
        fetch = (lambda p: slab_copy(idxn_ref, j, p, 1 - parity).start(priority=p % 2)) if prefetch else None
        _expert_token(j, bufs[parity], hbuf, pbuf, wbuf, ybuf, tb, gate_column, fetch)
        return carry

    for parity in range(2):
        for prefetch in (True, False):
            @pl.when((i % 2 == parity) & ((i + 1 < n) == prefetch))
            def _(parity=parity, prefetch=prefetch):
                lax.fori_loop(0, tb, functools.partial(token_body, parity=parity, prefetch=prefetch), 0)

    _unfold_residual(o_ref, x_ref, og_ref, ybuf, tb)


def peer_expert(h, idx, gate, x2, out_gate, rows_per_mod, uv):
    n, d = h.shape
    tb = PEER_EXPERT_TOKENS
    nb = n // tb
    m = out_gate.shape[0]
    rows = tb * PEER_PAIRS
    idx_rows = idx.reshape(PEER_PAIRS, n).T.reshape(nb, 1, rows)
    gate_blocks = PEER_ROUTE_TOKENS // tb
    return pl.pallas_call(
        _peer_expert_kernel,
        grid=(nb,),
        in_specs=[
            pl.BlockSpec((1, 1, rows), lambda i: (0, 0, 0), memory_space=pltpu.SMEM),
            pl.BlockSpec((1, 1, rows), lambda i: (jnp.minimum(i + 1, nb - 1), 0, 0), memory_space=pltpu.SMEM),
            pl.BlockSpec((tb, d), lambda i: (i, 0)),
            pl.BlockSpec((PEER_HEADS, PEER_TOPK, PEER_ROUTE_TOKENS), lambda i: (0, 0, i // gate_blocks)),
            pl.BlockSpec((tb, d), lambda i: (i, 0)),
            pl.BlockSpec((None, 1, d), lambda i: (i // (rows_per_mod // tb), 0, 0)),
            pl.BlockSpec(memory_space=pl.ANY),
        ],
        out_specs=pl.BlockSpec((tb, d), lambda i: (i, 0)),
        out_shape=jax.ShapeDtypeStruct((n, d), F32),
        scratch_shapes=[pltpu.VMEM((tb, PEER_PAIRS, PEER_SUB, PEER_FOLD), jnp.int32),
                        pltpu.VMEM((tb, PEER_PAIRS, PEER_SUB, PEER_FOLD), jnp.int32),
                        pltpu.SemaphoreType.DMA((2, tb)),
                        pltpu.VMEM((PEER_FOLD // LANES, PEER_SUB * tb, LANES), F32),
                        pltpu.VMEM((PEER_PAIRS * PEER_SUB, LANES), F32),
                        pltpu.VMEM((PEER_PAIRS, LANES), F32),
                        pltpu.VMEM((PEER_FOLD // LANES, tb * PEER_SUB, LANES), F32)],
        compiler_params=pltpu.CompilerParams(dimension_semantics=("arbitrary",), vmem_limit_bytes=VMEM_LIMIT,
                                             disable_bounds_checks=True),
    )(idx_rows, idx_rows, h, gate, x2, out_gate.reshape(m, 1, d), uv)


PEER_GROUP_STEPS = PEER_ROUTE_TOKENS // PEER_EXPERT_TOKENS
PEER_ITEM_HEADS = 2
PEER_GROUP_ITEMS = PEER_HEADS // PEER_ITEM_HEADS
PEER_ITEM_STEPS = PEER_GROUP_STEPS // PEER_GROUP_ITEMS
PEER_ROUTE_LEAD = PEER_GROUP_ITEMS


def _peer_fused_kernel(xr0_ref, xr_ref, g_ref, sh0_ref, sc0_ref, sh_ref, sc_ref, wq_hbm, sk_ref,
                       x_ref, og_ref, uv_hbm, o_ref,
                       wq_v, wq_sem, hs, hb16, gates, idxv, idxt, idx_s, idx_sem,
                       buf_even, buf_odd, sem, hbuf, pbuf, wbuf, ybuf, *, n_groups):
    i = pl.program_id(0)
    n = pl.num_programs(0)
    tb = PEER_EXPERT_TOKENS
    gs = PEER_GROUP_STEPS
    bufs = (buf_even, buf_odd)
    group = i // gs
    k = i % gs

    def ids_copy(slot):
        return pltpu.make_async_copy(idxt, idx_s.at[slot], idx_sem.at[slot])

    def route_item(t, carry):
        grp = t // PEER_GROUP_ITEMS
        part = t % PEER_GROUP_ITEMS
        slot3 = grp % 3

        def prepare(x_src, shift_ref, scale_ref):
            x = x_src[...]
            y = x * lax.rsqrt(jnp.mean(x * x, axis=-1, keepdims=True) + EPS)
            h = (y * g_ref[...]) * (1.0 + scale_ref[...]) + shift_ref[...]
            hs[slot3] = h
            hb16[...] = h.astype(BF16)

        @pl.when((part == 0) & (grp == 0))
        def _():
            prepare(xr0_ref, sh0_ref, sc0_ref)

        @pl.when((part == 0) & (grp > 0))
        def _():
            prepare(xr_ref, sh_ref, sc_ref)

        hb = hb16[...]
        for local in range(PEER_ITEM_HEADS):
            hd = part * PEER_ITEM_HEADS + local
            gate, ids = _route_head(hb, wq_v, sk_ref, hd)
            gates[slot3, hd] = gate
            idxv[pl.ds(pl.multiple_of(hd * PEER_TOPK, PEER_TOPK), PEER_TOPK), :] = ids

        @pl.when(part == PEER_GROUP_ITEMS - 1)
        def _():
            idxt[...] = idxv[...].T
            ids_copy(grp % 2).start()
        return carry

    def slab_copy(ids_slot, tok, j, r, parity):
        e = idx_s[ids_slot, tok, r]
        return pltpu.make_async_copy(uv_hbm.at[e], bufs[parity].at[j, r], sem.at[parity, j])

    def wait_token(j, parity):
        pltpu.make_async_copy(uv_hbm.at[pl.ds(0, PEER_PAIRS)], bufs[parity].at[j], sem.at[parity, j]).wait()

    @pl.when(i == 0)
    def _():
        cp = pltpu.make_async_copy(wq_hbm, wq_v, wq_sem)
        cp.start()
        cp.wait()

    t_now = i // PEER_ITEM_STEPS + PEER_ROUTE_LEAD
    t_lo = jnp.where(i == 0, 0, t_now)
    t_hi = jnp.where(i % PEER_ITEM_STEPS == 0, jnp.minimum(t_now + 1, n_groups * PEER_GROUP_ITEMS), t_lo)
    lax.fori_loop(t_lo, t_hi, route_item, 0)

    @pl.when(i == 0)
    def _():
        ids_copy(0).wait()

        def prime(j, carry):
            for r in range(PEER_PAIRS):
                slab_copy(0, j, j, r, 0).start(priority=r % 2)
            return carry
        lax.fori_loop(0, tb, prime, 0)

    @pl.when((k == gs - 1) & (group + 1 < n_groups))
    def _():
        ids_copy((group + 1) % 2).wait()

    slot3 = group % 3
    tok0 = pl.multiple_of(k * tb, tb)
    _fold_rows(hbuf, hs[slot3, pl.ds(tok0, tb), :], tb)
    lane = lax.broadcasted_iota(jnp.int32, (PEER_TOPK, PEER_ROUTE_TOKENS), 1)
    nxt = i + 1
    nxt_slot = (nxt // gs) % 2
    nxt_tok0 = (nxt % gs) * tb

    def token_body(j, carry, parity, prefetch):
        wait_token(j, parity)
        gate_column = lambda hd: jnp.sum(jnp.where(lane == tok0 + j, gates[slot3, hd], 0.0), axis=1, keepdims=True)
        fetch = ((lambda p: slab_copy(nxt_slot, nxt_tok0 + j, j, p, 1 - parity).start(priority=p % 2))
                 if prefetch else None)
        _expert_token(j, bufs[parity], hbuf, pbuf, wbuf, ybuf, tb, gate_column, fetch)
        return carry

    for parity in range(2):
        for prefetch in (True, False):
            @pl.when((i % 2 == parity) & ((i + 1 < n) == prefetch))
            def _(parity=parity, prefetch=prefetch):
                lax.fori_loop(0, tb, functools.partial(token_body, parity=parity, prefetch=prefetch), 0)

    _unfold_residual(o_ref, x_ref, og_ref, ybuf, tb)


def peer_fused(x2, g, shift, scale, out_gate, rows_per_mod, wq_h, sk, uv):
    n, d = x2.shape
    tb = PEER_EXPERT_TOKENS
    t = PEER_ROUTE_TOKENS
    gs = PEER_GROUP_STEPS
    n_groups = n // t
    m = shift.shape[0]
    groups_per_mod = rows_per_mod // t
    route_group = lambda i: jnp.minimum(i // PEER_GROUP_STEPS + 1, n_groups - 1)
    mod_next = lambda i: (route_group(i) // groups_per_mod, 0, 0)
    mod3 = lambda a: a.reshape(m, 1, d)
    slab = (tb, PEER_PAIRS, PEER_SUB, PEER_FOLD)
    return pl.pallas_call(
        functools.partial(_peer_fused_kernel, n_groups=n_groups),
        grid=(n // tb,),
        in_specs=[
            pl.BlockSpec((t, d), lambda i: (0, 0)),
            pl.BlockSpec((t, d), lambda i: (route_group(i), 0)),
            pl.BlockSpec((1, d), lambda i: (0, 0)),
            pl.BlockSpec((None, 1, d), lambda i: (0, 0, 0)),
            pl.BlockSpec((None, 1, d), lambda i: (0, 0, 0)),
            pl.BlockSpec((None, 1, d), mod_next),
            pl.BlockSpec((None, 1, d), mod_next),
            pl.BlockSpec(memory_space=pl.ANY),
            pl.BlockSpec((2, PEER_HEADS, PEER_N_KEYS, PEER_DK // 2), lambda i: (0, 0, 0, 0)),
            pl.BlockSpec((tb, d), lambda i: (i, 0)),
            pl.BlockSpec((None, 1, d), lambda i: (i // (rows_per_mod // tb), 0, 0)),
            pl.BlockSpec(memory_space=pl.ANY),
        ],
        out_specs=pl.BlockSpec((tb, d), lambda i: (i, 0)),
        out_shape=jax.ShapeDtypeStruct((n, d), F32),
        scratch_shapes=[
            pltpu.VMEM(wq_h.shape, BF16), pltpu.SemaphoreType.DMA(()),
            pltpu.VMEM((3, t, d), F32),
            pltpu.VMEM((t, d), BF16),
            pltpu.VMEM((3, PEER_HEADS, PEER_TOPK, t), F32),
            pltpu.VMEM((PEER_PAIRS, t), jnp.int32),
            pltpu.VMEM((t, PEER_PAIRS), jnp.int32),
            pltpu.SMEM((2, t, PEER_PAIRS), jnp.int32), pltpu.SemaphoreType.DMA((2,)),
            pltpu.VMEM(slab, jnp.int32), pltpu.VMEM(slab, jnp.int32), pltpu.SemaphoreType.DMA((2, tb)),
            pltpu.VMEM((PEER_FOLD // LANES, PEER_SUB * tb, LANES), F32),
            pltpu.VMEM((PEER_PAIRS * PEER_SUB, LANES), F32),
            pltpu.VMEM((PEER_PAIRS, LANES), F32),
            pltpu.VMEM((PEER_FOLD // LANES, tb * PEER_SUB, LANES), F32),
        ],
        compiler_params=pltpu.CompilerParams(dimension_semantics=("arbitrary",),
                                             vmem_limit_bytes=PEER_FUSED_VMEM_LIMIT, disable_bounds_checks=True),
    )(x2, x2, g.reshape(1, d), mod3(shift), mod3(scale), mod3(shift), mod3(scale), wq_h, sk,
      x2, mod3(out_gate), uv)


def peer_residual(x2, g, shift, scale, out_gate, rows_per_mod, wq_h, sk, uv):
    h, idx, gate = peer_route(x2, g, shift, scale, rows_per_mod, wq_h, sk)
    return peer_expert(h, idx, gate, x2, out_gate, rows_per_mod, uv)


def _final_norm_kernel(x_ref, g_ref, o_ref):
    x = x_ref[...]
    y = x * lax.rsqrt(jnp.mean(x * x, axis=-1, keepdims=True) + EPS)
    o_ref[...] = y * g_ref[...]


def final_norm(x2, g):
    n, d = x2.shape
    tm = 512
    return pl.pallas_call(
        _final_norm_kernel,
        grid=(n // tm,),
        in_specs=[pl.BlockSpec((tm, d), lambda i: (i, 0)), pl.BlockSpec((1, d), lambda i: (0, 0))],
        out_specs=pl.BlockSpec((tm, d), lambda i: (i, 0)),
        out_shape=jax.ShapeDtypeStruct((n, d), x2.dtype),
        compiler_params=_params(1),
    )(x2, g.reshape(1, d))


def _mixers(p, pz, n_batch, seq, ctx_len, is_ctx, tables, prm):
    if is_ctx:
        na = ctx_attention(p, n_batch, seq, (COL_NA_Q, COL_NA_K, COL_NA_V), GROUP_WIDTH, None)
        sw = ctx_attention(p, n_batch, seq, (COL_SWA_Q, COL_SWA_K, COL_SWA_V), SWA_KV_HEADS * HEAD_DIM, prm["sink"])
    else:
        na = na_attention(p, pz, n_batch, seq, ctx_len, prm["na_bias"])
        sw = swa_attention(p, pz, n_batch, seq, ctx_len, prm["sink"], tables["swa_q"], tables["swa_k"])
    rf, rb, rs = bidir_scan(p, n_batch, seq, (COL_RET_Q, COL_RET_K, COL_RET_V), RET_HEADS, RET_DK, RET_DV, "ret",
                            prm["ret_s0"], lg=prm["ret_lg"], rope=None if is_ctx else tables["ret"],
                            k_scale=RET_DK ** -0.5)
    gf, gb, gs = bidir_scan(p, n_batch, seq, (COL_GLA_Q, COL_GLA_K, COL_GLA_V), GLA_HEADS, GLA_DK, GLA_DV, "gla",
                            prm["gla_s0"], wup=prm["gla_wup"], bup=prm["gla_bup"], q_scale=GLA_DK ** -0.5)
    return (na, rf, rb, gf, gb, sw), (rs, gs)


def kernel(x, c, ctx, c_ctx, w_ada, b_ada, norm_attn_g, norm_ffn_g, w_in, na_rpb, ret_log_gamma,
           gla_w_gate_up, gla_b_gate, gla_norm_g, swa_sink, w_out, peer_w_q, peer_sub_keys,
           peer_u, peer_v, final_g):
    bsz, slen, d = x.shape
    zlen = ctx.shape[1]
    x2 = x.reshape(bsz * slen, d)
    z2 = ctx.reshape(bsz * zlen, d)
    tables = {"ret": rope_lane_tables(slen, RET_DK, 1),
              "swa_q": rope_lane_tables(slen, HEAD_DIM, SWA_HEADS),
              "swa_k": rope_lane_tables(slen, HEAD_DIM, SWA_KV_HEADS)}
    c_rows = jnp.zeros((8, d), F32).at[:bsz].set(c).at[bsz].set(c_ctx)
    for layer in range(DEPTH):
        has_next = layer < DEPTH - 1
        mod = adaln(c_rows, w_ada[layer], b_ada[layer])
        mx = [mod[:bsz, k * d:(k + 1) * d] for k in range(6)]
        mz = [mod[bsz:bsz + 1, k * d:(k + 1) * d] for k in range(6)]

        wi = w_in[layer]
        wp = jnp.concatenate([wi[:, :REF_COL_GLA_D], wi[:, REF_COL_SWA_Q:], wi[:, REF_COL_GLA_D:REF_COL_SWA_Q],
                              jnp.zeros((d, PROJ_WIDTH - REF_D_IN), F32)], axis=1).astype(BF16)
        px = modproj(x2, norm_attn_g[layer], mx[0], mx[1], slen, wp)
        pz = modproj(z2, norm_attn_g[layer], mz[0], mz[1], bsz * zlen, wp)

        wup = (jnp.zeros((2, LANES, GLA_HEADS * GLA_DK), F32)
               .at[0, :GLA_RANK].set(gla_w_gate_up[layer, 0])
               .at[1, GLA_RANK:2 * GLA_RANK].set(gla_w_gate_up[layer, 1])).astype(BF16)
        prm = {"na_bias": na_band_bias(na_rpb[layer]), "sink": swa_sink[layer],
               "ret_lg": jnp.repeat(ret_log_gamma[layer], RET_DK, axis=1),
               "gla_wup": wup, "gla_bup": gla_b_gate[layer].reshape(2, 1, GLA_HEADS * GLA_DK),
               "ret_s0": jnp.zeros((bsz, 2, RET_HEADS, RET_DV, RET_DK), F32),
               "gla_s0": jnp.zeros((bsz, 2, GLA_HEADS, GLA_DV, GLA_DK), F32)}
        mix_z, (ret_s, gla_s) = _mixers(pz, pz, bsz, zlen, zlen, True, tables, prm)
        prm["ret_s0"], prm["gla_s0"] = ret_s, gla_s
        mix_x, _ = _mixers(px, pz, bsz, slen, zlen, False, tables, prm)

        wo = w_out[layer].astype(BF16)
        x2 = outproj(*mix_x, px, gla_norm_g[layer], wo, x2, mx[2], slen)

        uv = pack_experts(peer_u[layer], peer_v[layer])
        wq_h = peer_w_q[layer].reshape(d, PEER_HEADS, PEER_DK).transpose(1, 0, 2).astype(BF16)
        sk = peer_sub_keys[layer].astype(BF16)
        x2 = peer_fused(x2, norm_ffn_g[layer], mx[3], mx[4], mx[5], slen, wq_h, sk, uv)
        if has_next:
            z2 = outproj(*mix_z, pz, gla_norm_g[layer], wo, z2, mz[2], bsz * zlen)
            z2 = peer_residual(z2, norm_ffn_g[layer], mz[3], mz[4], mz[5], bsz * zlen, wq_h, sk, uv)
    return final_norm(x2, final_g).reshape(bsz, slen, d)
```

```python
import functools

import jax
import jax.numpy as jnp
import numpy as np
from jax import lax
from jax.experimental import pallas as pl
from jax.experimental.pallas import tpu as pltpu

D_MODEL = 2048
DEPTH = 2
GRID_W = 64
EPS = 1e-6
ROPE_BASE = 10000.0

GROUP_WIDTH = D_MODEL // 4
HEAD_DIM = 64
NA_HEADS = GROUP_WIDTH // HEAD_DIM
NA_ROWS = 8
NA_COLS = 16
RET_HEADS = 4
RET_DK = GROUP_WIDTH // RET_HEADS
RET_DV = GROUP_WIDTH // RET_HEADS
GLA_HEADS = 4
GLA_DV = GROUP_WIDTH // GLA_HEADS
GLA_DK = GLA_DV // 2
GLA_RANK = 16
GLA_TAU = 16.0
SWA_HEADS = GROUP_WIDTH // HEAD_DIM
SWA_KV_HEADS = SWA_HEADS // 4
SWA_WINDOW = 128
SWA_BLOCK = 128
SCAN_CHUNK = 64
PEER_HEADS = 8
PEER_N_KEYS = 128
PEER_N_EXPERTS = PEER_N_KEYS * PEER_N_KEYS
PEER_DK = 256
PEER_TOPK = 16

LANES = 128
VMEM_LIMIT = 48 * 1024 * 1024
PEER_FUSED_VMEM_LIMIT = 56 * 1024 * 1024
BF16 = jnp.bfloat16
F32 = jnp.float32
NEG_INF = float("-inf")

COL_NA_Q, COL_NA_K, COL_NA_V = 0, 512, 1024
COL_RET_Q, COL_RET_K, COL_RET_V, COL_RET_G = 1536, 2048, 2560, 3072
COL_GLA_Q, COL_GLA_K, COL_GLA_V, COL_GLA_G = 3584, 3840, 4096, 4608
COL_SWA_Q, COL_SWA_K, COL_SWA_V = 5120, 5632, 5760
COL_GLA_D = 5888
REF_COL_GLA_D, REF_COL_SWA_Q, REF_D_IN = 5120, 5152, 5920
PROJ_WIDTH = 6144


def _silu(x):
    return x / (1.0 + jnp.exp(-x))


def _dot_nt(a, b):
    return lax.dot_general(a, b, (((1,), (1,)), ((), ())), preferred_element_type=F32)


def _dot_tn(a, b):
    return lax.dot_general(a, b, (((0,), (0,)), ((), ())), preferred_element_type=F32)


def _params(n_axes):
    return pltpu.CompilerParams(dimension_semantics=("arbitrary",) * n_axes, vmem_limit_bytes=VMEM_LIMIT)


def _rope_lanes(x, cs, sn, quarter):
    n = x.shape[-1]
    lane = lax.broadcasted_iota(jnp.int32, x.shape, x.ndim - 1)
    first = (lane % (2 * quarter)) < quarter
    swapped = jnp.where(first, pltpu.roll(x, n - quarter, x.ndim - 1), pltpu.roll(x, quarter, x.ndim - 1))
    return x * cs + swapped * sn


def rope_lane_tables(length, dh, copies):
    t = jnp.arange(length)
    pos = jnp.stack([t // GRID_W, t % GRID_W], axis=-1).astype(F32)
    quarter = dh // 4
    inv = ROPE_BASE ** (-jnp.arange(quarter, dtype=F32) / quarter)
    ang = pos[:, :, None] * inv
    cos, sin = jnp.cos(ang), jnp.sin(ang)
    cl = jnp.concatenate([cos[:, 0], cos[:, 0], cos[:, 1], cos[:, 1]], axis=-1)
    sl = jnp.concatenate([-sin[:, 0], sin[:, 0], -sin[:, 1], sin[:, 1]], axis=-1)
    return jnp.tile(cl, (1, copies)), jnp.tile(sl, (1, copies))


def _adaln_kernel(c_ref, w_ref, b_ref, o_ref):
    a = _silu(c_ref[...]).astype(BF16)
    o_ref[...] = jnp.dot(a, w_ref[...].astype(BF16), preferred_element_type=F32) + b_ref[...]


def adaln(c_rows, w, b):
    r, d = c_rows.shape
    m = w.shape[1]
    tn = 1024
    return pl.pallas_call(
        _adaln_kernel,
        grid=(m // tn,),
        in_specs=[pl.BlockSpec((r, d), lambda j: (0, 0)), pl.BlockSpec((d, tn), lambda j: (0, j)),
                  pl.BlockSpec((1, tn), lambda j: (0, j))],
        out_specs=pl.BlockSpec((r, tn), lambda j: (0, j)),
        out_shape=jax.ShapeDtypeStruct((r, m), F32),
        compiler_params=_params(1),
    )(c_rows, w, b.reshape(1, m))


def _modproj_kernel(x_ref, g_ref, shift_ref, scale_ref, w_ref, o_ref, hb_ref):
    @pl.when(pl.program_id(1) == 0)
    def _():
        x = x_ref[...]
        y = x * lax.rsqrt(jnp.mean(x * x, axis=-1, keepdims=True) + EPS)
        hb_ref[...] = ((y * g_ref[...]) * (1.0 + scale_ref[...]) + shift_ref[...]).astype(BF16)

    o_ref[...] = jnp.dot(hb_ref[...], w_ref[...], preferred_element_type=F32)


def modproj(x2, g, shift, scale, rows_per_mod, w):
    n, d = x2.shape
    wid = w.shape[1]
    tm = min(512, rows_per_mod)
    tn = 2048
    m = shift.shape[0]
    mod_map = lambda i, j: (i // (rows_per_mod // tm), 0, 0)
    return pl.pallas_call(
        _modproj_kernel,
        grid=(n // tm, wid // tn),
        in_specs=[pl.BlockSpec((tm, d), lambda i, j: (i, 0)), pl.BlockSpec((1, d), lambda i, j: (0, 0)),
                  pl.BlockSpec((None, 1, d), mod_map), pl.BlockSpec((None, 1, d), mod_map),
                  pl.BlockSpec((d, tn), lambda i, j: (0, j))],
        out_specs=pl.BlockSpec((tm, tn), lambda i, j: (i, j)),
        out_shape=jax.ShapeDtypeStruct((n, wid), F32),
        scratch_shapes=[pltpu.VMEM((tm, d), BF16)],
        compiler_params=_params(2),
    )(x2, g.reshape(1, d), shift.reshape(m, 1, d), scale.reshape(m, 1, d), w)


NA_QROWS = 4


def _na_kernel(q_ref, k_ref, v_ref, kz_ref, vz_ref, bias_ref, o_ref, *, rows):
    step = pl.program_id(2)
    dh = HEAD_DIM
    band = NA_ROWS * GRID_W
    kz = kz_ref[...].astype(BF16)
    vz = vz_ref[...].astype(BF16)
    heads = LANES // dh
    units = [(qr, hh) for qr in range(NA_QROWS) for hh in range(heads)]
    scores, vbands = {}, {}
    for qr in range(NA_QROWS):
        r = step * NA_QROWS + qr
        start = jnp.clip(r - NA_ROWS // 2, 0, rows - NA_ROWS)
        dr0 = start - r + NA_ROWS - 1
        tok0 = pl.multiple_of(start * GRID_W, GRID_W)
        kb = k_ref[pl.ds(tok0, band), :].astype(BF16)
        vbands[qr] = v_ref[pl.ds(tok0, band), :].astype(BF16)
        q = (q_ref[pl.ds(qr * GRID_W, GRID_W), :] * (dh ** -0.5)).astype(BF16)
        for hh in range(heads):
            sl = slice(hh * dh, (hh + 1) * dh)
            scores[qr, hh] = (_dot_nt(q[:, sl], kb[:, sl]) + bias_ref[hh, dr0],
                              _dot_nt(q[:, sl], kz[:, sl]))
    probs = {}
    for u in units:
        s_nb, s_cx = scores[u]
        m = jnp.maximum(jnp.max(s_nb, axis=1, keepdims=True), jnp.max(s_cx, axis=1, keepdims=True))
        p_nb = jnp.exp(s_nb - m)
        p_cx = jnp.exp(s_cx - m)
        den = jnp.sum(p_nb, axis=1, keepdims=True) + jnp.sum(p_cx, axis=1, keepdims=True)
        probs[u] = (p_nb.astype(BF16), p_cx.astype(BF16), den)
    for qr in range(NA_QROWS):
        outs = []
        for hh in range(heads):
            sl = slice(hh * dh, (hh + 1) * dh)
            p_nb, p_cx, den = probs[qr, hh]
            o = (jnp.dot(p_nb, vbands[qr][:, sl], preferred_element_type=F32)
                 + jnp.dot(p_cx, vz[:, sl], preferred_element_type=F32))
            outs.append(o / den)
        o_ref[pl.ds(qr * GRID_W, GRID_W), :] = jnp.concatenate(outs, axis=1)


def na_band_bias(rpb):
    col = jnp.arange(GRID_W)
    col_start = jnp.clip(col - NA_COLS // 2, 0, GRID_W - NA_COLS)
    col_ok = (col[None, :] >= col_start[:, None]) & (col[None, :] < col_start[:, None] + NA_COLS)
    d_col = jnp.clip(col[None, :] - col[:, None], -(NA_COLS - 1), NA_COLS - 1) + NA_COLS - 1
    d_row = jnp.arange(NA_ROWS)[:, None] + jnp.arange(NA_ROWS)[None, :]
    b = rpb.astype(F32)[:, d_row][..., d_col]
    b = jnp.where(col_ok[None, None, None], b, NEG_INF)
    return b.transpose(0, 1, 3, 2, 4).reshape(rpb.shape[0], NA_ROWS, GRID_W, NA_ROWS * GRID_W)


def na_attention(px, pz, n_batch, seq, ctx_len, bias):
    rows = seq // GRID_W
    tq = NA_QROWS * GRID_W
    nsteps = rows // NA_QROWS
    heads_per_blk = LANES // HEAD_DIM
    return pl.pallas_call(
        functools.partial(_na_kernel, rows=rows),
        grid=(n_batch, NA_HEADS // heads_per_blk, nsteps),
        in_specs=[
            pl.BlockSpec((tq, LANES), lambda b, hp, s: (b * nsteps + s, COL_NA_Q // LANES + hp)),
            pl.BlockSpec((seq, LANES), lambda b, hp, s: (b, COL_NA_K // LANES + hp)),
            pl.BlockSpec((seq, LANES), lambda b, hp, s: (b, COL_NA_V // LANES + hp)),
            pl.BlockSpec((ctx_len, LANES), lambda b, hp, s: (b, COL_NA_K // LANES + hp)),
            pl.BlockSpec((ctx_len, LANES), lambda b, hp, s: (b, COL_NA_V // LANES + hp)),
            pl.BlockSpec((heads_per_blk, NA_ROWS, GRID_W, NA_ROWS * GRID_W), lambda b, hp, s: (hp, 0, 0, 0)),
        ],
        out_specs=pl.BlockSpec((tq, LANES), lambda b, hp, s: (b * nsteps + s, hp)),
        out_shape=jax.ShapeDtypeStruct((n_batch * seq, GROUP_WIDTH), F32),
        compiler_params=_params(3),
    )(px, px, px, pz, pz, bias)


def _swa_kernel(q_ref, kp_ref, kc_ref, kn_ref, vp_ref, vc_ref, vn_ref, kz_ref, vz_ref, sink_ref,
                cq_ref, sq_ref, ckp_ref, skp_ref, ckc_ref, skc_ref, ckn_ref, skn_ref, o_ref):
    n = pl.program_id(1)
    nb = pl.num_programs(1)
    dh = HEAD_DIM
    blk = SWA_BLOCK
    quarter = dh // 4
    group = SWA_HEADS // SWA_KV_HEADS
    q = _rope_lanes(q_ref[...], cq_ref[...], sq_ref[...], quarter) * (dh ** -0.5)
    kp = _rope_lanes(kp_ref[...], ckp_ref[...], skp_ref[...], quarter).astype(BF16)
    kc = _rope_lanes(kc_ref[...], ckc_ref[...], skc_ref[...], quarter).astype(BF16)
    kn = _rope_lanes(kn_ref[...], ckn_ref[...], skn_ref[...], quarter).astype(BF16)
    kz = kz_ref[...].astype(BF16)
    vp, vc, vn, vz = (r[...].astype(BF16) for r in (vp_ref, vc_ref, vn_ref, vz_ref))
    qi = lax.broadcasted_iota(jnp.int32, (group * blk, blk), 0) % blk
    kj = lax.broadcasted_iota(jnp.int32, (group * blk, blk), 1)
    ok_p = (kj >= qi) & (n > 0)
    ok_n = (kj <= qi) & (n < nb - 1)
    outs = []
    for hk in range(SWA_KV_HEADS):
        ks = slice(hk * dh, (hk + 1) * dh)
        qs = jnp.concatenate([q[:, (hk * group + g) * dh:(hk * group + g + 1) * dh] for g in range(group)],
                             axis=0).astype(BF16)
        sink = jnp.concatenate([jnp.full((blk, 1), 1.0, F32) * sink_ref[hk * group + g] for g in range(group)],
                               axis=0)
        s_p = jnp.where(ok_p, _dot_nt(qs, kp[:, ks]), NEG_INF)
        s_c = _dot_nt(qs, kc[:, ks])
        s_n = jnp.where(ok_n, _dot_nt(qs, kn[:, ks]), NEG_INF)
        s_z = _dot_nt(qs, kz[:, ks])
        m = jnp.maximum(jnp.maximum(jnp.max(s_p, axis=1, keepdims=True), jnp.max(s_c, axis=1, keepdims=True)),
                        jnp.maximum(jnp.max(s_n, axis=1, keepdims=True), jnp.max(s_z, axis=1, keepdims=True)))
        m = jnp.maximum(m, sink)
        e_p, e_c, e_n, e_z = (jnp.exp(s - m) for s in (s_p, s_c, s_n, s_z))
        den = (jnp.sum(e_p, axis=1, keepdims=True) + jnp.sum(e_c, axis=1, keepdims=True)
               + jnp.sum(e_n, axis=1, keepdims=True) + jnp.sum(e_z, axis=1, keepdims=True) + jnp.exp(sink - m))
        o = (jnp.dot(e_p.astype(BF16), vp[:, ks], preferred_element_type=F32)
             + jnp.dot(e_c.astype(BF16), vc[:, ks], preferred_element_type=F32)
             + jnp.dot(e_n.astype(BF16), vn[:, ks], preferred_element_type=F32)
             + jnp.dot(e_z.astype(BF16), vz[:, ks], preferred_element_type=F32)) / den
        outs += [o[g * blk:(g + 1) * blk, :] for g in range(group)]
    o_ref[...] = jnp.concatenate(outs, axis=1)


def swa_attention(px, pz, n_batch, seq, ctx_len, sink, rope_q, rope_k):
    blk = SWA_BLOCK
    nb = seq // blk
    prev = lambda b, n: b * nb + jnp.maximum(n - 1, 0)
    cur = lambda b, n: b * nb + n
    nxt = lambda b, n: b * nb + jnp.minimum(n + 1, nb - 1)
    kv = lambda off, f: pl.BlockSpec((blk, LANES), lambda b, n: (f(b, n), off // LANES))
    tab = lambda w, f: pl.BlockSpec((blk, w), lambda b, n: (f(0, n), 0))
    cq, sq = rope_q
    ck, sk = rope_k
    return pl.pallas_call(
        _swa_kernel,
        grid=(n_batch, nb),
        in_specs=[
            pl.BlockSpec((blk, GROUP_WIDTH), lambda b, n: (cur(b, n), COL_SWA_Q // GROUP_WIDTH)),
            kv(COL_SWA_K, prev), kv(COL_SWA_K, cur), kv(COL_SWA_K, nxt),
            kv(COL_SWA_V, prev), kv(COL_SWA_V, cur), kv(COL_SWA_V, nxt),
            pl.BlockSpec((ctx_len, LANES), lambda b, n: (b, COL_SWA_K // LANES)),
            pl.BlockSpec((ctx_len, LANES), lambda b, n: (b, COL_SWA_V // LANES)),
            pl.BlockSpec(memory_space=pltpu.SMEM),
            tab(GROUP_WIDTH, cur), tab(GROUP_WIDTH, cur), tab(LANES, prev), tab(LANES, prev),
            tab(LANES, cur), tab(LANES, cur), tab(LANES, nxt), tab(LANES, nxt),
        ],
        out_specs=pl.BlockSpec((blk, GROUP_WIDTH), lambda b, n: (cur(b, n), 0)),
        out_shape=jax.ShapeDtypeStruct((n_batch * seq, GROUP_WIDTH), F32),
        compiler_params=_params(2),
    )(px, px, px, px, px, px, px, pz, pz, sink, cq, sq, ck, sk, ck, sk, ck, sk)


def _ctx_attn_kernel(q_ref, k_ref, v_ref, sink_ref, o_ref, *, group, use_sink):
    dh = HEAD_DIM
    q = (q_ref[...] * (dh ** -0.5)).astype(BF16)
    k = k_ref[...].astype(BF16)
    v = v_ref[...].astype(BF16)
    outs = []
    for qh in range(q.shape[1] // dh):
        ks = slice((qh // group) * dh, (qh // group + 1) * dh)
        s = _dot_nt(q[:, qh * dh:(qh + 1) * dh], k[:, ks])
        m = jnp.max(s, axis=1, keepdims=True)
        if use_sink:
            m = jnp.maximum(m, sink_ref[qh])
        e = jnp.exp(s - m)
        den = jnp.sum(e, axis=1, keepdims=True)
        if use_sink:
            den = den + jnp.exp(sink_ref[qh] - m)
        outs.append(jnp.dot(e.astype(BF16), v[:, ks], preferred_element_type=F32) / den)
    o_ref[...] = jnp.concatenate(outs, axis=1)


def ctx_attention(pz, n_batch, ctx_len, cols, kv_width, sink):
    qc, kc, vc = cols
    use_sink = sink is not None
    if sink is None:
        sink = jnp.zeros((GROUP_WIDTH // HEAD_DIM,), F32)
    return pl.pallas_call(
        functools.partial(_ctx_attn_kernel, group=GROUP_WIDTH // kv_width, use_sink=use_sink),
        grid=(n_batch,),
        in_specs=[pl.BlockSpec((ctx_len, GROUP_WIDTH), lambda b: (b, qc // GROUP_WIDTH)),
                  pl.BlockSpec((ctx_len, kv_width), lambda b: (b, kc // kv_width)),
                  pl.BlockSpec((ctx_len, kv_width), lambda b: (b, vc // kv_width)),
                  pl.BlockSpec(memory_space=pltpu.SMEM)],
        out_specs=pl.BlockSpec((ctx_len, GROUP_WIDTH), lambda b: (b, 0)),
        out_shape=jax.ShapeDtypeStruct((n_batch * ctx_len, GROUP_WIDTH), F32),
        compiler_params=_params(1),
    )(pz, pz, pz, sink)


SCAN_ROWS = 256


def _split3(x):
    a = x.astype(BF16)
    r = x - a.astype(F32)
    b = r.astype(BF16)
    c = (r - b.astype(F32)).astype(BF16)
    return a, b, c


def _scan_kernel(*refs, heads, dk, dv, kind, rope, q_scale, k_scale):
    it = iter(refs)
    qf, kf, vf, qb, kb, vb = (next(it) for _ in range(6))
    if kind == "ret":
        lg = next(it)
    else:
        df, db, wup, bup = (next(it) for _ in range(4))
    if rope:
        cosf, sinf, cosb, sinb = (next(it) for _ in range(4))
    s0 = next(it)
    of, ob, sfin = next(it), next(it), next(it)
    st = next(it)

    s = pl.program_id(1)
    c = SCAN_CHUNK
    nch = SCAN_ROWS // c
    hk = heads * dk

    @pl.when(s == 0)
    def _():
        st[...] = s0[...]

    r_i = lax.broadcasted_iota(jnp.int32, (c, c), 0)
    c_i = lax.broadcasted_iota(jnp.int32, (c, c), 1)
    masks = (r_i >= c_i, c_i > r_i)

    if kind == "ret":
        pos = lax.broadcasted_iota(jnp.int32, (c, hk), 0).astype(F32)
        gcums = ((pos + 1.0) * lg[0:1, :], (float(c) - pos) * lg[1:2, :])
    else:
        rr = lax.broadcasted_iota(jnp.int32, (SCAN_ROWS, SCAN_ROWS), 0)
        cc = lax.broadcasted_iota(jnp.int32, (SCAN_ROWS, SCAN_ROWS), 1)
        same = (rr // c) == (cc // c)
        tris = (jnp.where(same & (rr >= cc), 1.0, 0.0).astype(BF16),
                jnp.where(same & (cc >= rr), 1.0, 0.0).astype(BF16))

        def gate_cum(d_ref, direction):
            pre = jnp.dot(d_ref[...].astype(BF16), wup[direction], preferred_element_type=F32) + bup[direction]
            g = -(jnp.maximum(-pre, 0.0) + jnp.log1p(jnp.exp(-jnp.abs(pre)))) / GLA_TAU
            return sum(jnp.dot(tris[direction], p, preferred_element_type=F32) for p in _split3(g))

        gcums = (gate_cum(df, 0), gate_cum(db, 1))

    def one(direction, q_ref, k_ref, v_ref, o_ref, cos_ref, sin_ref, ch):
        rows = pl.ds(ch * c, c)
        q = q_ref[rows, :]
        k = k_ref[rows, :]
        v = v_ref[rows, :]
        if rope:
            cs = jnp.concatenate([cos_ref[rows, :]] * heads, axis=1)
            sn = jnp.concatenate([sin_ref[rows, :]] * heads, axis=1)
            q = _rope_lanes(q, cs, sn, dk // 4)
            k = _rope_lanes(k, cs, sn, dk // 4)
        if q_scale != 1.0:
            q = q * q_scale
        if k_scale != 1.0:
            k = k * k_scale
        gcum = gcums[direction] if kind == "ret" else gcums[direction][ch * c:(ch + 1) * c, :]
        gtot = gcum[c - 1:c, :] if direction == 0 else gcum[0:1, :]
        q_rel = (q * jnp.exp(gcum - gtot)).astype(BF16)
        k_rel = (k * jnp.exp(gtot - gcum)).astype(BF16)
        q_dec = (q * jnp.exp(gcum)).astype(BF16)
        dec = jnp.exp(gtot)
        vb16 = v.astype(BF16)
        outs = []
        for hd in range(heads):
            ks = slice(hd * dk, (hd + 1) * dk)
            vs = slice(hd * dv, (hd + 1) * dv)
            a = jnp.where(masks[direction], _dot_nt(q_rel[:, ks], k_rel[:, ks]), 0.0)
            state = st[direction, hd]
            o = jnp.dot(a.astype(BF16), vb16[:, vs], preferred_element_type=F32)
            o = o + _dot_nt(q_dec[:, ks], state.astype(BF16))
            st[direction, hd] = dec[:, ks] * state + _dot_tn(vb16[:, vs], k_rel[:, ks])
            outs.append(o)
        o_ref[rows, :] = jnp.concatenate(outs, axis=1)

    for ch in range(nch):
        one(0, qf, kf, vf, of, cosf if rope else None, sinf if rope else None, ch)
        one(1, qb, kb, vb, ob, cosb if rope else None, sinb if rope else None, nch - 1 - ch)

    @pl.when(s == pl.num_programs(1) - 1)
    def _():
        sfin[...] = st[...]


def bidir_scan(p, n_batch, seq, cols, heads, dk, dv, kind, s0, *, lg=None, wup=None, bup=None,
               rope=None, q_scale=1.0, k_scale=1.0):
    t = SCAN_ROWS
    nblk = seq // t
    hk, hv = heads * dk, heads * dv
    qc, kc, vc = cols
    fwd = lambda w, off: pl.BlockSpec((t, w), lambda b, s: (b * nblk + s, off // w))
    bwd = lambda w, off: pl.BlockSpec((t, w), lambda b, s: (b * nblk + nblk - 1 - s, off // w))
    const = lambda shape: pl.BlockSpec(shape, lambda b, s: (0,) * len(shape))
    args = [p] * 6
    specs = [fwd(hk, qc), fwd(hk, kc), fwd(hv, vc), bwd(hk, qc), bwd(hk, kc), bwd(hv, vc)]
    if kind == "ret":
        args += [lg]
        specs += [const((2, hk))]
    else:
        args += [p, p, wup, bup]
        specs += [fwd(LANES, COL_GLA_D), bwd(LANES, COL_GLA_D), const(wup.shape), const(bup.shape)]
    if rope is not None:
        cos, sin = rope
        args += [cos, sin, cos, sin]
        specs += [pl.BlockSpec((t, dk), lambda b, s: (s, 0)), pl.BlockSpec((t, dk), lambda b, s: (s, 0)),
                  pl.BlockSpec((t, dk), lambda b, s: (nblk - 1 - s, 0)),
                  pl.BlockSpec((t, dk), lambda b, s: (nblk - 1 - s, 0))]
    args += [s0]
    state_spec = pl.BlockSpec((None, 2, heads, dv, dk), lambda b, s: (b, 0, 0, 0, 0))
    specs += [state_spec]
    n = n_batch * seq
    kern = functools.partial(_scan_kernel, heads=heads, dk=dk, dv=dv, kind=kind, rope=rope is not None,
                             q_scale=q_scale, k_scale=k_scale)
    return pl.pallas_call(
        kern,
        grid=(n_batch, nblk),
        in_specs=specs,
        out_specs=[pl.BlockSpec((t, hv), lambda b, s: (b * nblk + s, 0)),
                   pl.BlockSpec((t, hv), lambda b, s: (b * nblk + nblk - 1 - s, 0)),
                   state_spec],
        out_shape=[jax.ShapeDtypeStruct((n, hv), F32), jax.ShapeDtypeStruct((n, hv), F32),
                   jax.ShapeDtypeStruct((n_batch, 2, heads, dv, dk), F32)],
        scratch_shapes=[pltpu.VMEM((2, heads, dv, dk), F32)],
        compiler_params=_params(2),
    )(*args)


def _outproj_kernel(na_ref, rf_ref, rb_ref, rg_ref, gf_ref, gb_ref, gg_ref, sw_ref, gn_ref, w_ref, x_ref, mg_ref,
                    o_ref, *, head_w):
    ry = rf_ref[...] + rb_ref[...]
    gy = gf_ref[...] + gb_ref[...]
    r_out, g_out = [], []
    for hd in range(ry.shape[1] // head_w):
        sl = slice(hd * head_w, (hd + 1) * head_w)
        r = ry[:, sl]
        mu = jnp.mean(r, axis=-1, keepdims=True)
        var = jnp.mean(jnp.square(r - mu), axis=-1, keepdims=True)
        r_out.append((r - mu) * lax.rsqrt(var + EPS))
        gq = gy[:, sl]
        g_out.append(gq * lax.rsqrt(jnp.mean(gq * gq, axis=-1, keepdims=True) + EPS) * gn_ref[...])
    ret = jnp.concatenate(r_out, axis=1) * _silu(rg_ref[...])
    gla = jnp.concatenate(g_out, axis=1) * _silu(gg_ref[...])
    mix = jnp.concatenate([na_ref[...], ret, gla, sw_ref[...]], axis=1).astype(BF16)
    o_ref[...] = x_ref[...] + mg_ref[...] * jnp.dot(mix, w_ref[...], preferred_element_type=F32)


def outproj(na, rf, rb, gf, gb, sw, p, gla_norm_g, w_out, x2, mg, rows_per_mod):
    n, d = x2.shape
    gw = GROUP_WIDTH
    tm = 256
    m = mg.shape[0]
    row = lambda w_: pl.BlockSpec((tm, w_), lambda i: (i, 0))
    return pl.pallas_call(
        functools.partial(_outproj_kernel, head_w=RET_DV),
        grid=(n // tm,),
        in_specs=[row(gw), row(gw), row(gw), pl.BlockSpec((tm, gw), lambda i: (i, COL_RET_G // gw)),
                  row(gw), row(gw), pl.BlockSpec((tm, gw), lambda i: (i, COL_GLA_G // gw)), row(gw),
                  pl.BlockSpec((1, GLA_DV), lambda i: (0, 0)),
                  pl.BlockSpec((d, d), lambda i: (0, 0)), row(d),
                  pl.BlockSpec((None, 1, d), lambda i: (i // (rows_per_mod // tm), 0, 0))],
        out_specs=row(d),
        out_shape=jax.ShapeDtypeStruct((n, d), F32),
        compiler_params=_params(1),
    )(na, rf, rb, p, gf, gb, p, sw, gla_norm_g.reshape(1, GLA_DV), w_out, x2, mg.reshape(m, 1, d))


PEER_PAIRS = PEER_HEADS * PEER_TOPK
PEER_ROUTE_TOKENS = 128
PEER_EXPERT_TOKENS = 16
PEER_SUB = 8
PEER_FOLD = D_MODEL // PEER_SUB
INV_SQRT2 = 0.7071067811865476


def _topk_cols(s, payload=None, order=None):
    row = lax.broadcasted_iota(jnp.int32, s.shape, 0) if order is None else order
    vals, idxs = [], []
    for _ in range(PEER_TOPK):
        m = jnp.max(s, axis=0, keepdims=True)
        am = jnp.min(jnp.where(s == m, row, jnp.iinfo(jnp.int32).max), axis=0, keepdims=True)
        sel = row == am
        vals.append(m)
        idxs.append(am if payload is None else jnp.max(jnp.where(sel, payload, -1), axis=0, keepdims=True))
        s = jnp.where(sel, -jnp.inf, s)
    return jnp.concatenate(vals, axis=0), jnp.concatenate(idxs, axis=0)


def _staircase_candidates(v0, i0, v1, i1):
    k = PEER_TOPK
    assert k == 16
    t = v0.shape[1]
    r8 = lax.broadcasted_iota(jnp.int32, (8, t), 0)
    r16 = lax.broadcasted_iota(jnp.int32, (k, t), 0)

    def piece(a_sl, b_sl):
        return v0[a_sl, :] + v1[b_sl, :], i0[a_sl, :] * PEER_N_KEYS + i1[b_sl, :]

    one = lambda j: slice(j, j + 1)
    lo = slice(0, 8)
    pieces = [
        (one(0), slice(0, k), None, r16),
        (one(1), lo, None, k + r8),
        (one(2), lo, r8 <= 4, 2 * k + r8),
        (one(3), lo, r8 <= 3, 3 * k + r8),
        (slice(8, k), one(0), None, (r8 + 8) * k),
        (lo, one(0), r8 >= 4, r8 * k),
        (lo, one(1), r8 >= 4, r8 * k + 1),
        (lo, one(2), r8 == 4, r8 * k + 2),
    ]
    sums, ids, orders = [], [], []
    for a_sl, b_sl, keep, order in pieces:
        s, e = piece(a_sl, b_sl)
        sums.append(s if keep is None else jnp.where(keep, s, NEG_INF))
        ids.append(e)
        orders.append(order)
    return jnp.concatenate(sums, axis=0), jnp.concatenate(ids, axis=0), jnp.concatenate(orders, axis=0)


def _route_head(hb, wq_ref, sk_ref, hd):
    half = PEER_DK // 2
    q = jnp.dot(hb, wq_ref[hd], preferred_element_type=F32)
    tops = []
    for p in range(2):
        qp = q[:, p * half:(p + 1) * half].astype(BF16)
        tops.append(_topk_cols(_dot_nt(sk_ref[p, hd], qp)))
    (v0, i0), (v1, i1) = tops
    best_s, best_e = _topk_cols(*_staircase_candidates(v0, i0, v1, i1))
    e = jnp.exp(best_s - best_s[0:1, :])
    return e / jnp.sum(e, axis=0, keepdims=True), best_e


def _peer_route_kernel(x_ref, g_ref, shift_ref, scale_ref, wq_ref, sk_ref, h_ref, idx_ref, gate_ref):
    x = x_ref[...]
    y = x * lax.rsqrt(jnp.mean(x * x, axis=-1, keepdims=True) + EPS)
    h = (y * g_ref[...]) * (1.0 + scale_ref[...]) + shift_ref[...]
    h_ref[...] = h
    hb = h.astype(BF16)

    def head_body(hd, carry):
        gate_ref[hd], idx_ref[hd] = _route_head(hb, wq_ref, sk_ref, hd)
        return carry

    lax.fori_loop(0, PEER_HEADS, head_body, 0, unroll=4)


def peer_route(x2, g, shift, scale, rows_per_mod, wq_h, sk):
    n, d = x2.shape
    t = PEER_ROUTE_TOKENS
    m = shift.shape[0]
    mod_map = lambda i: (i // (rows_per_mod // t), 0, 0)
    return pl.pallas_call(
        _peer_route_kernel,
        grid=(n // t,),
        in_specs=[
            pl.BlockSpec((t, d), lambda i: (i, 0)),
            pl.BlockSpec((1, d), lambda i: (0, 0)),
            pl.BlockSpec((None, 1, d), mod_map),
            pl.BlockSpec((None, 1, d), mod_map),
            pl.BlockSpec((PEER_HEADS, d, PEER_DK), lambda i: (0, 0, 0)),
            pl.BlockSpec((2, PEER_HEADS, PEER_N_KEYS, PEER_DK // 2), lambda i: (0, 0, 0, 0)),
        ],
        out_specs=[
            pl.BlockSpec((t, d), lambda i: (i, 0)),
            pl.BlockSpec((PEER_HEADS, PEER_TOPK, t), lambda i: (0, 0, i)),
            pl.BlockSpec((PEER_HEADS, PEER_TOPK, t), lambda i: (0, 0, i)),
        ],
        out_shape=[
            jax.ShapeDtypeStruct((n, d), F32),
            jax.ShapeDtypeStruct((PEER_HEADS, PEER_TOPK, n), jnp.int32),
            jax.ShapeDtypeStruct((PEER_HEADS, PEER_TOPK, n), F32),
        ],
        compiler_params=_params(1),
    )(x2, g.reshape(1, d), shift.reshape(m, 1, d), scale.reshape(m, 1, d), wq_h, sk)


def pack_experts(u, v):
    ub = lax.bitcast_convert_type(u.astype(BF16), jnp.uint16).astype(jnp.uint32)
    vb = lax.bitcast_convert_type(v.astype(BF16), jnp.uint16).astype(jnp.uint32)
    words = lax.bitcast_convert_type((ub << 16) | vb, jnp.int32)
    return words.reshape(u.shape[0], PEER_SUB, PEER_FOLD)


def _fold_rows(hbuf, h_rows, tb):
    for s in range(PEER_SUB):
        for c in range(PEER_FOLD // LANES):
            lo = s * PEER_FOLD + c * LANES
            hbuf[c, pl.ds(s * tb, tb), :] = h_rows[:, lo:lo + LANES]


def _unfold_residual(o_ref, x_ref, og_ref, ybuf, tb):
    for s in range(PEER_SUB):
        for c in range(PEER_FOLD // LANES):
            sl = slice(s * PEER_FOLD + c * LANES, s * PEER_FOLD + (c + 1) * LANES)
            o_ref[:, sl] = x_ref[:, sl] + og_ref[:, sl] * ybuf[c, pl.ds(s, tb, stride=PEER_SUB), :]


def _expert_token(j, buf, hbuf, pbuf, wbuf, ybuf, tb, gate_column, start_fetch):
    fold = PEER_FOLD
    hj = jnp.concatenate([hbuf[c, pl.ds(j, PEER_SUB, stride=tb), :] for c in range(fold // LANES)], axis=1)
    for p in range(PEER_PAIRS):
        u = lax.bitcast_convert_type(buf[j, p] & jnp.int32(-65536), F32)
        prod = u * hj
        pbuf[pl.ds(p * PEER_SUB, PEER_SUB), :] = prod[:, :LANES] + prod[:, LANES:]
    part = pbuf[pl.ds(0, PEER_PAIRS, stride=PEER_SUB), :]
    for s in range(1, PEER_SUB):
        part = part + pbuf[pl.ds(s, PEER_PAIRS, stride=PEER_SUB), :]
    sc = jnp.sum(part, axis=1, keepdims=True)
    act = 0.5 * sc * (1.0 + lax.erf(sc * INV_SQRT2))
    g = jnp.concatenate([gate_column(hd) for hd in range(PEER_HEADS)], axis=0)
    wbuf[...] = jnp.broadcast_to(g * act, (PEER_PAIRS, LANES))
    accs = [jnp.zeros((PEER_SUB, fold), F32) for _ in range(4)]
    for p in range(PEER_PAIRS):
        if start_fetch is not None:
            start_fetch(p)
        v = lax.bitcast_convert_type(buf[j, p] << 16, F32)
        wp = jnp.broadcast_to(wbuf[p:p + 1, :], (PEER_SUB, LANES))
        accs[p % 4] = accs[p % 4] + v * jnp.concatenate([wp, wp], axis=1)
    yj = (accs[0] + accs[1]) + (accs[2] + accs[3])
    for c in range(fold // LANES):
        ybuf[c, pl.ds(j * PEER_SUB, PEER_SUB), :] = yj[:, c * LANES:(c + 1) * LANES]


def _peer_expert_kernel(idx0_ref, idxn_ref, h_ref, gate_ref, x_ref, og_ref, uv_hbm, o_ref,
                        buf_even, buf_odd, sem, hbuf, pbuf, wbuf, ybuf):
    i = pl.program_id(0)
    n = pl.num_programs(0)
    tb = PEER_EXPERT_TOKENS
    fold = PEER_FOLD
    bufs = (buf_even, buf_odd)

    def slab_copy(idx_ref, j, r, parity):
        e = idx_ref[0, 0, j * PEER_PAIRS + r]
        return pltpu.make_async_copy(uv_hbm.at[e], bufs[parity].at[j, r], sem.at[parity, j])

    def wait_token(j, parity):
        pltpu.make_async_copy(uv_hbm.at[pl.ds(0, PEER_PAIRS)], bufs[parity].at[j], sem.at[parity, j]).wait()

    @pl.when(i == 0)
    def _():
        def prime(j, carry):
            for r in range(PEER_PAIRS):
                slab_copy(idx0_ref, j, r, 0).start(priority=r % 2)
            return carry
        lax.fori_loop(0, tb, prime, 0)

    lane = lax.broadcasted_iota(jnp.int32, (PEER_TOPK, PEER_ROUTE_TOKENS), 1)
    lane0 = (i % (PEER_ROUTE_TOKENS // tb)) * tb

    _fold_rows(hbuf, h_ref[...], tb)

    def token_body(j, carry, parity, prefetch):
        wait_token(j, parity)
        gate_column = lambda hd: jnp.sum(jnp.where(lane == lane0 + j, gate_ref[hd], 0.0), axis=1, keepdims=True)
        fetch = (lambda p: slab_copy(idxn_ref, j, p, 1 - parity).start(priority=p % 2)) if prefetch else None
        _expert_token(j, bufs[parity], hbuf, pbuf, wbuf, ybuf, tb, gate_column, fetch)
        return carry

    for parity in range(2):
        for prefetch in (True, False):
            @pl.when((i % 2 == parity) & ((i + 1 < n) == prefetch))
            def _(parity=parity, prefetch=prefetch):
                lax.fori_loop(0, tb, functools.partial(token_body, parity=parity, prefetch=prefetch), 0)

    _unfold_residual(o_ref, x_ref, og_ref, ybuf, tb)


def peer_expert(h, idx, gate, x2, out_gate, rows_per_mod, uv):
    n, d = h.shape
    tb = PEER_EXPERT_TOKENS
    nb = n // tb
    m = out_gate.shape[0]
    rows = tb * PEER_PAIRS
    idx_rows = idx.reshape(PEER_PAIRS, n).T.reshape(nb, 1, rows)
    gate_blocks = PEER_ROUTE_TOKENS // tb
    return pl.pallas_call(
        _peer_expert_kernel,
        grid=(nb,),
        in_specs=[
            pl.BlockSpec((1, 1, rows), lambda i: (0, 0, 0), memory_space=pltpu.SMEM),
            pl.BlockSpec((1, 1, rows), lambda i: (jnp.minimum(i + 1, nb - 1), 0, 0), memory_space=pltpu.SMEM),
            pl.BlockSpec((tb, d), lambda i: (i, 0)),
            pl.BlockSpec((PEER_HEADS, PEER_TOPK, PEER_ROUTE_TOKENS), lambda i: (0, 0, i // gate_blocks)),
            pl.BlockSpec((tb, d), lambda i: (i, 0)),
            pl.BlockSpec((None, 1, d), lambda i: (i // (rows_per_mod // tb), 0, 0)),
            pl.BlockSpec(memory_space=pl.ANY),
        ],
        out_specs=pl.BlockSpec((tb, d), lambda i: (i, 0)),
        out_shape=jax.ShapeDtypeStruct((n, d), F32),
        scratch_shapes=[pltpu.VMEM((tb, PEER_PAIRS, PEER_SUB, PEER_FOLD), jnp.int32),
                        pltpu.VMEM((tb, PEER_PAIRS, PEER_SUB, PEER_FOLD), jnp.int32),
                        pltpu.SemaphoreType.DMA((2, tb)),
                        pltpu.VMEM((PEER_FOLD // LANES, PEER_SUB * tb, LANES), F32),
                        pltpu.VMEM((PEER_PAIRS * PEER_SUB, LANES), F32),
                        pltpu.VMEM((PEER_PAIRS, LANES), F32),
                        pltpu.VMEM((PEER_FOLD // LANES, tb * PEER_SUB, LANES), F32)],
        compiler_params=pltpu.CompilerParams(dimension_semantics=("arbitrary",), vmem_limit_bytes=VMEM_LIMIT,
                                             disable_bounds_checks=True),
    )(idx_rows, idx_rows, h, gate, x2, out_gate.reshape(m, 1, d), uv)


PEER_GROUP_STEPS = PEER_ROUTE_TOKENS // PEER_EXPERT_TOKENS
PEER_ITEM_HEADS = 2
PEER_GROUP_ITEMS = PEER_HEADS // PEER_ITEM_HEADS
PEER_ITEM_STEPS = PEER_GROUP_STEPS // PEER_GROUP_ITEMS
PEER_ROUTE_LEAD = PEER_GROUP_ITEMS


def _peer_fused_kernel(xr0_ref, xr_ref, g_ref, sh0_ref, sc0_ref, sh_ref, sc_ref, wq_hbm, sk_ref,
                       x_ref, og_ref, uv_hbm, o_ref,
                       wq_v, wq_sem, hs, hb16, gates, idxv, idxt, idx_s, idx_sem,
                       buf_even, buf_odd, sem, hbuf, pbuf, wbuf, ybuf, *, n_groups):
    i = pl.program_id(0)
    n = pl.num_programs(0)
    tb = PEER_EXPERT_TOKENS
    gs = PEER_GROUP_STEPS
    bufs = (buf_even, buf_odd)
    group = i // gs
    k = i % gs

    def ids_copy(slot):
        return pltpu.make_async_copy(idxt, idx_s.at[slot], idx_sem.at[slot])

    def route_item(t, carry):
        grp = t // PEER_GROUP_ITEMS
        part = t % PEER_GROUP_ITEMS
        slot3 = grp % 3

        def prepare(x_src, shift_ref, scale_ref):
            x = x_src[...]
            y = x * lax.rsqrt(jnp.mean(x * x, axis=-1, keepdims=True) + EPS)
            h = (y * g_ref[...]) * (1.0 + scale_ref[...]) + shift_ref[...]
            hs[slot3] = h
            hb16[...] = h.astype(BF16)

        @pl.when((part == 0) & (grp == 0))
        def _():
            prepare(xr0_ref, sh0_ref, sc0_ref)

        @pl.when((part == 0) & (grp > 0))
        def _():
            prepare(xr_ref, sh_ref, sc_ref)

        hb = hb16[...]
        for local in range(PEER_ITEM_HEADS):
            hd = part * PEER_ITEM_HEADS + local
            gate, ids = _route_head(hb, wq_v, sk_ref, hd)
            gates[slot3, hd] = gate
            idxv[pl.ds(pl.multiple_of(hd * PEER_TOPK, PEER_TOPK), PEER_TOPK), :] = ids

        @pl.when(part == PEER_GROUP_ITEMS - 1)
        def _():
            idxt[...] = idxv[...].T
            ids_copy(grp % 2).start()
        return carry

    def slab_copy(ids_slot, tok, j, r, parity):
        e = idx_s[ids_slot, tok, r]
        return pltpu.make_async_copy(uv_hbm.at[e], bufs[parity].at[j, r], sem.at[parity, j])

    def wait_token(j, parity):
        pltpu.make_async_copy(uv_hbm.at[pl.ds(0, PEER_PAIRS)], bufs[parity].at[j], sem.at[parity, j]).wait()

    @pl.when(i == 0)
    def _():
        cp = pltpu.make_async_copy(wq_hbm, wq_v, wq_sem)
        cp.start()
        cp.wait()

    t_now = i // PEER_ITEM_STEPS + PEER_ROUTE_LEAD
    t_lo = jnp.where(i == 0, 0, t_now)
    t_hi = jnp.where(i % PEER_ITEM_STEPS == 0, jnp.minimum(t_now + 1, n_groups * PEER_GROUP_ITEMS), t_lo)
    lax.fori_loop(t_lo, t_hi, route_item, 0)

    @pl.when(i == 0)
    def _():
        ids_copy(0).wait()

        def prime(j, carry):
            for r in range(PEER_PAIRS):
                slab_copy(0, j, j, r, 0).start(priority=r % 2)
            return carry
        lax.fori_loop(0, tb, prime, 0)

    @pl.when((k == gs - 1) & (group + 1 < n_groups))
    def _():
        ids_copy((group + 1) % 2).wait()

    slot3 = group % 3
    tok0 = pl.multiple_of(k * tb, tb)
    _fold_rows(hbuf, hs[slot3, pl.ds(tok0, tb), :], tb)
    lane = lax.broadcasted_iota(jnp.int32, (PEER_TOPK, PEER_ROUTE_TOKENS), 1)
    nxt = i + 1
    nxt_slot = (nxt // gs) % 2
    nxt_tok0 = (nxt % gs) * tb

    def token_body(j, carry, parity, prefetch):
        wait_token(j, parity)
        gate_column = lambda hd: jnp.sum(jnp.where(lane == tok0 + j, gates[slot3, hd], 0.0), axis=1, keepdims=True)
        fetch = ((lambda p: slab_copy(nxt_slot, nxt_tok0 + j, j, p, 1 - parity).start(priority=p % 2))
                 if prefetch else None)
        _expert_token(j, bufs[parity], hbuf, pbuf, wbuf, ybuf, tb, gate_column, fetch)
        return carry

    for parity in range(2):
        for prefetch in (True, False):
            @pl.when((i % 2 == parity) & ((i + 1 < n) == prefetch))
            def _(parity=parity, prefetch=prefetch):
                lax.fori_loop(0, tb, functools.partial(token_body, parity=parity, prefetch=prefetch), 0)

    _unfold_residual(o_ref, x_ref, og_ref, ybuf, tb)


def peer_fused(x2, g, shift, scale, out_gate, rows_per_mod, wq_h, sk, uv):
    n, d = x2.shape
    tb = PEER_EXPERT_TOKENS
    t = PEER_ROUTE_TOKENS
    gs = PEER_GROUP_STEPS
    n_groups = n // t
    m = shift.shape[0]
    groups_per_mod = rows_per_mod // t
    route_group = lambda i: jnp.minimum(i // PEER_GROUP_STEPS + 1, n_groups - 1)
    mod_next = lambda i: (route_group(i) // groups_per_mod, 0, 0)
    mod3 = lambda a: a.reshape(m, 1, d)
    slab = (tb, PEER_PAIRS, PEER_SUB, PEER_FOLD)
    return pl.pallas_call(
        functools.partial(_peer_fused_kernel, n_groups=n_groups),
        grid=(n // tb,),
        in_specs=[
            pl.BlockSpec((t, d), lambda i: (0, 0)),
            pl.BlockSpec((t, d), lambda i: (route_group(i), 0)),
            pl.BlockSpec((1, d), lambda i: (0, 0)),
            pl.BlockSpec((None, 1, d), lambda i: (0, 0, 0)),
            pl.BlockSpec((None, 1, d), lambda i: (0, 0, 0)),
            pl.BlockSpec((None, 1, d), mod_next),
            pl.BlockSpec((None, 1, d), mod_next),
            pl.BlockSpec(memory_space=pl.ANY),
            pl.BlockSpec((2, PEER_HEADS, PEER_N_KEYS, PEER_DK // 2), lambda i: (0, 0, 0, 0)),
            pl.BlockSpec((tb, d), lambda i: (i, 0)),
            pl.BlockSpec((None, 1, d), lambda i: (i // (rows_per_mod // tb), 0, 0)),
            pl.BlockSpec(memory_space=pl.ANY),
        ],
        out_specs=pl.BlockSpec((tb, d), lambda i: (i, 0)),
        out_shape=jax.ShapeDtypeStruct((n, d), F32),
        scratch_shapes=[
            pltpu.VMEM(wq_h.shape, BF16), pltpu.SemaphoreType.DMA(()),
            pltpu.VMEM((3, t, d), F32),
            pltpu.VMEM((t, d), BF16),
            pltpu.VMEM((3, PEER_HEADS, PEER_TOPK, t), F32),
            pltpu.VMEM((PEER_PAIRS, t), jnp.int32),
            pltpu.VMEM((t, PEER_PAIRS), jnp.int32),
            pltpu.SMEM((2, t, PEER_PAIRS), jnp.int32), pltpu.SemaphoreType.DMA((2,)),
            pltpu.VMEM(slab, jnp.int32), pltpu.VMEM(slab, jnp.int32), pltpu.SemaphoreType.DMA((2, tb)),
            pltpu.VMEM((PEER_FOLD // LANES, PEER_SUB * tb, LANES), F32),
            pltpu.VMEM((PEER_PAIRS * PEER_SUB, LANES), F32),
            pltpu.VMEM((PEER_PAIRS, LANES), F32),
            pltpu.VMEM((PEER_FOLD // LANES, tb * PEER_SUB, LANES), F32),
        ],
        compiler_params=pltpu.CompilerParams(dimension_semantics=("arbitrary",),
                                             vmem_limit_bytes=PEER_FUSED_VMEM_LIMIT, disable_bounds_checks=True),
    )(x2, x2, g.reshape(1, d), mod3(shift), mod3(scale), mod3(shift), mod3(scale), wq_h, sk,
      x2, mod3(out_gate), uv)


PEER_STREAM_LEAD = PEER_HEADS + 3


def _topk_round(s, order, payload=None):
    m = jnp.max(s, axis=0, keepdims=True)
    am = jnp.min(jnp.where(s == m, order, jnp.iinfo(jnp.int32).max), axis=0, keepdims=True)
    sel = order == am
    out = am if payload is None else jnp.max(jnp.where(sel, payload, -1), axis=0, keepdims=True)
    return m, out, jnp.where(sel, NEG_INF, s)


def _peer_stream_kernel(xr0_ref, xr_ref, g_ref, sh0_ref, sc0_ref, sh_ref, sc_ref, wq_hbm, sk_ref,
                        x_ref, og_ref, uv_hbm, o_ref,
                        wq_v, wq_sem, hs, hb16, gates, idxv, idxt, idx_s, idx_sem,
                        s1buf, t1v, t1i, c_s, c_e, c_o, t2v, t2e,
                        buf_even, buf_odd, sem, hbuf, pbuf, wbuf, ybuf, *, n_groups):
    i = pl.program_id(0)
    n = pl.num_programs(0)
    tb = PEER_EXPERT_TOKENS
    assert tb == PEER_TOPK and PEER_GROUP_STEPS == PEER_HEADS
    gs = PEER_GROUP_STEPS
    nh = PEER_HEADS
    total = n_groups * nh
    half = PEER_DK // 2
    bufs = (buf_even, buf_odd)
    group = i // gs
    k = i % gs
    head_rows = lambda hd: pl.ds(pl.multiple_of(hd * PEER_TOPK, PEER_TOPK), PEER_TOPK)

    def ids_copy(slot):
        return pltpu.make_async_copy(idxt, idx_s.at[slot], idx_sem.at[slot])

    def publish_ids(slot):
        idxt[...] = idxv[...].T
        ids_copy(slot).start()

    def prepare(slot3, x_src, shift_ref, scale_ref):
        x = x_src[...]
        y = x * lax.rsqrt(jnp.mean(x * x, axis=-1, keepdims=True) + EPS)
        h = (y * g_ref[...]) * (1.0 + scale_ref[...]) + shift_ref[...]
        hs[slot3] = h
        hb16[...] = h.astype(BF16)

    def whole_head(slot3, hd):
        gate, ids = _route_head(hb16[...], wq_v, sk_ref, hd)
        gates[slot3, hd] = gate
        idxv[head_rows(hd), :] = ids

    def slab_copy(ids_slot, tok, j, r, parity):
        e = idx_s[ids_slot, tok, r]
        return pltpu.make_async_copy(uv_hbm.at[e], bufs[parity].at[j, r], sem.at[parity, j])

    def wait_token(j, parity):
        pltpu.make_async_copy(uv_hbm.at[pl.ds(0, PEER_PAIRS)], bufs[parity].at[j], sem.at[parity, j]).wait()

    @pl.when(i == 0)
    def _():
        cp = pltpu.make_async_copy(wq_hbm, wq_v, wq_sem)
        cp.start()
        cp.wait()
        s1buf[...] = jnp.zeros(s1buf.shape, F32)
        c_s[...] = jnp.zeros(c_s.shape, F32)
        c_e[...] = jnp.zeros(c_e.shape, jnp.int32)
        c_o[...] = jnp.zeros(c_o.shape, jnp.int32)
        prepare(0, xr0_ref, sh0_ref, sc0_ref)
        lax.fori_loop(0, nh, lambda hd, c: (whole_head(0, hd), c)[1], 0)
        publish_ids(0)
        ids_copy(0).wait()
        prepare(1, xr_ref, sh_ref, sc_ref)
        lax.fori_loop(0, PEER_STREAM_LEAD - nh, lambda hd, c: (whole_head(1, hd), c)[1], 0)

        def prime(j, carry):
            for r in range(PEER_PAIRS):
                slab_copy(0, j, j, r, 0).start(priority=r % 2)
            return carry
        lax.fori_loop(0, tb, prime, 0)

    item_c = i + PEER_STREAM_LEAD - 2
    @pl.when((item_c >= PEER_STREAM_LEAD) & (item_c < total))
    def _():
        grp = item_c // nh
        hd = item_c % nh
        best = t2v[...]
        e = jnp.exp(best - best[0:1, :])
        gates[grp % 3, hd] = e / jnp.sum(e, axis=0, keepdims=True)
        idxv[head_rows(hd), :] = t2e[...]

        @pl.when(hd == nh - 1)
        def _():
            publish_ids(grp % 2)

    item_s = i + PEER_STREAM_LEAD - 1
    @pl.when((item_s >= PEER_STREAM_LEAD) & (item_s < total))
    def _():
        c_s[...], c_e[...], c_o[...] = _staircase_candidates(t1v[0], t1i[0], t1v[1], t1i[1])

    item_f = i + PEER_STREAM_LEAD
    @pl.when(item_f < total)
    def _():
        grp = item_f // nh
        hd = item_f % nh

        @pl.when(hd == 0)
        def _():
            prepare(grp % 3, xr_ref, sh_ref, sc_ref)

        q = jnp.dot(hb16[...], wq_v[hd], preferred_element_type=F32)
        for p in range(2):
            s1buf[p] = _dot_nt(sk_ref[p, hd], q[:, p * half:(p + 1) * half].astype(BF16))

    @pl.when((k == gs - 1) & (group + 1 < n_groups))
    def _():
        ids_copy((group + 1) % 2).wait()

    slot3 = group % 3
    tok0 = pl.multiple_of(k * tb, tb)
    _fold_rows(hbuf, hs[slot3, pl.ds(tok0, tb), :], tb)
    lane = lax.broadcasted_iota(jnp.int32, (PEER_TOPK, PEER_ROUTE_TOKENS), 1)
    key_row = lax.broadcasted_iota(jnp.int32, (PEER_N_KEYS, PEER_ROUTE_TOKENS), 0)
    nxt = i + 1
    nxt_slot = (nxt // gs) % 2
    nxt_tok0 = (nxt % gs) * tb

    def routing_round(j):
        row = pl.ds(j, 1)
        for p in range(2):
            m, am, rest = _topk_round(s1buf[p], key_row)
            s1buf[p] = rest
            t1v[p, row, :] = m
            t1i[p, row, :] = am
        m, e, rest = _topk_round(c_s[...], c_o[...], c_e[...])
        c_s[...] = rest
        t2v[row, :] = m
        t2e[row, :] = e

    def token_body(j, carry, parity, prefetch):
        routing_round(j)
        wait_token(j, parity)
        gate_column = lambda hd: jnp.sum(jnp.where(lane == tok0 + j, gates[slot3, hd], 0.0), axis=1, keepdims=True)
        fetch = ((lambda p: slab_copy(nxt_slot, nxt_tok0 + j, j, p, 1 - parity).start(priority=p % 2))
                 if prefetch else None)
        _expert_token(j, bufs[parity], hbuf, pbuf, wbuf, ybuf, tb, gate_column, fetch)
        return carry

    for parity in range(2):
        for prefetch in (True, False):
            @pl.when((i % 2 == parity) & ((i + 1 < n) == prefetch))
            def _(parity=parity, prefetch=prefetch):
                lax.fori_loop(0, tb, functools.partial(token_body, parity=parity, prefetch=prefetch), 0)

    _unfold_residual(o_ref, x_ref, og_ref, ybuf, tb)


def peer_stream(x2, g, shift, scale, out_gate, rows_per_mod, wq_h, sk, uv):
    n, d = x2.shape
    tb = PEER_EXPERT_TOKENS
    t = PEER_ROUTE_TOKENS
    n_groups = n // t
    assert n_groups >= 2
    m = shift.shape[0]
    groups_per_mod = rows_per_mod // t
    route_group = lambda i: jnp.minimum((i + PEER_STREAM_LEAD) // PEER_HEADS, n_groups - 1)
    mod_next = lambda i: (route_group(i) // groups_per_mod, 0, 0)
    mod3 = lambda a: a.reshape(m, 1, d)
    slab = (tb, PEER_PAIRS, PEER_SUB, PEER_FOLD)
    n_cand = 72
    return pl.pallas_call(
        functools.partial(_peer_stream_kernel, n_groups=n_groups),
        grid=(n // tb,),
        in_specs=[
            pl.BlockSpec((t, d), lambda i: (0, 0)),
            pl.BlockSpec((t, d), lambda i: (route_group(i), 0)),
            pl.BlockSpec((1, d), lambda i: (0, 0)),
            pl.BlockSpec((None, 1, d), lambda i: (0, 0, 0)),
            pl.BlockSpec((None, 1, d), lambda i: (0, 0, 0)),
            pl.BlockSpec((None, 1, d), mod_next),
            pl.BlockSpec((None, 1, d), mod_next),
            pl.BlockSpec(memory_space=pl.ANY),
            pl.BlockSpec((2, PEER_HEADS, PEER_N_KEYS, PEER_DK // 2), lambda i: (0, 0, 0, 0)),
            pl.BlockSpec((tb, d), lambda i: (i, 0)),
            pl.BlockSpec((None, 1, d), lambda i: (i // (rows_per_mod // tb), 0, 0)),
            pl.BlockSpec(memory_space=pl.ANY),
        ],
        out_specs=pl.BlockSpec((tb, d), lambda i: (i, 0)),
        out_shape=jax.ShapeDtypeStruct((n, d), F32),
        scratch_shapes=[
            pltpu.VMEM(wq_h.shape, BF16), pltpu.SemaphoreType.DMA(()),
            pltpu.VMEM((3, t, d), F32),
            pltpu.VMEM((t, d), BF16),
            pltpu.VMEM((3, PEER_HEADS, PEER_TOPK, t), F32),
            pltpu.VMEM((PEER_PAIRS, t), jnp.int32),
            pltpu.VMEM((t, PEER_PAIRS), jnp.int32),
            pltpu.SMEM((2, t, PEER_PAIRS), jnp.int32), pltpu.SemaphoreType.DMA((2,)),
            pltpu.VMEM((2, PEER_N_KEYS, t), F32),
            pltpu.VMEM((2, PEER_TOPK, t), F32), pltpu.VMEM((2, PEER_TOPK, t), jnp.int32),
            pltpu.VMEM((n_cand, t), F32), pltpu.VMEM((n_cand, t), jnp.int32), pltpu.VMEM((n_cand, t), jnp.int32),
            pltpu.VMEM((PEER_TOPK, t), F32), pltpu.VMEM((PEER_TOPK, t), jnp.int32),
            pltpu.VMEM(slab, jnp.int32), pltpu.VMEM(slab, jnp.int32), pltpu.SemaphoreType.DMA((2, tb)),
            pltpu.VMEM((PEER_FOLD // LANES, PEER_SUB * tb, LANES), F32),
            pltpu.VMEM((PEER_PAIRS * PEER_SUB, LANES), F32),
            pltpu.VMEM((PEER_PAIRS, LANES), F32),
            pltpu.VMEM((PEER_FOLD // LANES, tb * PEER_SUB, LANES), F32),
        ],
        compiler_params=pltpu.CompilerParams(dimension_semantics=("arbitrary",),
                                             vmem_limit_bytes=PEER_FUSED_VMEM_LIMIT, disable_bounds_checks=True),
    )(x2, x2, g.reshape(1, d), mod3(shift), mod3(scale), mod3(shift), mod3(scale), wq_h, sk,
      x2, mod3(out_gate), uv)


def peer_residual(x2, g, shift, scale, out_gate, rows_per_mod, wq_h, sk, uv):
    h, idx, gate = peer_route(x2, g, shift, scale, rows_per_mod, wq_h, sk)
    return peer_expert(h, idx, gate, x2, out_gate, rows_per_mod, uv)


def _final_norm_kernel(x_ref, g_ref, o_ref):
    x = x_ref[...]
    y = x * lax.rsqrt(jnp.mean(x * x, axis=-1, keepdims=True) + EPS)
    o_ref[...] = y * g_ref[...]


def final_norm(x2, g):
    n, d = x2.shape
    tm = 512
    return pl.pallas_call(
        _final_norm_kernel,
        grid=(n // tm,),
        in_specs=[pl.BlockSpec((tm, d), lambda i: (i, 0)), pl.BlockSpec((1, d), lambda i: (0, 0))],
        out_specs=pl.BlockSpec((tm, d), lambda i: (i, 0)),
        out_shape=jax.ShapeDtypeStruct((n, d), x2.dtype),
        compiler_params=_params(1),
    )(x2, g.reshape(1, d))


def _mixers(p, pz, n_batch, seq, ctx_len, is_ctx, tables, prm):
    if is_ctx:
        na = ctx_attention(p, n_batch, seq, (COL_NA_Q, COL_NA_K, COL_NA_V), GROUP_WIDTH, None)
        sw = ctx_attention(p, n_batch, seq, (COL_SWA_Q, COL_SWA_K, COL_SWA_V), SWA_KV_HEADS * HEAD_DIM, prm["sink"])
    else:
        na = na_attention(p, pz, n_batch, seq, ctx_len, prm["na_bias"])
        sw = swa_attention(p, pz, n_batch, seq, ctx_len, prm["sink"], tables["swa_q"], tables["swa_k"])
    rf, rb, rs = bidir_scan(p, n_batch, seq, (COL_RET_Q, COL_RET_K, COL_RET_V), RET_HEADS, RET_DK, RET_DV, "ret",
                            prm["ret_s0"], lg=prm["ret_lg"], rope=None if is_ctx else tables["ret"],
                            k_scale=RET_DK ** -0.5)
    gf, gb, gs = bidir_scan(p, n_batch, seq, (COL_GLA_Q, COL_GLA_K, COL_GLA_V), GLA_HEADS, GLA_DK, GLA_DV, "gla",
                            prm["gla_s0"], wup=prm["gla_wup"], bup=prm["gla_bup"], q_scale=GLA_DK ** -0.5)
    return (na, rf, rb, gf, gb, sw), (rs, gs)


def kernel(x, c, ctx, c_ctx, w_ada, b_ada, norm_attn_g, norm_ffn_g, w_in, na_rpb, ret_log_gamma,
           gla_w_gate_up, gla_b_gate, gla_norm_g, swa_sink, w_out, peer_w_q, peer_sub_keys,
           peer_u, peer_v, final_g):
    bsz, slen, d = x.shape
    zlen = ctx.shape[1]
    x2 = x.reshape(bsz * slen, d)
    z2 = ctx.reshape(bsz * zlen, d)
    tables = {"ret": rope_lane_tables(slen, RET_DK, 1),
              "swa_q": rope_lane_tables(slen, HEAD_DIM, SWA_HEADS),
              "swa_k": rope_lane_tables(slen, HEAD_DIM, SWA_KV_HEADS)}
    c_rows = jnp.zeros((8, d), F32).at[:bsz].set(c).at[bsz].set(c_ctx)
    for layer in range(DEPTH):
        has_next = layer < DEPTH - 1
        mod = adaln(c_rows, w_ada[layer], b_ada[layer])
        mx = [mod[:bsz, k * d:(k + 1) * d] for k in range(6)]
        mz = [mod[bsz:bsz + 1, k * d:(k + 1) * d] for k in range(6)]

        wi = w_in[layer]
        wp = jnp.concatenate([wi[:, :REF_COL_GLA_D], wi[:, REF_COL_SWA_Q:], wi[:, REF_COL_GLA_D:REF_COL_SWA_Q],
                              jnp.zeros((d, PROJ_WIDTH - REF_D_IN), F32)], axis=1).astype(BF16)
        px = modproj(x2, norm_attn_g[layer], mx[0], mx[1], slen, wp)
        pz = modproj(z2, norm_attn_g[layer], mz[0], mz[1], bsz * zlen, wp)

        wup = (jnp.zeros((2, LANES, GLA_HEADS * GLA_DK), F32)
               .at[0, :GLA_RANK].set(gla_w_gate_up[layer, 0])
               .at[1, GLA_RANK:2 * GLA_RANK].set(gla_w_gate_up[layer, 1])).astype(BF16)
        prm = {"na_bias": na_band_bias(na_rpb[layer]), "sink": swa_sink[layer],
               "ret_lg": jnp.repeat(ret_log_gamma[layer], RET_DK, axis=1),
               "gla_wup": wup, "gla_bup": gla_b_gate[layer].reshape(2, 1, GLA_HEADS * GLA_DK),
               "ret_s0": jnp.zeros((bsz, 2, RET_HEADS, RET_DV, RET_DK), F32),
               "gla_s0": jnp.zeros((bsz, 2, GLA_HEADS, GLA_DV, GLA_DK), F32)}
        mix_z, (ret_s, gla_s) = _mixers(pz, pz, bsz, zlen, zlen, True, tables, prm)
        prm["ret_s0"], prm["gla_s0"] = ret_s, gla_s
        mix_x, _ = _mixers(px, pz, bsz, slen, zlen, False, tables, prm)

        wo = w_out[layer].astype(BF16)
        x2 = outproj(*mix_x, px, gla_norm_g[layer], wo, x2, mx[2], slen)

        uv = pack_experts(peer_u[layer], peer_v[layer])
        wq_h = peer_w_q[layer].reshape(d, PEER_HEADS, PEER_DK).transpose(1, 0, 2).astype(BF16)
        sk = peer_sub_keys[layer].astype(BF16)
        x2 = peer_stream(x2, norm_ffn_g[layer], mx[3], mx[4], mx[5], slen, wq_h, sk, uv)
        if has_next:
            z2 = outproj(*mix_z, pz, gla_norm_g[layer], wo, z2, mz[2], bsz * zlen)
            z2 = peer_residual(z2, norm_ffn_g[layer], mz[3], mz[4], mz[5], bsz * zlen, wq_h, sk, uv)
    return final_norm(x2, final_g).reshape(bsz, slen, d)
```

```python
import functools

import jax
import jax.numpy as jnp
import numpy as np
from jax import lax
from jax.experimental import pallas as pl
from jax.experimental.pallas import tpu as pltpu

D_MODEL = 2048
DEPTH = 2
GRID_W = 64
EPS = 1e-6
ROPE_BASE = 10000.0

GROUP_WIDTH = D_MODEL // 4
HEAD_DIM = 64
NA_HEADS = GROUP_WIDTH // HEAD_DIM
NA_ROWS = 8
NA_COLS = 16
RET_HEADS = 4
RET_DK = GROUP_WIDTH // RET_HEADS
RET_DV = GROUP_WIDTH // RET_HEADS
GLA_HEADS = 4
GLA_DV = GROUP_WIDTH // GLA_HEADS
GLA_DK = GLA_DV // 2
GLA_RANK = 16
GLA_TAU = 16.0
SWA_HEADS = GROUP_WIDTH // HEAD_DIM
SWA_KV_HEADS = SWA_HEADS // 4
SWA_WINDOW = 128
SWA_BLOCK = 128
SCAN_CHUNK = 64
PEER_HEADS = 8
PEER_N_KEYS = 128
PEER_N_EXPERTS = PEER_N_KEYS * PEER_N_KEYS
PEER_DK = 256
PEER_TOPK = 16

LANES = 128
VMEM_LIMIT = 48 * 1024 * 1024
BF16 = jnp.bfloat16
F32 = jnp.float32
NEG_INF = float("-inf")

COL_NA_Q, COL_NA_K, COL_NA_V = 0, 512, 1024
COL_RET_Q, COL_RET_K, COL_RET_V, COL_RET_G = 1536, 2048, 2560, 3072
COL_GLA_Q, COL_GLA_K, COL_GLA_V, COL_GLA_G = 3584, 3840, 4096, 4608
COL_SWA_Q, COL_SWA_K, COL_SWA_V = 5120, 5632, 5760
COL_GLA_D = 5888
REF_COL_GLA_D, REF_COL_SWA_Q, REF_D_IN = 5120, 5152, 5920
PROJ_WIDTH = 6144


def _silu(x):
    return x / (1.0 + jnp.exp(-x))


def _dot_nt(a, b):
    return lax.dot_general(a, b, (((1,), (1,)), ((), ())), preferred_element_type=F32)


def _dot_tn(a, b):
    return lax.dot_general(a, b, (((0,), (0,)), ((), ())), preferred_element_type=F32)


def _params(n_axes):
    return pltpu.CompilerParams(dimension_semantics=("arbitrary",) * n_axes, vmem_limit_bytes=VMEM_LIMIT)


def _rope_lanes(x, cs, sn, quarter):
    n = x.shape[-1]
    lane = lax.broadcasted_iota(jnp.int32, x.shape, x.ndim - 1)
    first = (lane % (2 * quarter)) < quarter
    swapped = jnp.where(first, pltpu.roll(x, n - quarter, x.ndim - 1), pltpu.roll(x, quarter, x.ndim - 1))
    return x * cs + swapped * sn


def rope_lane_tables(length, dh, copies):
    t = jnp.arange(length)
    pos = jnp.stack([t // GRID_W, t % GRID_W], axis=-1).astype(F32)
    quarter = dh // 4
    inv = ROPE_BASE ** (-jnp.arange(quarter, dtype=F32) / quarter)
    ang = pos[:, :, None] * inv
    cos, sin = jnp.cos(ang), jnp.sin(ang)
    cl = jnp.concatenate([cos[:, 0], cos[:, 0], cos[:, 1], cos[:, 1]], axis=-1)
    sl = jnp.concatenate([-sin[:, 0], sin[:, 0], -sin[:, 1], sin[:, 1]], axis=-1)
    return jnp.tile(cl, (1, copies)), jnp.tile(sl, (1, copies))


def _adaln_kernel(c_ref, w_ref, b_ref, o_ref):
    a = _silu(c_ref[...]).astype(BF16)
    o_ref[...] = jnp.dot(a, w_ref[...].astype(BF16), preferred_element_type=F32) + b_ref[...]


def adaln(c_rows, w, b):
    r, d = c_rows.shape
    m = w.shape[1]
    tn = 1024
    return pl.pallas_call(
        _adaln_kernel,
        grid=(m // tn,),
        in_specs=[pl.BlockSpec((r, d), lambda j: (0, 0)), pl.BlockSpec((d, tn), lambda j: (0, j)),
                  pl.BlockSpec((1, tn), lambda j: (0, j))],
        out_specs=pl.BlockSpec((r, tn), lambda j: (0, j)),
        out_shape=jax.ShapeDtypeStruct((r, m), F32),
        compiler_params=_params(1),
    )(c_rows, w, b.reshape(1, m))


def _modproj_kernel(x_ref, g_ref, shift_ref, scale_ref, w_ref, o_ref, hb_ref):
    @pl.when(pl.program_id(1) == 0)
    def _():
        x = x_ref[...]
        y = x * lax.rsqrt(jnp.mean(x * x, axis=-1, keepdims=True) + EPS)
        hb_ref[...] = ((y * g_ref[...]) * (1.0 + scale_ref[...]) + shift_ref[...]).astype(BF16)

    o_ref[...] = jnp.dot(hb_ref[...], w_ref[...], preferred_element_type=F32)


def modproj(x2, g, shift, scale, rows_per_mod, w):
    n, d = x2.shape
    wid = w.shape[1]
    tm = min(512, rows_per_mod)
    tn = 2048
    m = shift.shape[0]
    mod_map = lambda i, j: (i // (rows_per_mod // tm), 0, 0)
    return pl.pallas_call(
        _modproj_kernel,
        grid=(n // tm, wid // tn),
        in_specs=[pl.BlockSpec((tm, d), lambda i, j: (i, 0)), pl.BlockSpec((1, d), lambda i, j: (0, 0)),
                  pl.BlockSpec((None, 1, d), mod_map), pl.BlockSpec((None, 1, d), mod_map),
                  pl.BlockSpec((d, tn), lambda i, j: (0, j))],
        out_specs=pl.BlockSpec((tm, tn), lambda i, j: (i, j)),
        out_shape=jax.ShapeDtypeStruct((n, wid), F32),
        scratch_shapes=[pltpu.VMEM((tm, d), BF16)],
        compiler_params=_params(2),
    )(x2, g.reshape(1, d), shift.reshape(m, 1, d), scale.reshape(m, 1, d), w)


NA_QROWS = 4


def _na_kernel(q_ref, k_ref, v_ref, kz_ref, vz_ref, bias_ref, o_ref, *, rows):
    step = pl.program_id(2)
    dh = HEAD_DIM
    band = NA_ROWS * GRID_W
    kz = kz_ref[...].astype(BF16)
    vz = vz_ref[...].astype(BF16)
    heads = LANES // dh
    units = [(qr, hh) for qr in range(NA_QROWS) for hh in range(heads)]
    scores, vbands = {}, {}
    for qr in range(NA_QROWS):
        r = step * NA_QROWS + qr
        start = jnp.clip(r - NA_ROWS // 2, 0, rows - NA_ROWS)
        dr0 = start - r + NA_ROWS - 1
        tok0 = pl.multiple_of(start * GRID_W, GRID_W)
        kb = k_ref[pl.ds(tok0, band), :].astype(BF16)
        vbands[qr] = v_ref[pl.ds(tok0, band), :].astype(BF16)
        q = (q_ref[pl.ds(qr * GRID_W, GRID_W), :] * (dh ** -0.5)).astype(BF16)
        for hh in range(heads):
            sl = slice(hh * dh, (hh + 1) * dh)
            scores[qr, hh] = (_dot_nt(q[:, sl], kb[:, sl]) + bias_ref[hh, dr0],
                              _dot_nt(q[:, sl], kz[:, sl]))
    probs = {}
    for u in units:
        s_nb, s_cx = scores[u]
        m = jnp.maximum(jnp.max(s_nb, axis=1, keepdims=True), jnp.max(s_cx, axis=1, keepdims=True))
        p_nb = jnp.exp(s_nb - m)
        p_cx = jnp.exp(s_cx - m)
        den = jnp.sum(p_nb, axis=1, keepdims=True) + jnp.sum(p_cx, axis=1, keepdims=True)
        probs[u] = (p_nb.astype(BF16), p_cx.astype(BF16), den)
    for qr in range(NA_QROWS):
        outs = []
        for hh in range(heads):
            sl = slice(hh * dh, (hh + 1) * dh)
            p_nb, p_cx, den = probs[qr, hh]
            o = (jnp.dot(p_nb, vbands[qr][:, sl], preferred_element_type=F32)
                 + jnp.dot(p_cx, vz[:, sl], preferred_element_type=F32))
            outs.append(o / den)
        o_ref[pl.ds(qr * GRID_W, GRID_W), :] = jnp.concatenate(outs, axis=1)


def na_band_bias(rpb):
    col = jnp.arange(GRID_W)
    col_start = jnp.clip(col - NA_COLS // 2, 0, GRID_W - NA_COLS)
    col_ok = (col[None, :] >= col_start[:, None]) & (col[None, :] < col_start[:, None] + NA_COLS)
    d_col = jnp.clip(col[None, :] - col[:, None], -(NA_COLS - 1), NA_COLS - 1) + NA_COLS - 1
    d_row = jnp.arange(NA_ROWS)[:, None] + jnp.arange(NA_ROWS)[None, :]
    b = rpb.astype(F32)[:, d_row][..., d_col]
    b = jnp.where(col_ok[None, None, None], b, NEG_INF)
    return b.transpose(0, 1, 3, 2, 4).reshape(rpb.shape[0], NA_ROWS, GRID_W, NA_ROWS * GRID_W)


def na_attention(px, pz, n_batch, seq, ctx_len, bias):
    rows = seq // GRID_W
    tq = NA_QROWS * GRID_W
    nsteps = rows // NA_QROWS
    heads_per_blk = LANES // HEAD_DIM
    return pl.pallas_call(
        functools.partial(_na_kernel, rows=rows),
        grid=(n_batch, NA_HEADS // heads_per_blk, nsteps),
        in_specs=[
            pl.BlockSpec((tq, LANES), lambda b, hp, s: (b * nsteps + s, COL_NA_Q // LANES + hp)),
            pl.BlockSpec((seq, LANES), lambda b, hp, s: (b, COL_NA_K // LANES + hp)),
            pl.BlockSpec((seq, LANES), lambda b, hp, s: (b, COL_NA_V // LANES + hp)),
            pl.BlockSpec((ctx_len, LANES), lambda b, hp, s: (b, COL_NA_K // LANES + hp)),
            pl.BlockSpec((ctx_len, LANES), lambda b, hp, s: (b, COL_NA_V // LANES + hp)),
            pl.BlockSpec((heads_per_blk, NA_ROWS, GRID_W, NA_ROWS * GRID_W), lambda b, hp, s: (hp, 0, 0, 0)),
        ],
        out_specs=pl.BlockSpec((tq, LANES), lambda b, hp, s: (b * nsteps + s, hp)),
        out_shape=jax.ShapeDtypeStruct((n_batch * seq, GROUP_WIDTH), F32),
        compiler_params=_params(3),
    )(px, px, px, pz, pz, bias)


def _swa_kernel(q_ref, kp_ref, kc_ref, kn_ref, vp_ref, vc_ref, vn_ref, kz_ref, vz_ref, sink_ref,
                cq_ref, sq_ref, ckp_ref, skp_ref, ckc_ref, skc_ref, ckn_ref, skn_ref, o_ref):
    n = pl.program_id(1)
    nb = pl.num_programs(1)
    dh = HEAD_DIM
    blk = SWA_BLOCK
    quarter = dh // 4
    group = SWA_HEADS // SWA_KV_HEADS
    q = _rope_lanes(q_ref[...], cq_ref[...], sq_ref[...], quarter) * (dh ** -0.5)
    kp = _rope_lanes(kp_ref[...], ckp_ref[...], skp_ref[...], quarter).astype(BF16)
    kc = _rope_lanes(kc_ref[...], ckc_ref[...], skc_ref[...], quarter).astype(BF16)
    kn = _rope_lanes(kn_ref[...], ckn_ref[...], skn_ref[...], quarter).astype(BF16)
    kz = kz_ref[...].astype(BF16)
    vp, vc, vn, vz = (r[...].astype(BF16) for r in (vp_ref, vc_ref, vn_ref, vz_ref))
    qi = lax.broadcasted_iota(jnp.int32, (group * blk, blk), 0) % blk
    kj = lax.broadcasted_iota(jnp.int32, (group * blk, blk), 1)
    ok_p = (kj >= qi) & (n > 0)
    ok_n = (kj <= qi) & (n < nb - 1)
    outs = []
    for hk in range(SWA_KV_HEADS):
        ks = slice(hk * dh, (hk + 1) * dh)
        qs = jnp.concatenate([q[:, (hk * group + g) * dh:(hk * group + g + 1) * dh] for g in range(group)],
                             axis=0).astype(BF16)
        sink = jnp.concatenate([jnp.full((blk, 1), 1.0, F32) * sink_ref[hk * group + g] for g in range(group)],
                               axis=0)
        s_p = jnp.where(ok_p, _dot_nt(qs, kp[:, ks]), NEG_INF)
        s_c = _dot_nt(qs, kc[:, ks])
        s_n = jnp.where(ok_n, _dot_nt(qs, kn[:, ks]), NEG_INF)
        s_z = _dot_nt(qs, kz[:, ks])
        m = jnp.maximum(jnp.maximum(jnp.max(s_p, axis=1, keepdims=True), jnp.max(s_c, axis=1, keepdims=True)),
                        jnp.maximum(jnp.max(s_n, axis=1, keepdims=True), jnp.max(s_z, axis=1, keepdims=True)))
        m = jnp.maximum(m, sink)
        e_p, e_c, e_n, e_z = (jnp.exp(s - m) for s in (s_p, s_c, s_n, s_z))
        den = (jnp.sum(e_p, axis=1, keepdims=True) + jnp.sum(e_c, axis=1, keepdims=True)
               + jnp.sum(e_n, axis=1, keepdims=True) + jnp.sum(e_z, axis=1, keepdims=True) + jnp.exp(sink - m))
        o = (jnp.dot(e_p.astype(BF16), vp[:, ks], preferred_element_type=F32)
             + jnp.dot(e_c.astype(BF16), vc[:, ks], preferred_element_type=F32)
             + jnp.dot(e_n.astype(BF16), vn[:, ks], preferred_element_type=F32)
             + jnp.dot(e_z.astype(BF16), vz[:, ks], preferred_element_type=F32)) / den
        outs += [o[g * blk:(g + 1) * blk, :] for g in range(group)]
    o_ref[...] = jnp.concatenate(outs, axis=1)


def swa_attention(px, pz, n_batch, seq, ctx_len, sink, rope_q, rope_k):
    blk = SWA_BLOCK
    nb = seq // blk
    prev = lambda b, n: b * nb + jnp.maximum(n - 1, 0)
    cur = lambda b, n: b * nb + n
    nxt = lambda b, n: b * nb + jnp.minimum(n + 1, nb - 1)
    kv = lambda off, f: pl.BlockSpec((blk, LANES), lambda b, n: (f(b, n), off // LANES))
    tab = lambda w, f: pl.BlockSpec((blk, w), lambda b, n: (f(0, n), 0))
    cq, sq = rope_q
    ck, sk = rope_k
    return pl.pallas_call(
        _swa_kernel,
        grid=(n_batch, nb),
        in_specs=[
            pl.BlockSpec((blk, GROUP_WIDTH), lambda b, n: (cur(b, n), COL_SWA_Q // GROUP_WIDTH)),
            kv(COL_SWA_K, prev), kv(COL_SWA_K, cur), kv(COL_SWA_K, nxt),
            kv(COL_SWA_V, prev), kv(COL_SWA_V, cur), kv(COL_SWA_V, nxt),
            pl.BlockSpec((ctx_len, LANES), lambda b, n: (b, COL_SWA_K // LANES)),
            pl.BlockSpec((ctx_len, LANES), lambda b, n: (b, COL_SWA_V // LANES)),
            pl.BlockSpec(memory_space=pltpu.SMEM),
            tab(GROUP_WIDTH, cur), tab(GROUP_WIDTH, cur), tab(LANES, prev), tab(LANES, prev),
            tab(LANES, cur), tab(LANES, cur), tab(LANES, nxt), tab(LANES, nxt),
        ],
        out_specs=pl.BlockSpec((blk, GROUP_WIDTH), lambda b, n: (cur(b, n), 0)),
        out_shape=jax.ShapeDtypeStruct((n_batch * seq, GROUP_WIDTH), F32),
        compiler_params=_params(2),
    )(px, px, px, px, px, px, px, pz, pz, sink, cq, sq, ck, sk, ck, sk, ck, sk)


def _ctx_attn_kernel(q_ref, k_ref, v_ref, sink_ref, o_ref, *, group, use_sink):
    dh = HEAD_DIM
    q = (q_ref[...] * (dh ** -0.5)).astype(BF16)
    k = k_ref[...].astype(BF16)
    v = v_ref[...].astype(BF16)
    outs = []
    for qh in range(q.shape[1] // dh):
        ks = slice((qh // group) * dh, (qh // group + 1) * dh)
        s = _dot_nt(q[:, qh * dh:(qh + 1) * dh], k[:, ks])
        m = jnp.max(s, axis=1, keepdims=True)
        if use_sink:
            m = jnp.maximum(m, sink_ref[qh])
        e = jnp.exp(s - m)
        den = jnp.sum(e, axis=1, keepdims=True)
        if use_sink:
            den = den + jnp.exp(sink_ref[qh] - m)
        outs.append(jnp.dot(e.astype(BF16), v[:, ks], preferred_element_type=F32) / den)
    o_ref[...] = jnp.concatenate(outs, axis=1)


def ctx_attention(pz, n_batch, ctx_len, cols, kv_width, sink):
    qc, kc, vc = cols
    use_sink = sink is not None
    if sink is None:
        sink = jnp.zeros((GROUP_WIDTH // HEAD_DIM,), F32)
    return pl.pallas_call(
        functools.partial(_ctx_attn_kernel, group=GROUP_WIDTH // kv_width, use_sink=use_sink),
        grid=(n_batch,),
        in_specs=[pl.BlockSpec((ctx_len, GROUP_WIDTH), lambda b: (b, qc // GROUP_WIDTH)),
                  pl.BlockSpec((ctx_len, kv_width), lambda b: (b, kc // kv_width)),
                  pl.BlockSpec((ctx_len, kv_width), lambda b: (b, vc // kv_width)),
                  pl.BlockSpec(memory_space=pltpu.SMEM)],
        out_specs=pl.BlockSpec((ctx_len, GROUP_WIDTH), lambda b: (b, 0)),
        out_shape=jax.ShapeDtypeStruct((n_batch * ctx_len, GROUP_WIDTH), F32),
        compiler_params=_params(1),
    )(pz, pz, pz, sink)


SCAN_ROWS = 256


def _split3(x):
    a = x.astype(BF16)
    r = x - a.astype(F32)
    b = r.astype(BF16)
    c = (r - b.astype(F32)).astype(BF16)
    return a, b, c


def _scan_kernel(*refs, heads, dk, dv, kind, rope, q_scale, k_scale):
    it = iter(refs)
    qf, kf, vf, qb, kb, vb = (next(it) for _ in range(6))
    if kind == "ret":
        lg = next(it)
    else:
        df, db, wup, bup = (next(it) for _ in range(4))
    if rope:
        cosf, sinf, cosb, sinb = (next(it) for _ in range(4))
    s0 = next(it)
    of, ob, sfin = next(it), next(it), next(it)
    st = next(it)

    s = pl.program_id(1)
    c = SCAN_CHUNK
    nch = SCAN_ROWS // c
    hk = heads * dk

    @pl.when(s == 0)
    def _():
        st[...] = s0[...]

    r_i = lax.broadcasted_iota(jnp.int32, (c, c), 0)
    c_i = lax.broadcasted_iota(jnp.int32, (c, c), 1)
    masks = (r_i >= c_i, c_i > r_i)

    if kind == "ret":
        pos = lax.broadcasted_iota(jnp.int32, (c, hk), 0).astype(F32)
        gcums = ((pos + 1.0) * lg[0:1, :], (float(c) - pos) * lg[1:2, :])
    else:
        rr = lax.broadcasted_iota(jnp.int32, (SCAN_ROWS, SCAN_ROWS), 0)
        cc = lax.broadcasted_iota(jnp.int32, (SCAN_ROWS, SCAN_ROWS), 1)
        same = (rr // c) == (cc // c)
        tris = (jnp.where(same & (rr >= cc), 1.0, 0.0).astype(BF16),
                jnp.where(same & (cc >= rr), 1.0, 0.0).astype(BF16))

        def gate_cum(d_ref, direction):
            pre = jnp.dot(d_ref[...].astype(BF16), wup[direction], preferred_element_type=F32) + bup[direction]
            g = -(jnp.maximum(-pre, 0.0) + jnp.log1p(jnp.exp(-jnp.abs(pre)))) / GLA_TAU
            return sum(jnp.dot(tris[direction], p, preferred_element_type=F32) for p in _split3(g))

        gcums = (gate_cum(df, 0), gate_cum(db, 1))

    def one(direction, q_ref, k_ref, v_ref, o_ref, cos_ref, sin_ref, ch):
        rows = pl.ds(ch * c, c)
        q = q_ref[rows, :]
        k = k_ref[rows, :]
        v = v_ref[rows, :]
        if rope:
            cs = jnp.concatenate([cos_ref[rows, :]] * heads, axis=1)
            sn = jnp.concatenate([sin_ref[rows, :]] * heads, axis=1)
            q = _rope_lanes(q, cs, sn, dk // 4)
            k = _rope_lanes(k, cs, sn, dk // 4)
        if q_scale != 1.0:
            q = q * q_scale
        if k_scale != 1.0:
            k = k * k_scale
        gcum = gcums[direction] if kind == "ret" else gcums[direction][ch * c:(ch + 1) * c, :]
        gtot = gcum[c - 1:c, :] if direction == 0 else gcum[0:1, :]
        q_rel = (q * jnp.exp(gcum - gtot)).astype(BF16)
        k_rel = (k * jnp.exp(gtot - gcum)).astype(BF16)
        q_dec = (q * jnp.exp(gcum)).astype(BF16)
        dec = jnp.exp(gtot)
        vb16 = v.astype(BF16)
        outs = []
        for hd in range(heads):
            ks = slice(hd * dk, (hd + 1) * dk)
            vs = slice(hd * dv, (hd + 1) * dv)
            a = jnp.where(masks[direction], _dot_nt(q_rel[:, ks], k_rel[:, ks]), 0.0)
            state = st[direction, hd]
            o = jnp.dot(a.astype(BF16), vb16[:, vs], preferred_element_type=F32)
            o = o + _dot_nt(q_dec[:, ks], state.astype(BF16))
            st[direction, hd] = dec[:, ks] * state + _dot_tn(vb16[:, vs], k_rel[:, ks])
            outs.append(o)
        o_ref[rows, :] = jnp.concatenate(outs, axis=1)

    for ch in range(nch):
        one(0, qf, kf, vf, of, cosf if rope else None, sinf if rope else None, ch)
        one(1, qb, kb, vb, ob, cosb if rope else None, sinb if rope else None, nch - 1 - ch)

    @pl.when(s == pl.num_programs(1) - 1)
    def _():
        sfin[...] = st[...]


def bidir_scan(p, n_batch, seq, cols, heads, dk, dv, kind, s0, *, lg=None, wup=None, bup=None,
               rope=None, q_scale=1.0, k_scale=1.0):
    t = SCAN_ROWS
    nblk = seq // t
    hk, hv = heads * dk, heads * dv
    qc, kc, vc = cols
    fwd = lambda w, off: pl.BlockSpec((t, w), lambda b, s: (b * nblk + s, off // w))
    bwd = lambda w, off: pl.BlockSpec((t, w), lambda b, s: (b * nblk + nblk - 1 - s, off // w))
    const = lambda shape: pl.BlockSpec(shape, lambda b, s: (0,) * len(shape))
    args = [p] * 6
    specs = [fwd(hk, qc), fwd(hk, kc), fwd(hv, vc), bwd(hk, qc), bwd(hk, kc), bwd(hv, vc)]
    if kind == "ret":
        args += [lg]
        specs += [const((2, hk))]
    else:
        args += [p, p, wup, bup]
        specs += [fwd(LANES, COL_GLA_D), bwd(LANES, COL_GLA_D), const(wup.shape), const(bup.shape)]
    if rope is not None:
        cos, sin = rope
        args += [cos, sin, cos, sin]
        specs += [pl.BlockSpec((t, dk), lambda b, s: (s, 0)), pl.BlockSpec((t, dk), lambda b, s: (s, 0)),
                  pl.BlockSpec((t, dk), lambda b, s: (nblk - 1 - s, 0)),
                  pl.BlockSpec((t, dk), lambda b, s: (nblk - 1 - s, 0))]
    args += [s0]
    state_spec = pl.BlockSpec((None, 2, heads, dv, dk), lambda b, s: (b, 0, 0, 0, 0))
    specs += [state_spec]
    n = n_batch * seq
    kern = functools.partial(_scan_kernel, heads=heads, dk=dk, dv=dv, kind=kind, rope=rope is not None,
                             q_scale=q_scale, k_scale=k_scale)
    return pl.pallas_call(
        kern,
        grid=(n_batch, nblk),
        in_specs=specs,
        out_specs=[pl.BlockSpec((t, hv), lambda b, s: (b * nblk + s, 0)),
                   pl.BlockSpec((t, hv), lambda b, s: (b * nblk + nblk - 1 - s, 0)),
                   state_spec],
        out_shape=[jax.ShapeDtypeStruct((n, hv), F32), jax.ShapeDtypeStruct((n, hv), F32),
                   jax.ShapeDtypeStruct((n_batch, 2, heads, dv, dk), F32)],
        scratch_shapes=[pltpu.VMEM((2, heads, dv, dk), F32)],
        compiler_params=_params(2),
    )(*args)


def _outproj_kernel(na_ref, rf_ref, rb_ref, rg_ref, gf_ref, gb_ref, gg_ref, sw_ref, gn_ref, w_ref, x_ref, mg_ref,
                    o_ref, *, head_w):
    ry = rf_ref[...] + rb_ref[...]
    gy = gf_ref[...] + gb_ref[...]
    r_out, g_out = [], []
    for hd in range(ry.shape[1] // head_w):
        sl = slice(hd * head_w, (hd + 1) * head_w)
        r = ry[:, sl]
        mu = jnp.mean(r, axis=-1, keepdims=True)
        var = jnp.mean(jnp.square(r - mu), axis=-1, keepdims=True)
        r_out.append((r - mu) * lax.rsqrt(var + EPS))
        gq = gy[:, sl]
        g_out.append(gq * lax.rsqrt(jnp.mean(gq * gq, axis=-1, keepdims=True) + EPS) * gn_ref[...])
    ret = jnp.concatenate(r_out, axis=1) * _silu(rg_ref[...])
    gla = jnp.concatenate(g_out, axis=1) * _silu(gg_ref[...])
    mix = jnp.concatenate([na_ref[...], ret, gla, sw_ref[...]], axis=1).astype(BF16)
    o_ref[...] = x_ref[...] + mg_ref[...] * jnp.dot(mix, w_ref[...], preferred_element_type=F32)


def outproj(na, rf, rb, gf, gb, sw, p, gla_norm_g, w_out, x2, mg, rows_per_mod):
    n, d = x2.shape
    gw = GROUP_WIDTH
    tm = 256
    m = mg.shape[0]
    row = lambda w_: pl.BlockSpec((tm, w_), lambda i: (i, 0))
    return pl.pallas_call(
        functools.partial(_outproj_kernel, head_w=RET_DV),
        grid=(n // tm,),
        in_specs=[row(gw), row(gw), row(gw), pl.BlockSpec((tm, gw), lambda i: (i, COL_RET_G // gw)),
                  row(gw), row(gw), pl.BlockSpec((tm, gw), lambda i: (i, COL_GLA_G // gw)), row(gw),
                  pl.BlockSpec((1, GLA_DV), lambda i: (0, 0)),
                  pl.BlockSpec((d, d), lambda i: (0, 0)), row(d),
                  pl.BlockSpec((None, 1, d), lambda i: (i // (rows_per_mod // tm), 0, 0))],
        out_specs=row(d),
        out_shape=jax.ShapeDtypeStruct((n, d), F32),
        compiler_params=_params(1),
    )(na, rf, rb, p, gf, gb, p, sw, gla_norm_g.reshape(1, GLA_DV), w_out, x2, mg.reshape(m, 1, d))


PEER_PAIRS = PEER_HEADS * PEER_TOPK
PEER_ROUTE_TOKENS = 128
PEER_EXPERT_TOKENS = 8
PEER_SUB = 8
PEER_FOLD = D_MODEL // PEER_SUB
INV_SQRT2 = 0.7071067811865476


def _topk_cols(s, payload=None, order=None):
    row = lax.broadcasted_iota(jnp.int32, s.shape, 0) if order is None else order
    vals, idxs = [], []
    for _ in range(PEER_TOPK):
        m = jnp.max(s, axis=0, keepdims=True)
        am = jnp.min(jnp.where(s == m, row, jnp.iinfo(jnp.int32).max), axis=0, keepdims=True)
        sel = row == am
        vals.append(m)
        idxs.append(am if payload is None else jnp.max(jnp.where(sel, payload, -1), axis=0, keepdims=True))
        s = jnp.where(sel, -jnp.inf, s)
    return jnp.concatenate(vals, axis=0), jnp.concatenate(idxs, axis=0)


def _staircase_candidates(v0, i0, v1, i1):
    k = PEER_TOPK
    assert k == 16
    t = v0.shape[1]
    r8 = lax.broadcasted_iota(jnp.int32, (8, t), 0)
    r16 = lax.broadcasted_iota(jnp.int32, (k, t), 0)

    def piece(a_sl, b_sl):
        return v0[a_sl, :] + v1[b_sl, :], i0[a_sl, :] * PEER_N_KEYS + i1[b_sl, :]

    one = lambda j: slice(j, j + 1)
    lo = slice(0, 8)
    pieces = [
        (one(0), slice(0, k), None, r16),
        (one(1), lo, None, k + r8),
        (one(2), lo, r8 <= 4, 2 * k + r8),
        (one(3), lo, r8 <= 3, 3 * k + r8),
        (slice(8, k), one(0), None, (r8 + 8) * k),
        (lo, one(0), r8 >= 4, r8 * k),
        (lo, one(1), r8 >= 4, r8 * k + 1),
        (lo, one(2), r8 == 4, r8 * k + 2),
    ]
    sums, ids, orders = [], [], []
    for a_sl, b_sl, keep, order in pieces:
        s, e = piece(a_sl, b_sl)
        sums.append(s if keep is None else jnp.where(keep, s, NEG_INF))
        ids.append(e)
        orders.append(order)
    return jnp.concatenate(sums, axis=0), jnp.concatenate(ids, axis=0), jnp.concatenate(orders, axis=0)


def _route_head(hb, wq_ref, sk_ref, hd):
    half = PEER_DK // 2
    q = jnp.dot(hb, wq_ref[hd], preferred_element_type=F32)
    tops = []
    for p in range(2):
        qp = q[:, p * half:(p + 1) * half].astype(BF16)
        tops.append(_topk_cols(_dot_nt(sk_ref[p, hd], qp)))
    (v0, i0), (v1, i1) = tops
    best_s, best_e = _topk_cols(*_staircase_candidates(v0, i0, v1, i1))
    e = jnp.exp(best_s - best_s[0:1, :])
    return e / jnp.sum(e, axis=0, keepdims=True), best_e


def _peer_route_kernel(x_ref, g_ref, shift_ref, scale_ref, wq_ref, sk_ref, h_ref, idx_ref, gate_ref):
    x = x_ref[...]
    y = x * lax.rsqrt(jnp.mean(x * x, axis=-1, keepdims=True) + EPS)
    h = (y * g_ref[...]) * (1.0 + scale_ref[...]) + shift_ref[...]
    h_ref[...] = h
    hb = h.astype(BF16)

    for hd in range(PEER_HEADS):
        gate_ref[hd], idx_ref[hd] = _route_head(hb, wq_ref, sk_ref, hd)


def peer_route(x2, g, shift, scale, rows_per_mod, wq_h, sk):
    n, d = x2.shape
    t = PEER_ROUTE_TOKENS
    m = shift.shape[0]
    mod_map = lambda i: (i // (rows_per_mod // t), 0, 0)
    return pl.pallas_call(
        _peer_route_kernel,
        grid=(n // t,),
        in_specs=[
            pl.BlockSpec((t, d), lambda i: (i, 0)),
            pl.BlockSpec((1, d), lambda i: (0, 0)),
            pl.BlockSpec((None, 1, d), mod_map),
            pl.BlockSpec((None, 1, d), mod_map),
            pl.BlockSpec((PEER_HEADS, d, PEER_DK), lambda i: (0, 0, 0)),
            pl.BlockSpec((2, PEER_HEADS, PEER_N_KEYS, PEER_DK // 2), lambda i: (0, 0, 0, 0)),
        ],
        out_specs=[
            pl.BlockSpec((t, d), lambda i: (i, 0)),
            pl.BlockSpec((PEER_HEADS, PEER_TOPK, t), lambda i: (0, 0, i)),
            pl.BlockSpec((PEER_HEADS, PEER_TOPK, t), lambda i: (0, 0, i)),
        ],
        out_shape=[
            jax.ShapeDtypeStruct((n, d), F32),
            jax.ShapeDtypeStruct((PEER_HEADS, PEER_TOPK, n), jnp.int32),
            jax.ShapeDtypeStruct((PEER_HEADS, PEER_TOPK, n), F32),
        ],
        compiler_params=_params(1),
    )(x2, g.reshape(1, d), shift.reshape(m, 1, d), scale.reshape(m, 1, d), wq_h, sk)


def pack_experts(u, v):
    ub = lax.bitcast_convert_type(u.astype(BF16), jnp.uint16).astype(jnp.uint32)
    vb = lax.bitcast_convert_type(v.astype(BF16), jnp.uint16).astype(jnp.uint32)
    words = lax.bitcast_convert_type((ub << 16) | vb, jnp.int32)
    return words.reshape(u.shape[0], PEER_SUB, PEER_FOLD)


def _fold_rows(hbuf, h_rows, tb):
    for s in range(PEER_SUB):
        for c in range(PEER_FOLD // LANES):
            lo = s * PEER_FOLD + c * LANES
            hbuf[c, pl.ds(s * tb, tb), :] = h_rows[:, lo:lo + LANES]


def _unfold_residual(o_ref, x_ref, og_ref, ybuf, tb):
    for s in range(PEER_SUB):
        for c in range(PEER_FOLD // LANES):
            sl = slice(s * PEER_FOLD + c * LANES, s * PEER_FOLD + (c + 1) * LANES)
            o_ref[:, sl] = x_ref[:, sl] + og_ref[:, sl] * ybuf[c, pl.ds(s, tb, stride=PEER_SUB), :]


def _expert_weights(j, buf, hbuf, pbuf, wrow, tb, gate_column):
    fold = PEER_FOLD
    hj = jnp.concatenate([hbuf[c, pl.ds(j, PEER_SUB, stride=tb), :] for c in range(fold // LANES)], axis=1)
    for p in range(PEER_PAIRS):
        u = lax.bitcast_convert_type(buf[j, p] & jnp.int32(-65536), F32)
        prod = u * hj
        pbuf[pl.ds(p * PEER_SUB, PEER_SUB), :] = prod[:, :LANES] + prod[:, LANES:]
    part = pbuf[pl.ds(0, PEER_PAIRS, stride=PEER_SUB), :]
    for s in range(1, PEER_SUB):
        part = part + pbuf[pl.ds(s, PEER_PAIRS, stride=PEER_SUB), :]
    sc = jnp.sum(part, axis=1, keepdims=True)
    act = 0.5 * sc * (1.0 + lax.erf(sc * INV_SQRT2))
    g = jnp.concatenate([gate_column(hd) for hd in range(PEER_HEADS)], axis=0)
    wrow[...] = jnp.broadcast_to(g * act, (PEER_PAIRS, LANES))


def _expert_mix(j, buf, wrow, ybuf, start_fetch):
    fold = PEER_FOLD
    accs = [jnp.zeros((PEER_SUB, fold), F32) for _ in range(4)]
    for p in range(PEER_PAIRS):
        if start_fetch is not None:
            start_fetch(p)
        v = lax.bitcast_convert_type(buf[j, p] << 16, F32)
        wp = jnp.broadcast_to(wrow[p:p + 1, :], (PEER_SUB, LANES))
        accs[p % 4] = accs[p % 4] + v * jnp.concatenate([wp, wp], axis=1)
    yj = (accs[0] + accs[1]) + (accs[2] + accs[3])
    for c in range(fold // LANES):
        ybuf[c, pl.ds(j * PEER_SUB, PEER_SUB), :] = yj[:, c * LANES:(c + 1) * LANES]


def _peer_expert_kernel(idx0_ref, idxn_ref, h_ref, gate_ref, x_ref, og_ref, uv_hbm, o_ref,
                        buf_even, buf_odd, sem, hbuf, pbuf, wbuf, ybuf):
    i = pl.program_id(0)
    n = pl.num_programs(0)
    tb = PEER_EXPERT_TOKENS
    bufs = (buf_even, buf_odd)

    def slab_copy(idx_ref, j, r, parity):
        e = idx_ref[0, 0, j * PEER_PAIRS + r]
        return pltpu.make_async_copy(uv_hbm.at[e], bufs[parity].at[j, r], sem.at[parity, j])

    def wait_token(j, parity):
        pltpu.make_async_copy(uv_hbm.at[pl.ds(0, PEER_PAIRS)], bufs[parity].at[j], sem.at[parity, j]).wait()

    @pl.when(i == 0)
    def _():
        def prime(j, carry):
            for r in range(PEER_PAIRS):
                slab_copy(idx0_ref, j, r, 0).start(priority=r % 2)
            return carry
        lax.fori_loop(0, tb, prime, 0)

    lane = lax.broadcasted_iota(jnp.int32, (PEER_TOPK, PEER_ROUTE_TOKENS), 1)
    lane0 = (i % (PEER_ROUTE_TOKENS // tb)) * tb

    _fold_rows(hbuf, h_ref[...], tb)

    def fetcher(j, parity, prefetch):
        return (lambda r: slab_copy(idxn_ref, j, r, 1 - parity).start(priority=r % 2)) if prefetch else None

    def gate_column_of(j):
        return lambda hd: jnp.sum(jnp.where(lane == lane0 + j, gate_ref[hd], 0.0), axis=1, keepdims=True)

    def block(parity, prefetch):
        def body(j, carry):
            wait_token(j, parity)
            _expert_weights(j, bufs[parity], hbuf, pbuf, wbuf, tb, gate_column_of(j))
            _expert_mix(j, bufs[parity], wbuf, ybuf, fetcher(j, parity, prefetch))
            return carry
        lax.fori_loop(0, tb, body, 0)

    for parity in range(2):
        for prefetch in (True, False):
            @pl.when((i % 2 == parity) & ((i + 1 < n) == prefetch))
            def _(parity=parity, prefetch=prefetch):
                block(parity, prefetch)

    _unfold_residual(o_ref, x_ref, og_ref, ybuf, tb)


def peer_expert(h, idx, gate, x2, out_gate, rows_per_mod, uv):
    n, d = h.shape
    tb = PEER_EXPERT_TOKENS
    nb = n // tb
    m = out_gate.shape[0]
    rows = tb * PEER_PAIRS
    idx_rows = idx.reshape(PEER_PAIRS, n).T.reshape(nb, 1, rows)
    gate_blocks = PEER_ROUTE_TOKENS // tb
    return pl.pallas_call(
        _peer_expert_kernel,
        grid=(nb,),
        in_specs=[
            pl.BlockSpec((1, 1, rows), lambda i: (0, 0, 0), memory_space=pltpu.SMEM),
            pl.BlockSpec((1, 1, rows), lambda i: (jnp.minimum(i + 1, nb - 1), 0, 0), memory_space=pltpu.SMEM),
            pl.BlockSpec((tb, d), lambda i: (i, 0)),
            pl.BlockSpec((PEER_HEADS, PEER_TOPK, PEER_ROUTE_TOKENS), lambda i: (0, 0, i // gate_blocks)),
            pl.BlockSpec((tb, d), lambda i: (i, 0)),
            pl.BlockSpec((None, 1, d), lambda i: (i // (rows_per_mod // tb), 0, 0)),
            pl.BlockSpec(memory_space=pl.ANY),
        ],
        out_specs=pl.BlockSpec((tb, d), lambda i: (i, 0)),
        out_shape=jax.ShapeDtypeStruct((n, d), F32),
        scratch_shapes=[pltpu.VMEM((tb, PEER_PAIRS, PEER_SUB, PEER_FOLD), jnp.int32),
                        pltpu.VMEM((tb, PEER_PAIRS, PEER_SUB, PEER_FOLD), jnp.int32),
                        pltpu.SemaphoreType.DMA((2, tb)),
                        pltpu.VMEM((PEER_FOLD // LANES, PEER_SUB * tb, LANES), F32),
                        pltpu.VMEM((PEER_PAIRS * PEER_SUB, LANES), F32),
                        pltpu.VMEM((PEER_PAIRS, LANES), F32),
                        pltpu.VMEM((PEER_FOLD // LANES, tb * PEER_SUB, LANES), F32)],
        compiler_params=pltpu.CompilerParams(dimension_semantics=("arbitrary",), vmem_limit_bytes=VMEM_LIMIT,
                                             disable_bounds_checks=True),
    )(idx_rows, idx_rows, h, gate, x2, out_gate.reshape(m, 1, d), uv)


def peer_residual(x2, g, shift, scale, out_gate, rows_per_mod, wq_h, sk, uv):
    h, idx, gate = peer_route(x2, g, shift, scale, rows_per_mod, wq_h, sk)
    return peer_expert(h, idx, gate, x2, out_gate, rows_per_mod, uv)


def _final_norm_kernel(x_ref, g_ref, o_ref):
    x = x_ref[...]
    y = x * lax.rsqrt(jnp.mean(x * x, axis=-1, keepdims=True) + EPS)
    o_ref[...] = y * g_ref[...]


def final_norm(x2, g):
    n, d = x2.shape
    tm = 512
    return pl.pallas_call(
        _final_norm_kernel,
        grid=(n // tm,),
        in_specs=[pl.BlockSpec((tm, d), lambda i: (i, 0)), pl.BlockSpec((1, d), lambda i: (0, 0))],
        out_specs=pl.BlockSpec((tm, d), lambda i: (i, 0)),
        out_shape=jax.ShapeDtypeStruct((n, d), x2.dtype),
        compiler_params=_params(1),
    )(x2, g.reshape(1, d))


def _mixers(p, pz, n_batch, seq, ctx_len, is_ctx, tables, prm):
    if is_ctx:
        na = ctx_attention(p, n_batch, seq, (COL_NA_Q, COL_NA_K, COL_NA_V), GROUP_WIDTH, None)
        sw = ctx_attention(p, n_batch, seq, (COL_SWA_Q, COL_SWA_K, COL_SWA_V), SWA_KV_HEADS * HEAD_DIM, prm["sink"])
    else:
        na = na_attention(p, pz, n_batch, seq, ctx_len, prm["na_bias"])
        sw = swa_attention(p, pz, n_batch, seq, ctx_len, prm["sink"], tables["swa_q"], tables["swa_k"])
    rf, rb, rs = bidir_scan(p, n_batch, seq, (COL_RET_Q, COL_RET_K, COL_RET_V), RET_HEADS, RET_DK, RET_DV, "ret",
                            prm["ret_s0"], lg=prm["ret_lg"], rope=None if is_ctx else tables["ret"],
                            k_scale=RET_DK ** -0.5)
    gf, gb, gs = bidir_scan(p, n_batch, seq, (COL_GLA_Q, COL_GLA_K, COL_GLA_V), GLA_HEADS, GLA_DK, GLA_DV, "gla",
                            prm["gla_s0"], wup=prm["gla_wup"], bup=prm["gla_bup"], q_scale=GLA_DK ** -0.5)
    return (na, rf, rb, gf, gb, sw), (rs, gs)


def kernel(x, c, ctx, c_ctx, w_ada, b_ada, norm_attn_g, norm_ffn_g, w_in, na_rpb, ret_log_gamma,
           gla_w_gate_up, gla_b_gate, gla_norm_g, swa_sink, w_out, peer_w_q, peer_sub_keys,
           peer_u, peer_v, final_g):
    bsz, slen, d = x.shape
    zlen = ctx.shape[1]
    x2 = x.reshape(bsz * slen, d)
    z2 = ctx.reshape(bsz * zlen, d)
    tables = {"ret": rope_lane_tables(slen, RET_DK, 1),
              "swa_q": rope_lane_tables(slen, HEAD_DIM, SWA_HEADS),
              "swa_k": rope_lane_tables(slen, HEAD_DIM, SWA_KV_HEADS)}
    c_rows = jnp.zeros((8, d), F32).at[:bsz].set(c).at[bsz].set(c_ctx)
    for layer in range(DEPTH):
        has_next = layer < DEPTH - 1
        mod = adaln(c_rows, w_ada[layer], b_ada[layer])
        mx = [mod[:bsz, k * d:(k + 1) * d] for k in range(6)]
        mz = [mod[bsz:bsz + 1, k * d:(k + 1) * d] for k in range(6)]

        wi = w_in[layer]
        wp = jnp.concatenate([wi[:, :REF_COL_GLA_D], wi[:, REF_COL_SWA_Q:], wi[:, REF_COL_GLA_D:REF_COL_SWA_Q],
                              jnp.zeros((d, PROJ_WIDTH - REF_D_IN), F32)], axis=1).astype(BF16)
        px = modproj(x2, norm_attn_g[layer], mx[0], mx[1], slen, wp)
        pz = modproj(z2, norm_attn_g[layer], mz[0], mz[1], bsz * zlen, wp)

        wup = (jnp.zeros((2, LANES, GLA_HEADS * GLA_DK), F32)
               .at[0, :GLA_RANK].set(gla_w_gate_up[layer, 0])
               .at[1, GLA_RANK:2 * GLA_RANK].set(gla_w_gate_up[layer, 1])).astype(BF16)
        prm = {"na_bias": na_band_bias(na_rpb[layer]), "sink": swa_sink[layer],
               "ret_lg": jnp.repeat(ret_log_gamma[layer], RET_DK, axis=1),
               "gla_wup": wup, "gla_bup": gla_b_gate[layer].reshape(2, 1, GLA_HEADS * GLA_DK),
               "ret_s0": jnp.zeros((bsz, 2, RET_HEADS, RET_DV, RET_DK), F32),
               "gla_s0": jnp.zeros((bsz, 2, GLA_HEADS, GLA_DV, GLA_DK), F32)}
        mix_z, (ret_s, gla_s) = _mixers(pz, pz, bsz, zlen, zlen, True, tables, prm)
        prm["ret_s0"], prm["gla_s0"] = ret_s, gla_s
        mix_x, _ = _mixers(px, pz, bsz, slen, zlen, False, tables, prm)

        wo = w_out[layer].astype(BF16)
        x2 = outproj(*mix_x, px, gla_norm_g[layer], wo, x2, mx[2], slen)

        uv = pack_experts(peer_u[layer], peer_v[layer])
        wq_h = peer_w_q[layer].reshape(d, PEER_HEADS, PEER_DK).transpose(1, 0, 2).astype(BF16)
        sk = peer_sub_keys[layer].astype(BF16)
        x2 = peer_residual(x2, norm_ffn_g[layer], mx[3], mx[4], mx[5], slen, wq_h, sk, uv)
        if has_next:
            z2 = outproj(*mix_z, pz, gla_norm_g[layer], wo, z2, mz[2], bsz * zlen)
            z2 = peer_residual(z2, norm_ffn_g[layer], mz[3], mz[4], mz[5], bsz * zlen, wq_h, sk, uv)
    return final_norm(x2, final_g).reshape(bsz, slen, d)
```

```python
import functools

import jax
import jax.numpy as jnp
import numpy as np
from jax import lax
from jax.experimental import pallas as pl
from jax.experimental.pallas import tpu as pltpu

D_MODEL = 2048
DEPTH = 2
GRID_W = 64
EPS = 1e-6
ROPE_BASE = 10000.0

GROUP_WIDTH = D_MODEL // 4
HEAD_DIM = 64
NA_HEADS = GROUP_WIDTH // HEAD_DIM
NA_ROWS = 8
NA_COLS = 16
RET_HEADS = 4
RET_DK = GROUP_WIDTH // RET_HEADS
RET_DV = GROUP_WIDTH // RET_HEADS
GLA_HEADS = 4
GLA_DV = GROUP_WIDTH // GLA_HEADS
GLA_DK = GLA_DV // 2
GLA_RANK = 16
GLA_TAU = 16.0
SWA_HEADS = GROUP_WIDTH // HEAD_DIM
SWA_KV_HEADS = SWA_HEADS // 4
SWA_WINDOW = 128
SWA_BLOCK = 128
SCAN_CHUNK = 64
PEER_HEADS = 8
PEER_N_KEYS = 128
PEER_N_EXPERTS = PEER_N_KEYS * PEER_N_KEYS
PEER_DK = 256
PEER_TOPK = 16

LANES = 128
VMEM_LIMIT = 48 * 1024 * 1024
BF16 = jnp.bfloat16
F32 = jnp.float32
NEG_INF = float("-inf")

COL_NA_Q, COL_NA_K, COL_NA_V = 0, 512, 1024
COL_RET_Q, COL_RET_K, COL_RET_V, COL_RET_G = 1536, 2048, 2560, 3072
COL_GLA_Q, COL_GLA_K, COL_GLA_V, COL_GLA_G = 3584, 3840, 4096, 4608
COL_SWA_Q, COL_SWA_K, COL_SWA_V = 5120, 5632, 5760
COL_GLA_D = 5888
REF_COL_GLA_D, REF_COL_SWA_Q, REF_D_IN = 5120, 5152, 5920
PROJ_WIDTH = 6144


def _silu(x):
    return x / (1.0 + jnp.exp(-x))


def _dot_nt(a, b):
    return lax.dot_general(a, b, (((1,), (1,)), ((), ())), preferred_element_type=F32)


def _dot_tn(a, b):
    return lax.dot_general(a, b, (((0,), (0,)), ((), ())), preferred_element_type=F32)


def _params(n_axes):
    return pltpu.CompilerParams(dimension_semantics=("arbitrary",) * n_axes, vmem_limit_bytes=VMEM_LIMIT)


def _rope_lanes(x, cs, sn, quarter):
    n = x.shape[-1]
    lane = lax.broadcasted_iota(jnp.int32, x.shape, x.ndim - 1)
    first = (lane % (2 * quarter)) < quarter
    swapped = jnp.where(first, pltpu.roll(x, n - quarter, x.ndim - 1), pltpu.roll(x, quarter, x.ndim - 1))
    return x * cs + swapped * sn


def rope_lane_tables(length, dh, copies):
    t = jnp.arange(length)
    pos = jnp.stack([t // GRID_W, t % GRID_W], axis=-1).astype(F32)
    quarter = dh // 4
    inv = ROPE_BASE ** (-jnp.arange(quarter, dtype=F32) / quarter)
    ang = pos[:, :, None] * inv
    cos, sin = jnp.cos(ang), jnp.sin(ang)
    cl = jnp.concatenate([cos[:, 0], cos[:, 0], cos[:, 1], cos[:, 1]], axis=-1)
    sl = jnp.concatenate([-sin[:, 0], sin[:, 0], -sin[:, 1], sin[:, 1]], axis=-1)
    return jnp.tile(cl, (1, copies)), jnp.tile(sl, (1, copies))


def _adaln_kernel(c_ref, w_ref, b_ref, o_ref):
    a = _silu(c_ref[...]).astype(BF16)
    o_ref[...] = jnp.dot(a, w_ref[...].astype(BF16), preferred_element_type=F32) + b_ref[...]


def adaln(c_rows, w, b):
    r, d = c_rows.shape
    m = w.shape[1]
    tn = 1024
    return pl.pallas_call(
        _adaln_kernel,
        grid=(m // tn,),
        in_specs=[pl.BlockSpec((r, d), lambda j: (0, 0)), pl.BlockSpec((d, tn), lambda j: (0, j)),
                  pl.BlockSpec((1, tn), lambda j: (0, j))],
        out_specs=pl.BlockSpec((r, tn), lambda j: (0, j)),
        out_shape=jax.ShapeDtypeStruct((r, m), F32),
        compiler_params=_params(1),
    )(c_rows, w, b.reshape(1, m))


def _modproj_kernel(x_ref, g_ref, shift_ref, scale_ref, w_ref, o_ref, hb_ref):
    @pl.when(pl.program_id(1) == 0)
    def _():
        x = x_ref[...]
        y = x * lax.rsqrt(jnp.mean(x * x, axis=-1, keepdims=True) + EPS)
        hb_ref[...] = ((y * g_ref[...]) * (1.0 + scale_ref[...]) + shift_ref[...]).astype(BF16)

    o_ref[...] = jnp.dot(hb_ref[...], w_ref[...], preferred_element_type=F32)


def modproj(x2, g, shift, scale, rows_per_mod, w):
    n, d = x2.shape
    wid = w.shape[1]
    tm = min(512, rows_per_mod)
    tn = 2048
    m = shift.shape[0]
    mod_map = lambda i, j: (i // (rows_per_mod // tm), 0, 0)
    return pl.pallas_call(
        _modproj_kernel,
        grid=(n // tm, wid // tn),
        in_specs=[pl.BlockSpec((tm, d), lambda i, j: (i, 0)), pl.BlockSpec((1, d), lambda i, j: (0, 0)),
                  pl.BlockSpec((None, 1, d), mod_map), pl.BlockSpec((None, 1, d), mod_map),
                  pl.BlockSpec((d, tn), lambda i, j: (0, j))],
        out_specs=pl.BlockSpec((tm, tn), lambda i, j: (i, j)),
        out_shape=jax.ShapeDtypeStruct((n, wid), F32),
        scratch_shapes=[pltpu.VMEM((tm, d), BF16)],
        compiler_params=_params(2),
    )(x2, g.reshape(1, d), shift.reshape(m, 1, d), scale.reshape(m, 1, d), w)


NA_QROWS = 4


def _na_kernel(q_ref, k_ref, v_ref, kz_ref, vz_ref, bias_ref, o_ref, *, rows):
    step = pl.program_id(2)
    dh = HEAD_DIM
    band = NA_ROWS * GRID_W
    kz = kz_ref[...].astype(BF16)
    vz = vz_ref[...].astype(BF16)
    heads = LANES // dh
    units = [(qr, hh) for qr in range(NA_QROWS) for hh in range(heads)]
    scores, vbands = {}, {}
    for qr in range(NA_QROWS):
        r = step * NA_QROWS + qr
        start = jnp.clip(r - NA_ROWS // 2, 0, rows - NA_ROWS)
        dr0 = start - r + NA_ROWS - 1
        tok0 = pl.multiple_of(start * GRID_W, GRID_W)
        kb = k_ref[pl.ds(tok0, band), :].astype(BF16)
        vbands[qr] = v_ref[pl.ds(tok0, band), :].astype(BF16)
        q = (q_ref[pl.ds(qr * GRID_W, GRID_W), :] * (dh ** -0.5)).astype(BF16)
        for hh in range(heads):
            sl = slice(hh * dh, (hh + 1) * dh)
            scores[qr, hh] = (_dot_nt(q[:, sl], kb[:, sl]) + bias_ref[hh, dr0],
                              _dot_nt(q[:, sl], kz[:, sl]))
    probs = {}
    for u in units:
        s_nb, s_cx = scores[u]
        m = jnp.maximum(jnp.max(s_nb, axis=1, keepdims=True), jnp.max(s_cx, axis=1, keepdims=True))
        p_nb = jnp.exp(s_nb - m)
        p_cx = jnp.exp(s_cx - m)
        den = jnp.sum(p_nb, axis=1, keepdims=True) + jnp.sum(p_cx, axis=1, keepdims=True)
        probs[u] = (p_nb.astype(BF16), p_cx.astype(BF16), den)
    for qr in range(NA_QROWS):
        outs = []
        for hh in range(heads):
            sl = slice(hh * dh, (hh + 1) * dh)
            p_nb, p_cx, den = probs[qr, hh]
            o = (jnp.dot(p_nb, vbands[qr][:, sl], preferred_element_type=F32)
                 + jnp.dot(p_cx, vz[:, sl], preferred_element_type=F32))
            outs.append(o / den)
        o_ref[pl.ds(qr * GRID_W, GRID_W), :] = jnp.concatenate(outs, axis=1)


def na_band_bias(rpb):
    col = jnp.arange(GRID_W)
    col_start = jnp.clip(col - NA_COLS // 2, 0, GRID_W - NA_COLS)
    col_ok = (col[None, :] >= col_start[:, None]) & (col[None, :] < col_start[:, None] + NA_COLS)
    d_col = jnp.clip(col[None, :] - col[:, None], -(NA_COLS - 1), NA_COLS - 1) + NA_COLS - 1
    d_row = jnp.arange(NA_ROWS)[:, None] + jnp.arange(NA_ROWS)[None, :]
    b = rpb.astype(F32)[:, d_row][..., d_col]
    b = jnp.where(col_ok[None, None, None], b, NEG_INF)
    return b.transpose(0, 1, 3, 2, 4).reshape(rpb.shape[0], NA_ROWS, GRID_W, NA_ROWS * GRID_W)


def na_attention(px, pz, n_batch, seq, ctx_len, bias):
    rows = seq // GRID_W
    tq = NA_QROWS * GRID_W
    nsteps = rows // NA_QROWS
    heads_per_blk = LANES // HEAD_DIM
    return pl.pallas_call(
        functools.partial(_na_kernel, rows=rows),
        grid=(n_batch, NA_HEADS // heads_per_blk, nsteps),
        in_specs=[
            pl.BlockSpec((tq, LANES), lambda b, hp, s: (b * nsteps + s, COL_NA_Q // LANES + hp)),
            pl.BlockSpec((seq, LANES), lambda b, hp, s: (b, COL_NA_K // LANES + hp)),
            pl.BlockSpec((seq, LANES), lambda b, hp, s: (b, COL_NA_V // LANES + hp)),
            pl.BlockSpec((ctx_len, LANES), lambda b, hp, s: (b, COL_NA_K // LANES + hp)),
            pl.BlockSpec((ctx_len, LANES), lambda b, hp, s: (b, COL_NA_V // LANES + hp)),
            pl.BlockSpec((heads_per_blk, NA_ROWS, GRID_W, NA_ROWS * GRID_W), lambda b, hp, s: (hp, 0, 0, 0)),
        ],
        out_specs=pl.BlockSpec((tq, LANES), lambda b, hp, s: (b * nsteps + s, hp)),
        out_shape=jax.ShapeDtypeStruct((n_batch * seq, GROUP_WIDTH), F32),
        compiler_params=_params(3),
    )(px, px, px, pz, pz, bias)


def _swa_kernel(q_ref, kp_ref, kc_ref, kn_ref, vp_ref, vc_ref, vn_ref, kz_ref, vz_ref, sink_ref,
                cq_ref, sq_ref, ckp_ref, skp_ref, ckc_ref, skc_ref, ckn_ref, skn_ref, o_ref):
    n = pl.program_id(1)
    nb = pl.num_programs(1)
    dh = HEAD_DIM
    blk = SWA_BLOCK
    quarter = dh // 4
    group = SWA_HEADS // SWA_KV_HEADS
    q = _rope_lanes(q_ref[...], cq_ref[...], sq_ref[...], quarter) * (dh ** -0.5)
    kp = _rope_lanes(kp_ref[...], ckp_ref[...], skp_ref[...], quarter).astype(BF16)
    kc = _rope_lanes(kc_ref[...], ckc_ref[...], skc_ref[...], quarter).astype(BF16)
    kn = _rope_lanes(kn_ref[...], ckn_ref[...], skn_ref[...], quarter).astype(BF16)
    kz = kz_ref[...].astype(BF16)
    vp, vc, vn, vz = (r[...].astype(BF16) for r in (vp_ref, vc_ref, vn_ref, vz_ref))
    qi = lax.broadcasted_iota(jnp.int32, (group * blk, blk), 0) % blk
    kj = lax.broadcasted_iota(jnp.int32, (group * blk, blk), 1)
    ok_p = (kj >= qi) & (n > 0)
    ok_n = (kj <= qi) & (n < nb - 1)
    outs = []
    for hk in range(SWA_KV_HEADS):
        ks = slice(hk * dh, (hk + 1) * dh)
        qs = jnp.concatenate([q[:, (hk * group + g) * dh:(hk * group + g + 1) * dh] for g in range(group)],
                             axis=0).astype(BF16)
        sink = jnp.concatenate([jnp.full((blk, 1), 1.0, F32) * sink_ref[hk * group + g] for g in range(group)],
                               axis=0)
        s_p = jnp.where(ok_p, _dot_nt(qs, kp[:, ks]), NEG_INF)
        s_c = _dot_nt(qs, kc[:, ks])
        s_n = jnp.where(ok_n, _dot_nt(qs, kn[:, ks]), NEG_INF)
        s_z = _dot_nt(qs, kz[:, ks])
        m = jnp.maximum(jnp.maximum(jnp.max(s_p, axis=1, keepdims=True), jnp.max(s_c, axis=1, keepdims=True)),
                        jnp.maximum(jnp.max(s_n, axis=1, keepdims=True), jnp.max(s_z, axis=1, keepdims=True)))
        m = jnp.maximum(m, sink)
        e_p, e_c, e_n, e_z = (jnp.exp(s - m) for s in (s_p, s_c, s_n, s_z))
        den = (jnp.sum(e_p, axis=1, keepdims=True) + jnp.sum(e_c, axis=1, keepdims=True)
               + jnp.sum(e_n, axis=1, keepdims=True) + jnp.sum(e_z, axis=1, keepdims=True) + jnp.exp(sink - m))
        o = (jnp.dot(e_p.astype(BF16), vp[:, ks], preferred_element_type=F32)
             + jnp.dot(e_c.astype(BF16), vc[:, ks], preferred_element_type=F32)
             + jnp.dot(e_n.astype(BF16), vn[:, ks], preferred_element_type=F32)
             + jnp.dot(e_z.astype(BF16), vz[:, ks], preferred_element_type=F32)) / den
        outs += [o[g * blk:(g + 1) * blk, :] for g in range(group)]
    o_ref[...] = jnp.concatenate(outs, axis=1)


def swa_attention(px, pz, n_batch, seq, ctx_len, sink, rope_q, rope_k):
    blk = SWA_BLOCK
    nb = seq // blk
    prev = lambda b, n: b * nb + jnp.maximum(n - 1, 0)
    cur = lambda b, n: b * nb + n
    nxt = lambda b, n: b * nb + jnp.minimum(n + 1, nb - 1)
    kv = lambda off, f: pl.BlockSpec((blk, LANES), lambda b, n: (f(b, n), off // LANES))
    tab = lambda w, f: pl.BlockSpec((blk, w), lambda b, n: (f(0, n), 0))
    cq, sq = rope_q
    ck, sk = rope_k
    return pl.pallas_call(
        _swa_kernel,
        grid=(n_batch, nb),
        in_specs=[
            pl.BlockSpec((blk, GROUP_WIDTH), lambda b, n: (cur(b, n), COL_SWA_Q // GROUP_WIDTH)),
            kv(COL_SWA_K, prev), kv(COL_SWA_K, cur), kv(COL_SWA_K, nxt),
            kv(COL_SWA_V, prev), kv(COL_SWA_V, cur), kv(COL_SWA_V, nxt),
            pl.BlockSpec((ctx_len, LANES), lambda b, n: (b, COL_SWA_K // LANES)),
            pl.BlockSpec((ctx_len, LANES), lambda b, n: (b, COL_SWA_V // LANES)),
            pl.BlockSpec(memory_space=pltpu.SMEM),
            tab(GROUP_WIDTH, cur), tab(GROUP_WIDTH, cur), tab(LANES, prev), tab(LANES, prev),
            tab(LANES, cur), tab(LANES, cur), tab(LANES, nxt), tab(LANES, nxt),
        ],
        out_specs=pl.BlockSpec((blk, GROUP_WIDTH), lambda b, n: (cur(b, n), 0)),
        out_shape=jax.ShapeDtypeStruct((n_batch * seq, GROUP_WIDTH), F32),
        compiler_params=_params(2),
    )(px, px, px, px, px, px, px, pz, pz, sink, cq, sq, ck, sk, ck, sk, ck, sk)


def _ctx_attn_kernel(q_ref, k_ref, v_ref, sink_ref, o_ref, *, group, use_sink):
    dh = HEAD_DIM
    q = (q_ref[...] * (dh ** -0.5)).astype(BF16)
    k = k_ref[...].astype(BF16)
    v = v_ref[...].astype(BF16)
    outs = []
    for qh in range(q.shape[1] // dh):
        ks = slice((qh // group) * dh, (qh // group + 1) * dh)
        s = _dot_nt(q[:, qh * dh:(qh + 1) * dh], k[:, ks])
        m = jnp.max(s, axis=1, keepdims=True)
        if use_sink:
            m = jnp.maximum(m, sink_ref[qh])
        e = jnp.exp(s - m)
        den = jnp.sum(e, axis=1, keepdims=True)
        if use_sink:
            den = den + jnp.exp(sink_ref[qh] - m)
        outs.append(jnp.dot(e.astype(BF16), v[:, ks], preferred_element_type=F32) / den)
    o_ref[...] = jnp.concatenate(outs, axis=1)


def ctx_attention(pz, n_batch, ctx_len, cols, kv_width, sink):
    qc, kc, vc = cols
    use_sink = sink is not None
    if sink is None:
        sink = jnp.zeros((GROUP_WIDTH // HEAD_DIM,), F32)
    return pl.pallas_call(
        functools.partial(_ctx_attn_kernel, group=GROUP_WIDTH // kv_width, use_sink=use_sink),
        grid=(n_batch,),
        in_specs=[pl.BlockSpec((ctx_len, GROUP_WIDTH), lambda b: (b, qc // GROUP_WIDTH)),
                  pl.BlockSpec((ctx_len, kv_width), lambda b: (b, kc // kv_width)),
                  pl.BlockSpec((ctx_len, kv_width), lambda b: (b, vc // kv_width)),
                  pl.BlockSpec(memory_space=pltpu.SMEM)],
        out_specs=pl.BlockSpec((ctx_len, GROUP_WIDTH), lambda b: (b, 0)),
        out_shape=jax.ShapeDtypeStruct((n_batch * ctx_len, GROUP_WIDTH), F32),
        compiler_params=_params(1),
    )(pz, pz, pz, sink)


SCAN_ROWS = 256


def _split3(x):
    a = x.astype(BF16)
    r = x - a.astype(F32)
    b = r.astype(BF16)
    c = (r - b.astype(F32)).astype(BF16)
    return a, b, c


def _scan_kernel(*refs, heads, dk, dv, kind, rope, q_scale, k_scale):
    it = iter(refs)
    qf, kf, vf, qb, kb, vb = (next(it) for _ in range(6))
    if kind == "ret":
        lg = next(it)
    else:
        df, db, wup, bup = (next(it) for _ in range(4))
    if rope:
        cosf, sinf, cosb, sinb = (next(it) for _ in range(4))
    s0 = next(it)
    of, ob, sfin = next(it), next(it), next(it)
    st = next(it)

    s = pl.program_id(1)
    c = SCAN_CHUNK
    nch = SCAN_ROWS // c
    hk = heads * dk

    @pl.when(s == 0)
    def _():
        st[...] = s0[...]

    r_i = lax.broadcasted_iota(jnp.int32, (c, c), 0)
    c_i = lax.broadcasted_iota(jnp.int32, (c, c), 1)
    masks = (r_i >= c_i, c_i > r_i)

    if kind == "ret":
        pos = lax.broadcasted_iota(jnp.int32, (c, hk), 0).astype(F32)
        gcums = ((pos + 1.0) * lg[0:1, :], (float(c) - pos) * lg[1:2, :])
    else:
        rr = lax.broadcasted_iota(jnp.int32, (SCAN_ROWS, SCAN_ROWS), 0)
        cc = lax.broadcasted_iota(jnp.int32, (SCAN_ROWS, SCAN_ROWS), 1)
        same = (rr // c) == (cc // c)
        tris = (jnp.where(same & (rr >= cc), 1.0, 0.0).astype(BF16),
                jnp.where(same & (cc >= rr), 1.0, 0.0).astype(BF16))

        def gate_cum(d_ref, direction):
            pre = jnp.dot(d_ref[...].astype(BF16), wup[direction], preferred_element_type=F32) + bup[direction]
            g = -(jnp.maximum(-pre, 0.0) + jnp.log1p(jnp.exp(-jnp.abs(pre)))) / GLA_TAU
            return sum(jnp.dot(tris[direction], p, preferred_element_type=F32) for p in _split3(g))

        gcums = (gate_cum(df, 0), gate_cum(db, 1))

    def one(direction, q_ref, k_ref, v_ref, o_ref, cos_ref, sin_ref, ch):
        rows = pl.ds(ch * c, c)
        q = q_ref[rows, :]
        k = k_ref[rows, :]
        v = v_ref[rows, :]
        if rope:
            cs = jnp.concatenate([cos_ref[rows, :]] * heads, axis=1)
            sn = jnp.concatenate([sin_ref[rows, :]] * heads, axis=1)
            q = _rope_lanes(q, cs, sn, dk // 4)
            k = _rope_lanes(k, cs, sn, dk // 4)
        if q_scale != 1.0:
            q = q * q_scale
        if k_scale != 1.0:
            k = k * k_scale
        gcum = gcums[direction] if kind == "ret" else gcums[direction][ch * c:(ch + 1) * c, :]
        gtot = gcum[c - 1:c, :] if direction == 0 else gcum[0:1, :]
        q_rel = (q * jnp.exp(gcum - gtot)).astype(BF16)
        k_rel = (k * jnp.exp(gtot - gcum)).astype(BF16)
        q_dec = (q * jnp.exp(gcum)).astype(BF16)
        dec = jnp.exp(gtot)
        vb16 = v.astype(BF16)
        outs = []
        for hd in range(heads):
            ks = slice(hd * dk, (hd + 1) * dk)
            vs = slice(hd * dv, (hd + 1) * dv)
            a = jnp.where(masks[direction], _dot_nt(q_rel[:, ks], k_rel[:, ks]), 0.0)
            state = st[direction, hd]
            o = jnp.dot(a.astype(BF16), vb16[:, vs], preferred_element_type=F32)
            o = o + _dot_nt(q_dec[:, ks], state.astype(BF16))
            st[direction, hd] = dec[:, ks] * state + _dot_tn(vb16[:, vs], k_rel[:, ks])
            outs.append(o)
        o_ref[rows, :] = jnp.concatenate(outs, axis=1)

    for ch in range(nch):
        one(0, qf, kf, vf, of, cosf if rope else None, sinf if rope else None, ch)
        one(1, qb, kb, vb, ob, cosb if rope else None, sinb if rope else None, nch - 1 - ch)

    @pl.when(s == pl.num_programs(1) - 1)
    def _():
        sfin[...] = st[...]


def bidir_scan(p, n_batch, seq, cols, heads, dk, dv, kind, s0, *, lg=None, wup=None, bup=None,
               rope=None, q_scale=1.0, k_scale=1.0):
    t = SCAN_ROWS
    nblk = seq // t
    hk, hv = heads * dk, heads * dv
    qc, kc, vc = cols
    fwd = lambda w, off: pl.BlockSpec((t, w), lambda b, s: (b * nblk + s, off // w))
    bwd = lambda w, off: pl.BlockSpec((t, w), lambda b, s: (b * nblk + nblk - 1 - s, off // w))
    const = lambda shape: pl.BlockSpec(shape, lambda b, s: (0,) * len(shape))
    args = [p] * 6
    specs = [fwd(hk, qc), fwd(hk, kc), fwd(hv, vc), bwd(hk, qc), bwd(hk, kc), bwd(hv, vc)]
    if kind == "ret":
        args += [lg]
        specs += [const((2, hk))]
    else:
        args += [p, p, wup, bup]
        specs += [fwd(LANES, COL_GLA_D), bwd(LANES, COL_GLA_D), const(wup.shape), const(bup.shape)]
    if rope is not None:
        cos, sin = rope
        args += [cos, sin, cos, sin]
        specs += [pl.BlockSpec((t, dk), lambda b, s: (s, 0)), pl.BlockSpec((t, dk), lambda b, s: (s, 0)),
                  pl.BlockSpec((t, dk), lambda b, s: (nblk - 1 - s, 0)),
                  pl.BlockSpec((t, dk), lambda b, s: (nblk - 1 - s, 0))]
    args += [s0]
    state_spec = pl.BlockSpec((None, 2, heads, dv, dk), lambda b, s: (b, 0, 0, 0, 0))
    specs += [state_spec]
    n = n_batch * seq
    kern = functools.partial(_scan_kernel, heads=heads, dk=dk, dv=dv, kind=kind, rope=rope is not None,
                             q_scale=q_scale, k_scale=k_scale)
    return pl.pallas_call(
        kern,
        grid=(n_batch, nblk),
        in_specs=specs,
        out_specs=[pl.BlockSpec((t, hv), lambda b, s: (b * nblk + s, 0)),
                   pl.BlockSpec((t, hv), lambda b, s: (b * nblk + nblk - 1 - s, 0)),
                   state_spec],
        out_shape=[jax.ShapeDtypeStruct((n, hv), F32), jax.ShapeDtypeStruct((n, hv), F32),
                   jax.ShapeDtypeStruct((n_batch, 2, heads, dv, dk), F32)],
        scratch_shapes=[pltpu.VMEM((2, heads, dv, dk), F32)],
        compiler_params=_params(2),
    )(*args)


def _outproj_kernel(na_ref, rf_ref, rb_ref, rg_ref, gf_ref, gb_ref, gg_ref, sw_ref, gn_ref, w_ref, x_ref, mg_ref,
                    o_ref, *, head_w):
    ry = rf_ref[...] + rb_ref[...]
    gy = gf_ref[...] + gb_ref[...]
    r_out, g_out = [], []
    for hd in range(ry.shape[1] // head_w):
        sl = slice(hd * head_w, (hd + 1) * head_w)
        r = ry[:, sl]
        mu = jnp.mean(r, axis=-1, keepdims=True)
        var = jnp.mean(jnp.square(r - mu), axis=-1, keepdims=True)
        r_out.append((r - mu) * lax.rsqrt(var + EPS))
        gq = gy[:, sl]
        g_out.append(gq * lax.rsqrt(jnp.mean(gq * gq, axis=-1, keepdims=True) + EPS) * gn_ref[...])
    ret = jnp.concatenate(r_out, axis=1) * _silu(rg_ref[...])
    gla = jnp.concatenate(g_out, axis=1) * _silu(gg_ref[...])
    mix = jnp.concatenate([na_ref[...], ret, gla, sw_ref[...]], axis=1).astype(BF16)
    o_ref[...] = x_ref[...] + mg_ref[...] * jnp.dot(mix, w_ref[...], preferred_element_type=F32)


def outproj(na, rf, rb, gf, gb, sw, p, gla_norm_g, w_out, x2, mg, rows_per_mod):
    n, d = x2.shape
    gw = GROUP_WIDTH
    tm = 256
    m = mg.shape[0]
    row = lambda w_: pl.BlockSpec((tm, w_), lambda i: (i, 0))
    return pl.pallas_call(
        functools.partial(_outproj_kernel, head_w=RET_DV),
        grid=(n // tm,),
        in_specs=[row(gw), row(gw), row(gw), pl.BlockSpec((tm, gw), lambda i: (i, COL_RET_G // gw)),
                  row(gw), row(gw), pl.BlockSpec((tm, gw), lambda i: (i, COL_GLA_G // gw)), row(gw),
                  pl.BlockSpec((1, GLA_DV), lambda i: (0, 0)),
                  pl.BlockSpec((d, d), lambda i: (0, 0)), row(d),
                  pl.BlockSpec((None, 1, d), lambda i: (i // (rows_per_mod // tm), 0, 0))],
        out_specs=row(d),
        out_shape=jax.ShapeDtypeStruct((n, d), F32),
        compiler_params=_params(1),
    )(na, rf, rb, p, gf, gb, p, sw, gla_norm_g.reshape(1, GLA_DV), w_out, x2, mg.reshape(m, 1, d))


PEER_PAIRS = PEER_HEADS * PEER_TOPK
PEER_ROUTE_TOKENS = 128
PEER_EXPERT_TOKENS = 8
PEER_SUB = 8
PEER_FOLD = D_MODEL // PEER_SUB
INV_SQRT2 = 0.7071067811865476


def _topk_cols(s, payload=None, order=None):
    row = lax.broadcasted_iota(jnp.int32, s.shape, 0) if order is None else order
    vals, idxs = [], []
    for _ in range(PEER_TOPK):
        m = jnp.max(s, axis=0, keepdims=True)
        am = jnp.min(jnp.where(s == m, row, jnp.iinfo(jnp.int32).max), axis=0, keepdims=True)
        sel = row == am
        vals.append(m)
        idxs.append(am if payload is None else jnp.max(jnp.where(sel, payload, -1), axis=0, keepdims=True))
        s = jnp.where(sel, -jnp.inf, s)
    return jnp.concatenate(vals, axis=0), jnp.concatenate(idxs, axis=0)


def _staircase_candidates(v0, i0, v1, i1):
    k = PEER_TOPK
    assert k == 16
    t = v0.shape[1]
    r8 = lax.broadcasted_iota(jnp.int32, (8, t), 0)
    r16 = lax.broadcasted_iota(jnp.int32, (k, t), 0)

    def piece(a_sl, b_sl):
        return v0[a_sl, :] + v1[b_sl, :], i0[a_sl, :] * PEER_N_KEYS + i1[b_sl, :]

    one = lambda j: slice(j, j + 1)
    lo = slice(0, 8)
    pieces = [
        (one(0), slice(0, k), None, r16),
        (one(1), lo, None, k + r8),
        (one(2), lo, r8 <= 4, 2 * k + r8),
        (one(3), lo, r8 <= 3, 3 * k + r8),
        (slice(8, k), one(0), None, (r8 + 8) * k),
        (lo, one(0), r8 >= 4, r8 * k),
        (lo, one(1), r8 >= 4, r8 * k + 1),
        (lo, one(2), r8 == 4, r8 * k + 2),
    ]
    sums, ids, orders = [], [], []
    for a_sl, b_sl, keep, order in pieces:
        s, e = piece(a_sl, b_sl)
        sums.append(s if keep is None else jnp.where(keep, s, NEG_INF))
        ids.append(e)
        orders.append(order)
    return jnp.concatenate(sums, axis=0), jnp.concatenate(ids, axis=0), jnp.concatenate(orders, axis=0)


def _route_head(hb, wq_ref, sk_ref, hd):
    half = PEER_DK // 2
    q = jnp.dot(hb, wq_ref[hd], preferred_element_type=F32)
    tops = []
    for p in range(2):
        qp = q[:, p * half:(p + 1) * half].astype(BF16)
        tops.append(_topk_cols(_dot_nt(sk_ref[p, hd], qp)))
    (v0, i0), (v1, i1) = tops
    best_s, best_e = _topk_cols(*_staircase_candidates(v0, i0, v1, i1))
    e = jnp.exp(best_s - best_s[0:1, :])
    return e / jnp.sum(e, axis=0, keepdims=True), best_e


def _peer_route_kernel(x_ref, g_ref, shift_ref, scale_ref, wq_ref, sk_ref, h_ref, idx_ref, gate_ref):
    x = x_ref[...]
    y = x * lax.rsqrt(jnp.mean(x * x, axis=-1, keepdims=True) + EPS)
    h = (y * g_ref[...]) * (1.0 + scale_ref[...]) + shift_ref[...]
    h_ref[...] = h
    hb = h.astype(BF16)

    for hd in range(PEER_HEADS):
        gate_ref[hd], idx_ref[hd] = _route_head(hb, wq_ref, sk_ref, hd)


def peer_route(x2, g, shift, scale, rows_per_mod, wq_h, sk):
    n, d = x2.shape
    t = PEER_ROUTE_TOKENS
    m = shift.shape[0]
    mod_map = lambda i: (i // (rows_per_mod // t), 0, 0)
    return pl.pallas_call(
        _peer_route_kernel,
        grid=(n // t,),
        in_specs=[
            pl.BlockSpec((t, d), lambda i: (i, 0)),
            pl.BlockSpec((1, d), lambda i: (0, 0)),
            pl.BlockSpec((None, 1, d), mod_map),
            pl.BlockSpec((None, 1, d), mod_map),
            pl.BlockSpec((PEER_HEADS, d, PEER_DK), lambda i: (0, 0, 0)),
            pl.BlockSpec((2, PEER_HEADS, PEER_N_KEYS, PEER_DK // 2), lambda i: (0, 0, 0, 0)),
        ],
        out_specs=[
            pl.BlockSpec((t, d), lambda i: (i, 0)),
            pl.BlockSpec((PEER_HEADS, PEER_TOPK, t), lambda i: (0, 0, i)),
            pl.BlockSpec((PEER_HEADS, PEER_TOPK, t), lambda i: (0, 0, i)),
        ],
        out_shape=[
            jax.ShapeDtypeStruct((n, d), F32),
            jax.ShapeDtypeStruct((PEER_HEADS, PEER_TOPK, n), jnp.int32),
            jax.ShapeDtypeStruct((PEER_HEADS, PEER_TOPK, n), F32),
        ],
        compiler_params=_params(1),
    )(x2, g.reshape(1, d), shift.reshape(m, 1, d), scale.reshape(m, 1, d), wq_h, sk)


def pack_experts(u, v):
    ub = lax.bitcast_convert_type(u.astype(BF16), jnp.uint16).astype(jnp.uint32)
    vb = lax.bitcast_convert_type(v.astype(BF16), jnp.uint16).astype(jnp.uint32)
    words = lax.bitcast_convert_type((ub << 16) | vb, jnp.int32)
    return words.reshape(u.shape[0], PEER_SUB, PEER_FOLD)


def _fold_rows(hbuf, h_rows, tb):
    for s in range(PEER_SUB):
        for c in range(PEER_FOLD // LANES):
            lo = s * PEER_FOLD + c * LANES
            hbuf[c, pl.ds(s * tb, tb), :] = h_rows[:, lo:lo + LANES]


def _unfold_residual(o_ref, x_ref, og_ref, ybuf, tb):
    for s in range(PEER_SUB):
        for c in range(PEER_FOLD // LANES):
            sl = slice(s * PEER_FOLD + c * LANES, s * PEER_FOLD + (c + 1) * LANES)
            o_ref[:, sl] = x_ref[:, sl] + og_ref[:, sl] * ybuf[c, pl.ds(s, tb, stride=PEER_SUB), :]


def _expert_weights(j, buf, hbuf, pbuf, wrow, tb, gate_column):
    fold = PEER_FOLD
    hj = jnp.concatenate([hbuf[c, pl.ds(j, PEER_SUB, stride=tb), :] for c in range(fold // LANES)], axis=1)
    for p in range(PEER_PAIRS):
        u = lax.bitcast_convert_type(buf[j, p] & jnp.int32(-65536), F32)
        prod = u * hj
        pbuf[pl.ds(p * PEER_SUB, PEER_SUB), :] = prod[:, :LANES] + prod[:, LANES:]
    part = pbuf[pl.ds(0, PEER_PAIRS, stride=PEER_SUB), :]
    for s in range(1, PEER_SUB):
        part = part + pbuf[pl.ds(s, PEER_PAIRS, stride=PEER_SUB), :]
    sc = jnp.sum(part, axis=1, keepdims=True)
    act = 0.5 * sc * (1.0 + lax.erf(sc * INV_SQRT2))
    g = jnp.concatenate([gate_column(hd) for hd in range(PEER_HEADS)], axis=0)
    wrow[...] = jnp.broadcast_to(g * act, (PEER_PAIRS, LANES))


def _expert_mix(j, buf, wrow, ybuf, start_fetch):
    fold = PEER_FOLD
    accs = [jnp.zeros((PEER_SUB, fold), F32) for _ in range(4)]
    for p in range(PEER_PAIRS):
        if start_fetch is not None:
            start_fetch(p)
        v = lax.bitcast_convert_type(buf[j, p] << 16, F32)
        wp = jnp.broadcast_to(wrow[p:p + 1, :], (PEER_SUB, LANES))
        accs[p % 4] = accs[p % 4] + v * jnp.concatenate([wp, wp], axis=1)
    yj = (accs[0] + accs[1]) + (accs[2] + accs[3])
    for c in range(fold // LANES):
        ybuf[c, pl.ds(j * PEER_SUB, PEER_SUB), :] = yj[:, c * LANES:(c + 1) * LANES]


def _peer_expert_kernel(idx0_ref, idxn_ref, h_ref, gate_ref, x_ref, og_ref, uv_hbm, o_ref,
                        buf_even, buf_odd, sem, hbuf, pbuf, wbuf, ybuf):
    i = pl.program_id(0)
    n = pl.num_programs(0)
    tb = PEER_EXPERT_TOKENS
    bufs = (buf_even, buf_odd)

    def slab_copy(idx_ref, j, r, parity):
        e = idx_ref[0, 0, j * PEER_PAIRS + r]
        return pltpu.make_async_copy(uv_hbm.at[e], bufs[parity].at[j, r], sem.at[parity, j])

    def wait_token(j, parity):
        pltpu.make_async_copy(uv_hbm.at[pl.ds(0, PEER_PAIRS)], bufs[parity].at[j], sem.at[parity, j]).wait()

    @pl.when(i == 0)
    def _():
        def prime(j, carry):
            for r in range(PEER_PAIRS):
                slab_copy(idx0_ref, j, r, 0).start(priority=r % 2)
            return carry
        lax.fori_loop(0, tb, prime, 0)

    lane = lax.broadcasted_iota(jnp.int32, (PEER_TOPK, PEER_ROUTE_TOKENS), 1)
    lane0 = (i % (PEER_ROUTE_TOKENS // tb)) * tb

    _fold_rows(hbuf, h_ref[...], tb)

    def fetcher(j, parity, prefetch):
        return (lambda r: slab_copy(idxn_ref, j, r, 1 - parity).start(priority=r % 2)) if prefetch else None

    def gate_column_of(j):
        return lambda hd: jnp.sum(jnp.where(lane == lane0 + j, gate_ref[hd], 0.0), axis=1, keepdims=True)

    def block(parity, prefetch):
        for j in range(tb):
            wait_token(j, parity)
            _expert_weights(j, bufs[parity], hbuf, pbuf, wbuf, tb, gate_column_of(j))
            _expert_mix(j, bufs[parity], wbuf, ybuf, fetcher(j, parity, prefetch))

    for parity in range(2):
        for prefetch in (True, False):
            @pl.when((i % 2 == parity) & ((i + 1 < n) == prefetch))
            def _(parity=parity, prefetch=prefetch):
                block(parity, prefetch)

    _unfold_residual(o_ref, x_ref, og_ref, ybuf, tb)


def peer_expert(h, idx, gate, x2, out_gate, rows_per_mod, uv):
    n, d = h.shape
    tb = PEER_EXPERT_TOKENS
    nb = n // tb
    m = out_gate.shape[0]
    rows = tb * PEER_PAIRS
    idx_rows = idx.reshape(PEER_PAIRS, n).T.reshape(nb, 1, rows)
    gate_blocks = PEER_ROUTE_TOKENS // tb
    return pl.pallas_call(
        _peer_expert_kernel,
        grid=(nb,),
        in_specs=[
            pl.BlockSpec((1, 1, rows), lambda i: (0, 0, 0), memory_space=pltpu.SMEM),
            pl.BlockSpec((1, 1, rows), lambda i: (jnp.minimum(i + 1, nb - 1), 0, 0), memory_space=pltpu.SMEM),
            pl.BlockSpec((tb, d), lambda i: (i, 0)),
            pl.BlockSpec((PEER_HEADS, PEER_TOPK, PEER_ROUTE_TOKENS), lambda i: (0, 0, i // gate_blocks)),
            pl.BlockSpec((tb, d), lambda i: (i, 0)),
            pl.BlockSpec((None, 1, d), lambda i: (i // (rows_per_mod // tb), 0, 0)),
            pl.BlockSpec(memory_space=pl.ANY),
        ],
        out_specs=pl.BlockSpec((tb, d), lambda i: (i, 0)),
        out_shape=jax.ShapeDtypeStruct((n, d), F32),
        scratch_shapes=[pltpu.VMEM((tb, PEER_PAIRS, PEER_SUB, PEER_FOLD), jnp.int32),
                        pltpu.VMEM((tb, PEER_PAIRS, PEER_SUB, PEER_FOLD), jnp.int32),
                        pltpu.SemaphoreType.DMA((2, tb)),
                        pltpu.VMEM((PEER_FOLD // LANES, PEER_SUB * tb, LANES), F32),
                        pltpu.VMEM((PEER_PAIRS * PEER_SUB, LANES), F32),
                        pltpu.VMEM((PEER_PAIRS, LANES), F32),
                        pltpu.VMEM((PEER_FOLD // LANES, tb * PEER_SUB, LANES), F32)],
        compiler_params=pltpu.CompilerParams(dimension_semantics=("arbitrary",), vmem_limit_bytes=VMEM_LIMIT,
                                             disable_bounds_checks=True),
    )(idx_rows, idx_rows, h, gate, x2, out_gate.reshape(m, 1, d), uv)


def peer_residual(x2, g, shift, scale, out_gate, rows_per_mod, wq_h, sk, uv):
    h, idx, gate = peer_route(x2, g, shift, scale, rows_per_mod, wq_h, sk)
    return peer_expert(h, idx, gate, x2, out_gate, rows_per_mod, uv)


def _final_norm_kernel(x_ref, g_ref, o_ref):
    x = x_ref[...]
    y = x * lax.rsqrt(jnp.mean(x * x, axis=-1, keepdims=True) + EPS)
    o_ref[...] = y * g_ref[...]


def final_norm(x2, g):
    n, d = x2.shape
    tm = 512
    return pl.pallas_call(
        _final_norm_kernel,
        grid=(n // tm,),
        in_specs=[pl.BlockSpec((tm, d), lambda i: (i, 0)), pl.BlockSpec((1, d), lambda i: (0, 0))],
        out_specs=pl.BlockSpec((tm, d), lambda i: (i, 0)),
        out_shape=jax.ShapeDtypeStruct((n, d), x2.dtype),
        compiler_params=_params(1),
    )(x2, g.reshape(1, d))


def _mixers(p, pz, n_batch, seq, ctx_len, is_ctx, tables, prm):
    if is_ctx:
        na = ctx_attention(p, n_batch, seq, (COL_NA_Q, COL_NA_K, COL_NA_V), GROUP_WIDTH, None)
        sw = ctx_attention(p, n_batch, seq, (COL_SWA_Q, COL_SWA_K, COL_SWA_V), SWA_KV_HEADS * HEAD_DIM, prm["sink"])
    else:
        na = na_attention(p, pz, n_batch, seq, ctx_len, prm["na_bias"])
        sw = swa_attention(p, pz, n_batch, seq, ctx_len, prm["sink"], tables["swa_q"], tables["swa_k"])
    rf, rb, rs = bidir_scan(p, n_batch, seq, (COL_RET_Q, COL_RET_K, COL_RET_V), RET_HEADS, RET_DK, RET_DV, "ret",
                            prm["ret_s0"], lg=prm["ret_lg"], rope=None if is_ctx else tables["ret"],
                            k_scale=RET_DK ** -0.5)
    gf, gb, gs = bidir_scan(p, n_batch, seq, (COL_GLA_Q, COL_GLA_K, COL_GLA_V), GLA_HEADS, GLA_DK, GLA_DV, "gla",
                            prm["gla_s0"], wup=prm["gla_wup"], bup=prm["gla_bup"], q_scale=GLA_DK ** -0.5)
    return (na, rf, rb, gf, gb, sw), (rs, gs)


def kernel(x, c, ctx, c_ctx, w_ada, b_ada, norm_attn_g, norm_ffn_g, w_in, na_rpb, ret_log_gamma,
           gla_w_gate_up, gla_b_gate, gla_norm_g, swa_sink, w_out, peer_w_q, peer_sub_keys,
           peer_u, peer_v, final_g):
    bsz, slen, d = x.shape
    zlen = ctx.shape[1]
    x2 = x.reshape(bsz * slen, d)
    z2 = ctx.reshape(bsz * zlen, d)
    tables = {"ret": rope_lane_tables(slen, RET_DK, 1),
              "swa_q": rope_lane_tables(slen, HEAD_DIM, SWA_HEADS),
              "swa_k": rope_lane_tables(slen, HEAD_DIM, SWA_KV_HEADS)}
    c_rows = jnp.zeros((8, d), F32).at[:bsz].set(c).at[bsz].set(c_ctx)
    for layer in range(DEPTH):
        has_next = layer < DEPTH - 1
        mod = adaln(c_rows, w_ada[layer], b_ada[layer])
        mx = [mod[:bsz, k * d:(k + 1) * d] for k in range(6)]
        mz = [mod[bsz:bsz + 1, k * d:(k + 1) * d] for k in range(6)]

        wi = w_in[layer]
        wp = jnp.concatenate([wi[:, :REF_COL_GLA_D], wi[:, REF_COL_SWA_Q:], wi[:, REF_COL_GLA_D:REF_COL_SWA_Q],
                              jnp.zeros((d, PROJ_WIDTH - REF_D_IN), F32)], axis=1).astype(BF16)
        px = modproj(x2, norm_attn_g[layer], mx[0], mx[1], slen, wp)
        pz = modproj(z2, norm_attn_g[layer], mz[0], mz[1], bsz * zlen, wp)

        wup = (jnp.zeros((2, LANES, GLA_HEADS * GLA_DK), F32)
               .at[0, :GLA_RANK].set(gla_w_gate_up[layer, 0])
               .at[1, GLA_RANK:2 * GLA_RANK].set(gla_w_gate_up[layer, 1])).astype(BF16)
        prm = {"na_bias": na_band_bias(na_rpb[layer]), "sink": swa_sink[layer],
               "ret_lg": jnp.repeat(ret_log_gamma[layer], RET_DK, axis=1),
               "gla_wup": wup, "gla_bup": gla_b_gate[layer].reshape(2, 1, GLA_HEADS * GLA_DK),
               "ret_s0": jnp.zeros((bsz, 2, RET_HEADS, RET_DV, RET_DK), F32),
               "gla_s0": jnp.zeros((bsz, 2, GLA_HEADS, GLA_DV, GLA_DK), F32)}
        mix_z, (ret_s, gla_s) = _mixers(pz, pz, bsz, zlen, zlen, True, tables, prm)
        prm["ret_s0"], prm["gla_s0"] = ret_s, gla_s
        mix_x, _ = _mixers(px, pz, bsz, slen, zlen, False, tables, prm)

        wo = w_out[layer].astype(BF16)
        x2 = outproj(*mix_x, px, gla_norm_g[layer], wo, x2, mx[2], slen)

        uv = pack_experts(peer_u[layer], peer_v[layer])
        wq_h = peer_w_q[layer].reshape(d, PEER_HEADS, PEER_DK).transpose(1, 0, 2).astype(BF16)
        sk = peer_sub_keys[layer].astype(BF16)
        x2 = peer_residual(x2, norm_ffn_g[layer], mx[3], mx[4], mx[5], slen, wq_h, sk, uv)
        if has_next:
            z2 = outproj(*mix_z, pz, gla_norm_g[layer], wo, z2, mz[2], bsz * zlen)
            z2 = peer_residual(z2, norm_ffn_g[layer], mz[3], mz[4], mz[5], bsz * zlen, wq_h, sk, uv)
    return final_norm(x2, final_g).reshape(bsz, slen, d)
```

```python
import functools

import jax
import jax.numpy as jnp
import numpy as np
from jax import lax
from jax.experimental import pallas as pl
from jax.experimental.pallas import tpu as pltpu

D_MODEL = 2048
DEPTH = 2
GRID_W = 64
EPS = 1e-6
ROPE_BASE = 10000.0

GROUP_WIDTH = D_MODEL // 4
HEAD_DIM = 64
NA_HEADS = GROUP_WIDTH // HEAD_DIM
NA_ROWS = 8
NA_COLS = 16
RET_HEADS = 4
RET_DK = GROUP_WIDTH // RET_HEADS
RET_DV = GROUP_WIDTH // RET_HEADS
GLA_HEADS = 4
GLA_DV = GROUP_WIDTH // GLA_HEADS
GLA_DK = GLA_DV // 2
GLA_RANK = 16
GLA_TAU = 16.0
SWA_HEADS = GROUP_WIDTH // HEAD_DIM
SWA_KV_HEADS = SWA_HEADS // 4
SWA_WINDOW = 128
SWA_BLOCK = 128
SCAN_CHUNK = 64
PEER_HEADS = 8
PEER_N_KEYS = 128
PEER_N_EXPERTS = PEER_N_KEYS * PEER_N_KEYS
PEER_DK = 256
PEER_TOPK = 16

LANES = 128
VMEM_LIMIT = 48 * 1024 * 1024
BF16 = jnp.bfloat16
F32 = jnp.float32
NEG_INF = float("-inf")

COL_NA_Q, COL_NA_K, COL_NA_V = 0, 512, 1024
COL_RET_Q, COL_RET_K, COL_RET_V, COL_RET_G = 1536, 2048, 2560, 3072
COL_GLA_Q, COL_GLA_K, COL_GLA_V, COL_GLA_G = 3584, 3840, 4096, 4608
COL_SWA_Q, COL_SWA_K, COL_SWA_V = 5120, 5632, 5760
COL_GLA_D = 5888
REF_COL_GLA_D, REF_COL_SWA_Q, REF_D_IN = 5120, 5152, 5920
PROJ_WIDTH = 6144


def _silu(x):
    return x / (1.0 + jnp.exp(-x))


def _dot_nt(a, b):
    return lax.dot_general(a, b, (((1,), (1,)), ((), ())), preferred_element_type=F32)


def _dot_tn(a, b):
    return lax.dot_general(a, b, (((0,), (0,)), ((), ())), preferred_element_type=F32)


def _params(n_axes):
    return pltpu.CompilerParams(dimension_semantics=("arbitrary",) * n_axes, vmem_limit_bytes=VMEM_LIMIT)


def _rope_lanes(x, cs, sn, quarter):
    n = x.shape[-1]
    lane = lax.broadcasted_iota(jnp.int32, x.shape, x.ndim - 1)
    first = (lane % (2 * quarter)) < quarter
    swapped = jnp.where(first, pltpu.roll(x, n - quarter, x.ndim - 1), pltpu.roll(x, quarter, x.ndim - 1))
    return x * cs + swapped * sn


def rope_lane_tables(length, dh, copies):
    t = jnp.arange(length)
    pos = jnp.stack([t // GRID_W, t % GRID_W], axis=-1).astype(F32)
    quarter = dh // 4
    inv = ROPE_BASE ** (-jnp.arange(quarter, dtype=F32) / quarter)
    ang = pos[:, :, None] * inv
    cos, sin = jnp.cos(ang), jnp.sin(ang)
    cl = jnp.concatenate([cos[:, 0], cos[:, 0], cos[:, 1], cos[:, 1]], axis=-1)
    sl = jnp.concatenate([-sin[:, 0], sin[:, 0], -sin[:, 1], sin[:, 1]], axis=-1)
    return jnp.tile(cl, (1, copies)), jnp.tile(sl, (1, copies))


def _adaln_kernel(c_ref, w_ref, b_ref, o_ref):
    a = _silu(c_ref[...]).astype(BF16)
    o_ref[...] = jnp.dot(a, w_ref[...].astype(BF16), preferred_element_type=F32) + b_ref[...]


def adaln(c_rows, w, b):
    r, d = c_rows.shape
    m = w.shape[1]
    tn = 1024
    return pl.pallas_call(
        _adaln_kernel,
        grid=(m // tn,),
        in_specs=[pl.BlockSpec((r, d), lambda j: (0, 0)), pl.BlockSpec((d, tn), lambda j: (0, j)),
                  pl.BlockSpec((1, tn), lambda j: (0, j))],
        out_specs=pl.BlockSpec((r, tn), lambda j: (0, j)),
        out_shape=jax.ShapeDtypeStruct((r, m), F32),
        compiler_params=_params(1),
    )(c_rows, w, b.reshape(1, m))


def _modproj_kernel(x_ref, g_ref, shift_ref, scale_ref, w_ref, o_ref, hb_ref):
    @pl.when(pl.program_id(1) == 0)
    def _():
        x = x_ref[...]
        y = x * lax.rsqrt(jnp.mean(x * x, axis=-1, keepdims=True) + EPS)
        hb_ref[...] = ((y * g_ref[...]) * (1.0 + scale_ref[...]) + shift_ref[...]).astype(BF16)

    o_ref[...] = jnp.dot(hb_ref[...], w_ref[...], preferred_element_type=F32)


def modproj(x2, g, shift, scale, rows_per_mod, w):
    n, d = x2.shape
    wid = w.shape[1]
    tm = min(512, rows_per_mod)
    tn = 2048
    m = shift.shape[0]
    mod_map = lambda i, j: (i // (rows_per_mod // tm), 0, 0)
    return pl.pallas_call(
        _modproj_kernel,
        grid=(n // tm, wid // tn),
        in_specs=[pl.BlockSpec((tm, d), lambda i, j: (i, 0)), pl.BlockSpec((1, d), lambda i, j: (0, 0)),
                  pl.BlockSpec((None, 1, d), mod_map), pl.BlockSpec((None, 1, d), mod_map),
                  pl.BlockSpec((d, tn), lambda i, j: (0, j))],
        out_specs=pl.BlockSpec((tm, tn), lambda i, j: (i, j)),
        out_shape=jax.ShapeDtypeStruct((n, wid), F32),
        scratch_shapes=[pltpu.VMEM((tm, d), BF16)],
        compiler_params=_params(2),
    )(x2, g.reshape(1, d), shift.reshape(m, 1, d), scale.reshape(m, 1, d), w)


NA_QROWS = 4


def _na_kernel(q_ref, k_ref, v_ref, kz_ref, vz_ref, bias_ref, o_ref, *, rows):
    step = pl.program_id(2)
    dh = HEAD_DIM
    band = NA_ROWS * GRID_W
    kz = kz_ref[...].astype(BF16)
    vz = vz_ref[...].astype(BF16)
    heads = LANES // dh
    units = [(qr, hh) for qr in range(NA_QROWS) for hh in range(heads)]
    scores, vbands = {}, {}
    for qr in range(NA_QROWS):
        r = step * NA_QROWS + qr
        start = jnp.clip(r - NA_ROWS // 2, 0, rows - NA_ROWS)
        dr0 = start - r + NA_ROWS - 1
        tok0 = pl.multiple_of(start * GRID_W, GRID_W)
        kb = k_ref[pl.ds(tok0, band), :].astype(BF16)
        vbands[qr] = v_ref[pl.ds(tok0, band), :].astype(BF16)
        q = (q_ref[pl.ds(qr * GRID_W, GRID_W), :] * (dh ** -0.5)).astype(BF16)
        for hh in range(heads):
            sl = slice(hh * dh, (hh + 1) * dh)
            scores[qr, hh] = (_dot_nt(q[:, sl], kb[:, sl]) + bias_ref[hh, dr0],
                              _dot_nt(q[:, sl], kz[:, sl]))
    probs = {}
    for u in units:
        s_nb, s_cx = scores[u]
        m = jnp.maximum(jnp.max(s_nb, axis=1, keepdims=True), jnp.max(s_cx, axis=1, keepdims=True))
        p_nb = jnp.exp(s_nb - m)
        p_cx = jnp.exp(s_cx - m)
        den = jnp.sum(p_nb, axis=1, keepdims=True) + jnp.sum(p_cx, axis=1, keepdims=True)
        probs[u] = (p_nb.astype(BF16), p_cx.astype(BF16), den)
    for qr in range(NA_QROWS):
        outs = []
        for hh in range(heads):
            sl = slice(hh * dh, (hh + 1) * dh)
            p_nb, p_cx, den = probs[qr, hh]
            o = (jnp.dot(p_nb, vbands[qr][:, sl], preferred_element_type=F32)
                 + jnp.dot(p_cx, vz[:, sl], preferred_element_type=F32))
            outs.append(o / den)
        o_ref[pl.ds(qr * GRID_W, GRID_W), :] = jnp.concatenate(outs, axis=1)


def na_band_bias(rpb):
    col = jnp.arange(GRID_W)
    col_start = jnp.clip(col - NA_COLS // 2, 0, GRID_W - NA_COLS)
    col_ok = (col[None, :] >= col_start[:, None]) & (col[None, :] < col_start[:, None] + NA_COLS)
    d_col = jnp.clip(col[None, :] - col[:, None], -(NA_COLS - 1), NA_COLS - 1) + NA_COLS - 1
    d_row = jnp.arange(NA_ROWS)[:, None] + jnp.arange(NA_ROWS)[None, :]
    b = rpb.astype(F32)[:, d_row][..., d_col]
    b = jnp.where(col_ok[None, None, None], b, NEG_INF)
    return b.transpose(0, 1, 3, 2, 4).reshape(rpb.shape[0], NA_ROWS, GRID_W, NA_ROWS * GRID_W)


def na_attention(px, pz, n_batch, seq, ctx_len, bias):
    rows = seq // GRID_W
    tq = NA_QROWS * GRID_W
    nsteps = rows // NA_QROWS
    heads_per_blk = LANES // HEAD_DIM
    return pl.pallas_call(
        functools.partial(_na_kernel, rows=rows),
        grid=(n_batch, NA_HEADS // heads_per_blk, nsteps),
        in_specs=[
            pl.BlockSpec((tq, LANES), lambda b, hp, s: (b * nsteps + s, COL_NA_Q // LANES + hp)),
            pl.BlockSpec((seq, LANES), lambda b, hp, s: (b, COL_NA_K // LANES + hp)),
            pl.BlockSpec((seq, LANES), lambda b, hp, s: (b, COL_NA_V // LANES + hp)),
            pl.BlockSpec((ctx_len, LANES), lambda b, hp, s: (b, COL_NA_K // LANES + hp)),
            pl.BlockSpec((ctx_len, LANES), lambda b, hp, s: (b, COL_NA_V // LANES + hp)),
            pl.BlockSpec((heads_per_blk, NA_ROWS, GRID_W, NA_ROWS * GRID_W), lambda b, hp, s: (hp, 0, 0, 0)),
        ],
        out_specs=pl.BlockSpec((tq, LANES), lambda b, hp, s: (b * nsteps + s, hp)),
        out_shape=jax.ShapeDtypeStruct((n_batch * seq, GROUP_WIDTH), F32),
        compiler_params=_params(3),
    )(px, px, px, pz, pz, bias)


def _swa_kernel(q_ref, kp_ref, kc_ref, kn_ref, vp_ref, vc_ref, vn_ref, kz_ref, vz_ref, sink_ref,
                cq_ref, sq_ref, ckp_ref, skp_ref, ckc_ref, skc_ref, ckn_ref, skn_ref, o_ref):
    n = pl.program_id(1)
    nb = pl.num_programs(1)
    dh = HEAD_DIM
    blk = SWA_BLOCK
    quarter = dh // 4
    group = SWA_HEADS // SWA_KV_HEADS
    q = _rope_lanes(q_ref[...], cq_ref[...], sq_ref[...], quarter) * (dh ** -0.5)
    kp = _rope_lanes(kp_ref[...], ckp_ref[...], skp_ref[...], quarter).astype(BF16)
    kc = _rope_lanes(kc_ref[...], ckc_ref[...], skc_ref[...], quarter).astype(BF16)
    kn = _rope_lanes(kn_ref[...], ckn_ref[...], skn_ref[...], quarter).astype(BF16)
    kz = kz_ref[...].astype(BF16)
    vp, vc, vn, vz = (r[...].astype(BF16) for r in (vp_ref, vc_ref, vn_ref, vz_ref))
    qi = lax.broadcasted_iota(jnp.int32, (group * blk, blk), 0) % blk
    kj = lax.broadcasted_iota(jnp.int32, (group * blk, blk), 1)
    ok_p = (kj >= qi) & (n > 0)
    ok_n = (kj <= qi) & (n < nb - 1)
    outs = []
    for hk in range(SWA_KV_HEADS):
        ks = slice(hk * dh, (hk + 1) * dh)
        qs = jnp.concatenate([q[:, (hk * group + g) * dh:(hk * group + g + 1) * dh] for g in range(group)],
                             axis=0).astype(BF16)
        sink = jnp.concatenate([jnp.full((blk, 1), 1.0, F32) * sink_ref[hk * group + g] for g in range(group)],
                               axis=0)
        s_p = jnp.where(ok_p, _dot_nt(qs, kp[:, ks]), NEG_INF)
        s_c = _dot_nt(qs, kc[:, ks])
        s_n = jnp.where(ok_n, _dot_nt(qs, kn[:, ks]), NEG_INF)
        s_z = _dot_nt(qs, kz[:, ks])
        m = jnp.maximum(jnp.maximum(jnp.max(s_p, axis=1, keepdims=True), jnp.max(s_c, axis=1, keepdims=True)),
                        jnp.maximum(jnp.max(s_n, axis=1, keepdims=True), jnp.max(s_z, axis=1, keepdims=True)))
        m = jnp.maximum(m, sink)
        e_p, e_c, e_n, e_z = (jnp.exp(s - m) for s in (s_p, s_c, s_n, s_z))
        den = (jnp.sum(e_p, axis=1, keepdims=True) + jnp.sum(e_c, axis=1, keepdims=True)
               + jnp.sum(e_n, axis=1, keepdims=True) + jnp.sum(e_z, axis=1, keepdims=True) + jnp.exp(sink - m))
        o = (jnp.dot(e_p.astype(BF16), vp[:, ks], preferred_element_type=F32)
             + jnp.dot(e_c.astype(BF16), vc[:, ks], preferred_element_type=F32)
             + jnp.dot(e_n.astype(BF16), vn[:, ks], preferred_element_type=F32)
             + jnp.dot(e_z.astype(BF16), vz[:, ks], preferred_element_type=F32)) / den
        outs += [o[g * blk:(g + 1) * blk, :] for g in range(group)]
    o_ref[...] = jnp.concatenate(outs, axis=1)


def swa_attention(px, pz, n_batch, seq, ctx_len, sink, rope_q, rope_k):
    blk = SWA_BLOCK
    nb = seq // blk
    prev = lambda b, n: b * nb + jnp.maximum(n - 1, 0)
    cur = lambda b, n: b * nb + n
    nxt = lambda b, n: b * nb + jnp.minimum(n + 1, nb - 1)
    kv = lambda off, f: pl.BlockSpec((blk, LANES), lambda b, n: (f(b, n), off // LANES))
    tab = lambda w, f: pl.BlockSpec((blk, w), lambda b, n: (f(0, n), 0))
    cq, sq = rope_q
    ck, sk = rope_k
    return pl.pallas_call(
        _swa_kernel,
        grid=(n_batch, nb),
        in_specs=[
            pl.BlockSpec((blk, GROUP_WIDTH), lambda b, n: (cur(b, n), COL_SWA_Q // GROUP_WIDTH)),
            kv(COL_SWA_K, prev), kv(COL_SWA_K, cur), kv(COL_SWA_K, nxt),
            kv(COL_SWA_V, prev), kv(COL_SWA_V, cur), kv(COL_SWA_V, nxt),
            pl.BlockSpec((ctx_len, LANES), lambda b, n: (b, COL_SWA_K // LANES)),
            pl.BlockSpec((ctx_len, LANES), lambda b, n: (b, COL_SWA_V // LANES)),
            pl.BlockSpec(memory_space=pltpu.SMEM),
            tab(GROUP_WIDTH, cur), tab(GROUP_WIDTH, cur), tab(LANES, prev), tab(LANES, prev),
            tab(LANES, cur), tab(LANES, cur), tab(LANES, nxt), tab(LANES, nxt),
        ],
        out_specs=pl.BlockSpec((blk, GROUP_WIDTH), lambda b, n: (cur(b, n), 0)),
        out_shape=jax.ShapeDtypeStruct((n_batch * seq, GROUP_WIDTH), F32),
        compiler_params=_params(2),
    )(px, px, px, px, px, px, px, pz, pz, sink, cq, sq, ck, sk, ck, sk, ck, sk)


def _ctx_attn_kernel(q_ref, k_ref, v_ref, sink_ref, o_ref, *, group, use_sink):
    dh = HEAD_DIM
    q = (q_ref[...] * (dh ** -0.5)).astype(BF16)
    k = k_ref[...].astype(BF16)
    v = v_ref[...].astype(BF16)
    outs = []
    for qh in range(q.shape[1] // dh):
        ks = slice((qh // group) * dh, (qh // group + 1) * dh)
        s = _dot_nt(q[:, qh * dh:(qh + 1) * dh], k[:, ks])
        m = jnp.max(s, axis=1, keepdims=True)
        if use_sink:
            m = jnp.maximum(m, sink_ref[qh])
        e = jnp.exp(s - m)
        den = jnp.sum(e, axis=1, keepdims=True)
        if use_sink:
            den = den + jnp.exp(sink_ref[qh] - m)
        outs.append(jnp.dot(e.astype(BF16), v[:, ks], preferred_element_type=F32) / den)
    o_ref[...] = jnp.concatenate(outs, axis=1)


def ctx_attention(pz, n_batch, ctx_len, cols, kv_width, sink):
    qc, kc, vc = cols
    use_sink = sink is not None
    if sink is None:
        sink = jnp.zeros((GROUP_WIDTH // HEAD_DIM,), F32)
    return pl.pallas_call(
        functools.partial(_ctx_attn_kernel, group=GROUP_WIDTH // kv_width, use_sink=use_sink),
        grid=(n_batch,),
        in_specs=[pl.BlockSpec((ctx_len, GROUP_WIDTH), lambda b: (b, qc // GROUP_WIDTH)),
                  pl.BlockSpec((ctx_len, kv_width), lambda b: (b, kc // kv_width)),
                  pl.BlockSpec((ctx_len, kv_width), lambda b: (b, vc // kv_width)),
                  pl.BlockSpec(memory_space=pltpu.SMEM)],
        out_specs=pl.BlockSpec((ctx_len, GROUP_WIDTH), lambda b: (b, 0)),
        out_shape=jax.ShapeDtypeStruct((n_batch * ctx_len, GROUP_WIDTH), F32),
        compiler_params=_params(1),
    )(pz, pz, pz, sink)


SCAN_ROWS = 256


def _split3(x):
    a = x.astype(BF16)
    r = x - a.astype(F32)
    b = r.astype(BF16)
    c = (r - b.astype(F32)).astype(BF16)
    return a, b, c


def _scan_kernel(*refs, heads, dk, dv, kind, rope, q_scale, k_scale):
    it = iter(refs)
    qf, kf, vf, qb, kb, vb = (next(it) for _ in range(6))
    if kind == "ret":
        lg = next(it)
    else:
        df, db, wup, bup = (next(it) for _ in range(4))
    if rope:
        cosf, sinf, cosb, sinb = (next(it) for _ in range(4))
    s0 = next(it)
    of, ob, sfin = next(it), next(it), next(it)
    st = next(it)

    s = pl.program_id(1)
    c = SCAN_CHUNK
    nch = SCAN_ROWS // c
    hk = heads * dk

    @pl.when(s == 0)
    def _():
        st[...] = s0[...]

    r_i = lax.broadcasted_iota(jnp.int32, (c, c), 0)
    c_i = lax.broadcasted_iota(jnp.int32, (c, c), 1)
    masks = (r_i >= c_i, c_i > r_i)

    if kind == "ret":
        pos = lax.broadcasted_iota(jnp.int32, (c, hk), 0).astype(F32)
        gcums = ((pos + 1.0) * lg[0:1, :], (float(c) - pos) * lg[1:2, :])
    else:
        rr = lax.broadcasted_iota(jnp.int32, (SCAN_ROWS, SCAN_ROWS), 0)
        cc = lax.broadcasted_iota(jnp.int32, (SCAN_ROWS, SCAN_ROWS), 1)
        same = (rr // c) == (cc // c)
        tris = (jnp.where(same & (rr >= cc), 1.0, 0.0).astype(BF16),
                jnp.where(same & (cc >= rr), 1.0, 0.0).astype(BF16))

        def gate_cum(d_ref, direction):
            pre = jnp.dot(d_ref[...].astype(BF16), wup[direction], preferred_element_type=F32) + bup[direction]
            g = -(jnp.maximum(-pre, 0.0) + jnp.log1p(jnp.exp(-jnp.abs(pre)))) / GLA_TAU
            return sum(jnp.dot(tris[direction], p, preferred_element_type=F32) for p in _split3(g))

        gcums = (gate_cum(df, 0), gate_cum(db, 1))

    def one(direction, q_ref, k_ref, v_ref, o_ref, cos_ref, sin_ref, ch):
        rows = pl.ds(ch * c, c)
        q = q_ref[rows, :]
        k = k_ref[rows, :]
        v = v_ref[rows, :]
        if rope:
            cs = jnp.concatenate([cos_ref[rows, :]] * heads, axis=1)
            sn = jnp.concatenate([sin_ref[rows, :]] * heads, axis=1)
            q = _rope_lanes(q, cs, sn, dk // 4)
            k = _rope_lanes(k, cs, sn, dk // 4)
        if q_scale != 1.0:
            q = q * q_scale
        if k_scale != 1.0:
            k = k * k_scale
        gcum = gcums[direction] if kind == "ret" else gcums[direction][ch * c:(ch + 1) * c, :]
        gtot = gcum[c - 1:c, :] if direction == 0 else gcum[0:1, :]
        q_rel = (q * jnp.exp(gcum - gtot)).astype(BF16)
        k_rel = (k * jnp.exp(gtot - gcum)).astype(BF16)
        q_dec = (q * jnp.exp(gcum)).astype(BF16)
        dec = jnp.exp(gtot)
        vb16 = v.astype(BF16)
        outs = []
        for hd in range(heads):
            ks = slice(hd * dk, (hd + 1) * dk)
            vs = slice(hd * dv, (hd + 1) * dv)
            a = jnp.where(masks[direction], _dot_nt(q_rel[:, ks], k_rel[:, ks]), 0.0)
            state = st[direction, hd]
            o = jnp.dot(a.astype(BF16), vb16[:, vs], preferred_element_type=F32)
            o = o + _dot_nt(q_dec[:, ks], state.astype(BF16))
            st[direction, hd] = dec[:, ks] * state + _dot_tn(vb16[:, vs], k_rel[:, ks])
            outs.append(o)
        o_ref[rows, :] = jnp.concatenate(outs, axis=1)

    for ch in range(nch):
        one(0, qf, kf, vf, of, cosf if rope else None, sinf if rope else None, ch)
        one(1, qb, kb, vb, ob, cosb if rope else None, sinb if rope else None, nch - 1 - ch)

    @pl.when(s == pl.num_programs(1) - 1)
    def _():
        sfin[...] = st[...]


def bidir_scan(p, n_batch, seq, cols, heads, dk, dv, kind, s0, *, lg=None, wup=None, bup=None,
               rope=None, q_scale=1.0, k_scale=1.0):
    t = SCAN_ROWS
    nblk = seq // t
    hk, hv = heads * dk, heads * dv
    qc, kc, vc = cols
    fwd = lambda w, off: pl.BlockSpec((t, w), lambda b, s: (b * nblk + s, off // w))
    bwd = lambda w, off: pl.BlockSpec((t, w), lambda b, s: (b * nblk + nblk - 1 - s, off // w))
    const = lambda shape: pl.BlockSpec(shape, lambda b, s: (0,) * len(shape))
    args = [p] * 6
    specs = [fwd(hk, qc), fwd(hk, kc), fwd(hv, vc), bwd(hk, qc), bwd(hk, kc), bwd(hv, vc)]
    if kind == "ret":
        args += [lg]
        specs += [const((2, hk))]
    else:
        args += [p, p, wup, bup]
        specs += [fwd(LANES, COL_GLA_D), bwd(LANES, COL_GLA_D), const(wup.shape), const(bup.shape)]
    if rope is not None:
        cos, sin = rope
        args += [cos, sin, cos, sin]
        specs += [pl.BlockSpec((t, dk), lambda b, s: (s, 0)), pl.BlockSpec((t, dk), lambda b, s: (s, 0)),
                  pl.BlockSpec((t, dk), lambda b, s: (nblk - 1 - s, 0)),
                  pl.BlockSpec((t, dk), lambda b, s: (nblk - 1 - s, 0))]
    args += [s0]
    state_spec = pl.BlockSpec((None, 2, heads, dv, dk), lambda b, s: (b, 0, 0, 0, 0))
    specs += [state_spec]
    n = n_batch * seq
    kern = functools.partial(_scan_kernel, heads=heads, dk=dk, dv=dv, kind=kind, rope=rope is not None,
                             q_scale=q_scale, k_scale=k_scale)
    return pl.pallas_call(
        kern,
        grid=(n_batch, nblk),
        in_specs=specs,
        out_specs=[pl.BlockSpec((t, hv), lambda b, s: (b * nblk + s, 0)),
                   pl.BlockSpec((t, hv), lambda b, s: (b * nblk + nblk - 1 - s, 0)),
                   state_spec],
        out_shape=[jax.ShapeDtypeStruct((n, hv), F32), jax.ShapeDtypeStruct((n, hv), F32),
                   jax.ShapeDtypeStruct((n_batch, 2, heads, dv, dk), F32)],
        scratch_shapes=[pltpu.VMEM((2, heads, dv, dk), F32)],
        compiler_params=_params(2),
    )(*args)


def _outproj_kernel(na_ref, rf_ref, rb_ref, rg_ref, gf_ref, gb_ref, gg_ref, sw_ref, gn_ref, w_ref, x_ref, mg_ref,
                    o_ref, *, head_w):
    ry = rf_ref[...] + rb_ref[...]
    gy = gf_ref[...] + gb_ref[...]
    r_out, g_out = [], []
    for hd in range(ry.shape[1] // head_w):
        sl = slice(hd * head_w, (hd + 1) * head_w)
        r = ry[:, sl]
        mu = jnp.mean(r, axis=-1, keepdims=True)
        var = jnp.mean(jnp.square(r - mu), axis=-1, keepdims=True)
        r_out.append((r - mu) * lax.rsqrt(var + EPS))
        gq = gy[:, sl]
        g_out.append(gq * lax.rsqrt(jnp.mean(gq * gq, axis=-1, keepdims=True) + EPS) * gn_ref[...])
    ret = jnp.concatenate(r_out, axis=1) * _silu(rg_ref[...])
    gla = jnp.concatenate(g_out, axis=1) * _silu(gg_ref[...])
    mix = jnp.concatenate([na_ref[...], ret, gla, sw_ref[...]], axis=1).astype(BF16)
    o_ref[...] = x_ref[...] + mg_ref[...] * jnp.dot(mix, w_ref[...], preferred_element_type=F32)


def outproj(na, rf, rb, gf, gb, sw, p, gla_norm_g, w_out, x2, mg, rows_per_mod):
    n, d = x2.shape
    gw = GROUP_WIDTH
    tm = 256
    m = mg.shape[0]
    row = lambda w_: pl.BlockSpec((tm, w_), lambda i: (i, 0))
    return pl.pallas_call(
        functools.partial(_outproj_kernel, head_w=RET_DV),
        grid=(n // tm,),
        in_specs=[row(gw), row(gw), row(gw), pl.BlockSpec((tm, gw), lambda i: (i, COL_RET_G // gw)),
                  row(gw), row(gw), pl.BlockSpec((tm, gw), lambda i: (i, COL_GLA_G // gw)), row(gw),
                  pl.BlockSpec((1, GLA_DV), lambda i: (0, 0)),
                  pl.BlockSpec((d, d), lambda i: (0, 0)), row(d),
                  pl.BlockSpec((None, 1, d), lambda i: (i // (rows_per_mod // tm), 0, 0))],
        out_specs=row(d),
        out_shape=jax.ShapeDtypeStruct((n, d), F32),
        compiler_params=_params(1),
    )(na, rf, rb, p, gf, gb, p, sw, gla_norm_g.reshape(1, GLA_DV), w_out, x2, mg.reshape(m, 1, d))


PEER_PAIRS = PEER_HEADS * PEER_TOPK
PEER_ROUTE_TOKENS = 128
PEER_EXPERT_TOKENS = 8
PEER_WAIT_GROUP = 4
PEER_SUB = 8
PEER_FOLD = D_MODEL // PEER_SUB
INV_SQRT2 = 0.7071067811865476


def _topk_cols(s, payload=None, order=None):
    row = lax.broadcasted_iota(jnp.int32, s.shape, 0) if order is None else order
    vals, idxs = [], []
    for _ in range(PEER_TOPK):
        m = jnp.max(s, axis=0, keepdims=True)
        am = jnp.min(jnp.where(s == m, row, jnp.iinfo(jnp.int32).max), axis=0, keepdims=True)
        sel = row == am
        vals.append(m)
        idxs.append(am if payload is None else jnp.max(jnp.where(sel, payload, -1), axis=0, keepdims=True))
        s = jnp.where(sel, -jnp.inf, s)
    return jnp.concatenate(vals, axis=0), jnp.concatenate(idxs, axis=0)


def _staircase_candidates(v0, i0, v1, i1):
    k = PEER_TOPK
    assert k == 16
    t = v0.shape[1]
    r8 = lax.broadcasted_iota(jnp.int32, (8, t), 0)
    r16 = lax.broadcasted_iota(jnp.int32, (k, t), 0)

    def piece(a_sl, b_sl):
        return v0[a_sl, :] + v1[b_sl, :], i0[a_sl, :] * PEER_N_KEYS + i1[b_sl, :]

    one = lambda j: slice(j, j + 1)
    lo = slice(0, 8)
    pieces = [
        (one(0), slice(0, k), None, r16),
        (one(1), lo, None, k + r8),
        (one(2), lo, r8 <= 4, 2 * k + r8),
        (one(3), lo, r8 <= 3, 3 * k + r8),
        (slice(8, k), one(0), None, (r8 + 8) * k),
        (lo, one(0), r8 >= 4, r8 * k),
        (lo, one(1), r8 >= 4, r8 * k + 1),
        (lo, one(2), r8 == 4, r8 * k + 2),
    ]
    sums, ids, orders = [], [], []
    for a_sl, b_sl, keep, order in pieces:
        s, e = piece(a_sl, b_sl)
        sums.append(s if keep is None else jnp.where(keep, s, NEG_INF))
        ids.append(e)
        orders.append(order)
    return jnp.concatenate(sums, axis=0), jnp.concatenate(ids, axis=0), jnp.concatenate(orders, axis=0)


def _route_head(hb, wq_ref, sk_ref, hd):
    half = PEER_DK // 2
    q = jnp.dot(hb, wq_ref[hd], preferred_element_type=F32)
    tops = []
    for p in range(2):
        qp = q[:, p * half:(p + 1) * half].astype(BF16)
        tops.append(_topk_cols(_dot_nt(sk_ref[p, hd], qp)))
    (v0, i0), (v1, i1) = tops
    best_s, best_e = _topk_cols(*_staircase_candidates(v0, i0, v1, i1))
    e = jnp.exp(best_s - best_s[0:1, :])
    return e / jnp.sum(e, axis=0, keepdims=True), best_e


def _peer_route_kernel(x_ref, g_ref, shift_ref, scale_ref, wq_ref, sk_ref, h_ref, idx_ref, gate_ref):
    x = x_ref[...]
    y = x * lax.rsqrt(jnp.mean(x * x, axis=-1, keepdims=True) + EPS)
    h = (y * g_ref[...]) * (1.0 + scale_ref[...]) + shift_ref[...]
    h_ref[...] = h
    hb = h.astype(BF16)

    for hd in range(PEER_HEADS):
        gate_ref[hd], idx_ref[hd] = _route_head(hb, wq_ref, sk_ref, hd)


def peer_route(x2, g, shift, scale, rows_per_mod, wq_h, sk):
    n, d = x2.shape
    t = PEER_ROUTE_TOKENS
    m = shift.shape[0]
    mod_map = lambda i: (i // (rows_per_mod // t), 0, 0)
    return pl.pallas_call(
        _peer_route_kernel,
        grid=(n // t,),
        in_specs=[
            pl.BlockSpec((t, d), lambda i: (i, 0)),
            pl.BlockSpec((1, d), lambda i: (0, 0)),
            pl.BlockSpec((None, 1, d), mod_map),
            pl.BlockSpec((None, 1, d), mod_map),
            pl.BlockSpec((PEER_HEADS, d, PEER_DK), lambda i: (0, 0, 0)),
            pl.BlockSpec((2, PEER_HEADS, PEER_N_KEYS, PEER_DK // 2), lambda i: (0, 0, 0, 0)),
        ],
        out_specs=[
            pl.BlockSpec((t, d), lambda i: (i, 0)),
            pl.BlockSpec((PEER_HEADS, PEER_TOPK, t), lambda i: (0, 0, i)),
            pl.BlockSpec((PEER_HEADS, PEER_TOPK, t), lambda i: (0, 0, i)),
        ],
        out_shape=[
            jax.ShapeDtypeStruct((n, d), F32),
            jax.ShapeDtypeStruct((PEER_HEADS, PEER_TOPK, n), jnp.int32),
            jax.ShapeDtypeStruct((PEER_HEADS, PEER_TOPK, n), F32),
        ],
        compiler_params=_params(1),
    )(x2, g.reshape(1, d), shift.reshape(m, 1, d), scale.reshape(m, 1, d), wq_h, sk)


def pack_experts(u, v):
    ub = lax.bitcast_convert_type(u.astype(BF16), jnp.uint16).astype(jnp.uint32)
    vb = lax.bitcast_convert_type(v.astype(BF16), jnp.uint16).astype(jnp.uint32)
    words = lax.bitcast_convert_type((ub << 16) | vb, jnp.int32)
    return words.reshape(u.shape[0], PEER_SUB, PEER_FOLD)


def _fold_rows(hbuf, h_rows, tb):
    for s in range(PEER_SUB):
        for c in range(PEER_FOLD // LANES):
            lo = s * PEER_FOLD + c * LANES
            hbuf[c, pl.ds(s * tb, tb), :] = h_rows[:, lo:lo + LANES]


def _unfold_residual(o_ref, x_ref, og_ref, ybuf, tb):
    for s in range(PEER_SUB):
        for c in range(PEER_FOLD // LANES):
            sl = slice(s * PEER_FOLD + c * LANES, s * PEER_FOLD + (c + 1) * LANES)
            o_ref[:, sl] = x_ref[:, sl] + og_ref[:, sl] * ybuf[c, pl.ds(s, tb, stride=PEER_SUB), :]


def _expert_weights(j, buf, hbuf, pbuf, wrow, tb, gate_column):
    fold = PEER_FOLD
    hj = jnp.concatenate([hbuf[c, pl.ds(j, PEER_SUB, stride=tb), :] for c in range(fold // LANES)], axis=1)
    for p in range(PEER_PAIRS):
        u = lax.bitcast_convert_type(buf[j, p] & jnp.int32(-65536), F32)
        prod = u * hj
        pbuf[pl.ds(p * PEER_SUB, PEER_SUB), :] = prod[:, :LANES] + prod[:, LANES:]
    part = pbuf[pl.ds(0, PEER_PAIRS, stride=PEER_SUB), :]
    for s in range(1, PEER_SUB):
        part = part + pbuf[pl.ds(s, PEER_PAIRS, stride=PEER_SUB), :]
    sc = jnp.sum(part, axis=1, keepdims=True)
    act = 0.5 * sc * (1.0 + lax.erf(sc * INV_SQRT2))
    g = jnp.concatenate([gate_column(hd) for hd in range(PEER_HEADS)], axis=0)
    wrow[...] = jnp.broadcast_to(g * act, (PEER_PAIRS, LANES))


def _expert_mix(j, buf, wrow, ybuf, start_fetch):
    fold = PEER_FOLD
    accs = [jnp.zeros((PEER_SUB, fold), F32) for _ in range(4)]
    for p in range(PEER_PAIRS):
        if start_fetch is not None:
            start_fetch(p)
        v = lax.bitcast_convert_type(buf[j, p] << 16, F32)
        wp = jnp.broadcast_to(wrow[p:p + 1, :], (PEER_SUB, LANES))
        accs[p % 4] = accs[p % 4] + v * jnp.concatenate([wp, wp], axis=1)
    yj = (accs[0] + accs[1]) + (accs[2] + accs[3])
    for c in range(fold // LANES):
        ybuf[c, pl.ds(j * PEER_SUB, PEER_SUB), :] = yj[:, c * LANES:(c + 1) * LANES]


def _peer_expert_kernel(idx0_ref, idxn_ref, h_ref, gate_ref, x_ref, og_ref, uv_hbm, o_ref,
                        buf_even, buf_odd, sem, hbuf, pbuf, wbuf, ybuf):
    i = pl.program_id(0)
    n = pl.num_programs(0)
    tb = PEER_EXPERT_TOKENS
    bufs = (buf_even, buf_odd)

    def slab_copy(idx_ref, j, r, parity):
        e = idx_ref[0, 0, j * PEER_PAIRS + r]
        return pltpu.make_async_copy(uv_hbm.at[e], bufs[parity].at[j, r], sem.at[parity, j])

    def wait_token(j, parity):
        pltpu.make_async_copy(uv_hbm.at[pl.ds(0, PEER_PAIRS)], bufs[parity].at[j], sem.at[parity, j]).wait()

    @pl.when(i == 0)
    def _():
        def prime(j, carry):
            for r in range(PEER_PAIRS):
                slab_copy(idx0_ref, j, r, 0).start(priority=r % 2)
            return carry
        lax.fori_loop(0, tb, prime, 0)

    lane = lax.broadcasted_iota(jnp.int32, (PEER_TOPK, PEER_ROUTE_TOKENS), 1)
    lane0 = (i % (PEER_ROUTE_TOKENS // tb)) * tb

    _fold_rows(hbuf, h_ref[...], tb)

    def fetcher(j, parity, prefetch):
        return (lambda r: slab_copy(idxn_ref, j, r, 1 - parity).start(priority=r % 2)) if prefetch else None

    def gate_column_of(j):
        return lambda hd: jnp.sum(jnp.where(lane == lane0 + j, gate_ref[hd], 0.0), axis=1, keepdims=True)

    def block(parity, prefetch):
        for j in range(tb):
            if j % PEER_WAIT_GROUP == 0:
                for jj in range(j, j + PEER_WAIT_GROUP):
                    wait_token(jj, parity)
            _expert_weights(j, bufs[parity], hbuf, pbuf, wbuf, tb, gate_column_of(j))
            _expert_mix(j, bufs[parity], wbuf, ybuf, fetcher(j, parity, prefetch))

    for parity in range(2):
        for prefetch in (True, False):
            @pl.when((i % 2 == parity) & ((i + 1 < n) == prefetch))
            def _(parity=parity, prefetch=prefetch):
                block(parity, prefetch)

    _unfold_residual(o_ref, x_ref, og_ref, ybuf, tb)


def peer_expert(h, idx, gate, x2, out_gate, rows_per_mod, uv):
    n, d = h.shape
    tb = PEER_EXPERT_TOKENS
    nb = n // tb
    m = out_gate.shape[0]
    rows = tb * PEER_PAIRS
    idx_rows = idx.reshape(PEER_PAIRS, n).T.reshape(nb, 1, rows)
    gate_blocks = PEER_ROUTE_TOKENS // tb
    return pl.pallas_call(
        _peer_expert_kernel,
        grid=(nb,),
        in_specs=[
            pl.BlockSpec((1, 1, rows), lambda i: (0, 0, 0), memory_space=pltpu.SMEM),
            pl.BlockSpec((1, 1, rows), lambda i: (jnp.minimum(i + 1, nb - 1), 0, 0), memory_space=pltpu.SMEM),
            pl.BlockSpec((tb, d), lambda i: (i, 0)),
            pl.BlockSpec((PEER_HEADS, PEER_TOPK, PEER_ROUTE_TOKENS), lambda i: (0, 0, i // gate_blocks)),
            pl.BlockSpec((tb, d), lambda i: (i, 0)),
            pl.BlockSpec((None, 1, d), lambda i: (i // (rows_per_mod // tb), 0, 0)),
            pl.BlockSpec(memory_space=pl.ANY),
        ],
        out_specs=pl.BlockSpec((tb, d), lambda i: (i, 0)),
        out_shape=jax.ShapeDtypeStruct((n, d), F32),
        scratch_shapes=[pltpu.VMEM((tb, PEER_PAIRS, PEER_SUB, PEER_FOLD), jnp.int32),
                        pltpu.VMEM((tb, PEER_PAIRS, PEER_SUB, PEER_FOLD), jnp.int32),
                        pltpu.SemaphoreType.DMA((2, tb)),
                        pltpu.VMEM((PEER_FOLD // LANES, PEER_SUB * tb, LANES), F32),
                        pltpu.VMEM((PEER_PAIRS * PEER_SUB, LANES), F32),
                        pltpu.VMEM((PEER_PAIRS, LANES), F32),
                        pltpu.VMEM((PEER_FOLD // LANES, tb * PEER_SUB, LANES), F32)],
        compiler_params=pltpu.CompilerParams(dimension_semantics=("arbitrary",), vmem_limit_bytes=VMEM_LIMIT,
                                             disable_bounds_checks=True),
    )(idx_rows, idx_rows, h, gate, x2, out_gate.reshape(m, 1, d), uv)


def peer_residual(x2, g, shift, scale, out_gate, rows_per_mod, wq_h, sk, uv):
    h, idx, gate = peer_route(x2, g, shift, scale, rows_per_mod, wq_h, sk)
    return peer_expert(h, idx, gate, x2, out_gate, rows_per_mod, uv)


def _final_norm_kernel(x_ref, g_ref, o_ref):
    x = x_ref[...]
    y = x * lax.rsqrt(jnp.mean(x * x, axis=-1, keepdims=True) + EPS)
    o_ref[...] = y * g_ref[...]


def final_norm(x2, g):
    n, d = x2.shape
    tm = 512
    return pl.pallas_call(
        _final_norm_kernel,
        grid=(n // tm,),
        in_specs=[pl.BlockSpec((tm, d), lambda i: (i, 0)), pl.BlockSpec((1, d), lambda i: (0, 0))],
        out_specs=pl.BlockSpec((tm, d), lambda i: (i, 0)),
        out_shape=jax.ShapeDtypeStruct((n, d), x2.dtype),
        compiler_params=_params(1),
    )(x2, g.reshape(1, d))


def _mixers(p, pz, n_batch, seq, ctx_len, is_ctx, tables, prm):
    if is_ctx:
        na = ctx_attention(p, n_batch, seq, (COL_NA_Q, COL_NA_K, COL_NA_V), GROUP_WIDTH, None)
        sw = ctx_attention(p, n_batch, seq, (COL_SWA_Q, COL_SWA_K, COL_SWA_V), SWA_KV_HEADS * HEAD_DIM, prm["sink"])
    else:
        na = na_attention(p, pz, n_batch, seq, ctx_len, prm["na_bias"])
        sw = swa_attention(p, pz, n_batch, seq, ctx_len, prm["sink"], tables["swa_q"], tables["swa_k"])
    rf, rb, rs = bidir_scan(p, n_batch, seq, (COL_RET_Q, COL_RET_K, COL_RET_V), RET_HEADS, RET_DK, RET_DV, "ret",
                            prm["ret_s0"], lg=prm["ret_lg"], rope=None if is_ctx else tables["ret"],
                            k_scale=RET_DK ** -0.5)
    gf, gb, gs = bidir_scan(p, n_batch, seq, (COL_GLA_Q, COL_GLA_K, COL_GLA_V), GLA_HEADS, GLA_DK, GLA_DV, "gla",
                            prm["gla_s0"], wup=prm["gla_wup"], bup=prm["gla_bup"], q_scale=GLA_DK ** -0.5)
    return (na, rf, rb, gf, gb, sw), (rs, gs)


def kernel(x, c, ctx, c_ctx, w_ada, b_ada, norm_attn_g, norm_ffn_g, w_in, na_rpb, ret_log_gamma,
           gla_w_gate_up, gla_b_gate, gla_norm_g, swa_sink, w_out, peer_w_q, peer_sub_keys,
           peer_u, peer_v, final_g):
    bsz, slen, d = x.shape
    zlen = ctx.shape[1]
    x2 = x.reshape(bsz * slen, d)
    z2 = ctx.reshape(bsz * zlen, d)
    tables = {"ret": rope_lane_tables(slen, RET_DK, 1),
              "swa_q": rope_lane_tables(slen, HEAD_DIM, SWA_HEADS),
              "swa_k": rope_lane_tables(slen, HEAD_DIM, SWA_KV_HEADS)}
    c_rows = jnp.zeros((8, d), F32).at[:bsz].set(c).at[bsz].set(c_ctx)
    for layer in range(DEPTH):
        has_next = layer < DEPTH - 1
        mod = adaln(c_rows, w_ada[layer], b_ada[layer])
        mx = [mod[:bsz, k * d:(k + 1) * d] for k in range(6)]
        mz = [mod[bsz:bsz + 1, k * d:(k + 1) * d] for k in range(6)]

        wi = w_in[layer]
        wp = jnp.concatenate([wi[:, :REF_COL_GLA_D], wi[:, REF_COL_SWA_Q:], wi[:, REF_COL_GLA_D:REF_COL_SWA_Q],
                              jnp.zeros((d, PROJ_WIDTH - REF_D_IN), F32)], axis=1).astype(BF16)
        px = modproj(x2, norm_attn_g[layer], mx[0], mx[1], slen, wp)
        pz = modproj(z2, norm_attn_g[layer], mz[0], mz[1], bsz * zlen, wp)

        wup = (jnp.zeros((2, LANES, GLA_HEADS * GLA_DK), F32)
               .at[0, :GLA_RANK].set(gla_w_gate_up[layer, 0])
               .at[1, GLA_RANK:2 * GLA_RANK].set(gla_w_gate_up[layer, 1])).astype(BF16)
        prm = {"na_bias": na_band_bias(na_rpb[layer]), "sink": swa_sink[layer],
               "ret_lg": jnp.repeat(ret_log_gamma[layer], RET_DK, axis=1),
               "gla_wup": wup, "gla_bup": gla_b_gate[layer].reshape(2, 1, GLA_HEADS * GLA_DK),
               "ret_s0": jnp.zeros((bsz, 2, RET_HEADS, RET_DV, RET_DK), F32),
               "gla_s0": jnp.zeros((bsz, 2, GLA_HEADS, GLA_DV, GLA_DK), F32)}
        mix_z, (ret_s, gla_s) = _mixers(pz, pz, bsz, zlen, zlen, True, tables, prm)
        prm["ret_s0"], prm["gla_s0"] = ret_s, gla_s
        mix_x, _ = _mixers(px, pz, bsz, slen, zlen, False, tables, prm)

        wo = w_out[layer].astype(BF16)
        x2 = outproj(*mix_x, px, gla_norm_g[layer], wo, x2, mx[2], slen)

        uv = pack_experts(peer_u[layer], peer_v[layer])
        wq_h = peer_w_q[layer].reshape(d, PEER_HEADS, PEER_DK).transpose(1, 0, 2).astype(BF16)
        sk = peer_sub_keys[layer].astype(BF16)
        x2 = peer_residual(x2, norm_ffn_g[layer], mx[3], mx[4], mx[5], slen, wq_h, sk, uv)
        if has_next:
            z2 = outproj(*mix_z, pz, gla_norm_g[layer], wo, z2, mz[2], bsz * zlen)
            z2 = peer_residual(z2, norm_ffn_g[layer], mz[3], mz[4], mz[5], bsz * zlen, wq_h, sk, uv)
    return final_norm(x2, final_g).reshape(bsz, slen, d)
```

```python
import functools

import jax
import jax.numpy as jnp
import numpy as np
from jax import lax
from jax.experimental import pallas as pl
from jax.experimental.pallas import tpu as pltpu

D_MODEL = 2048
DEPTH = 2
GRID_W = 64
EPS = 1e-6
ROPE_BASE = 10000.0

GROUP_WIDTH = D_MODEL // 4
HEAD_DIM = 64
NA_HEADS = GROUP_WIDTH // HEAD_DIM
NA_ROWS = 8
NA_COLS = 16
RET_HEADS = 4
RET_DK = GROUP_WIDTH // RET_HEADS
RET_DV = GROUP_WIDTH // RET_HEADS
GLA_HEADS = 4
GLA_DV = GROUP_WIDTH // GLA_HEADS
GLA_DK = GLA_DV // 2
GLA_RANK = 16
GLA_TAU = 16.0
SWA_HEADS = GROUP_WIDTH // HEAD_DIM
SWA_KV_HEADS = SWA_HEADS // 4
SWA_WINDOW = 128
SWA_BLOCK = 128
SCAN_CHUNK = 64
PEER_HEADS = 8
PEER_N_KEYS = 128
PEER_N_EXPERTS = PEER_N_KEYS * PEER_N_KEYS
PEER_DK = 256
PEER_TOPK = 16

LANES = 128
VMEM_LIMIT = 48 * 1024 * 1024
BF16 = jnp.bfloat16
F32 = jnp.float32
NEG_INF = float("-inf")

COL_NA_Q, COL_NA_K, COL_NA_V = 0, 512, 1024
COL_RET_Q, COL_RET_K, COL_RET_V, COL_RET_G = 1536, 2048, 2560, 3072
COL_GLA_Q, COL_GLA_K, COL_GLA_V, COL_GLA_G = 3584, 3840, 4096, 4608
COL_SWA_Q, COL_SWA_K, COL_SWA_V = 5120, 5632, 5760
COL_GLA_D = 5888
REF_COL_GLA_D, REF_COL_SWA_Q, REF_D_IN = 5120, 5152, 5920
PROJ_WIDTH = 6144


def _silu(x):
    return x / (1.0 + jnp.exp(-x))


def _dot_nt(a, b):
    return lax.dot_general(a, b, (((1,), (1,)), ((), ())), preferred_element_type=F32)


def _dot_tn(a, b):
    return lax.dot_general(a, b, (((0,), (0,)), ((), ())), preferred_element_type=F32)


def _params(n_axes):
    return pltpu.CompilerParams(dimension_semantics=("arbitrary",) * n_axes, vmem_limit_bytes=VMEM_LIMIT)


def _rope_lanes(x, cs, sn, quarter):
    n = x.shape[-1]
    lane = lax.broadcasted_iota(jnp.int32, x.shape, x.ndim - 1)
    first = (lane % (2 * quarter)) < quarter
    swapped = jnp.where(first, pltpu.roll(x, n - quarter, x.ndim - 1), pltpu.roll(x, quarter, x.ndim - 1))
    return x * cs + swapped * sn


def rope_lane_tables(length, dh, copies):
    t = jnp.arange(length)
    pos = jnp.stack([t // GRID_W, t % GRID_W], axis=-1).astype(F32)
    quarter = dh // 4
    inv = ROPE_BASE ** (-jnp.arange(quarter, dtype=F32) / quarter)
    ang = pos[:, :, None] * inv
    cos, sin = jnp.cos(ang), jnp.sin(ang)
    cl = jnp.concatenate([cos[:, 0], cos[:, 0], cos[:, 1], cos[:, 1]], axis=-1)
    sl = jnp.concatenate([-sin[:, 0], sin[:, 0], -sin[:, 1], sin[:, 1]], axis=-1)
    return jnp.tile(cl, (1, copies)), jnp.tile(sl, (1, copies))


def _adaln_kernel(c_ref, w_ref, b_ref, o_ref):
    a = _silu(c_ref[...]).astype(BF16)
    o_ref[...] = jnp.dot(a, w_ref[...].astype(BF16), preferred_element_type=F32) + b_ref[...]


def adaln(c_rows, w, b):
    r, d = c_rows.shape
    m = w.shape[1]
    tn = 1024
    return pl.pallas_call(
        _adaln_kernel,
        grid=(m // tn,),
        in_specs=[pl.BlockSpec((r, d), lambda j: (0, 0)), pl.BlockSpec((d, tn), lambda j: (0, j)),
                  pl.BlockSpec((1, tn), lambda j: (0, j))],
        out_specs=pl.BlockSpec((r, tn), lambda j: (0, j)),
        out_shape=jax.ShapeDtypeStruct((r, m), F32),
        compiler_params=_params(1),
    )(c_rows, w, b.reshape(1, m))


def _modproj_kernel(x_ref, g_ref, shift_ref, scale_ref, w_ref, o_ref, hb_ref):
    @pl.when(pl.program_id(1) == 0)
    def _():
        x = x_ref[...]
        y = x * lax.rsqrt(jnp.mean(x * x, axis=-1, keepdims=True) + EPS)
        hb_ref[...] = ((y * g_ref[...]) * (1.0 + scale_ref[...]) + shift_ref[...]).astype(BF16)

    o_ref[...] = jnp.dot(hb_ref[...], w_ref[...], preferred_element_type=F32)


def modproj(x2, g, shift, scale, rows_per_mod, w):
    n, d = x2.shape
    wid = w.shape[1]
    tm = min(512, rows_per_mod)
    tn = 2048
    m = shift.shape[0]
    mod_map = lambda i, j: (i // (rows_per_mod // tm), 0, 0)
    return pl.pallas_call(
        _modproj_kernel,
        grid=(n // tm, wid // tn),
        in_specs=[pl.BlockSpec((tm, d), lambda i, j: (i, 0)), pl.BlockSpec((1, d), lambda i, j: (0, 0)),
                  pl.BlockSpec((None, 1, d), mod_map), pl.BlockSpec((None, 1, d), mod_map),
                  pl.BlockSpec((d, tn), lambda i, j: (0, j))],
        out_specs=pl.BlockSpec((tm, tn), lambda i, j: (i, j)),
        out_shape=jax.ShapeDtypeStruct((n, wid), F32),
        scratch_shapes=[pltpu.VMEM((tm, d), BF16)],
        compiler_params=_params(2),
    )(x2, g.reshape(1, d), shift.reshape(m, 1, d), scale.reshape(m, 1, d), w)


NA_QROWS = 4


def _na_kernel(q_ref, k_ref, v_ref, kz_ref, vz_ref, bias_ref, o_ref, *, rows):
    step = pl.program_id(2)
    dh = HEAD_DIM
    band = NA_ROWS * GRID_W
    kz = kz_ref[...].astype(BF16)
    vz = vz_ref[...].astype(BF16)
    heads = LANES // dh
    units = [(qr, hh) for qr in range(NA_QROWS) for hh in range(heads)]
    scores, vbands = {}, {}
    for qr in range(NA_QROWS):
        r = step * NA_QROWS + qr
        start = jnp.clip(r - NA_ROWS // 2, 0, rows - NA_ROWS)
        dr0 = start - r + NA_ROWS - 1
        tok0 = pl.multiple_of(start * GRID_W, GRID_W)
        kb = k_ref[pl.ds(tok0, band), :].astype(BF16)
        vbands[qr] = v_ref[pl.ds(tok0, band), :].astype(BF16)
        q = (q_ref[pl.ds(qr * GRID_W, GRID_W), :] * (dh ** -0.5)).astype(BF16)
        for hh in range(heads):
            sl = slice(hh * dh, (hh + 1) * dh)
            scores[qr, hh] = (_dot_nt(q[:, sl], kb[:, sl]) + bias_ref[hh, dr0],
                              _dot_nt(q[:, sl], kz[:, sl]))
    probs = {}
    for u in units:
        s_nb, s_cx = scores[u]
        m = jnp.maximum(jnp.max(s_nb, axis=1, keepdims=True), jnp.max(s_cx, axis=1, keepdims=True))
        p_nb = jnp.exp(s_nb - m)
        p_cx = jnp.exp(s_cx - m)
        den = jnp.sum(p_nb, axis=1, keepdims=True) + jnp.sum(p_cx, axis=1, keepdims=True)
        probs[u] = (p_nb.astype(BF16), p_cx.astype(BF16), den)
    for qr in range(NA_QROWS):
        outs = []
        for hh in range(heads):
            sl = slice(hh * dh, (hh + 1) * dh)
            p_nb, p_cx, den = probs[qr, hh]
            o = (jnp.dot(p_nb, vbands[qr][:, sl], preferred_element_type=F32)
                 + jnp.dot(p_cx, vz[:, sl], preferred_element_type=F32))
            outs.append(o / den)
        o_ref[pl.ds(qr * GRID_W, GRID_W), :] = jnp.concatenate(outs, axis=1)


def na_band_bias(rpb):
    col = jnp.arange(GRID_W)
    col_start = jnp.clip(col - NA_COLS // 2, 0, GRID_W - NA_COLS)
    col_ok = (col[None, :] >= col_start[:, None]) & (col[None, :] < col_start[:, None] + NA_COLS)
    d_col = jnp.clip(col[None, :] - col[:, None], -(NA_COLS - 1), NA_COLS - 1) + NA_COLS - 1
    d_row = jnp.arange(NA_ROWS)[:, None] + jnp.arange(NA_ROWS)[None, :]
    b = rpb.astype(F32)[:, d_row][..., d_col]
    b = jnp.where(col_ok[None, None, None], b, NEG_INF)
    return b.transpose(0, 1, 3, 2, 4).reshape(rpb.shape[0], NA_ROWS, GRID_W, NA_ROWS * GRID_W)


def na_attention(px, pz, n_batch, seq, ctx_len, bias):
    rows = seq // GRID_W
    tq = NA_QROWS * GRID_W
    nsteps = rows // NA_QROWS
    heads_per_blk = LANES // HEAD_DIM
    return pl.pallas_call(
        functools.partial(_na_kernel, rows=rows),
        grid=(n_batch, NA_HEADS // heads_per_blk, nsteps),
        in_specs=[
            pl.BlockSpec((tq, LANES), lambda b, hp, s: (b * nsteps + s, COL_NA_Q // LANES + hp)),
            pl.BlockSpec((seq, LANES), lambda b, hp, s: (b, COL_NA_K // LANES + hp)),
            pl.BlockSpec((seq, LANES), lambda b, hp, s: (b, COL_NA_V // LANES + hp)),
            pl.BlockSpec((ctx_len, LANES), lambda b, hp, s: (b, COL_NA_K // LANES + hp)),
            pl.BlockSpec((ctx_len, LANES), lambda b, hp, s: (b, COL_NA_V // LANES + hp)),
            pl.BlockSpec((heads_per_blk, NA_ROWS, GRID_W, NA_ROWS * GRID_W), lambda b, hp, s: (hp, 0, 0, 0)),
        ],
        out_specs=pl.BlockSpec((tq, LANES), lambda b, hp, s: (b * nsteps + s, hp)),
        out_shape=jax.ShapeDtypeStruct((n_batch * seq, GROUP_WIDTH), F32),
        compiler_params=_params(3),
    )(px, px, px, pz, pz, bias)


def _swa_kernel(q_ref, kp_ref, kc_ref, kn_ref, vp_ref, vc_ref, vn_ref, kz_ref, vz_ref, sink_ref,
                cq_ref, sq_ref, ckp_ref, skp_ref, ckc_ref, skc_ref, ckn_ref, skn_ref, o_ref):
    n = pl.program_id(1)
    nb = pl.num_programs(1)
    dh = HEAD_DIM
    blk = SWA_BLOCK
    quarter = dh // 4
    group = SWA_HEADS // SWA_KV_HEADS
    q = _rope_lanes(q_ref[...], cq_ref[...], sq_ref[...], quarter) * (dh ** -0.5)
    kp = _rope_lanes(kp_ref[...], ckp_ref[...], skp_ref[...], quarter).astype(BF16)
    kc = _rope_lanes(kc_ref[...], ckc_ref[...], skc_ref[...], quarter).astype(BF16)
    kn = _rope_lanes(kn_ref[...], ckn_ref[...], skn_ref[...], quarter).astype(BF16)
    kz = kz_ref[...].astype(BF16)
    vp, vc, vn, vz = (r[...].astype(BF16) for r in (vp_ref, vc_ref, vn_ref, vz_ref))
    qi = lax.broadcasted_iota(jnp.int32, (group * blk, blk), 0) % blk
    kj = lax.broadcasted_iota(jnp.int32, (group * blk, blk), 1)
    ok_p = (kj >= qi) & (n > 0)
    ok_n = (kj <= qi) & (n < nb - 1)
    outs = []
    for hk in range(SWA_KV_HEADS):
        ks = slice(hk * dh, (hk + 1) * dh)
        qs = jnp.concatenate([q[:, (hk * group + g) * dh:(hk * group + g + 1) * dh] for g in range(group)],
                             axis=0).astype(BF16)
        sink = jnp.concatenate([jnp.full((blk, 1), 1.0, F32) * sink_ref[hk * group + g] for g in range(group)],
                               axis=0)
        s_p = jnp.where(ok_p, _dot_nt(qs, kp[:, ks]), NEG_INF)
        s_c = _dot_nt(qs, kc[:, ks])
        s_n = jnp.where(ok_n, _dot_nt(qs, kn[:, ks]), NEG_INF)
        s_z = _dot_nt(qs, kz[:, ks])
        m = jnp.maximum(jnp.maximum(jnp.max(s_p, axis=1, keepdims=True), jnp.max(s_c, axis=1, keepdims=True)),
                        jnp.maximum(jnp.max(s_n, axis=1, keepdims=True), jnp.max(s_z, axis=1, keepdims=True)))
        m = jnp.maximum(m, sink)
        e_p, e_c, e_n, e_z = (jnp.exp(s - m) for s in (s_p, s_c, s_n, s_z))
        den = (jnp.sum(e_p, axis=1, keepdims=True) + jnp.sum(e_c, axis=1, keepdims=True)
               + jnp.sum(e_n, axis=1, keepdims=True) + jnp.sum(e_z, axis=1, keepdims=True) + jnp.exp(sink - m))
        o = (jnp.dot(e_p.astype(BF16), vp[:, ks], preferred_element_type=F32)
             + jnp.dot(e_c.astype(BF16), vc[:, ks], preferred_element_type=F32)
             + jnp.dot(e_n.astype(BF16), vn[:, ks], preferred_element_type=F32)
             + jnp.dot(e_z.astype(BF16), vz[:, ks], preferred_element_type=F32)) / den
        outs += [o[g * blk:(g + 1) * blk, :] for g in range(group)]
    o_ref[...] = jnp.concatenate(outs, axis=1)


def swa_attention(px, pz, n_batch, seq, ctx_len, sink, rope_q, rope_k):
    blk = SWA_BLOCK
    nb = seq // blk
    prev = lambda b, n: b * nb + jnp.maximum(n - 1, 0)
    cur = lambda b, n: b * nb + n
    nxt = lambda b, n: b * nb + jnp.minimum(n + 1, nb - 1)
    kv = lambda off, f: pl.BlockSpec((blk, LANES), lambda b, n: (f(b, n), off // LANES))
    tab = lambda w, f: pl.BlockSpec((blk, w), lambda b, n: (f(0, n), 0))
    cq, sq = rope_q
    ck, sk = rope_k
    return pl.pallas_call(
        _swa_kernel,
        grid=(n_batch, nb),
        in_specs=[
            pl.BlockSpec((blk, GROUP_WIDTH), lambda b, n: (cur(b, n), COL_SWA_Q // GROUP_WIDTH)),
            kv(COL_SWA_K, prev), kv(COL_SWA_K, cur), kv(COL_SWA_K, nxt),
            kv(COL_SWA_V, prev), kv(COL_SWA_V, cur), kv(COL_SWA_V, nxt),
            pl.BlockSpec((ctx_len, LANES), lambda b, n: (b, COL_SWA_K // LANES)),
            pl.BlockSpec((ctx_len, LANES), lambda b, n: (b, COL_SWA_V // LANES)),
            pl.BlockSpec(memory_space=pltpu.SMEM),
            tab(GROUP_WIDTH, cur), tab(GROUP_WIDTH, cur), tab(LANES, prev), tab(LANES, prev),
            tab(LANES, cur), tab(LANES, cur), tab(LANES, nxt), tab(LANES, nxt),
        ],
        out_specs=pl.BlockSpec((blk, GROUP_WIDTH), lambda b, n: (cur(b, n), 0)),
        out_shape=jax.ShapeDtypeStruct((n_batch * seq, GROUP_WIDTH), F32),
        compiler_params=_params(2),
    )(px, px, px, px, px, px, px, pz, pz, sink, cq, sq, ck, sk, ck, sk, ck, sk)


def _ctx_attn_kernel(q_ref, k_ref, v_ref, sink_ref, o_ref, *, group, use_sink):
    dh = HEAD_DIM
    q = (q_ref[...] * (dh ** -0.5)).astype(BF16)
    k = k_ref[...].astype(BF16)
    v = v_ref[...].astype(BF16)
    outs = []
    for qh in range(q.shape[1] // dh):
        ks = slice((qh // group) * dh, (qh // group + 1) * dh)
        s = _dot_nt(q[:, qh * dh:(qh + 1) * dh], k[:, ks])
        m = jnp.max(s, axis=1, keepdims=True)
        if use_sink:
            m = jnp.maximum(m, sink_ref[qh])
        e = jnp.exp(s - m)
        den = jnp.sum(e, axis=1, keepdims=True)
        if use_sink:
            den = den + jnp.exp(sink_ref[qh] - m)
        outs.append(jnp.dot(e.astype(BF16), v[:, ks], preferred_element_type=F32) / den)
    o_ref[...] = jnp.concatenate(outs, axis=1)


def ctx_attention(pz, n_batch, ctx_len, cols, kv_width, sink):
    qc, kc, vc = cols
    use_sink = sink is not None
    if sink is None:
        sink = jnp.zeros((GROUP_WIDTH // HEAD_DIM,), F32)
    return pl.pallas_call(
        functools.partial(_ctx_attn_kernel, group=GROUP_WIDTH // kv_width, use_sink=use_sink),
        grid=(n_batch,),
        in_specs=[pl.BlockSpec((ctx_len, GROUP_WIDTH), lambda b: (b, qc // GROUP_WIDTH)),
                  pl.BlockSpec((ctx_len, kv_width), lambda b: (b, kc // kv_width)),
                  pl.BlockSpec((ctx_len, kv_width), lambda b: (b, vc // kv_width)),
                  pl.BlockSpec(memory_space=pltpu.SMEM)],
        out_specs=pl.BlockSpec((ctx_len, GROUP_WIDTH), lambda b: (b, 0)),
        out_shape=jax.ShapeDtypeStruct((n_batch * ctx_len, GROUP_WIDTH), F32),
        compiler_params=_params(1),
    )(pz, pz, pz, sink)


SCAN_ROWS = 256


def _split3(x):
    a = x.astype(BF16)
    r = x - a.astype(F32)
    b = r.astype(BF16)
    c = (r - b.astype(F32)).astype(BF16)
    return a, b, c


def _scan_kernel(*refs, heads, dk, dv, kind, rope, q_scale, k_scale):
    it = iter(refs)
    qf, kf, vf, qb, kb, vb = (next(it) for _ in range(6))
    if kind == "ret":
        lg = next(it)
    else:
        df, db, wup, bup = (next(it) for _ in range(4))
    if rope:
        cosf, sinf, cosb, sinb = (next(it) for _ in range(4))
    s0 = next(it)
    of, ob, sfin = next(it), next(it), next(it)
    st = next(it)

    s = pl.program_id(1)
    c = SCAN_CHUNK
    nch = SCAN_ROWS // c
    hk = heads * dk

    @pl.when(s == 0)
    def _():
        st[...] = s0[...]

    r_i = lax.broadcasted_iota(jnp.int32, (c, c), 0)
    c_i = lax.broadcasted_iota(jnp.int32, (c, c), 1)
    masks = (r_i >= c_i, c_i > r_i)

    if kind == "ret":
        pos = lax.broadcasted_iota(jnp.int32, (c, hk), 0).astype(F32)
        gcums = ((pos + 1.0) * lg[0:1, :], (float(c) - pos) * lg[1:2, :])
    else:
        rr = lax.broadcasted_iota(jnp.int32, (SCAN_ROWS, SCAN_ROWS), 0)
        cc = lax.broadcasted_iota(jnp.int32, (SCAN_ROWS, SCAN_ROWS), 1)
        same = (rr // c) == (cc // c)
        tris = (jnp.where(same & (rr >= cc), 1.0, 0.0).astype(BF16),
                jnp.where(same & (cc >= rr), 1.0, 0.0).astype(BF16))

        def gate_cum(d_ref, direction):
            pre = jnp.dot(d_ref[...].astype(BF16), wup[direction], preferred_element_type=F32) + bup[direction]
            g = -(jnp.maximum(-pre, 0.0) + jnp.log1p(jnp.exp(-jnp.abs(pre)))) / GLA_TAU
            return sum(jnp.dot(tris[direction], p, preferred_element_type=F32) for p in _split3(g))

        gcums = (gate_cum(df, 0), gate_cum(db, 1))

    def one(direction, q_ref, k_ref, v_ref, o_ref, cos_ref, sin_ref, ch):
        rows = pl.ds(ch * c, c)
        q = q_ref[rows, :]
        k = k_ref[rows, :]
        v = v_ref[rows, :]
        if rope:
            cs = jnp.concatenate([cos_ref[rows, :]] * heads, axis=1)
            sn = jnp.concatenate([sin_ref[rows, :]] * heads, axis=1)
            q = _rope_lanes(q, cs, sn, dk // 4)
            k = _rope_lanes(k, cs, sn, dk // 4)
        if q_scale != 1.0:
            q = q * q_scale
        if k_scale != 1.0:
            k = k * k_scale
        gcum = gcums[direction] if kind == "ret" else gcums[direction][ch * c:(ch + 1) * c, :]
        gtot = gcum[c - 1:c, :] if direction == 0 else gcum[0:1, :]
        q_rel = (q * jnp.exp(gcum - gtot)).astype(BF16)
        k_rel = (k * jnp.exp(gtot - gcum)).astype(BF16)
        q_dec = (q * jnp.exp(gcum)).astype(BF16)
        dec = jnp.exp(gtot)
        vb16 = v.astype(BF16)
        outs = []
        for hd in range(heads):
            ks = slice(hd * dk, (hd + 1) * dk)
            vs = slice(hd * dv, (hd + 1) * dv)
            a = jnp.where(masks[direction], _dot_nt(q_rel[:, ks], k_rel[:, ks]), 0.0)
            state = st[direction, hd]
            o = jnp.dot(a.astype(BF16), vb16[:, vs], preferred_element_type=F32)
            o = o + _dot_nt(q_dec[:, ks], state.astype(BF16))
            st[direction, hd] = dec[:, ks] * state + _dot_tn(vb16[:, vs], k_rel[:, ks])
            outs.append(o)
        o_ref[rows, :] = jnp.concatenate(outs, axis=1)

    for ch in range(nch):
        one(0, qf, kf, vf, of, cosf if rope else None, sinf if rope else None, ch)
        one(1, qb, kb, vb, ob, cosb if rope else None, sinb if rope else None, nch - 1 - ch)

    @pl.when(s == pl.num_programs(1) - 1)
    def _():
        sfin[...] = st[...]


def bidir_scan(p, n_batch, seq, cols, heads, dk, dv, kind, s0, *, lg=None, wup=None, bup=None,
               rope=None, q_scale=1.0, k_scale=1.0):
    t = SCAN_ROWS
    nblk = seq // t
    hk, hv = heads * dk, heads * dv
    qc, kc, vc = cols
    fwd = lambda w, off: pl.BlockSpec((t, w), lambda b, s: (b * nblk + s, off // w))
    bwd = lambda w, off: pl.BlockSpec((t, w), lambda b, s: (b * nblk + nblk - 1 - s, off // w))
    const = lambda shape: pl.BlockSpec(shape, lambda b, s: (0,) * len(shape))
    args = [p] * 6
    specs = [fwd(hk, qc), fwd(hk, kc), fwd(hv, vc), bwd(hk, qc), bwd(hk, kc), bwd(hv, vc)]
    if kind == "ret":
        args += [lg]
        specs += [const((2, hk))]
    else:
        args += [p, p, wup, bup]
        specs += [fwd(LANES, COL_GLA_D), bwd(LANES, COL_GLA_D), const(wup.shape), const(bup.shape)]
    if rope is not None:
        cos, sin = rope
        args += [cos, sin, cos, sin]
        specs += [pl.BlockSpec((t, dk), lambda b, s: (s, 0)), pl.BlockSpec((t, dk), lambda b, s: (s, 0)),
                  pl.BlockSpec((t, dk), lambda b, s: (nblk - 1 - s, 0)),
                  pl.BlockSpec((t, dk), lambda b, s: (nblk - 1 - s, 0))]
    args += [s0]
    state_spec = pl.BlockSpec((None, 2, heads, dv, dk), lambda b, s: (b, 0, 0, 0, 0))
    specs += [state_spec]
    n = n_batch * seq
    kern = functools.partial(_scan_kernel, heads=heads, dk=dk, dv=dv, kind=kind, rope=rope is not None,
                             q_scale=q_scale, k_scale=k_scale)
    return pl.pallas_call(
        kern,
        grid=(n_batch, nblk),
        in_specs=specs,
        out_specs=[pl.BlockSpec((t, hv), lambda b, s: (b * nblk + s, 0)),
                   pl.BlockSpec((t, hv), lambda b, s: (b * nblk + nblk - 1 - s, 0)),
                   state_spec],
        out_shape=[jax.ShapeDtypeStruct((n, hv), F32), jax.ShapeDtypeStruct((n, hv), F32),
                   jax.ShapeDtypeStruct((n_batch, 2, heads, dv, dk), F32)],
        scratch_shapes=[pltpu.VMEM((2, heads, dv, dk), F32)],
        compiler_params=_params(2),
    )(*args)


def _outproj_kernel(na_ref, rf_ref, rb_ref, rg_ref, gf_ref, gb_ref, gg_ref, sw_ref, gn_ref, w_ref, x_ref, mg_ref,
                    o_ref, *, head_w):
    ry = rf_ref[...] + rb_ref[...]
    gy = gf_ref[...] + gb_ref[...]
    r_out, g_out = [], []
    for hd in range(ry.shape[1] // head_w):
        sl = slice(hd * head_w, (hd + 1) * head_w)
        r = ry[:, sl]
        mu = jnp.mean(r, axis=-1, keepdims=True)
        var = jnp.mean(jnp.square(r - mu), axis=-1, keepdims=True)
        r_out.append((r - mu) * lax.rsqrt(var + EPS))
        gq = gy[:, sl]
        g_out.append(gq * lax.rsqrt(jnp.mean(gq * gq, axis=-1, keepdims=True) + EPS) * gn_ref[...])
    ret = jnp.concatenate(r_out, axis=1) * _silu(rg_ref[...])
    gla = jnp.concatenate(g_out, axis=1) * _silu(gg_ref[...])
    mix = jnp.concatenate([na_ref[...], ret, gla, sw_ref[...]], axis=1).astype(BF16)
    o_ref[...] = x_ref[...] + mg_ref[...] * jnp.dot(mix, w_ref[...], preferred_element_type=F32)


def outproj(na, rf, rb, gf, gb, sw, p, gla_norm_g, w_out, x2, mg, rows_per_mod):
    n, d = x2.shape
    gw = GROUP_WIDTH
    tm = 256
    m = mg.shape[0]
    row = lambda w_: pl.BlockSpec((tm, w_), lambda i: (i, 0))
    return pl.pallas_call(
        functools.partial(_outproj_kernel, head_w=RET_DV),
        grid=(n // tm,),
        in_specs=[row(gw), row(gw), row(gw), pl.BlockSpec((tm, gw), lambda i: (i, COL_RET_G // gw)),
                  row(gw), row(gw), pl.BlockSpec((tm, gw), lambda i: (i, COL_GLA_G // gw)), row(gw),
                  pl.BlockSpec((1, GLA_DV), lambda i: (0, 0)),
                  pl.BlockSpec((d, d), lambda i: (0, 0)), row(d),
                  pl.BlockSpec((None, 1, d), lambda i: (i // (rows_per_mod // tm), 0, 0))],
        out_specs=row(d),
        out_shape=jax.ShapeDtypeStruct((n, d), F32),
        compiler_params=_params(1),
    )(na, rf, rb, p, gf, gb, p, sw, gla_norm_g.reshape(1, GLA_DV), w_out, x2, mg.reshape(m, 1, d))


PEER_PAIRS = PEER_HEADS * PEER_TOPK
PEER_ROUTE_TOKENS = 128
PEER_EXPERT_TOKENS = 8
PEER_WAIT_GROUP = 4
PEER_SUB = 8
PEER_FOLD = D_MODEL // PEER_SUB
INV_SQRT2 = 0.7071067811865476


def _topk_cols(s, payload=None, order=None):
    row = lax.broadcasted_iota(jnp.int32, s.shape, 0) if order is None else order
    vals, idxs = [], []
    for _ in range(PEER_TOPK):
        m = jnp.max(s, axis=0, keepdims=True)
        am = jnp.min(jnp.where(s == m, row, jnp.iinfo(jnp.int32).max), axis=0, keepdims=True)
        sel = row == am
        vals.append(m)
        idxs.append(am if payload is None else jnp.max(jnp.where(sel, payload, -1), axis=0, keepdims=True))
        s = jnp.where(sel, -jnp.inf, s)
    return jnp.concatenate(vals, axis=0), jnp.concatenate(idxs, axis=0)


def _staircase_candidates(v0, i0, v1, i1):
    k = PEER_TOPK
    assert k == 16
    t = v0.shape[1]
    r8 = lax.broadcasted_iota(jnp.int32, (8, t), 0)
    r16 = lax.broadcasted_iota(jnp.int32, (k, t), 0)

    def piece(a_sl, b_sl):
        return v0[a_sl, :] + v1[b_sl, :], i0[a_sl, :] * PEER_N_KEYS + i1[b_sl, :]

    one = lambda j: slice(j, j + 1)
    lo = slice(0, 8)
    pieces = [
        (one(0), slice(0, k), None, r16),
        (one(1), lo, None, k + r8),
        (one(2), lo, r8 <= 4, 2 * k + r8),
        (one(3), lo, r8 <= 3, 3 * k + r8),
        (slice(8, k), one(0), None, (r8 + 8) * k),
        (lo, one(0), r8 >= 4, r8 * k),
        (lo, one(1), r8 >= 4, r8 * k + 1),
        (lo, one(2), r8 == 4, r8 * k + 2),
    ]
    sums, ids, orders = [], [], []
    for a_sl, b_sl, keep, order in pieces:
        s, e = piece(a_sl, b_sl)
        sums.append(s if keep is None else jnp.where(keep, s, NEG_INF))
        ids.append(e)
        orders.append(order)
    return jnp.concatenate(sums, axis=0), jnp.concatenate(ids, axis=0), jnp.concatenate(orders, axis=0)


def _route_head(hb, wq_ref, sk_ref, hd):
    half = PEER_DK // 2
    q = jnp.dot(hb, wq_ref[hd], preferred_element_type=F32)
    tops = []
    for p in range(2):
        qp = q[:, p * half:(p + 1) * half].astype(BF16)
        tops.append(_topk_cols(_dot_nt(sk_ref[p, hd], qp)))
    (v0, i0), (v1, i1) = tops
    best_s, best_e = _topk_cols(*_staircase_candidates(v0, i0, v1, i1))
    e = jnp.exp(best_s - best_s[0:1, :])
    return e / jnp.sum(e, axis=0, keepdims=True), best_e


def _peer_route_kernel(x_ref, g_ref, shift_ref, scale_ref, wq_ref, sk_ref, h_ref, idx_ref, gate_ref):
    x = x_ref[...]
    y = x * lax.rsqrt(jnp.mean(x * x, axis=-1, keepdims=True) + EPS)
    h = (y * g_ref[...]) * (1.0 + scale_ref[...]) + shift_ref[...]
    h_ref[...] = h
    hb = h.astype(BF16)

    for hd in range(PEER_HEADS):
        gate_ref[hd], idx_ref[hd] = _route_head(hb, wq_ref, sk_ref, hd)


def peer_route(x2, g, shift, scale, rows_per_mod, wq_h, sk):
    n, d = x2.shape
    t = PEER_ROUTE_TOKENS
    m = shift.shape[0]
    mod_map = lambda i: (i // (rows_per_mod // t), 0, 0)
    return pl.pallas_call(
        _peer_route_kernel,
        grid=(n // t,),
        in_specs=[
            pl.BlockSpec((t, d), lambda i: (i, 0)),
            pl.BlockSpec((1, d), lambda i: (0, 0)),
            pl.BlockSpec((None, 1, d), mod_map),
            pl.BlockSpec((None, 1, d), mod_map),
            pl.BlockSpec((PEER_HEADS, d, PEER_DK), lambda i: (0, 0, 0)),
            pl.BlockSpec((2, PEER_HEADS, PEER_N_KEYS, PEER_DK // 2), lambda i: (0, 0, 0, 0)),
        ],
        out_specs=[
            pl.BlockSpec((t, d), lambda i: (i, 0)),
            pl.BlockSpec((PEER_HEADS, PEER_TOPK, t), lambda i: (0, 0, i)),
            pl.BlockSpec((PEER_HEADS, PEER_TOPK, t), lambda i: (0, 0, i)),
        ],
        out_shape=[
            jax.ShapeDtypeStruct((n, d), F32),
            jax.ShapeDtypeStruct((PEER_HEADS, PEER_TOPK, n), jnp.int32),
            jax.ShapeDtypeStruct((PEER_HEADS, PEER_TOPK, n), F32),
        ],
        compiler_params=_params(1),
    )(x2, g.reshape(1, d), shift.reshape(m, 1, d), scale.reshape(m, 1, d), wq_h, sk)


def pack_experts(u, v):
    def bf16_bits(a):
        b = lax.bitcast_convert_type(a, jnp.uint32)
        rounded = (b + jnp.uint32(0x7FFF) + ((b >> 16) & jnp.uint32(1))) >> 16
        is_nan = (b & jnp.uint32(0x7FFFFFFF)) > jnp.uint32(0x7F800000)
        return jnp.where(is_nan, (b >> 16) | jnp.uint32(0x40), rounded)

    words = lax.bitcast_convert_type((bf16_bits(u) << 16) | bf16_bits(v), jnp.int32)
    return words.reshape(u.shape[0], PEER_SUB, PEER_FOLD)


def _fold_rows(hbuf, h_rows, tb):
    for s in range(PEER_SUB):
        for c in range(PEER_FOLD // LANES):
            lo = s * PEER_FOLD + c * LANES
            hbuf[c, pl.ds(s * tb, tb), :] = h_rows[:, lo:lo + LANES]


def _unfold_residual(o_ref, x_ref, og_ref, ybuf, tb):
    for s in range(PEER_SUB):
        for c in range(PEER_FOLD // LANES):
            sl = slice(s * PEER_FOLD + c * LANES, s * PEER_FOLD + (c + 1) * LANES)
            o_ref[:, sl] = x_ref[:, sl] + og_ref[:, sl] * ybuf[c, pl.ds(s, tb, stride=PEER_SUB), :]


def _expert_weights(j, buf, hbuf, pbuf, wrow, tb, gate_column):
    fold = PEER_FOLD
    hj = jnp.concatenate([hbuf[c, pl.ds(j, PEER_SUB, stride=tb), :] for c in range(fold // LANES)], axis=1)
    for p in range(PEER_PAIRS):
        u = lax.bitcast_convert_type(buf[j, p] & jnp.int32(-65536), F32)
        prod = u * hj
        pbuf[pl.ds(p * PEER_SUB, PEER_SUB), :] = prod[:, :LANES] + prod[:, LANES:]
    part = pbuf[pl.ds(0, PEER_PAIRS, stride=PEER_SUB), :]
    for s in range(1, PEER_SUB):
        part = part + pbuf[pl.ds(s, PEER_PAIRS, stride=PEER_SUB), :]
    sc = jnp.sum(part, axis=1, keepdims=True)
    act = 0.5 * sc * (1.0 + lax.erf(sc * INV_SQRT2))
    g = jnp.concatenate([gate_column(hd) for hd in range(PEER_HEADS)], axis=0)
    wrow[...] = jnp.broadcast_to(g * act, (PEER_PAIRS, LANES))


def _expert_mix(j, buf, wrow, ybuf, start_fetch):
    fold = PEER_FOLD
    accs = [jnp.zeros((PEER_SUB, fold), F32) for _ in range(4)]
    for p in range(PEER_PAIRS):
        if start_fetch is not None:
            start_fetch(p)
        v = lax.bitcast_convert_type(buf[j, p] << 16, F32)
        wp = jnp.broadcast_to(wrow[p:p + 1, :], (PEER_SUB, LANES))
        accs[p % 4] = accs[p % 4] + v * jnp.concatenate([wp, wp], axis=1)
    yj = (accs[0] + accs[1]) + (accs[2] + accs[3])
    for c in range(fold // LANES):
        ybuf[c, pl.ds(j * PEER_SUB, PEER_SUB), :] = yj[:, c * LANES:(c + 1) * LANES]


def _peer_expert_kernel(idx0_ref, idxn_ref, h_ref, gate_ref, x_ref, og_ref, uv_hbm, o_ref,
                        buf_even, buf_odd, sem, hbuf, pbuf, wbuf, ybuf):
    i = pl.program_id(0)
    n = pl.num_programs(0)
    tb = PEER_EXPERT_TOKENS
    bufs = (buf_even, buf_odd)

    def slab_copy(idx_ref, j, r, parity):
        e = idx_ref[0, 0, j * PEER_PAIRS + r]
        return pltpu.make_async_copy(uv_hbm.at[e], bufs[parity].at[j, r], sem.at[parity, j])

    def wait_token(j, parity):
        pltpu.make_async_copy(uv_hbm.at[pl.ds(0, PEER_PAIRS)], bufs[parity].at[j], sem.at[parity, j]).wait()

    @pl.when(i == 0)
    def _():
        def prime(j, carry):
            for r in range(PEER_PAIRS):
                slab_copy(idx0_ref, j, r, 0).start(priority=r % 2)
            return carry
        lax.fori_loop(0, tb, prime, 0)

    lane = lax.broadcasted_iota(jnp.int32, (PEER_TOPK, PEER_ROUTE_TOKENS), 1)
    lane0 = (i % (PEER_ROUTE_TOKENS // tb)) * tb

    _fold_rows(hbuf, h_ref[...], tb)

    def fetcher(j, parity, prefetch):
        return (lambda r: slab_copy(idxn_ref, j, r, 1 - parity).start(priority=r % 2)) if prefetch else None

    def gate_column_of(j):
        return lambda hd: jnp.sum(jnp.where(lane == lane0 + j, gate_ref[hd], 0.0), axis=1, keepdims=True)

    def block(parity, prefetch):
        for j in range(tb):
            if j % PEER_WAIT_GROUP == 0:
                for jj in range(j, j + PEER_WAIT_GROUP):
                    wait_token(jj, parity)
            _expert_weights(j, bufs[parity], hbuf, pbuf, wbuf, tb, gate_column_of(j))
            _expert_mix(j, bufs[parity], wbuf, ybuf, fetcher(j, parity, prefetch))

    for parity in range(2):
        for prefetch in (True, False):
            @pl.when((i % 2 == parity) & ((i + 1 < n) == prefetch))
            def _(parity=parity, prefetch=prefetch):
                block(parity, prefetch)

    _unfold_residual(o_ref, x_ref, og_ref, ybuf, tb)


def peer_expert(h, idx, gate, x2, out_gate, rows_per_mod, uv):
    n, d = h.shape
    tb = PEER_EXPERT_TOKENS
    nb = n // tb
    m = out_gate.shape[0]
    rows = tb * PEER_PAIRS
    idx_rows = idx.reshape(PEER_PAIRS, n).T.reshape(nb, 1, rows)
    gate_blocks = PEER_ROUTE_TOKENS // tb
    return pl.pallas_call(
        _peer_expert_kernel,
        grid=(nb,),
        in_specs=[
            pl.BlockSpec((1, 1, rows), lambda i: (0, 0, 0), memory_space=pltpu.SMEM),
            pl.BlockSpec((1, 1, rows), lambda i: (jnp.minimum(i + 1, nb - 1), 0, 0), memory_space=pltpu.SMEM),
            pl.BlockSpec((tb, d), lambda i: (i, 0)),
            pl.BlockSpec((PEER_HEADS, PEER_TOPK, PEER_ROUTE_TOKENS), lambda i: (0, 0, i // gate_blocks)),
            pl.BlockSpec((tb, d), lambda i: (i, 0)),
            pl.BlockSpec((None, 1, d), lambda i: (i // (rows_per_mod // tb), 0, 0)),
            pl.BlockSpec(memory_space=pl.ANY),
        ],
        out_specs=pl.BlockSpec((tb, d), lambda i: (i, 0)),
        out_shape=jax.ShapeDtypeStruct((n, d), F32),
        scratch_shapes=[pltpu.VMEM((tb, PEER_PAIRS, PEER_SUB, PEER_FOLD), jnp.int32),
                        pltpu.VMEM((tb, PEER_PAIRS, PEER_SUB, PEER_FOLD), jnp.int32),
                        pltpu.SemaphoreType.DMA((2, tb)),
                        pltpu.VMEM((PEER_FOLD // LANES, PEER_SUB * tb, LANES), F32),
                        pltpu.VMEM((PEER_PAIRS * PEER_SUB, LANES), F32),
                        pltpu.VMEM((PEER_PAIRS, LANES), F32),
                        pltpu.VMEM((PEER_FOLD // LANES, tb * PEER_SUB, LANES), F32)],
        compiler_params=pltpu.CompilerParams(dimension_semantics=("arbitrary",), vmem_limit_bytes=VMEM_LIMIT,
                                             disable_bounds_checks=True),
    )(idx_rows, idx_rows, h, gate, x2, out_gate.reshape(m, 1, d), uv)


def peer_residual(x2, g, shift, scale, out_gate, rows_per_mod, wq_h, sk, uv):
    h, idx, gate = peer_route(x2, g, shift, scale, rows_per_mod, wq_h, sk)
    return peer_expert(h, idx, gate, x2, out_gate, rows_per_mod, uv)


def _final_norm_kernel(x_ref, g_ref, o_ref):
    x = x_ref[...]
    y = x * lax.rsqrt(jnp.mean(x * x, axis=-1, keepdims=True) + EPS)
    o_ref[...] = y * g_ref[...]


def final_norm(x2, g):
    n, d = x2.shape
    tm = 512
    return pl.pallas_call(
        _final_norm_kernel,
        grid=(n // tm,),
        in_specs=[pl.BlockSpec((tm, d), lambda i: (i, 0)), pl.BlockSpec((1, d), lambda i: (0, 0))],
        out_specs=pl.BlockSpec((tm, d), lambda i: (i, 0)),
        out_shape=jax.ShapeDtypeStruct((n, d), x2.dtype),
        compiler_params=_params(1),
    )(x2, g.reshape(1, d))


def _mixers(p, pz, n_batch, seq, ctx_len, is_ctx, tables, prm):
    if is_ctx:
        na = ctx_attention(p, n_batch, seq, (COL_NA_Q, COL_NA_K, COL_NA_V), GROUP_WIDTH, None)
        sw = ctx_attention(p, n_batch, seq, (COL_SWA_Q, COL_SWA_K, COL_SWA_V), SWA_KV_HEADS * HEAD_DIM, prm["sink"])
    else:
        na = na_attention(p, pz, n_batch, seq, ctx_len, prm["na_bias"])
        sw = swa_attention(p, pz, n_batch, seq, ctx_len, prm["sink"], tables["swa_q"], tables["swa_k"])
    rf, rb, rs = bidir_scan(p, n_batch, seq, (COL_RET_Q, COL_RET_K, COL_RET_V), RET_HEADS, RET_DK, RET_DV, "ret",
                            prm["ret_s0"], lg=prm["ret_lg"], rope=None if is_ctx else tables["ret"],
                            k_scale=RET_DK ** -0.5)
    gf, gb, gs = bidir_scan(p, n_batch, seq, (COL_GLA_Q, COL_GLA_K, COL_GLA_V), GLA_HEADS, GLA_DK, GLA_DV, "gla",
                            prm["gla_s0"], wup=prm["gla_wup"], bup=prm["gla_bup"], q_scale=GLA_DK ** -0.5)
    return (na, rf, rb, gf, gb, sw), (rs, gs)


def kernel(x, c, ctx, c_ctx, w_ada, b_ada, norm_attn_g, norm_ffn_g, w_in, na_rpb, ret_log_gamma,
           gla_w_gate_up, gla_b_gate, gla_norm_g, swa_sink, w_out, peer_w_q, peer_sub_keys,
           peer_u, peer_v, final_g):
    bsz, slen, d = x.shape
    zlen = ctx.shape[1]
    x2 = x.reshape(bsz * slen, d)
    z2 = ctx.reshape(bsz * zlen, d)
    tables = {"ret": rope_lane_tables(slen, RET_DK, 1),
              "swa_q": rope_lane_tables(slen, HEAD_DIM, SWA_HEADS),
              "swa_k": rope_lane_tables(slen, HEAD_DIM, SWA_KV_HEADS)}
    c_rows = jnp.zeros((8, d), F32).at[:bsz].set(c).at[bsz].set(c_ctx)
    for layer in range(DEPTH):
        has_next = layer < DEPTH - 1
        mod = adaln(c_rows, w_ada[layer], b_ada[layer])
        mx = [mod[:bsz, k * d:(k + 1) * d] for k in range(6)]
        mz = [mod[bsz:bsz + 1, k * d:(k + 1) * d] for k in range(6)]

        wi = w_in[layer]
        wp = jnp.concatenate([wi[:, :REF_COL_GLA_D], wi[:, REF_COL_SWA_Q:], wi[:, REF_COL_GLA_D:REF_COL_SWA_Q],
                              jnp.zeros((d, PROJ_WIDTH - REF_D_IN), F32)], axis=1).astype(BF16)
        px = modproj(x2, norm_attn_g[layer], mx[0], mx[1], slen, wp)
        pz = modproj(z2, norm_attn_g[layer], mz[0], mz[1], bsz * zlen, wp)

        wup = (jnp.zeros((2, LANES, GLA_HEADS * GLA_DK), F32)
               .at[0, :GLA_RANK].set(gla_w_gate_up[layer, 0])
               .at[1, GLA_RANK:2 * GLA_RANK].set(gla_w_gate_up[layer, 1])).astype(BF16)
        prm = {"na_bias": na_band_bias(na_rpb[layer]), "sink": swa_sink[layer],
               "ret_lg": jnp.repeat(ret_log_gamma[layer], RET_DK, axis=1),
               "gla_wup": wup, "gla_bup": gla_b_gate[layer].reshape(2, 1, GLA_HEADS * GLA_DK),
               "ret_s0": jnp.zeros((bsz, 2, RET_HEADS, RET_DV, RET_DK), F32),
               "gla_s0": jnp.zeros((bsz, 2, GLA_HEADS, GLA_DV, GLA_DK), F32)}
        mix_z, (ret_s, gla_s) = _mixers(pz, pz, bsz, zlen, zlen, True, tables, prm)
        prm["ret_s0"], prm["gla_s0"] = ret_s, gla_s
        mix_x, _ = _mixers(px, pz, bsz, slen, zlen, False, tables, prm)

        wo = w_out[layer].astype(BF16)
        x2 = outproj(*mix_x, px, gla_norm_g[layer], wo, x2, mx[2], slen)

        uv = pack_experts(peer_u[layer], peer_v[layer])
        wq_h = peer_w_q[layer].reshape(d, PEER_HEADS, PEER_DK).transpose(1, 0, 2).astype(BF16)
        sk = peer_sub_keys[layer].astype(BF16)
        x2 = peer_residual(x2, norm_ffn_g[layer], mx[3], mx[4], mx[5], slen, wq_h, sk, uv)
        if has_next:
            z2 = outproj(*mix_z, pz, gla_norm_g[layer], wo, z2, mz[2], bsz * zlen)
            z2 = peer_residual(z2, norm_ffn_g[layer], mz[3], mz[4], mz[5], bsz * zlen, wq_h, sk, uv)
    return final_norm(x2, final_g).reshape(bsz, slen, d)
```

```python
import functools

import jax
import jax.numpy as jnp
import numpy as np
from jax import lax
from jax.experimental import pallas as pl
from jax.experimental.pallas import tpu as pltpu

D_MODEL = 2048
DEPTH = 2
GRID_W = 64
EPS = 1e-6
ROPE_BASE = 10000.0

GROUP_WIDTH = D_MODEL // 4
HEAD_DIM = 64
NA_HEADS = GROUP_WIDTH // HEAD_DIM
NA_ROWS = 8
NA_COLS = 16
RET_HEADS = 4
RET_DK = GROUP_WIDTH // RET_HEADS
RET_DV = GROUP_WIDTH // RET_HEADS
GLA_HEADS = 4
GLA_DV = GROUP_WIDTH // GLA_HEADS
GLA_DK = GLA_DV // 2
GLA_RANK = 16
GLA_TAU = 16.0
SWA_HEADS = GROUP_WIDTH // HEAD_DIM
SWA_KV_HEADS = SWA_HEADS // 4
SWA_WINDOW = 128
SWA_BLOCK = 128
SCAN_CHUNK = 64
PEER_HEADS = 8
PEER_N_KEYS = 128
PEER_N_EXPERTS = PEER_N_KEYS * PEER_N_KEYS
PEER_DK = 256
PEER_TOPK = 16

LANES = 128
VMEM_LIMIT = 48 * 1024 * 1024
BF16 = jnp.bfloat16
F32 = jnp.float32
NEG_INF = float("-inf")

COL_NA_Q, COL_NA_K, COL_NA_V = 0, 512, 1024
COL_RET_Q, COL_RET_K, COL_RET_V, COL_RET_G = 1536, 2048, 2560, 3072
COL_GLA_Q, COL_GLA_K, COL_GLA_V, COL_GLA_G = 3584, 3840, 4096, 4608
COL_SWA_Q, COL_SWA_K, COL_SWA_V = 5120, 5632, 5760
COL_GLA_D = 5888
REF_COL_GLA_D, REF_COL_SWA_Q, REF_D_IN = 5120, 5152, 5920
PROJ_WIDTH = 6144


def _silu(x):
    return x / (1.0 + jnp.exp(-x))


def _dot_nt(a, b):
    return lax.dot_general(a, b, (((1,), (1,)), ((), ())), preferred_element_type=F32)


def _dot_tn(a, b):
    return lax.dot_general(a, b, (((0,), (0,)), ((), ())), preferred_element_type=F32)


def _params(n_axes):
    return pltpu.CompilerParams(dimension_semantics=("arbitrary",) * n_axes, vmem_limit_bytes=VMEM_LIMIT)


def _rope_lanes(x, cs, sn, quarter):
    n = x.shape[-1]
    lane = lax.broadcasted_iota(jnp.int32, x.shape, x.ndim - 1)
    first = (lane % (2 * quarter)) < quarter
    swapped = jnp.where(first, pltpu.roll(x, n - quarter, x.ndim - 1), pltpu.roll(x, quarter, x.ndim - 1))
    return x * cs + swapped * sn


def rope_lane_tables(length, dh, copies):
    t = jnp.arange(length)
    pos = jnp.stack([t // GRID_W, t % GRID_W], axis=-1).astype(F32)
    quarter = dh // 4
    inv = ROPE_BASE ** (-jnp.arange(quarter, dtype=F32) / quarter)
    ang = pos[:, :, None] * inv
    cos, sin = jnp.cos(ang), jnp.sin(ang)
    cl = jnp.concatenate([cos[:, 0], cos[:, 0], cos[:, 1], cos[:, 1]], axis=-1)
    sl = jnp.concatenate([-sin[:, 0], sin[:, 0], -sin[:, 1], sin[:, 1]], axis=-1)
    return jnp.tile(cl, (1, copies)), jnp.tile(sl, (1, copies))


def _adaln_kernel(c_ref, w_ref, b_ref, o_ref):
    a = _silu(c_ref[...]).astype(BF16)
    o_ref[...] = jnp.dot(a, w_ref[...].astype(BF16), preferred_element_type=F32) + b_ref[...]


def adaln(c_rows, w, b):
    r, d = c_rows.shape
    m = w.shape[1]
    tn = 1024
    return pl.pallas_call(
        _adaln_kernel,
        grid=(m // tn,),
        in_specs=[pl.BlockSpec((r, d), lambda j: (0, 0)), pl.BlockSpec((d, tn), lambda j: (0, j)),
                  pl.BlockSpec((1, tn), lambda j: (0, j))],
        out_specs=pl.BlockSpec((r, tn), lambda j: (0, j)),
        out_shape=jax.ShapeDtypeStruct((r, m), F32),
        compiler_params=_params(1),
    )(c_rows, w, b.reshape(1, m))


def _modproj_kernel(x_ref, g_ref, shift_ref, scale_ref, w_ref, o_ref, hb_ref):
    @pl.when(pl.program_id(1) == 0)
    def _():
        x = x_ref[...]
        y = x * lax.rsqrt(jnp.mean(x * x, axis=-1, keepdims=True) + EPS)
        hb_ref[...] = ((y * g_ref[...]) * (1.0 + scale_ref[...]) + shift_ref[...]).astype(BF16)

    o_ref[...] = jnp.dot(hb_ref[...], w_ref[...], preferred_element_type=F32)


def modproj(x2, g, shift, scale, rows_per_mod, w):
    n, d = x2.shape
    wid = w.shape[1]
    tm = min(512, rows_per_mod)
    tn = 2048
    m = shift.shape[0]
    mod_map = lambda i, j: (i // (rows_per_mod // tm), 0, 0)
    return pl.pallas_call(
        _modproj_kernel,
        grid=(n // tm, wid // tn),
        in_specs=[pl.BlockSpec((tm, d), lambda i, j: (i, 0)), pl.BlockSpec((1, d), lambda i, j: (0, 0)),
                  pl.BlockSpec((None, 1, d), mod_map), pl.BlockSpec((None, 1, d), mod_map),
                  pl.BlockSpec((d, tn), lambda i, j: (0, j))],
        out_specs=pl.BlockSpec((tm, tn), lambda i, j: (i, j)),
        out_shape=jax.ShapeDtypeStruct((n, wid), F32),
        scratch_shapes=[pltpu.VMEM((tm, d), BF16)],
        compiler_params=_params(2),
    )(x2, g.reshape(1, d), shift.reshape(m, 1, d), scale.reshape(m, 1, d), w)


NA_QROWS = 8


def _na_kernel(q_ref, k_ref, v_ref, kz_ref, vz_ref, bias_ref, o_ref, *, rows):
    step = pl.program_id(2)
    dh = HEAD_DIM
    band = NA_ROWS * GRID_W
    kz = kz_ref[...].astype(BF16)
    vz = vz_ref[...].astype(BF16)
    heads = LANES // dh
    units = [(qr, hh) for qr in range(NA_QROWS) for hh in range(heads)]
    scores, vbands = {}, {}
    for qr in range(NA_QROWS):
        r = step * NA_QROWS + qr
        start = jnp.clip(r - NA_ROWS // 2, 0, rows - NA_ROWS)
        dr0 = start - r + NA_ROWS - 1
        tok0 = pl.multiple_of(start * GRID_W, GRID_W)
        kb = k_ref[pl.ds(tok0, band), :].astype(BF16)
        vbands[qr] = v_ref[pl.ds(tok0, band), :].astype(BF16)
        q = (q_ref[pl.ds(qr * GRID_W, GRID_W), :] * (dh ** -0.5)).astype(BF16)
        for hh in range(heads):
            sl = slice(hh * dh, (hh + 1) * dh)
            scores[qr, hh] = (_dot_nt(q[:, sl], kb[:, sl]) + bias_ref[hh, dr0],
                              _dot_nt(q[:, sl], kz[:, sl]))
    probs = {}
    for u in units:
        s_nb, s_cx = scores[u]
        m = jnp.maximum(jnp.max(s_nb, axis=1, keepdims=True), jnp.max(s_cx, axis=1, keepdims=True))
        p_nb = jnp.exp(s_nb - m)
        p_cx = jnp.exp(s_cx - m)
        den = jnp.sum(p_nb, axis=1, keepdims=True) + jnp.sum(p_cx, axis=1, keepdims=True)
        probs[u] = (p_nb.astype(BF16), p_cx.astype(BF16), den)
    for qr in range(NA_QROWS):
        outs = []
        for hh in range(heads):
            sl = slice(hh * dh, (hh + 1) * dh)
            p_nb, p_cx, den = probs[qr, hh]
            o = (jnp.dot(p_nb, vbands[qr][:, sl], preferred_element_type=F32)
                 + jnp.dot(p_cx, vz[:, sl], preferred_element_type=F32))
            outs.append(o / den)
        o_ref[pl.ds(qr * GRID_W, GRID_W), :] = jnp.concatenate(outs, axis=1)


def na_band_bias(rpb):
    col = jnp.arange(GRID_W)
    col_start = jnp.clip(col - NA_COLS // 2, 0, GRID_W - NA_COLS)
    col_ok = (col[None, :] >= col_start[:, None]) & (col[None, :] < col_start[:, None] + NA_COLS)
    d_col = jnp.clip(col[None, :] - col[:, None], -(NA_COLS - 1), NA_COLS - 1) + NA_COLS - 1
    d_row = jnp.arange(NA_ROWS)[:, None] + jnp.arange(NA_ROWS)[None, :]
    b = rpb.astype(F32)[:, d_row][..., d_col]
    b = jnp.where(col_ok[None, None, None], b, NEG_INF)
    return b.transpose(0, 1, 3, 2, 4).reshape(rpb.shape[0], NA_ROWS, GRID_W, NA_ROWS * GRID_W)


def na_attention(px, pz, n_batch, seq, ctx_len, bias):
    rows = seq // GRID_W
    tq = NA_QROWS * GRID_W
    nsteps = rows // NA_QROWS
    heads_per_blk = LANES // HEAD_DIM
    return pl.pallas_call(
        functools.partial(_na_kernel, rows=rows),
        grid=(n_batch, NA_HEADS // heads_per_blk, nsteps),
        in_specs=[
            pl.BlockSpec((tq, LANES), lambda b, hp, s: (b * nsteps + s, COL_NA_Q // LANES + hp)),
            pl.BlockSpec((seq, LANES), lambda b, hp, s: (b, COL_NA_K // LANES + hp)),
            pl.BlockSpec((seq, LANES), lambda b, hp, s: (b, COL_NA_V // LANES + hp)),
            pl.BlockSpec((ctx_len, LANES), lambda b, hp, s: (b, COL_NA_K // LANES + hp)),
            pl.BlockSpec((ctx_len, LANES), lambda b, hp, s: (b, COL_NA_V // LANES + hp)),
            pl.BlockSpec((heads_per_blk, NA_ROWS, GRID_W, NA_ROWS * GRID_W), lambda b, hp, s: (hp, 0, 0, 0)),
        ],
        out_specs=pl.BlockSpec((tq, LANES), lambda b, hp, s: (b * nsteps + s, hp)),
        out_shape=jax.ShapeDtypeStruct((n_batch * seq, GROUP_WIDTH), F32),
        compiler_params=_params(3),
    )(px, px, px, pz, pz, bias)


def _swa_kernel(q_ref, kp_ref, kc_ref, kn_ref, vp_ref, vc_ref, vn_ref, kz_ref, vz_ref, sink_ref,
                cq_ref, sq_ref, ckp_ref, skp_ref, ckc_ref, skc_ref, ckn_ref, skn_ref, o_ref):
    n = pl.program_id(1)
    nb = pl.num_programs(1)
    dh = HEAD_DIM
    blk = SWA_BLOCK
    quarter = dh // 4
    group = SWA_HEADS // SWA_KV_HEADS
    q = _rope_lanes(q_ref[...], cq_ref[...], sq_ref[...], quarter) * (dh ** -0.5)
    kp = _rope_lanes(kp_ref[...], ckp_ref[...], skp_ref[...], quarter).astype(BF16)
    kc = _rope_lanes(kc_ref[...], ckc_ref[...], skc_ref[...], quarter).astype(BF16)
    kn = _rope_lanes(kn_ref[...], ckn_ref[...], skn_ref[...], quarter).astype(BF16)
    kz = kz_ref[...].astype(BF16)
    vp, vc, vn, vz = (r[...].astype(BF16) for r in (vp_ref, vc_ref, vn_ref, vz_ref))
    qi = lax.broadcasted_iota(jnp.int32, (group * blk, blk), 0) % blk
    kj = lax.broadcasted_iota(jnp.int32, (group * blk, blk), 1)
    ok_p = (kj >= qi) & (n > 0)
    ok_n = (kj <= qi) & (n < nb - 1)
    outs = []
    for hk in range(SWA_KV_HEADS):
        ks = slice(hk * dh, (hk + 1) * dh)
        qs = jnp.concatenate([q[:, (hk * group + g) * dh:(hk * group + g + 1) * dh] for g in range(group)],
                             axis=0).astype(BF16)
        sink = jnp.concatenate([jnp.full((blk, 1), 1.0, F32) * sink_ref[hk * group + g] for g in range(group)],
                               axis=0)
        s_p = jnp.where(ok_p, _dot_nt(qs, kp[:, ks]), NEG_INF)
        s_c = _dot_nt(qs, kc[:, ks])
        s_n = jnp.where(ok_n, _dot_nt(qs, kn[:, ks]), NEG_INF)
        s_z = _dot_nt(qs, kz[:, ks])
        m = jnp.maximum(jnp.maximum(jnp.max(s_p, axis=1, keepdims=True), jnp.max(s_c, axis=1, keepdims=True)),
                        jnp.maximum(jnp.max(s_n, axis=1, keepdims=True), jnp.max(s_z, axis=1, keepdims=True)))
        m = jnp.maximum(m, sink)
        e_p, e_c, e_n, e_z = (jnp.exp(s - m) for s in (s_p, s_c, s_n, s_z))
        den = (jnp.sum(e_p, axis=1, keepdims=True) + jnp.sum(e_c, axis=1, keepdims=True)
               + jnp.sum(e_n, axis=1, keepdims=True) + jnp.sum(e_z, axis=1, keepdims=True) + jnp.exp(sink - m))
        o = (jnp.dot(e_p.astype(BF16), vp[:, ks], preferred_element_type=F32)
             + jnp.dot(e_c.astype(BF16), vc[:, ks], preferred_element_type=F32)
             + jnp.dot(e_n.astype(BF16), vn[:, ks], preferred_element_type=F32)
             + jnp.dot(e_z.astype(BF16), vz[:, ks], preferred_element_type=F32)) / den
        outs += [o[g * blk:(g + 1) * blk, :] for g in range(group)]
    o_ref[...] = jnp.concatenate(outs, axis=1)


def swa_attention(px, pz, n_batch, seq, ctx_len, sink, rope_q, rope_k):
    blk = SWA_BLOCK
    nb = seq // blk
    prev = lambda b, n: b * nb + jnp.maximum(n - 1, 0)
    cur = lambda b, n: b * nb + n
    nxt = lambda b, n: b * nb + jnp.minimum(n + 1, nb - 1)
    kv = lambda off, f: pl.BlockSpec((blk, LANES), lambda b, n: (f(b, n), off // LANES))
    tab = lambda w, f: pl.BlockSpec((blk, w), lambda b, n: (f(0, n), 0))
    cq, sq = rope_q
    ck, sk = rope_k
    return pl.pallas_call(
        _swa_kernel,
        grid=(n_batch, nb),
        in_specs=[
            pl.BlockSpec((blk, GROUP_WIDTH), lambda b, n: (cur(b, n), COL_SWA_Q // GROUP_WIDTH)),
            kv(COL_SWA_K, prev), kv(COL_SWA_K, cur), kv(COL_SWA_K, nxt),
            kv(COL_SWA_V, prev), kv(COL_SWA_V, cur), kv(COL_SWA_V, nxt),
            pl.BlockSpec((ctx_len, LANES), lambda b, n: (b, COL_SWA_K // LANES)),
            pl.BlockSpec((ctx_len, LANES), lambda b, n: (b, COL_SWA_V // LANES)),
            pl.BlockSpec(memory_space=pltpu.SMEM),
            tab(GROUP_WIDTH, cur), tab(GROUP_WIDTH, cur), tab(LANES, prev), tab(LANES, prev),
            tab(LANES, cur), tab(LANES, cur), tab(LANES, nxt), tab(LANES, nxt),
        ],
        out_specs=pl.BlockSpec((blk, GROUP_WIDTH), lambda b, n: (cur(b, n), 0)),
        out_shape=jax.ShapeDtypeStruct((n_batch * seq, GROUP_WIDTH), F32),
        compiler_params=_params(2),
    )(px, px, px, px, px, px, px, pz, pz, sink, cq, sq, ck, sk, ck, sk, ck, sk)


def _ctx_attn_kernel(q_ref, k_ref, v_ref, sink_ref, o_ref, *, group, use_sink):
    dh = HEAD_DIM
    q = (q_ref[...] * (dh ** -0.5)).astype(BF16)
    k = k_ref[...].astype(BF16)
    v = v_ref[...].astype(BF16)
    outs = []
    for qh in range(q.shape[1] // dh):
        ks = slice((qh // group) * dh, (qh // group + 1) * dh)
        s = _dot_nt(q[:, qh * dh:(qh + 1) * dh], k[:, ks])
        m = jnp.max(s, axis=1, keepdims=True)
        if use_sink:
            m = jnp.maximum(m, sink_ref[qh])
        e = jnp.exp(s - m)
        den = jnp.sum(e, axis=1, keepdims=True)
        if use_sink:
            den = den + jnp.exp(sink_ref[qh] - m)
        outs.append(jnp.dot(e.astype(BF16), v[:, ks], preferred_element_type=F32) / den)
    o_ref[...] = jnp.concatenate(outs, axis=1)


def ctx_attention(pz, n_batch, ctx_len, cols, kv_width, sink):
    qc, kc, vc = cols
    use_sink = sink is not None
    if sink is None:
        sink = jnp.zeros((GROUP_WIDTH // HEAD_DIM,), F32)
    return pl.pallas_call(
        functools.partial(_ctx_attn_kernel, group=GROUP_WIDTH // kv_width, use_sink=use_sink),
        grid=(n_batch,),
        in_specs=[pl.BlockSpec((ctx_len, GROUP_WIDTH), lambda b: (b, qc // GROUP_WIDTH)),
                  pl.BlockSpec((ctx_len, kv_width), lambda b: (b, kc // kv_width)),
                  pl.BlockSpec((ctx_len, kv_width), lambda b: (b, vc // kv_width)),
                  pl.BlockSpec(memory_space=pltpu.SMEM)],
        out_specs=pl.BlockSpec((ctx_len, GROUP_WIDTH), lambda b: (b, 0)),
        out_shape=jax.ShapeDtypeStruct((n_batch * ctx_len, GROUP_WIDTH), F32),
        compiler_params=_params(1),
    )(pz, pz, pz, sink)


SCAN_ROWS = 256


def _split3(x):
    a = x.astype(BF16)
    r = x - a.astype(F32)
    b = r.astype(BF16)
    c = (r - b.astype(F32)).astype(BF16)
    return a, b, c


def _scan_kernel(*refs, heads, dk, dv, kind, rope, q_scale, k_scale):
    it = iter(refs)
    qf, kf, vf, qb, kb, vb = (next(it) for _ in range(6))
    if kind == "ret":
        lg = next(it)
    else:
        df, db, wup, bup = (next(it) for _ in range(4))
    if rope:
        cosf, sinf, cosb, sinb = (next(it) for _ in range(4))
    s0 = next(it)
    of, ob, sfin = next(it), next(it), next(it)
    st = next(it)

    s = pl.program_id(1)
    c = SCAN_CHUNK
    nch = SCAN_ROWS // c
    hk = heads * dk

    @pl.when(s == 0)
    def _():
        st[...] = s0[...]

    r_i = lax.broadcasted_iota(jnp.int32, (c, c), 0)
    c_i = lax.broadcasted_iota(jnp.int32, (c, c), 1)
    masks = (r_i >= c_i, c_i > r_i)

    if kind == "ret":
        pos = lax.broadcasted_iota(jnp.int32, (c, hk), 0).astype(F32)
        gcums = ((pos + 1.0) * lg[0:1, :], (float(c) - pos) * lg[1:2, :])
    else:
        rr = lax.broadcasted_iota(jnp.int32, (SCAN_ROWS, SCAN_ROWS), 0)
        cc = lax.broadcasted_iota(jnp.int32, (SCAN_ROWS, SCAN_ROWS), 1)
        same = (rr // c) == (cc // c)
        tris = (jnp.where(same & (rr >= cc), 1.0, 0.0).astype(BF16),
                jnp.where(same & (cc >= rr), 1.0, 0.0).astype(BF16))

        def gate_cum(d_ref, direction):
            pre = jnp.dot(d_ref[...].astype(BF16), wup[direction], preferred_element_type=F32) + bup[direction]
            g = -(jnp.maximum(-pre, 0.0) + jnp.log1p(jnp.exp(-jnp.abs(pre)))) / GLA_TAU
            return sum(jnp.dot(tris[direction], p, preferred_element_type=F32) for p in _split3(g))

        gcums = (gate_cum(df, 0), gate_cum(db, 1))

    def one(direction, q_ref, k_ref, v_ref, o_ref, cos_ref, sin_ref, ch):
        rows = pl.ds(ch * c, c)
        q = q_ref[rows, :]
        k = k_ref[rows, :]
        v = v_ref[rows, :]
        if rope:
            cs = jnp.concatenate([cos_ref[rows, :]] * heads, axis=1)
            sn = jnp.concatenate([sin_ref[rows, :]] * heads, axis=1)
            q = _rope_lanes(q, cs, sn, dk // 4)
            k = _rope_lanes(k, cs, sn, dk // 4)
        if q_scale != 1.0:
            q = q * q_scale
        if k_scale != 1.0:
            k = k * k_scale
        gcum = gcums[direction] if kind == "ret" else gcums[direction][ch * c:(ch + 1) * c, :]
        gtot = gcum[c - 1:c, :] if direction == 0 else gcum[0:1, :]
        q_rel = (q * jnp.exp(gcum - gtot)).astype(BF16)
        k_rel = (k * jnp.exp(gtot - gcum)).astype(BF16)
        q_dec = (q * jnp.exp(gcum)).astype(BF16)
        dec = jnp.exp(gtot)
        vb16 = v.astype(BF16)
        outs = []
        for hd in range(heads):
            ks = slice(hd * dk, (hd + 1) * dk)
            vs = slice(hd * dv, (hd + 1) * dv)
            a = jnp.where(masks[direction], _dot_nt(q_rel[:, ks], k_rel[:, ks]), 0.0)
            state = st[direction, hd]
            o = jnp.dot(a.astype(BF16), vb16[:, vs], preferred_element_type=F32)
            o = o + _dot_nt(q_dec[:, ks], state.astype(BF16))
            st[direction, hd] = dec[:, ks] * state + _dot_tn(vb16[:, vs], k_rel[:, ks])
            outs.append(o)
        o_ref[rows, :] = jnp.concatenate(outs, axis=1)

    for ch in range(nch):
        one(0, qf, kf, vf, of, cosf if rope else None, sinf if rope else None, ch)
        one(1, qb, kb, vb, ob, cosb if rope else None, sinb if rope else None, nch - 1 - ch)

    @pl.when(s == pl.num_programs(1) - 1)
    def _():
        sfin[...] = st[...]


def bidir_scan(p, n_batch, seq, cols, heads, dk, dv, kind, s0, *, lg=None, wup=None, bup=None,
               rope=None, q_scale=1.0, k_scale=1.0):
    t = SCAN_ROWS
    nblk = seq // t
    hk, hv = heads * dk, heads * dv
    qc, kc, vc = cols
    fwd = lambda w, off: pl.BlockSpec((t, w), lambda b, s: (b * nblk + s, off // w))
    bwd = lambda w, off: pl.BlockSpec((t, w), lambda b, s: (b * nblk + nblk - 1 - s, off // w))
    const = lambda shape: pl.BlockSpec(shape, lambda b, s: (0,) * len(shape))
    args = [p] * 6
    specs = [fwd(hk, qc), fwd(hk, kc), fwd(hv, vc), bwd(hk, qc), bwd(hk, kc), bwd(hv, vc)]
    if kind == "ret":
        args += [lg]
        specs += [const((2, hk))]
    else:
        args += [p, p, wup, bup]
        specs += [fwd(LANES, COL_GLA_D), bwd(LANES, COL_GLA_D), const(wup.shape), const(bup.shape)]
    if rope is not None:
        cos, sin = rope
        args += [cos, sin, cos, sin]
        specs += [pl.BlockSpec((t, dk), lambda b, s: (s, 0)), pl.BlockSpec((t, dk), lambda b, s: (s, 0)),
                  pl.BlockSpec((t, dk), lambda b, s: (nblk - 1 - s, 0)),
                  pl.BlockSpec((t, dk), lambda b, s: (nblk - 1 - s, 0))]
    args += [s0]
    state_spec = pl.BlockSpec((None, 2, heads, dv, dk), lambda b, s: (b, 0, 0, 0, 0))
    specs += [state_spec]
    n = n_batch * seq
    kern = functools.partial(_scan_kernel, heads=heads, dk=dk, dv=dv, kind=kind, rope=rope is not None,
                             q_scale=q_scale, k_scale=k_scale)
    return pl.pallas_call(
        kern,
        grid=(n_batch, nblk),
        in_specs=specs,
        out_specs=[pl.BlockSpec((t, hv), lambda b, s: (b * nblk + s, 0)),
                   pl.BlockSpec((t, hv), lambda b, s: (b * nblk + nblk - 1 - s, 0)),
                   state_spec],
        out_shape=[jax.ShapeDtypeStruct((n, hv), F32), jax.ShapeDtypeStruct((n, hv), F32),
                   jax.ShapeDtypeStruct((n_batch, 2, heads, dv, dk), F32)],
        scratch_shapes=[pltpu.VMEM((2, heads, dv, dk), F32)],
        compiler_params=_params(2),
    )(*args)


def _outproj_kernel(na_ref, rf_ref, rb_ref, rg_ref, gf_ref, gb_ref, gg_ref, sw_ref, gn_ref, w_ref, x_ref, mg_ref,
                    o_ref, *, head_w):
    ry = rf_ref[...] + rb_ref[...]
    gy = gf_ref[...] + gb_ref[...]
    r_out, g_out = [], []
    for hd in range(ry.shape[1] // head_w):
        sl = slice(hd * head_w, (hd + 1) * head_w)
        r = ry[:, sl]
        mu = jnp.mean(r, axis=-1, keepdims=True)
        var = jnp.mean(jnp.square(r - mu), axis=-1, keepdims=True)
        r_out.append((r - mu) * lax.rsqrt(var + EPS))
        gq = gy[:, sl]
        g_out.append(gq * lax.rsqrt(jnp.mean(gq * gq, axis=-1, keepdims=True) + EPS) * gn_ref[...])
    ret = jnp.concatenate(r_out, axis=1) * _silu(rg_ref[...])
    gla = jnp.concatenate(g_out, axis=1) * _silu(gg_ref[...])
    mix = jnp.concatenate([na_ref[...], ret, gla, sw_ref[...]], axis=1).astype(BF16)
    o_ref[...] = x_ref[...] + mg_ref[...] * jnp.dot(mix, w_ref[...], preferred_element_type=F32)


def outproj(na, rf, rb, gf, gb, sw, p, gla_norm_g, w_out, x2, mg, rows_per_mod):
    n, d = x2.shape
    gw = GROUP_WIDTH
    tm = 256
    m = mg.shape[0]
    row = lambda w_: pl.BlockSpec((tm, w_), lambda i: (i, 0))
    return pl.pallas_call(
        functools.partial(_outproj_kernel, head_w=RET_DV),
        grid=(n // tm,),
        in_specs=[row(gw), row(gw), row(gw), pl.BlockSpec((tm, gw), lambda i: (i, COL_RET_G // gw)),
                  row(gw), row(gw), pl.BlockSpec((tm, gw), lambda i: (i, COL_GLA_G // gw)), row(gw),
                  pl.BlockSpec((1, GLA_DV), lambda i: (0, 0)),
                  pl.BlockSpec((d, d), lambda i: (0, 0)), row(d),
                  pl.BlockSpec((None, 1, d), lambda i: (i // (rows_per_mod // tm), 0, 0))],
        out_specs=row(d),
        out_shape=jax.ShapeDtypeStruct((n, d), F32),
        compiler_params=_params(1),
    )(na, rf, rb, p, gf, gb, p, sw, gla_norm_g.reshape(1, GLA_DV), w_out, x2, mg.reshape(m, 1, d))


PEER_PAIRS = PEER_HEADS * PEER_TOPK
PEER_ROUTE_TOKENS = 128
PEER_EXPERT_TOKENS = 8
PEER_WAIT_GROUP = 4
PEER_SUB = 8
PEER_FOLD = D_MODEL // PEER_SUB
INV_SQRT2 = 0.7071067811865476


def _topk_cols(s, payload=None, order=None):
    row = lax.broadcasted_iota(jnp.int32, s.shape, 0) if order is None else order
    vals, idxs = [], []
    for _ in range(PEER_TOPK):
        m = jnp.max(s, axis=0, keepdims=True)
        am = jnp.min(jnp.where(s == m, row, jnp.iinfo(jnp.int32).max), axis=0, keepdims=True)
        sel = row == am
        vals.append(m)
        idxs.append(am if payload is None else jnp.max(jnp.where(sel, payload, -1), axis=0, keepdims=True))
        s = jnp.where(sel, -jnp.inf, s)
    return jnp.concatenate(vals, axis=0), jnp.concatenate(idxs, axis=0)


def _staircase_candidates(v0, i0, v1, i1):
    k = PEER_TOPK
    assert k == 16
    t = v0.shape[1]
    r8 = lax.broadcasted_iota(jnp.int32, (8, t), 0)
    r16 = lax.broadcasted_iota(jnp.int32, (k, t), 0)

    def piece(a_sl, b_sl):
        return v0[a_sl, :] + v1[b_sl, :], i0[a_sl, :] * PEER_N_KEYS + i1[b_sl, :]

    one = lambda j: slice(j, j + 1)
    lo = slice(0, 8)
    pieces = [
        (one(0), slice(0, k), None, r16),
        (one(1), lo, None, k + r8),
        (one(2), lo, r8 <= 4, 2 * k + r8),
        (one(3), lo, r8 <= 3, 3 * k + r8),
        (slice(8, k), one(0), None, (r8 + 8) * k),
        (lo, one(0), r8 >= 4, r8 * k),
        (lo, one(1), r8 >= 4, r8 * k + 1),
        (lo, one(2), r8 == 4, r8 * k + 2),
    ]
    sums, ids, orders = [], [], []
    for a_sl, b_sl, keep, order in pieces:
        s, e = piece(a_sl, b_sl)
        sums.append(s if keep is None else jnp.where(keep, s, NEG_INF))
        ids.append(e)
        orders.append(order)
    return jnp.concatenate(sums, axis=0), jnp.concatenate(ids, axis=0), jnp.concatenate(orders, axis=0)


def _route_head(hb, wq_ref, sk_ref, hd):
    half = PEER_DK // 2
    q = jnp.dot(hb, wq_ref[hd], preferred_element_type=F32)
    tops = []
    for p in range(2):
        qp = q[:, p * half:(p + 1) * half].astype(BF16)
        tops.append(_topk_cols(_dot_nt(sk_ref[p, hd], qp)))
    (v0, i0), (v1, i1) = tops
    best_s, best_e = _topk_cols(*_staircase_candidates(v0, i0, v1, i1))
    e = jnp.exp(best_s - best_s[0:1, :])
    return e / jnp.sum(e, axis=0, keepdims=True), best_e


def _peer_route_kernel(x_ref, g_ref, shift_ref, scale_ref, wq_ref, sk_ref, h_ref, idx_ref, gate_ref):
    x = x_ref[...]
    y = x * lax.rsqrt(jnp.mean(x * x, axis=-1, keepdims=True) + EPS)
    h = (y * g_ref[...]) * (1.0 + scale_ref[...]) + shift_ref[...]
    h_ref[...] = h
    hb = h.astype(BF16)

    for hd in range(PEER_HEADS):
        gate_ref[hd], idx_ref[hd] = _route_head(hb, wq_ref, sk_ref, hd)


def peer_route(x2, g, shift, scale, rows_per_mod, wq_h, sk):
    n, d = x2.shape
    t = PEER_ROUTE_TOKENS
    m = shift.shape[0]
    mod_map = lambda i: (i // (rows_per_mod // t), 0, 0)
    return pl.pallas_call(
        _peer_route_kernel,
        grid=(n // t,),
        in_specs=[
            pl.BlockSpec((t, d), lambda i: (i, 0)),
            pl.BlockSpec((1, d), lambda i: (0, 0)),
            pl.BlockSpec((None, 1, d), mod_map),
            pl.BlockSpec((None, 1, d), mod_map),
            pl.BlockSpec((PEER_HEADS, d, PEER_DK), lambda i: (0, 0, 0)),
            pl.BlockSpec((2, PEER_HEADS, PEER_N_KEYS, PEER_DK // 2), lambda i: (0, 0, 0, 0)),
        ],
        out_specs=[
            pl.BlockSpec((t, d), lambda i: (i, 0)),
            pl.BlockSpec((PEER_HEADS, PEER_TOPK, t), lambda i: (0, 0, i)),
            pl.BlockSpec((PEER_HEADS, PEER_TOPK, t), lambda i: (0, 0, i)),
        ],
        out_shape=[
            jax.ShapeDtypeStruct((n, d), F32),
            jax.ShapeDtypeStruct((PEER_HEADS, PEER_TOPK, n), jnp.int32),
            jax.ShapeDtypeStruct((PEER_HEADS, PEER_TOPK, n), F32),
        ],
        compiler_params=_params(1),
    )(x2, g.reshape(1, d), shift.reshape(m, 1, d), scale.reshape(m, 1, d), wq_h, sk)


def pack_experts(u, v):
    def bf16_bits(a):
        b = lax.bitcast_convert_type(a, jnp.uint32)
        rounded = (b + jnp.uint32(0x7FFF) + ((b >> 16) & jnp.uint32(1))) >> 16
        is_nan = (b & jnp.uint32(0x7FFFFFFF)) > jnp.uint32(0x7F800000)
        return jnp.where(is_nan, (b >> 16) | jnp.uint32(0x40), rounded)

    words = lax.bitcast_convert_type((bf16_bits(u) << 16) | bf16_bits(v), jnp.int32)
    return words.reshape(u.shape[0], PEER_SUB, PEER_FOLD)


def _fold_rows(hbuf, h_rows, tb):
    for s in range(PEER_SUB):
        for c in range(PEER_FOLD // LANES):
            lo = s * PEER_FOLD + c * LANES
            hbuf[c, pl.ds(s * tb, tb), :] = h_rows[:, lo:lo + LANES]


def _unfold_residual(o_ref, x_ref, og_ref, ybuf, tb):
    for s in range(PEER_SUB):
        for c in range(PEER_FOLD // LANES):
            sl = slice(s * PEER_FOLD + c * LANES, s * PEER_FOLD + (c + 1) * LANES)
            o_ref[:, sl] = x_ref[:, sl] + og_ref[:, sl] * ybuf[c, pl.ds(s, tb, stride=PEER_SUB), :]


def _expert_weights(j, buf, hbuf, pbuf, wrow, tb, gate_column):
    fold = PEER_FOLD
    hj = jnp.concatenate([hbuf[c, pl.ds(j, PEER_SUB, stride=tb), :] for c in range(fold // LANES)], axis=1)
    for p in range(PEER_PAIRS):
        u = lax.bitcast_convert_type(buf[j, p] & jnp.int32(-65536), F32)
        prod = u * hj
        pbuf[pl.ds(p * PEER_SUB, PEER_SUB), :] = prod[:, :LANES] + prod[:, LANES:]
    part = pbuf[pl.ds(0, PEER_PAIRS, stride=PEER_SUB), :]
    for s in range(1, PEER_SUB):
        part = part + pbuf[pl.ds(s, PEER_PAIRS, stride=PEER_SUB), :]
    sc = jnp.sum(part, axis=1, keepdims=True)
    act = 0.5 * sc * (1.0 + lax.erf(sc * INV_SQRT2))
    g = jnp.concatenate([gate_column(hd) for hd in range(PEER_HEADS)], axis=0)
    wrow[...] = jnp.broadcast_to(g * act, (PEER_PAIRS, LANES))


def _expert_mix(j, buf, wrow, ybuf, start_fetch):
    fold = PEER_FOLD
    accs = [jnp.zeros((PEER_SUB, fold), F32) for _ in range(4)]
    for p in range(PEER_PAIRS):
        if start_fetch is not None:
            start_fetch(p)
        v = lax.bitcast_convert_type(buf[j, p] << 16, F32)
        wp = jnp.broadcast_to(wrow[p:p + 1, :], (PEER_SUB, LANES))
        accs[p % 4] = accs[p % 4] + v * jnp.concatenate([wp, wp], axis=1)
    yj = (accs[0] + accs[1]) + (accs[2] + accs[3])
    for c in range(fold // LANES):
        ybuf[c, pl.ds(j * PEER_SUB, PEER_SUB), :] = yj[:, c * LANES:(c + 1) * LANES]


def _peer_expert_kernel(idx0_ref, idxn_ref, h_ref, gate_ref, x_ref, og_ref, uv_hbm, o_ref,
                        buf_even, buf_odd, sem, hbuf, pbuf, wbuf, ybuf):
    i = pl.program_id(0)
    n = pl.num_programs(0)
    tb = PEER_EXPERT_TOKENS
    bufs = (buf_even, buf_odd)

    def slab_copy(idx_ref, j, r, parity):
        e = idx_ref[0, 0, j * PEER_PAIRS + r]
        return pltpu.make_async_copy(uv_hbm.at[e], bufs[parity].at[j, r], sem.at[parity, j])

    def wait_token(j, parity):
        pltpu.make_async_copy(uv_hbm.at[pl.ds(0, PEER_PAIRS)], bufs[parity].at[j], sem.at[parity, j]).wait()

    @pl.when(i == 0)
    def _():
        def prime(j, carry):
            for r in range(PEER_PAIRS):
                slab_copy(idx0_ref, j, r, 0).start(priority=r % 2)
            return carry
        lax.fori_loop(0, tb, prime, 0)

    lane = lax.broadcasted_iota(jnp.int32, (PEER_TOPK, PEER_ROUTE_TOKENS), 1)
    lane0 = (i % (PEER_ROUTE_TOKENS // tb)) * tb

    _fold_rows(hbuf, h_ref[...], tb)

    def fetcher(j, parity, prefetch):
        return (lambda r: slab_copy(idxn_ref, j, r, 1 - parity).start(priority=r % 2)) if prefetch else None

    def gate_column_of(j):
        return lambda hd: jnp.sum(jnp.where(lane == lane0 + j, gate_ref[hd], 0.0), axis=1, keepdims=True)

    def block(parity, prefetch):
        for j in range(tb):
            if j % PEER_WAIT_GROUP == 0:
                for jj in range(j, j + PEER_WAIT_GROUP):
                    wait_token(jj, parity)
            _expert_weights(j, bufs[parity], hbuf, pbuf, wbuf, tb, gate_column_of(j))
            _expert_mix(j, bufs[parity], wbuf, ybuf, fetcher(j, parity, prefetch))

    for parity in range(2):
        for prefetch in (True, False):
            @pl.when((i % 2 == parity) & ((i + 1 < n) == prefetch))
            def _(parity=parity, prefetch=prefetch):
                block(parity, prefetch)

    _unfold_residual(o_ref, x_ref, og_ref, ybuf, tb)


def peer_expert(h, idx, gate, x2, out_gate, rows_per_mod, uv):
    n, d = h.shape
    tb = PEER_EXPERT_TOKENS
    nb = n // tb
    m = out_gate.shape[0]
    rows = tb * PEER_PAIRS
    idx_rows = idx.reshape(PEER_PAIRS, n).T.reshape(nb, 1, rows)
    gate_blocks = PEER_ROUTE_TOKENS // tb
    return pl.pallas_call(
        _peer_expert_kernel,
        grid=(nb,),
        in_specs=[
            pl.BlockSpec((1, 1, rows), lambda i: (0, 0, 0), memory_space=pltpu.SMEM),
            pl.BlockSpec((1, 1, rows), lambda i: (jnp.minimum(i + 1, nb - 1), 0, 0), memory_space=pltpu.SMEM),
            pl.BlockSpec((tb, d), lambda i: (i, 0)),
            pl.BlockSpec((PEER_HEADS, PEER_TOPK, PEER_ROUTE_TOKENS), lambda i: (0, 0, i // gate_blocks)),
            pl.BlockSpec((tb, d), lambda i: (i, 0)),
            pl.BlockSpec((None, 1, d), lambda i: (i // (rows_per_mod // tb), 0, 0)),
            pl.BlockSpec(memory_space=pl.ANY),
        ],
        out_specs=pl.BlockSpec((tb, d), lambda i: (i, 0)),
        out_shape=jax.ShapeDtypeStruct((n, d), F32),
        scratch_shapes=[pltpu.VMEM((tb, PEER_PAIRS, PEER_SUB, PEER_FOLD), jnp.int32),
                        pltpu.VMEM((tb, PEER_PAIRS, PEER_SUB, PEER_FOLD), jnp.int32),
                        pltpu.SemaphoreType.DMA((2, tb)),
                        pltpu.VMEM((PEER_FOLD // LANES, PEER_SUB * tb, LANES), F32),
                        pltpu.VMEM((PEER_PAIRS * PEER_SUB, LANES), F32),
                        pltpu.VMEM((PEER_PAIRS, LANES), F32),
                        pltpu.VMEM((PEER_FOLD // LANES, tb * PEER_SUB, LANES), F32)],
        compiler_params=pltpu.CompilerParams(dimension_semantics=("arbitrary",), vmem_limit_bytes=VMEM_LIMIT,
                                             disable_bounds_checks=True),
    )(idx_rows, idx_rows, h, gate, x2, out_gate.reshape(m, 1, d), uv)


def peer_residual(x2, g, shift, scale, out_gate, rows_per_mod, wq_h, sk, uv):
    h, idx, gate = peer_route(x2, g, shift, scale, rows_per_mod, wq_h, sk)
    return peer_expert(h, idx, gate, x2, out_gate, rows_per_mod, uv)


def _final_norm_kernel(x_ref, g_ref, o_ref):
    x = x_ref[...]
    y = x * lax.rsqrt(jnp.mean(x * x, axis=-1, keepdims=True) + EPS)
    o_ref[...] = y * g_ref[...]


def final_norm(x2, g):
    n, d = x2.shape
    tm = 512
    return pl.pallas_call(
        _final_norm_kernel,
        grid=(n // tm,),
        in_specs=[pl.BlockSpec((tm, d), lambda i: (i, 0)), pl.BlockSpec((1, d), lambda i: (0, 0))],
        out_specs=pl.BlockSpec((tm, d), lambda i: (i, 0)),
        out_shape=jax.ShapeDtypeStruct((n, d), x2.dtype),
        compiler_params=_params(1),
    )(x2, g.reshape(1, d))


def _mixers(p, pz, n_batch, seq, ctx_len, is_ctx, tables, prm):
    if is_ctx:
        na = ctx_attention(p, n_batch, seq, (COL_NA_Q, COL_NA_K, COL_NA_V), GROUP_WIDTH, None)
        sw = ctx_attention(p, n_batch, seq, (COL_SWA_Q, COL_SWA_K, COL_SWA_V), SWA_KV_HEADS * HEAD_DIM, prm["sink"])
    else:
        na = na_attention(p, pz, n_batch, seq, ctx_len, prm["na_bias"])
        sw = swa_attention(p, pz, n_batch, seq, ctx_len, prm["sink"], tables["swa_q"], tables["swa_k"])
    rf, rb, rs = bidir_scan(p, n_batch, seq, (COL_RET_Q, COL_RET_K, COL_RET_V), RET_HEADS, RET_DK, RET_DV, "ret",
                            prm["ret_s0"], lg=prm["ret_lg"], rope=None if is_ctx else tables["ret"],
                            k_scale=RET_DK ** -0.5)
    gf, gb, gs = bidir_scan(p, n_batch, seq, (COL_GLA_Q, COL_GLA_K, COL_GLA_V), GLA_HEADS, GLA_DK, GLA_DV, "gla",
                            prm["gla_s0"], wup=prm["gla_wup"], bup=prm["gla_bup"], q_scale=GLA_DK ** -0.5)
    return (na, rf, rb, gf, gb, sw), (rs, gs)


def kernel(x, c, ctx, c_ctx, w_ada, b_ada, norm_attn_g, norm_ffn_g, w_in, na_rpb, ret_log_gamma,
           gla_w_gate_up, gla_b_gate, gla_norm_g, swa_sink, w_out, peer_w_q, peer_sub_keys,
           peer_u, peer_v, final_g):
    bsz, slen, d = x.shape
    zlen = ctx.shape[1]
    x2 = x.reshape(bsz * slen, d)
    z2 = ctx.reshape(bsz * zlen, d)
    tables = {"ret": rope_lane_tables(slen, RET_DK, 1),
              "swa_q": rope_lane_tables(slen, HEAD_DIM, SWA_HEADS),
              "swa_k": rope_lane_tables(slen, HEAD_DIM, SWA_KV_HEADS)}
    c_rows = jnp.zeros((8, d), F32).at[:bsz].set(c).at[bsz].set(c_ctx)
    for layer in range(DEPTH):
        has_next = layer < DEPTH - 1
        mod = adaln(c_rows, w_ada[layer], b_ada[layer])
        mx = [mod[:bsz, k * d:(k + 1) * d] for k in range(6)]
        mz = [mod[bsz:bsz + 1, k * d:(k + 1) * d] for k in range(6)]

        wi = w_in[layer]
        wp = jnp.concatenate([wi[:, :REF_COL_GLA_D], wi[:, REF_COL_SWA_Q:], wi[:, REF_COL_GLA_D:REF_COL_SWA_Q],
                              jnp.zeros((d, PROJ_WIDTH - REF_D_IN), F32)], axis=1).astype(BF16)
        px = modproj(x2, norm_attn_g[layer], mx[0], mx[1], slen, wp)
        pz = modproj(z2, norm_attn_g[layer], mz[0], mz[1], bsz * zlen, wp)

        wup = (jnp.zeros((2, LANES, GLA_HEADS * GLA_DK), F32)
               .at[0, :GLA_RANK].set(gla_w_gate_up[layer, 0])
               .at[1, GLA_RANK:2 * GLA_RANK].set(gla_w_gate_up[layer, 1])).astype(BF16)
        prm = {"na_bias": na_band_bias(na_rpb[layer]), "sink": swa_sink[layer],
               "ret_lg": jnp.repeat(ret_log_gamma[layer], RET_DK, axis=1),
               "gla_wup": wup, "gla_bup": gla_b_gate[layer].reshape(2, 1, GLA_HEADS * GLA_DK),
               "ret_s0": jnp.zeros((bsz, 2, RET_HEADS, RET_DV, RET_DK), F32),
               "gla_s0": jnp.zeros((bsz, 2, GLA_HEADS, GLA_DV, GLA_DK), F32)}
        mix_z, (ret_s, gla_s) = _mixers(pz, pz, bsz, zlen, zlen, True, tables, prm)
        prm["ret_s0"], prm["gla_s0"] = ret_s, gla_s
        mix_x, _ = _mixers(px, pz, bsz, slen, zlen, False, tables, prm)

        wo = w_out[layer].astype(BF16)
        x2 = outproj(*mix_x, px, gla_norm_g[layer], wo, x2, mx[2], slen)

        uv = pack_experts(peer_u[layer], peer_v[layer])
        wq_h = peer_w_q[layer].reshape(d, PEER_HEADS, PEER_DK).transpose(1, 0, 2).astype(BF16)
        sk = peer_sub_keys[layer].astype(BF16)
        x2 = peer_residual(x2, norm_ffn_g[layer], mx[3], mx[4], mx[5], slen, wq_h, sk, uv)
        if has_next:
            z2 = outproj(*mix_z, pz, gla_norm_g[layer], wo, z2, mz[2], bsz * zlen)
            z2 = peer_residual(z2, norm_ffn_g[layer], mz[3], mz[4], mz[5], bsz * zlen, wq_h, sk, uv)
    return final_norm(x2, final_g).reshape(bsz, slen, d)
```

```python
import functools

import jax
import jax.numpy as jnp
import numpy as np
from jax import lax
from jax.experimental import pallas as pl
from jax.experimental.pallas import tpu as pltpu

D_MODEL = 2048
DEPTH = 2
GRID_W = 64
EPS = 1e-6
ROPE_BASE = 10000.0

GROUP_WIDTH = D_MODEL // 4
HEAD_DIM = 64
NA_HEADS = GROUP_WIDTH // HEAD_DIM
NA_ROWS = 8
NA_COLS = 16
RET_HEADS = 4
RET_DK = GROUP_WIDTH // RET_HEADS
RET_DV = GROUP_WIDTH // RET_HEADS
GLA_HEADS = 4
GLA_DV = GROUP_WIDTH // GLA_HEADS
GLA_DK = GLA_DV // 2
GLA_RANK = 16
GLA_TAU = 16.0
SWA_HEADS = GROUP_WIDTH // HEAD_DIM
SWA_KV_HEADS = SWA_HEADS // 4
SWA_WINDOW = 128
SWA_BLOCK = 128
SCAN_CHUNK = 64
PEER_HEADS = 8
PEER_N_KEYS = 128
PEER_N_EXPERTS = PEER_N_KEYS * PEER_N_KEYS
PEER_DK = 256
PEER_TOPK = 16

LANES = 128
VMEM_LIMIT = 48 * 1024 * 1024
BF16 = jnp.bfloat16
F32 = jnp.float32
NEG_INF = float("-inf")

COL_NA_Q, COL_NA_K, COL_NA_V = 0, 512, 1024
COL_RET_Q, COL_RET_K, COL_RET_V, COL_RET_G = 1536, 2048, 2560, 3072
COL_GLA_Q, COL_GLA_K, COL_GLA_V, COL_GLA_G = 3584, 3840, 4096, 4608
COL_SWA_Q, COL_SWA_K, COL_SWA_V = 5120, 5632, 5760
COL_GLA_D = 5888
REF_COL_GLA_D, REF_COL_SWA_Q, REF_D_IN = 5120, 5152, 5920
PROJ_WIDTH = 6144


def _silu(x):
    return x / (1.0 + jnp.exp(-x))


def _dot_nt(a, b):
    return lax.dot_general(a, b, (((1,), (1,)), ((), ())), preferred_element_type=F32)


def _dot_tn(a, b):
    return lax.dot_general(a, b, (((0,), (0,)), ((), ())), preferred_element_type=F32)


def _params(n_axes):
    return pltpu.CompilerParams(dimension_semantics=("arbitrary",) * n_axes, vmem_limit_bytes=VMEM_LIMIT)


def _rope_lanes(x, cs, sn, quarter):
    n = x.shape[-1]
    lane = lax.broadcasted_iota(jnp.int32, x.shape, x.ndim - 1)
    first = (lane % (2 * quarter)) < quarter
    swapped = jnp.where(first, pltpu.roll(x, n - quarter, x.ndim - 1), pltpu.roll(x, quarter, x.ndim - 1))
    return x * cs + swapped * sn


def rope_lane_tables(length, dh, copies):
    t = jnp.arange(length)
    pos = jnp.stack([t // GRID_W, t % GRID_W], axis=-1).astype(F32)
    quarter = dh // 4
    inv = ROPE_BASE ** (-jnp.arange(quarter, dtype=F32) / quarter)
    ang = pos[:, :, None] * inv
    cos, sin = jnp.cos(ang), jnp.sin(ang)
    cl = jnp.concatenate([cos[:, 0], cos[:, 0], cos[:, 1], cos[:, 1]], axis=-1)
    sl = jnp.concatenate([-sin[:, 0], sin[:, 0], -sin[:, 1], sin[:, 1]], axis=-1)
    return jnp.tile(cl, (1, copies)), jnp.tile(sl, (1, copies))


def _adaln_kernel(c_ref, w_ref, b_ref, o_ref):
    a = _silu(c_ref[...]).astype(BF16)
    o_ref[...] = jnp.dot(a, w_ref[...].astype(BF16), preferred_element_type=F32) + b_ref[...]


def adaln(c_rows, w, b):
    r, d = c_rows.shape
    m = w.shape[1]
    tn = 1024
    return pl.pallas_call(
        _adaln_kernel,
        grid=(m // tn,),
        in_specs=[pl.BlockSpec((r, d), lambda j: (0, 0)), pl.BlockSpec((d, tn), lambda j: (0, j)),
                  pl.BlockSpec((1, tn), lambda j: (0, j))],
        out_specs=pl.BlockSpec((r, tn), lambda j: (0, j)),
        out_shape=jax.ShapeDtypeStruct((r, m), F32),
        compiler_params=_params(1),
    )(c_rows, w, b.reshape(1, m))


def _modproj_kernel(x_ref, g_ref, shift_ref, scale_ref, w_ref, o_ref, hb_ref):
    @pl.when(pl.program_id(1) == 0)
    def _():
        x = x_ref[...]
        y = x * lax.rsqrt(jnp.mean(x * x, axis=-1, keepdims=True) + EPS)
        hb_ref[...] = ((y * g_ref[...]) * (1.0 + scale_ref[...]) + shift_ref[...]).astype(BF16)

    o_ref[...] = jnp.dot(hb_ref[...], w_ref[...], preferred_element_type=F32)


def modproj(x2, g, shift, scale, rows_per_mod, w):
    n, d = x2.shape
    wid = w.shape[1]
    tm = min(512, rows_per_mod)
    tn = 2048
    m = shift.shape[0]
    mod_map = lambda i, j: (i // (rows_per_mod // tm), 0, 0)
    return pl.pallas_call(
        _modproj_kernel,
        grid=(n // tm, wid // tn),
        in_specs=[pl.BlockSpec((tm, d), lambda i, j: (i, 0)), pl.BlockSpec((1, d), lambda i, j: (0, 0)),
                  pl.BlockSpec((None, 1, d), mod_map), pl.BlockSpec((None, 1, d), mod_map),
                  pl.BlockSpec((d, tn), lambda i, j: (0, j))],
        out_specs=pl.BlockSpec((tm, tn), lambda i, j: (i, j)),
        out_shape=jax.ShapeDtypeStruct((n, wid), F32),
        scratch_shapes=[pltpu.VMEM((tm, d), BF16)],
        compiler_params=_params(2),
    )(x2, g.reshape(1, d), shift.reshape(m, 1, d), scale.reshape(m, 1, d), w)


NA_QROWS = 8


def _na_kernel(q_ref, k_ref, v_ref, kz_ref, vz_ref, bias_ref, o_ref, *, rows):
    step = pl.program_id(2)
    dh = HEAD_DIM
    band = NA_ROWS * GRID_W
    kz = kz_ref[...].astype(BF16)
    vz = vz_ref[...].astype(BF16)
    heads = LANES // dh
    units = [(qr, hh) for qr in range(NA_QROWS) for hh in range(heads)]
    scores, vbands = {}, {}
    for qr in range(NA_QROWS):
        r = step * NA_QROWS + qr
        start = jnp.clip(r - NA_ROWS // 2, 0, rows - NA_ROWS)
        dr0 = start - r + NA_ROWS - 1
        tok0 = pl.multiple_of(start * GRID_W, GRID_W)
        kb = k_ref[pl.ds(tok0, band), :].astype(BF16)
        vbands[qr] = v_ref[pl.ds(tok0, band), :].astype(BF16)
        q = (q_ref[pl.ds(qr * GRID_W, GRID_W), :] * (dh ** -0.5)).astype(BF16)
        for hh in range(heads):
            sl = slice(hh * dh, (hh + 1) * dh)
            scores[qr, hh] = (_dot_nt(q[:, sl], kb[:, sl]) + bias_ref[hh, dr0],
                              _dot_nt(q[:, sl], kz[:, sl]))
    probs = {}
    for u in units:
        s_nb, s_cx = scores[u]
        m = jnp.maximum(jnp.max(s_nb, axis=1, keepdims=True), jnp.max(s_cx, axis=1, keepdims=True))
        p_nb = jnp.exp(s_nb - m)
        p_cx = jnp.exp(s_cx - m)
        den = jnp.sum(p_nb, axis=1, keepdims=True) + jnp.sum(p_cx, axis=1, keepdims=True)
        probs[u] = (p_nb.astype(BF16), p_cx.astype(BF16), den)
    for qr in range(NA_QROWS):
        outs = []
        for hh in range(heads):
            sl = slice(hh * dh, (hh + 1) * dh)
            p_nb, p_cx, den = probs[qr, hh]
            o = (jnp.dot(p_nb, vbands[qr][:, sl], preferred_element_type=F32)
                 + jnp.dot(p_cx, vz[:, sl], preferred_element_type=F32))
            outs.append(o / den)
        o_ref[pl.ds(qr * GRID_W, GRID_W), :] = jnp.concatenate(outs, axis=1)


def na_band_bias(rpb):
    col = jnp.arange(GRID_W)
    col_start = jnp.clip(col - NA_COLS // 2, 0, GRID_W - NA_COLS)
    col_ok = (col[None, :] >= col_start[:, None]) & (col[None, :] < col_start[:, None] + NA_COLS)
    d_col = jnp.clip(col[None, :] - col[:, None], -(NA_COLS - 1), NA_COLS - 1) + NA_COLS - 1
    d_row = jnp.arange(NA_ROWS)[:, None] + jnp.arange(NA_ROWS)[None, :]
    b = rpb.astype(F32)[:, d_row][..., d_col]
    b = jnp.where(col_ok[None, None, None], b, NEG_INF)
    return b.transpose(0, 1, 3, 2, 4).reshape(rpb.shape[0], NA_ROWS, GRID_W, NA_ROWS * GRID_W)


def na_attention(px, pz, n_batch, seq, ctx_len, bias):
    rows = seq // GRID_W
    tq = NA_QROWS * GRID_W
    nsteps = rows // NA_QROWS
    heads_per_blk = LANES // HEAD_DIM
    return pl.pallas_call(
        functools.partial(_na_kernel, rows=rows),
        grid=(n_batch, NA_HEADS // heads_per_blk, nsteps),
        in_specs=[
            pl.BlockSpec((tq, LANES), lambda b, hp, s: (b * nsteps + s, COL_NA_Q // LANES + hp)),
            pl.BlockSpec((seq, LANES), lambda b, hp, s: (b, COL_NA_K // LANES + hp)),
            pl.BlockSpec((seq, LANES), lambda b, hp, s: (b, COL_NA_V // LANES + hp)),
            pl.BlockSpec((ctx_len, LANES), lambda b, hp, s: (b, COL_NA_K // LANES + hp)),
            pl.BlockSpec((ctx_len, LANES), lambda b, hp, s: (b, COL_NA_V // LANES + hp)),
            pl.BlockSpec((heads_per_blk, NA_ROWS, GRID_W, NA_ROWS * GRID_W), lambda b, hp, s: (hp, 0, 0, 0)),
        ],
        out_specs=pl.BlockSpec((tq, LANES), lambda b, hp, s: (b * nsteps + s, hp)),
        out_shape=jax.ShapeDtypeStruct((n_batch * seq, GROUP_WIDTH), F32),
        compiler_params=_params(3),
    )(px, px, px, pz, pz, bias)


def _swa_kernel(q_ref, kp_ref, kc_ref, kn_ref, vp_ref, vc_ref, vn_ref, kz_ref, vz_ref, sink_ref,
                cq_ref, sq_ref, ckp_ref, skp_ref, ckc_ref, skc_ref, ckn_ref, skn_ref, o_ref):
    n = pl.program_id(1)
    nb = pl.num_programs(1)
    dh = HEAD_DIM
    blk = SWA_BLOCK
    quarter = dh // 4
    group = SWA_HEADS // SWA_KV_HEADS
    q = _rope_lanes(q_ref[...], cq_ref[...], sq_ref[...], quarter) * (dh ** -0.5)
    kp = _rope_lanes(kp_ref[...], ckp_ref[...], skp_ref[...], quarter).astype(BF16)
    kc = _rope_lanes(kc_ref[...], ckc_ref[...], skc_ref[...], quarter).astype(BF16)
    kn = _rope_lanes(kn_ref[...], ckn_ref[...], skn_ref[...], quarter).astype(BF16)
    kz = kz_ref[...].astype(BF16)
    vp, vc, vn, vz = (r[...].astype(BF16) for r in (vp_ref, vc_ref, vn_ref, vz_ref))
    qi = lax.broadcasted_iota(jnp.int32, (group * blk, blk), 0) % blk
    kj = lax.broadcasted_iota(jnp.int32, (group * blk, blk), 1)
    ok_p = (kj >= qi) & (n > 0)
    ok_n = (kj <= qi) & (n < nb - 1)
    outs = []
    for hk in range(SWA_KV_HEADS):
        ks = slice(hk * dh, (hk + 1) * dh)
        qs = jnp.concatenate([q[:, (hk * group + g) * dh:(hk * group + g + 1) * dh] for g in range(group)],
                             axis=0).astype(BF16)
        sink = jnp.concatenate([jnp.full((blk, 1), 1.0, F32) * sink_ref[hk * group + g] for g in range(group)],
                               axis=0)
        s_p = jnp.where(ok_p, _dot_nt(qs, kp[:, ks]), NEG_INF)
        s_c = _dot_nt(qs, kc[:, ks])
        s_n = jnp.where(ok_n, _dot_nt(qs, kn[:, ks]), NEG_INF)
        s_z = _dot_nt(qs, kz[:, ks])
        m = jnp.maximum(jnp.maximum(jnp.max(s_p, axis=1, keepdims=True), jnp.max(s_c, axis=1, keepdims=True)),
                        jnp.maximum(jnp.max(s_n, axis=1, keepdims=True), jnp.max(s_z, axis=1, keepdims=True)))
        m = jnp.maximum(m, sink)
        e_p, e_c, e_n, e_z = (jnp.exp(s - m) for s in (s_p, s_c, s_n, s_z))
        den = (jnp.sum(e_p, axis=1, keepdims=True) + jnp.sum(e_c, axis=1, keepdims=True)
               + jnp.sum(e_n, axis=1, keepdims=True) + jnp.sum(e_z, axis=1, keepdims=True) + jnp.exp(sink - m))
        o = (jnp.dot(e_p.astype(BF16), vp[:, ks], preferred_element_type=F32)
             + jnp.dot(e_c.astype(BF16), vc[:, ks], preferred_element_type=F32)
             + jnp.dot(e_n.astype(BF16), vn[:, ks], preferred_element_type=F32)
             + jnp.dot(e_z.astype(BF16), vz[:, ks], preferred_element_type=F32)) / den
        outs += [o[g * blk:(g + 1) * blk, :] for g in range(group)]
    o_ref[...] = jnp.concatenate(outs, axis=1)


def swa_attention(px, pz, n_batch, seq, ctx_len, sink, rope_q, rope_k):
    blk = SWA_BLOCK
    nb = seq // blk
    prev = lambda b, n: b * nb + jnp.maximum(n - 1, 0)
    cur = lambda b, n: b * nb + n
    nxt = lambda b, n: b * nb + jnp.minimum(n + 1, nb - 1)
    kv = lambda off, f: pl.BlockSpec((blk, LANES), lambda b, n: (f(b, n), off // LANES))
    tab = lambda w, f: pl.BlockSpec((blk, w), lambda b, n: (f(0, n), 0))
    cq, sq = rope_q
    ck, sk = rope_k
    return pl.pallas_call(
        _swa_kernel,
        grid=(n_batch, nb),
        in_specs=[
            pl.BlockSpec((blk, GROUP_WIDTH), lambda b, n: (cur(b, n), COL_SWA_Q // GROUP_WIDTH)),
            kv(COL_SWA_K, prev), kv(COL_SWA_K, cur), kv(COL_SWA_K, nxt),
            kv(COL_SWA_V, prev), kv(COL_SWA_V, cur), kv(COL_SWA_V, nxt),
            pl.BlockSpec((ctx_len, LANES), lambda b, n: (b, COL_SWA_K // LANES)),
            pl.BlockSpec((ctx_len, LANES), lambda b, n: (b, COL_SWA_V // LANES)),
            pl.BlockSpec(memory_space=pltpu.SMEM),
            tab(GROUP_WIDTH, cur), tab(GROUP_WIDTH, cur), tab(LANES, prev), tab(LANES, prev),
            tab(LANES, cur), tab(LANES, cur), tab(LANES, nxt), tab(LANES, nxt),
        ],
        out_specs=pl.BlockSpec((blk, GROUP_WIDTH), lambda b, n: (cur(b, n), 0)),
        out_shape=jax.ShapeDtypeStruct((n_batch * seq, GROUP_WIDTH), F32),
        compiler_params=_params(2),
    )(px, px, px, px, px, px, px, pz, pz, sink, cq, sq, ck, sk, ck, sk, ck, sk)


def _ctx_attn_kernel(q_ref, k_ref, v_ref, sink_ref, o_ref, *, group, use_sink):
    dh = HEAD_DIM
    q = (q_ref[...] * (dh ** -0.5)).astype(BF16)
    k = k_ref[...].astype(BF16)
    v = v_ref[...].astype(BF16)
    outs = []
    for qh in range(q.shape[1] // dh):
        ks = slice((qh // group) * dh, (qh // group + 1) * dh)
        s = _dot_nt(q[:, qh * dh:(qh + 1) * dh], k[:, ks])
        m = jnp.max(s, axis=1, keepdims=True)
        if use_sink:
            m = jnp.maximum(m, sink_ref[qh])
        e = jnp.exp(s - m)
        den = jnp.sum(e, axis=1, keepdims=True)
        if use_sink:
            den = den + jnp.exp(sink_ref[qh] - m)
        outs.append(jnp.dot(e.astype(BF16), v[:, ks], preferred_element_type=F32) / den)
    o_ref[...] = jnp.concatenate(outs, axis=1)


def ctx_attention(pz, n_batch, ctx_len, cols, kv_width, sink):
    qc, kc, vc = cols
    use_sink = sink is not None
    if sink is None:
        sink = jnp.zeros((GROUP_WIDTH // HEAD_DIM,), F32)
    return pl.pallas_call(
        functools.partial(_ctx_attn_kernel, group=GROUP_WIDTH // kv_width, use_sink=use_sink),
        grid=(n_batch,),
        in_specs=[pl.BlockSpec((ctx_len, GROUP_WIDTH), lambda b: (b, qc // GROUP_WIDTH)),
                  pl.BlockSpec((ctx_len, kv_width), lambda b: (b, kc // kv_width)),
                  pl.BlockSpec((ctx_len, kv_width), lambda b: (b, vc // kv_width)),
                  pl.BlockSpec(memory_space=pltpu.SMEM)],
        out_specs=pl.BlockSpec((ctx_len, GROUP_WIDTH), lambda b: (b, 0)),
        out_shape=jax.ShapeDtypeStruct((n_batch * ctx_len, GROUP_WIDTH), F32),
        compiler_params=_params(1),
    )(pz, pz, pz, sink)


SCAN_ROWS = 256


def _split3(x):
    a = x.astype(BF16)
    r = x - a.astype(F32)
    b = r.astype(BF16)
    c = (r - b.astype(F32)).astype(BF16)
    return a, b, c


def _scan_kernel(*refs, heads, dk, dv, kind, rope, q_scale, k_scale):
    it = iter(refs)
    qf, kf, vf, qb, kb, vb = (next(it) for _ in range(6))
    if kind == "ret":
        lg = next(it)
    else:
        df, db, wup, bup = (next(it) for _ in range(4))
    if rope:
        cosf, sinf, cosb, sinb = (next(it) for _ in range(4))
    s0 = next(it)
    of, ob, sfin = next(it), next(it), next(it)
    st = next(it)

    s = pl.program_id(1)
    c = SCAN_CHUNK
    nch = SCAN_ROWS // c
    hk = heads * dk

    @pl.when(s == 0)
    def _():
        st[...] = s0[...]

    r_i = lax.broadcasted_iota(jnp.int32, (c, c), 0)
    c_i = lax.broadcasted_iota(jnp.int32, (c, c), 1)
    masks = (r_i >= c_i, c_i > r_i)

    if kind == "ret":
        pos = lax.broadcasted_iota(jnp.int32, (c, hk), 0).astype(F32)
        gcums = ((pos + 1.0) * lg[0:1, :], (float(c) - pos) * lg[1:2, :])
    else:
        rr = lax.broadcasted_iota(jnp.int32, (SCAN_ROWS, SCAN_ROWS), 0)
        cc = lax.broadcasted_iota(jnp.int32, (SCAN_ROWS, SCAN_ROWS), 1)
        same = (rr // c) == (cc // c)
        tris = (jnp.where(same & (rr >= cc), 1.0, 0.0).astype(BF16),
                jnp.where(same & (cc >= rr), 1.0, 0.0).astype(BF16))

        def gate_cum(d_ref, direction):
            pre = jnp.dot(d_ref[...].astype(BF16), wup[direction], preferred_element_type=F32) + bup[direction]
            g = -(jnp.maximum(-pre, 0.0) + jnp.log1p(jnp.exp(-jnp.abs(pre)))) / GLA_TAU
            return sum(jnp.dot(tris[direction], p, preferred_element_type=F32) for p in _split3(g))

        gcums = (gate_cum(df, 0), gate_cum(db, 1))

    def one(direction, q_ref, k_ref, v_ref, o_ref, cos_ref, sin_ref, ch):
        rows = pl.ds(ch * c, c)
        q = q_ref[rows, :]
        k = k_ref[rows, :]
        v = v_ref[rows, :]
        if rope:
            cs = jnp.concatenate([cos_ref[rows, :]] * heads, axis=1)
            sn = jnp.concatenate([sin_ref[rows, :]] * heads, axis=1)
            q = _rope_lanes(q, cs, sn, dk // 4)
            k = _rope_lanes(k, cs, sn, dk // 4)
        if q_scale != 1.0:
            q = q * q_scale
        if k_scale != 1.0:
            k = k * k_scale
        gcum = gcums[direction] if kind == "ret" else gcums[direction][ch * c:(ch + 1) * c, :]
        gtot = gcum[c - 1:c, :] if direction == 0 else gcum[0:1, :]
        q_rel = (q * jnp.exp(gcum - gtot)).astype(BF16)
        k_rel = (k * jnp.exp(gtot - gcum)).astype(BF16)
        q_dec = (q * jnp.exp(gcum)).astype(BF16)
        dec = jnp.exp(gtot)
        vb16 = v.astype(BF16)
        outs = []
        for hd in range(heads):
            ks = slice(hd * dk, (hd + 1) * dk)
            vs = slice(hd * dv, (hd + 1) * dv)
            a = jnp.where(masks[direction], _dot_nt(q_rel[:, ks], k_rel[:, ks]), 0.0)
            state = st[direction, hd]
            o = jnp.dot(a.astype(BF16), vb16[:, vs], preferred_element_type=F32)
            o = o + _dot_nt(q_dec[:, ks], state.astype(BF16))
            st[direction, hd] = dec[:, ks] * state + _dot_tn(vb16[:, vs], k_rel[:, ks])
            outs.append(o)
        o_ref[rows, :] = jnp.concatenate(outs, axis=1)

    for ch in range(nch):
        one(0, qf, kf, vf, of, cosf if rope else None, sinf if rope else None, ch)
        one(1, qb, kb, vb, ob, cosb if rope else None, sinb if rope else None, nch - 1 - ch)

    @pl.when(s == pl.num_programs(1) - 1)
    def _():
        sfin[...] = st[...]


def bidir_scan(p, n_batch, seq, cols, heads, dk, dv, kind, s0, *, lg=None, wup=None, bup=None,
               rope=None, q_scale=1.0, k_scale=1.0):
    t = SCAN_ROWS
    nblk = seq // t
    hk, hv = heads * dk, heads * dv
    qc, kc, vc = cols
    fwd = lambda w, off: pl.BlockSpec((t, w), lambda b, s: (b * nblk + s, off // w))
    bwd = lambda w, off: pl.BlockSpec((t, w), lambda b, s: (b * nblk + nblk - 1 - s, off // w))
    const = lambda shape: pl.BlockSpec(shape, lambda b, s: (0,) * len(shape))
    args = [p] * 6
    specs = [fwd(hk, qc), fwd(hk, kc), fwd(hv, vc), bwd(hk, qc), bwd(hk, kc), bwd(hv, vc)]
    if kind == "ret":
        args += [lg]
        specs += [const((2, hk))]
    else:
        args += [p, p, wup, bup]
        specs += [fwd(LANES, COL_GLA_D), bwd(LANES, COL_GLA_D), const(wup.shape), const(bup.shape)]
    if rope is not None:
        cos, sin = rope
        args += [cos, sin, cos, sin]
        specs += [pl.BlockSpec((t, dk), lambda b, s: (s, 0)), pl.BlockSpec((t, dk), lambda b, s: (s, 0)),
                  pl.BlockSpec((t, dk), lambda b, s: (nblk - 1 - s, 0)),
                  pl.BlockSpec((t, dk), lambda b, s: (nblk - 1 - s, 0))]
    args += [s0]
    state_spec = pl.BlockSpec((None, 2, heads, dv, dk), lambda b, s: (b, 0, 0, 0, 0))
    specs += [state_spec]
    n = n_batch * seq
    kern = functools.partial(_scan_kernel, heads=heads, dk=dk, dv=dv, kind=kind, rope=rope is not None,
                             q_scale=q_scale, k_scale=k_scale)
    return pl.pallas_call(
        kern,
        grid=(n_batch, nblk),
        in_specs=specs,
        out_specs=[pl.BlockSpec((t, hv), lambda b, s: (b * nblk + s, 0)),
                   pl.BlockSpec((t, hv), lambda b, s: (b * nblk + nblk - 1 - s, 0)),
                   state_spec],
        out_shape=[jax.ShapeDtypeStruct((n, hv), F32), jax.ShapeDtypeStruct((n, hv), F32),
                   jax.ShapeDtypeStruct((n_batch, 2, heads, dv, dk), F32)],
        scratch_shapes=[pltpu.VMEM((2, heads, dv, dk), F32)],
        compiler_params=_params(2),
    )(*args)


def _outproj_kernel(na_ref, rf_ref, rb_ref, rg_ref, gf_ref, gb_ref, gg_ref, sw_ref, gn_ref, w_ref, x_ref, mg_ref,
                    o_ref, *, head_w):
    ry = rf_ref[...] + rb_ref[...]
    gy = gf_ref[...] + gb_ref[...]
    r_out, g_out = [], []
    for hd in range(ry.shape[1] // head_w):
        sl = slice(hd * head_w, (hd + 1) * head_w)
        r = ry[:, sl]
        mu = jnp.mean(r, axis=-1, keepdims=True)
        var = jnp.mean(jnp.square(r - mu), axis=-1, keepdims=True)
        r_out.append((r - mu) * lax.rsqrt(var + EPS))
        gq = gy[:, sl]
        g_out.append(gq * lax.rsqrt(jnp.mean(gq * gq, axis=-1, keepdims=True) + EPS) * gn_ref[...])
    ret = jnp.concatenate(r_out, axis=1) * _silu(rg_ref[...])
    gla = jnp.concatenate(g_out, axis=1) * _silu(gg_ref[...])
    mix = jnp.concatenate([na_ref[...], ret, gla, sw_ref[...]], axis=1).astype(BF16)
    o_ref[...] = x_ref[...] + mg_ref[...] * jnp.dot(mix, w_ref[...], preferred_element_type=F32)


def outproj(na, rf, rb, gf, gb, sw, p, gla_norm_g, w_out, x2, mg, rows_per_mod):
    n, d = x2.shape
    gw = GROUP_WIDTH
    tm = 256
    m = mg.shape[0]
    row = lambda w_: pl.BlockSpec((tm, w_), lambda i: (i, 0))
    return pl.pallas_call(
        functools.partial(_outproj_kernel, head_w=RET_DV),
        grid=(n // tm,),
        in_specs=[row(gw), row(gw), row(gw), pl.BlockSpec((tm, gw), lambda i: (i, COL_RET_G // gw)),
                  row(gw), row(gw), pl.BlockSpec((tm, gw), lambda i: (i, COL_GLA_G // gw)), row(gw),
                  pl.BlockSpec((1, GLA_DV), lambda i: (0, 0)),
                  pl.BlockSpec((d, d), lambda i: (0, 0)), row(d),
                  pl.BlockSpec((None, 1, d), lambda i: (i // (rows_per_mod // tm), 0, 0))],
        out_specs=row(d),
        out_shape=jax.ShapeDtypeStruct((n, d), F32),
        compiler_params=_params(1),
    )(na, rf, rb, p, gf, gb, p, sw, gla_norm_g.reshape(1, GLA_DV), w_out, x2, mg.reshape(m, 1, d))


PEER_PAIRS = PEER_HEADS * PEER_TOPK
PEER_ROUTE_TOKENS = 128
PEER_EXPERT_TOKENS = 8
PEER_WAIT_GROUP = 4
PEER_SUB = 8
PEER_FOLD = D_MODEL // PEER_SUB
INV_SQRT2 = 0.7071067811865476


def _topk_cols(s, payload=None, order=None):
    row = lax.broadcasted_iota(jnp.int32, s.shape, 0) if order is None else order
    vals, idxs = [], []
    for _ in range(PEER_TOPK):
        m = jnp.max(s, axis=0, keepdims=True)
        am = jnp.min(jnp.where(s == m, row, jnp.iinfo(jnp.int32).max), axis=0, keepdims=True)
        sel = row == am
        vals.append(m)
        idxs.append(am if payload is None else jnp.max(jnp.where(sel, payload, -1), axis=0, keepdims=True))
        s = jnp.where(sel, -jnp.inf, s)
    return jnp.concatenate(vals, axis=0), jnp.concatenate(idxs, axis=0)


def _staircase_candidates(v0, i0, v1, i1):
    k = PEER_TOPK
    assert k == 16
    t = v0.shape[1]
    r8 = lax.broadcasted_iota(jnp.int32, (8, t), 0)
    r16 = lax.broadcasted_iota(jnp.int32, (k, t), 0)

    def piece(a_sl, b_sl):
        return v0[a_sl, :] + v1[b_sl, :], i0[a_sl, :] * PEER_N_KEYS + i1[b_sl, :]

    one = lambda j: slice(j, j + 1)
    lo = slice(0, 8)
    pieces = [
        (one(0), slice(0, k), None, r16),
        (one(1), lo, None, k + r8),
        (one(2), lo, r8 <= 4, 2 * k + r8),
        (one(3), lo, r8 <= 3, 3 * k + r8),
        (slice(8, k), one(0), None, (r8 + 8) * k),
        (lo, one(0), r8 >= 4, r8 * k),
        (lo, one(1), r8 >= 4, r8 * k + 1),
        (lo, one(2), r8 == 4, r8 * k + 2),
    ]
    sums, ids, orders = [], [], []
    for a_sl, b_sl, keep, order in pieces:
        s, e = piece(a_sl, b_sl)
        sums.append(s if keep is None else jnp.where(keep, s, NEG_INF))
        ids.append(e)
        orders.append(order)
    return jnp.concatenate(sums, axis=0), jnp.concatenate(ids, axis=0), jnp.concatenate(orders, axis=0)


def _route_head(hb, wq_ref, sk_ref, hd):
    half = PEER_DK // 2
    q = jnp.dot(hb, wq_ref[hd], preferred_element_type=F32)
    tops = []
    for p in range(2):
        qp = q[:, p * half:(p + 1) * half].astype(BF16)
        tops.append(_topk_cols(_dot_nt(sk_ref[p, hd], qp)))
    (v0, i0), (v1, i1) = tops
    best_s, best_e = _topk_cols(*_staircase_candidates(v0, i0, v1, i1))
    e = jnp.exp(best_s - best_s[0:1, :])
    return e / jnp.sum(e, axis=0, keepdims=True), best_e


def _peer_route_kernel(x_ref, g_ref, shift_ref, scale_ref, wq_ref, sk_ref, h_ref, idx_ref, gate_ref):
    x = x_ref[...]
    y = x * lax.rsqrt(jnp.mean(x * x, axis=-1, keepdims=True) + EPS)
    h = (y * g_ref[...]) * (1.0 + scale_ref[...]) + shift_ref[...]
    h_ref[...] = h
    hb = h.astype(BF16)

    for hd in range(PEER_HEADS):
        gate_ref[hd], idx_ref[hd] = _route_head(hb, wq_ref, sk_ref, hd)


def peer_route(x2, g, shift, scale, rows_per_mod, wq_h, sk):
    n, d = x2.shape
    t = PEER_ROUTE_TOKENS
    m = shift.shape[0]
    mod_map = lambda i: (i // (rows_per_mod // t), 0, 0)
    return pl.pallas_call(
        _peer_route_kernel,
        grid=(n // t,),
        in_specs=[
            pl.BlockSpec((t, d), lambda i: (i, 0)),
            pl.BlockSpec((1, d), lambda i: (0, 0)),
            pl.BlockSpec((None, 1, d), mod_map),
            pl.BlockSpec((None, 1, d), mod_map),
            pl.BlockSpec((PEER_HEADS, d, PEER_DK), lambda i: (0, 0, 0)),
            pl.BlockSpec((2, PEER_HEADS, PEER_N_KEYS, PEER_DK // 2), lambda i: (0, 0, 0, 0)),
        ],
        out_specs=[
            pl.BlockSpec((t, d), lambda i: (i, 0)),
            pl.BlockSpec((PEER_HEADS, PEER_TOPK, t), lambda i: (0, 0, i)),
            pl.BlockSpec((PEER_HEADS, PEER_TOPK, t), lambda i: (0, 0, i)),
        ],
        out_shape=[
            jax.ShapeDtypeStruct((n, d), F32),
            jax.ShapeDtypeStruct((PEER_HEADS, PEER_TOPK, n), jnp.int32),
            jax.ShapeDtypeStruct((PEER_HEADS, PEER_TOPK, n), F32),
        ],
        compiler_params=_params(1),
    )(x2, g.reshape(1, d), shift.reshape(m, 1, d), scale.reshape(m, 1, d), wq_h, sk)


def pack_experts(u, v):
    def bf16_bits(a):
        b = lax.bitcast_convert_type(a, jnp.uint32)
        rounded = (b + jnp.uint32(0x7FFF) + ((b >> 16) & jnp.uint32(1))) >> 16
        is_nan = (b & jnp.uint32(0x7FFFFFFF)) > jnp.uint32(0x7F800000)
        return jnp.where(is_nan, (b >> 16) | jnp.uint32(0x40), rounded)

    words = lax.bitcast_convert_type((bf16_bits(u) << 16) | bf16_bits(v), jnp.int32)
    return words.reshape(u.shape[0], PEER_SUB, PEER_FOLD)


def _fold_rows(hbuf, h_rows, tb):
    for s in range(PEER_SUB):
        for c in range(PEER_FOLD // LANES):
            lo = s * PEER_FOLD + c * LANES
            hbuf[c, pl.ds(s * tb, tb), :] = h_rows[:, lo:lo + LANES]


def _unfold_residual(o_ref, x_ref, og_ref, ybuf, tb):
    for s in range(PEER_SUB):
        for c in range(PEER_FOLD // LANES):
            sl = slice(s * PEER_FOLD + c * LANES, s * PEER_FOLD + (c + 1) * LANES)
            o_ref[:, sl] = x_ref[:, sl] + og_ref[:, sl] * ybuf[c, pl.ds(s, tb, stride=PEER_SUB), :]


def _expert_weights(j, buf, hbuf, pbuf, wrow, tb, gate_column):
    fold = PEER_FOLD
    hj = jnp.concatenate([hbuf[c, pl.ds(j, PEER_SUB, stride=tb), :] for c in range(fold // LANES)], axis=1)
    for p in range(PEER_PAIRS):
        u = lax.bitcast_convert_type(buf[j, p] & jnp.int32(-65536), F32)
        prod = u * hj
        pbuf[pl.ds(p * PEER_SUB, PEER_SUB), :] = prod[:, :LANES] + prod[:, LANES:]
    part = pbuf[pl.ds(0, PEER_PAIRS, stride=PEER_SUB), :]
    for s in range(1, PEER_SUB):
        part = part + pbuf[pl.ds(s, PEER_PAIRS, stride=PEER_SUB), :]
    sc = jnp.sum(part, axis=1, keepdims=True)
    act = 0.5 * sc * (1.0 + lax.erf(sc * INV_SQRT2))
    g = jnp.concatenate([gate_column(hd) for hd in range(PEER_HEADS)], axis=0)
    wrow[...] = jnp.broadcast_to(g * act, (PEER_PAIRS, LANES))


def _expert_mix(j, buf, wrow, ybuf, start_fetch):
    fold = PEER_FOLD
    accs = [jnp.zeros((PEER_SUB, fold), F32) for _ in range(4)]
    for p in range(PEER_PAIRS):
        if start_fetch is not None:
            start_fetch(p)
        v = lax.bitcast_convert_type(buf[j, p] << 16, F32)
        wp = jnp.broadcast_to(wrow[p:p + 1, :], (PEER_SUB, LANES))
        accs[p % 4] = accs[p % 4] + v * jnp.concatenate([wp, wp], axis=1)
    yj = (accs[0] + accs[1]) + (accs[2] + accs[3])
    for c in range(fold // LANES):
        ybuf[c, pl.ds(j * PEER_SUB, PEER_SUB), :] = yj[:, c * LANES:(c + 1) * LANES]


def _peer_expert_kernel(idx0_ref, idxn_ref, h_ref, gate_ref, x_ref, og_ref, uv_hbm, o_ref,
                        buf_even, buf_odd, sem, hbuf, pbuf, wbuf, ybuf, *, n_blocks):
    i = pl.program_id(0)
    n = n_blocks
    tb = PEER_EXPERT_TOKENS
    bufs = (buf_even, buf_odd)

    def slab_copy(idx_ref, j, r, parity):
        e = idx_ref[0, 0, j * PEER_PAIRS + r]
        return pltpu.make_async_copy(uv_hbm.at[e], bufs[parity].at[j, r], sem.at[parity, j])

    def wait_token(j, parity):
        pltpu.make_async_copy(uv_hbm.at[pl.ds(0, PEER_PAIRS)], bufs[parity].at[j], sem.at[parity, j]).wait()

    @pl.when(i == 0)
    def _():
        def prime(j, carry):
            for r in range(PEER_PAIRS):
                slab_copy(idx0_ref, j, r, 0).start(priority=r % 2)
            return carry
        lax.fori_loop(0, tb, prime, 0)

    lane = lax.broadcasted_iota(jnp.int32, (PEER_TOPK, PEER_ROUTE_TOKENS), 1)
    lane0 = (i % (PEER_ROUTE_TOKENS // tb)) * tb

    _fold_rows(hbuf, h_ref[...], tb)

    def fetcher(j, parity, prefetch):
        return (lambda r: slab_copy(idxn_ref, j, r, 1 - parity).start(priority=r % 2)) if prefetch else None

    def gate_column_of(j):
        return lambda hd: jnp.sum(jnp.where(lane == lane0 + j, gate_ref[hd], 0.0), axis=1, keepdims=True)

    def block(parity, prefetch):
        for j in range(tb):
            if j % PEER_WAIT_GROUP == 0:
                for jj in range(j, j + PEER_WAIT_GROUP):
                    wait_token(jj, parity)
            _expert_weights(j, bufs[parity], hbuf, pbuf, wbuf, tb, gate_column_of(j))
            _expert_mix(j, bufs[parity], wbuf, ybuf, fetcher(j, parity, prefetch))

    for parity in range(2):
        for prefetch in (True, False):
            @pl.when((i % 2 == parity) & ((i + 1 < n) == prefetch))
            def _(parity=parity, prefetch=prefetch):
                block(parity, prefetch)

    _unfold_residual(o_ref, x_ref, og_ref, ybuf, tb)


def peer_expert(h, idx, gate, x2, out_gate, rows_per_mod, uv):
    n, d = h.shape
    tb = PEER_EXPERT_TOKENS
    nb = n // tb
    m = out_gate.shape[0]
    rows = tb * PEER_PAIRS
    idx_rows = idx.reshape(PEER_PAIRS, n).T.reshape(nb, 1, rows)
    gate_blocks = PEER_ROUTE_TOKENS // tb
    return pl.pallas_call(
        functools.partial(_peer_expert_kernel, n_blocks=nb),
        grid=(nb,),
        in_specs=[
            pl.BlockSpec((1, 1, rows), lambda i: (0, 0, 0), memory_space=pltpu.SMEM),
            pl.BlockSpec((1, 1, rows), lambda i: (jnp.minimum(i + 1, nb - 1), 0, 0), memory_space=pltpu.SMEM),
            pl.BlockSpec((tb, d), lambda i: (i, 0)),
            pl.BlockSpec((PEER_HEADS, PEER_TOPK, PEER_ROUTE_TOKENS), lambda i: (0, 0, i // gate_blocks)),
            pl.BlockSpec((tb, d), lambda i: (i, 0)),
            pl.BlockSpec((None, 1, d), lambda i: (i // (rows_per_mod // tb), 0, 0)),
            pl.BlockSpec(memory_space=pl.ANY),
        ],
        out_specs=pl.BlockSpec((tb, d), lambda i: (i, 0)),
        out_shape=jax.ShapeDtypeStruct((n, d), F32),
        scratch_shapes=[pltpu.VMEM((tb, PEER_PAIRS, PEER_SUB, PEER_FOLD), jnp.int32),
                        pltpu.VMEM((tb, PEER_PAIRS, PEER_SUB, PEER_FOLD), jnp.int32),
                        pltpu.SemaphoreType.DMA((2, tb)),
                        pltpu.VMEM((PEER_FOLD // LANES, PEER_SUB * tb, LANES), F32),
                        pltpu.VMEM((PEER_PAIRS * PEER_SUB, LANES), F32),
                        pltpu.VMEM((PEER_PAIRS, LANES), F32),
                        pltpu.VMEM((PEER_FOLD // LANES, tb * PEER_SUB, LANES), F32)],
        compiler_params=pltpu.CompilerParams(dimension_semantics=("arbitrary",), vmem_limit_bytes=VMEM_LIMIT,
                                             disable_bounds_checks=True),
    )(idx_rows, idx_rows, h, gate, x2, out_gate.reshape(m, 1, d), uv)


def peer_residual(x2, g, shift, scale, out_gate, rows_per_mod, wq_h, sk, uv):
    h, idx, gate = peer_route(x2, g, shift, scale, rows_per_mod, wq_h, sk)
    return peer_expert(h, idx, gate, x2, out_gate, rows_per_mod, uv)


def _final_norm_kernel(x_ref, g_ref, o_ref):
    x = x_ref[...]
    y = x * lax.rsqrt(jnp.mean(x * x, axis=-1, keepdims=True) + EPS)
    o_ref[...] = y * g_ref[...]


def final_norm(x2, g):
    n, d = x2.shape
    tm = 512
    return pl.pallas_call(
        _final_norm_kernel,
        grid=(n // tm,),
        in_specs=[pl.BlockSpec((tm, d), lambda i: (i, 0)), pl.BlockSpec((1, d), lambda i: (0, 0))],
        out_specs=pl.BlockSpec((tm, d), lambda i: (i, 0)),
        out_shape=jax.ShapeDtypeStruct((n, d), x2.dtype),
        compiler_params=_params(1),
    )(x2, g.reshape(1, d))


def _mixers(p, pz, n_batch, seq, ctx_len, is_ctx, tables, prm):
    if is_ctx:
        na = ctx_attention(p, n_batch, seq, (COL_NA_Q, COL_NA_K, COL_NA_V), GROUP_WIDTH, None)
        sw = ctx_attention(p, n_batch, seq, (COL_SWA_Q, COL_SWA_K, COL_SWA_V), SWA_KV_HEADS * HEAD_DIM, prm["sink"])
    else:
        na = na_attention(p, pz, n_batch, seq, ctx_len, prm["na_bias"])
        sw = swa_attention(p, pz, n_batch, seq, ctx_len, prm["sink"], tables["swa_q"], tables["swa_k"])
    rf, rb, rs = bidir_scan(p, n_batch, seq, (COL_RET_Q, COL_RET_K, COL_RET_V), RET_HEADS, RET_DK, RET_DV, "ret",
                            prm["ret_s0"], lg=prm["ret_lg"], rope=None if is_ctx else tables["ret"],
                            k_scale=RET_DK ** -0.5)
    gf, gb, gs = bidir_scan(p, n_batch, seq, (COL_GLA_Q, COL_GLA_K, COL_GLA_V), GLA_HEADS, GLA_DK, GLA_DV, "gla",
                            prm["gla_s0"], wup=prm["gla_wup"], bup=prm["gla_bup"], q_scale=GLA_DK ** -0.5)
    return (na, rf, rb, gf, gb, sw), (rs, gs)


def kernel(x, c, ctx, c_ctx, w_ada, b_ada, norm_attn_g, norm_ffn_g, w_in, na_rpb, ret_log_gamma,
           gla_w_gate_up, gla_b_gate, gla_norm_g, swa_sink, w_out, peer_w_q, peer_sub_keys,
           peer_u, peer_v, final_g):
    bsz, slen, d = x.shape
    zlen = ctx.shape[1]
    x2 = x.reshape(bsz * slen, d)
    z2 = ctx.reshape(bsz * zlen, d)
    tables = {"ret": rope_lane_tables(slen, RET_DK, 1),
              "swa_q": rope_lane_tables(slen, HEAD_DIM, SWA_HEADS),
              "swa_k": rope_lane_tables(slen, HEAD_DIM, SWA_KV_HEADS)}
    c_rows = jnp.zeros((8, d), F32).at[:bsz].set(c).at[bsz].set(c_ctx)
    for layer in range(DEPTH):
        has_next = layer < DEPTH - 1
        mod = adaln(c_rows, w_ada[layer], b_ada[layer])
        mx = [mod[:bsz, k * d:(k + 1) * d] for k in range(6)]
        mz = [mod[bsz:bsz + 1, k * d:(k + 1) * d] for k in range(6)]

        wi = w_in[layer]
        wp = jnp.concatenate([wi[:, :REF_COL_GLA_D], wi[:, REF_COL_SWA_Q:], wi[:, REF_COL_GLA_D:REF_COL_SWA_Q],
                              jnp.zeros((d, PROJ_WIDTH - REF_D_IN), F32)], axis=1).astype(BF16)
        px = modproj(x2, norm_attn_g[layer], mx[0], mx[1], slen, wp)
        pz = modproj(z2, norm_attn_g[layer], mz[0], mz[1], bsz * zlen, wp)

        wup = (jnp.zeros((2, LANES, GLA_HEADS * GLA_DK), F32)
               .at[0, :GLA_RANK].set(gla_w_gate_up[layer, 0])
               .at[1, GLA_RANK:2 * GLA_RANK].set(gla_w_gate_up[layer, 1])).astype(BF16)
        prm = {"na_bias": na_band_bias(na_rpb[layer]), "sink": swa_sink[layer],
               "ret_lg": jnp.repeat(ret_log_gamma[layer], RET_DK, axis=1),
               "gla_wup": wup, "gla_bup": gla_b_gate[layer].reshape(2, 1, GLA_HEADS * GLA_DK),
               "ret_s0": jnp.zeros((bsz, 2, RET_HEADS, RET_DV, RET_DK), F32),
               "gla_s0": jnp.zeros((bsz, 2, GLA_HEADS, GLA_DV, GLA_DK), F32)}
        mix_z, (ret_s, gla_s) = _mixers(pz, pz, bsz, zlen, zlen, True, tables, prm)
        prm["ret_s0"], prm["gla_s0"] = ret_s, gla_s
        mix_x, _ = _mixers(px, pz, bsz, slen, zlen, False, tables, prm)

        wo = w_out[layer].astype(BF16)
        x2 = outproj(*mix_x, px, gla_norm_g[layer], wo, x2, mx[2], slen)

        uv = pack_experts(peer_u[layer], peer_v[layer])
        wq_h = peer_w_q[layer].reshape(d, PEER_HEADS, PEER_DK).transpose(1, 0, 2).astype(BF16)
        sk = peer_sub_keys[layer].astype(BF16)
        x2 = peer_residual(x2, norm_ffn_g[layer], mx[3], mx[4], mx[5], slen, wq_h, sk, uv)
        if has_next:
            z2 = outproj(*mix_z, pz, gla_norm_g[layer], wo, z2, mz[2], bsz * zlen)
            z2 = peer_residual(z2, norm_ffn_g[layer], mz[3], mz[4], mz[5], bsz * zlen, wq_h, sk, uv)
    return final_norm(x2, final_g).reshape(bsz, slen, d)
```

```python
import functools

import jax
import jax.numpy as jnp
import numpy as np
from jax import lax
from jax.experimental import pallas as pl
from jax.experimental.pallas import tpu as pltpu

D_MODEL = 2048
DEPTH = 2
GRID_W = 64
EPS = 1e-6
ROPE_BASE = 10000.0

GROUP_WIDTH = D_MODEL // 4
HEAD_DIM = 64
NA_HEADS = GROUP_WIDTH // HEAD_DIM
NA_ROWS = 8
NA_COLS = 16
RET_HEADS = 4
RET_DK = GROUP_WIDTH // RET_HEADS
RET_DV = GROUP_WIDTH // RET_HEADS
GLA_HEADS = 4
GLA_DV = GROUP_WIDTH // GLA_HEADS
GLA_DK = GLA_DV // 2
GLA_RANK = 16
GLA_TAU = 16.0
SWA_HEADS = GROUP_WIDTH // HEAD_DIM
SWA_KV_HEADS = SWA_HEADS // 4
SWA_WINDOW = 128
SWA_BLOCK = 128
SCAN_CHUNK = 64
PEER_HEADS = 8
PEER_N_KEYS = 128
PEER_N_EXPERTS = PEER_N_KEYS * PEER_N_KEYS
PEER_DK = 256
PEER_TOPK = 16

LANES = 128
VMEM_LIMIT = 48 * 1024 * 1024
BF16 = jnp.bfloat16
F32 = jnp.float32
NEG_INF = float("-inf")

COL_NA_Q, COL_NA_K, COL_NA_V = 0, 512, 1024
COL_RET_Q, COL_RET_K, COL_RET_V, COL_RET_G = 1536, 2048, 2560, 3072
COL_GLA_Q, COL_GLA_K, COL_GLA_V, COL_GLA_G = 3584, 3840, 4096, 4608
COL_SWA_Q, COL_SWA_K, COL_SWA_V = 5120, 5632, 5760
COL_GLA_D = 5888
REF_COL_GLA_D, REF_COL_SWA_Q, REF_D_IN = 5120, 5152, 5920
PROJ_WIDTH = 6144


def _silu(x):
    return x / (1.0 + jnp.exp(-x))


def _dot_nt(a, b):
    return lax.dot_general(a, b, (((1,), (1,)), ((), ())), preferred_element_type=F32)


def _dot_tn(a, b):
    return lax.dot_general(a, b, (((0,), (0,)), ((), ())), preferred_element_type=F32)


def _params(n_axes):
    return pltpu.CompilerParams(dimension_semantics=("arbitrary",) * n_axes, vmem_limit_bytes=VMEM_LIMIT)


def _rope_lanes(x, cs, sn, quarter):
    n = x.shape[-1]
    lane = lax.broadcasted_iota(jnp.int32, x.shape, x.ndim - 1)
    first = (lane % (2 * quarter)) < quarter
    swapped = jnp.where(first, pltpu.roll(x, n - quarter, x.ndim - 1), pltpu.roll(x, quarter, x.ndim - 1))
    return x * cs + swapped * sn


def rope_lane_tables(length, dh, copies):
    t = jnp.arange(length)
    pos = jnp.stack([t // GRID_W, t % GRID_W], axis=-1).astype(F32)
    quarter = dh // 4
    inv = ROPE_BASE ** (-jnp.arange(quarter, dtype=F32) / quarter)
    ang = pos[:, :, None] * inv
    cos, sin = jnp.cos(ang), jnp.sin(ang)
    cl = jnp.concatenate([cos[:, 0], cos[:, 0], cos[:, 1], cos[:, 1]], axis=-1)
    sl = jnp.concatenate([-sin[:, 0], sin[:, 0], -sin[:, 1], sin[:, 1]], axis=-1)
    return jnp.tile(cl, (1, copies)), jnp.tile(sl, (1, copies))


def _adaln_kernel(c_ref, w_ref, b_ref, o_ref):
    a = _silu(c_ref[...]).astype(BF16)
    o_ref[...] = jnp.dot(a, w_ref[...].astype(BF16), preferred_element_type=F32) + b_ref[...]


def adaln(c_rows, w, b):
    r, d = c_rows.shape
    m = w.shape[1]
    tn = 1024
    return pl.pallas_call(
        _adaln_kernel,
        grid=(m // tn,),
        in_specs=[pl.BlockSpec((r, d), lambda j: (0, 0)), pl.BlockSpec((d, tn), lambda j: (0, j)),
                  pl.BlockSpec((1, tn), lambda j: (0, j))],
        out_specs=pl.BlockSpec((r, tn), lambda j: (0, j)),
        out_shape=jax.ShapeDtypeStruct((r, m), F32),
        compiler_params=_params(1),
    )(c_rows, w, b.reshape(1, m))


def _modproj_kernel(x_ref, g_ref, shift_ref, scale_ref, w_ref, o_ref, hb_ref):
    @pl.when(pl.program_id(1) == 0)
    def _():
        x = x_ref[...]
        y = x * lax.rsqrt(jnp.mean(x * x, axis=-1, keepdims=True) + EPS)
        hb_ref[...] = ((y * g_ref[...]) * (1.0 + scale_ref[...]) + shift_ref[...]).astype(BF16)

    o_ref[...] = jnp.dot(hb_ref[...], w_ref[...], preferred_element_type=F32)


def modproj(x2, g, shift, scale, rows_per_mod, w):
    n, d = x2.shape
    wid = w.shape[1]
    tm = min(512, rows_per_mod)
    tn = 2048
    m = shift.shape[0]
    mod_map = lambda i, j: (i // (rows_per_mod // tm), 0, 0)
    return pl.pallas_call(
        _modproj_kernel,
        grid=(n // tm, wid // tn),
        in_specs=[pl.BlockSpec((tm, d), lambda i, j: (i, 0)), pl.BlockSpec((1, d), lambda i, j: (0, 0)),
                  pl.BlockSpec((None, 1, d), mod_map), pl.BlockSpec((None, 1, d), mod_map),
                  pl.BlockSpec((d, tn), lambda i, j: (0, j))],
        out_specs=pl.BlockSpec((tm, tn), lambda i, j: (i, j)),
        out_shape=jax.ShapeDtypeStruct((n, wid), F32),
        scratch_shapes=[pltpu.VMEM((tm, d), BF16)],
        compiler_params=_params(2),
    )(x2, g.reshape(1, d), shift.reshape(m, 1, d), scale.reshape(m, 1, d), w)


NA_QROWS = 8


def _na_kernel(q_ref, k_ref, v_ref, kz_ref, vz_ref, bias_ref, o_ref, *, rows):
    step = pl.program_id(2)
    dh = HEAD_DIM
    band = NA_ROWS * GRID_W
    kz = kz_ref[...].astype(BF16)
    vz = vz_ref[...].astype(BF16)
    heads = LANES // dh
    units = [(qr, hh) for qr in range(NA_QROWS) for hh in range(heads)]
    scores, vbands = {}, {}
    for qr in range(NA_QROWS):
        r = step * NA_QROWS + qr
        start = jnp.clip(r - NA_ROWS // 2, 0, rows - NA_ROWS)
        dr0 = start - r + NA_ROWS - 1
        tok0 = pl.multiple_of(start * GRID_W, GRID_W)
        kb = k_ref[pl.ds(tok0, band), :].astype(BF16)
        vbands[qr] = v_ref[pl.ds(tok0, band), :].astype(BF16)
        q = (q_ref[pl.ds(qr * GRID_W, GRID_W), :] * (dh ** -0.5)).astype(BF16)
        for hh in range(heads):
            sl = slice(hh * dh, (hh + 1) * dh)
            scores[qr, hh] = (_dot_nt(q[:, sl], kb[:, sl]) + bias_ref[hh, dr0],
                              _dot_nt(q[:, sl], kz[:, sl]))
    probs = {}
    for u in units:
        s_nb, s_cx = scores[u]
        m = jnp.maximum(jnp.max(s_nb, axis=1, keepdims=True), jnp.max(s_cx, axis=1, keepdims=True))
        p_nb = jnp.exp(s_nb - m)
        p_cx = jnp.exp(s_cx - m)
        den = jnp.sum(p_nb, axis=1, keepdims=True) + jnp.sum(p_cx, axis=1, keepdims=True)
        probs[u] = (p_nb.astype(BF16), p_cx.astype(BF16), den)
    for qr in range(NA_QROWS):
        outs = []
        for hh in range(heads):
            sl = slice(hh * dh, (hh + 1) * dh)
            p_nb, p_cx, den = probs[qr, hh]
            o = (jnp.dot(p_nb, vbands[qr][:, sl], preferred_element_type=F32)
                 + jnp.dot(p_cx, vz[:, sl], preferred_element_type=F32))
            outs.append(o / den)
        o_ref[pl.ds(qr * GRID_W, GRID_W), :] = jnp.concatenate(outs, axis=1)


def na_band_bias(rpb):
    col = jnp.arange(GRID_W)
    col_start = jnp.clip(col - NA_COLS // 2, 0, GRID_W - NA_COLS)
    col_ok = (col[None, :] >= col_start[:, None]) & (col[None, :] < col_start[:, None] + NA_COLS)
    d_col = jnp.clip(col[None, :] - col[:, None], -(NA_COLS - 1), NA_COLS - 1) + NA_COLS - 1
    d_row = jnp.arange(NA_ROWS)[:, None] + jnp.arange(NA_ROWS)[None, :]
    b = rpb.astype(F32)[:, d_row][..., d_col]
    b = jnp.where(col_ok[None, None, None], b, NEG_INF)
    return b.transpose(0, 1, 3, 2, 4).reshape(rpb.shape[0], NA_ROWS, GRID_W, NA_ROWS * GRID_W)


def na_attention(px, pz, n_batch, seq, ctx_len, bias):
    rows = seq // GRID_W
    tq = NA_QROWS * GRID_W
    nsteps = rows // NA_QROWS
    heads_per_blk = LANES // HEAD_DIM
    return pl.pallas_call(
        functools.partial(_na_kernel, rows=rows),
        grid=(n_batch, NA_HEADS // heads_per_blk, nsteps),
        in_specs=[
            pl.BlockSpec((tq, LANES), lambda b, hp, s: (b * nsteps + s, COL_NA_Q // LANES + hp)),
            pl.BlockSpec((seq, LANES), lambda b, hp, s: (b, COL_NA_K // LANES + hp)),
            pl.BlockSpec((seq, LANES), lambda b, hp, s: (b, COL_NA_V // LANES + hp)),
            pl.BlockSpec((ctx_len, LANES), lambda b, hp, s: (b, COL_NA_K // LANES + hp)),
            pl.BlockSpec((ctx_len, LANES), lambda b, hp, s: (b, COL_NA_V // LANES + hp)),
            pl.BlockSpec((heads_per_blk, NA_ROWS, GRID_W, NA_ROWS * GRID_W), lambda b, hp, s: (hp, 0, 0, 0)),
        ],
        out_specs=pl.BlockSpec((tq, LANES), lambda b, hp, s: (b * nsteps + s, hp)),
        out_shape=jax.ShapeDtypeStruct((n_batch * seq, GROUP_WIDTH), F32),
        compiler_params=_params(3),
    )(px, px, px, pz, pz, bias)


def _swa_kernel(q_ref, kp_ref, kc_ref, kn_ref, vp_ref, vc_ref, vn_ref, kz_ref, vz_ref, sink_ref,
                cq_ref, sq_ref, ckp_ref, skp_ref, ckc_ref, skc_ref, ckn_ref, skn_ref, o_ref):
    n = pl.program_id(1)
    nb = pl.num_programs(1)
    dh = HEAD_DIM
    blk = SWA_BLOCK
    quarter = dh // 4
    group = SWA_HEADS // SWA_KV_HEADS
    q = _rope_lanes(q_ref[...], cq_ref[...], sq_ref[...], quarter) * (dh ** -0.5)
    kp = _rope_lanes(kp_ref[...], ckp_ref[...], skp_ref[...], quarter).astype(BF16)
    kc = _rope_lanes(kc_ref[...], ckc_ref[...], skc_ref[...], quarter).astype(BF16)
    kn = _rope_lanes(kn_ref[...], ckn_ref[...], skn_ref[...], quarter).astype(BF16)
    kz = kz_ref[...].astype(BF16)
    vp, vc, vn, vz = (r[...].astype(BF16) for r in (vp_ref, vc_ref, vn_ref, vz_ref))
    qi = lax.broadcasted_iota(jnp.int32, (group * blk, blk), 0) % blk
    kj = lax.broadcasted_iota(jnp.int32, (group * blk, blk), 1)
    ok_p = (kj >= qi) & (n > 0)
    ok_n = (kj <= qi) & (n < nb - 1)
    outs = []
    for hk in range(SWA_KV_HEADS):
        ks = slice(hk * dh, (hk + 1) * dh)
        qs = jnp.concatenate([q[:, (hk * group + g) * dh:(hk * group + g + 1) * dh] for g in range(group)],
                             axis=0).astype(BF16)
        sink = jnp.concatenate([jnp.full((blk, 1), 1.0, F32) * sink_ref[hk * group + g] for g in range(group)],
                               axis=0)
        s_p = jnp.where(ok_p, _dot_nt(qs, kp[:, ks]), NEG_INF)
        s_c = _dot_nt(qs, kc[:, ks])
        s_n = jnp.where(ok_n, _dot_nt(qs, kn[:, ks]), NEG_INF)
        s_z = _dot_nt(qs, kz[:, ks])
        m = jnp.maximum(jnp.maximum(jnp.max(s_p, axis=1, keepdims=True), jnp.max(s_c, axis=1, keepdims=True)),
                        jnp.maximum(jnp.max(s_n, axis=1, keepdims=True), jnp.max(s_z, axis=1, keepdims=True)))
        m = jnp.maximum(m, sink)
        e_p, e_c, e_n, e_z = (jnp.exp(s - m) for s in (s_p, s_c, s_n, s_z))
        den = (jnp.sum(e_p, axis=1, keepdims=True) + jnp.sum(e_c, axis=1, keepdims=True)
               + jnp.sum(e_n, axis=1, keepdims=True) + jnp.sum(e_z, axis=1, keepdims=True) + jnp.exp(sink - m))
        o = (jnp.dot(e_p.astype(BF16), vp[:, ks], preferred_element_type=F32)
             + jnp.dot(e_c.astype(BF16), vc[:, ks], preferred_element_type=F32)
             + jnp.dot(e_n.astype(BF16), vn[:, ks], preferred_element_type=F32)
             + jnp.dot(e_z.astype(BF16), vz[:, ks], preferred_element_type=F32)) / den
        outs += [o[g * blk:(g + 1) * blk, :] for g in range(group)]
    o_ref[...] = jnp.concatenate(outs, axis=1)


def swa_attention(px, pz, n_batch, seq, ctx_len, sink, rope_q, rope_k):
    blk = SWA_BLOCK
    nb = seq // blk
    prev = lambda b, n: b * nb + jnp.maximum(n - 1, 0)
    cur = lambda b, n: b * nb + n
    nxt = lambda b, n: b * nb + jnp.minimum(n + 1, nb - 1)
    kv = lambda off, f: pl.BlockSpec((blk, LANES), lambda b, n: (f(b, n), off // LANES))
    tab = lambda w, f: pl.BlockSpec((blk, w), lambda b, n: (f(0, n), 0))
    cq, sq = rope_q
    ck, sk = rope_k
    return pl.pallas_call(
        _swa_kernel,
        grid=(n_batch, nb),
        in_specs=[
            pl.BlockSpec((blk, GROUP_WIDTH), lambda b, n: (cur(b, n), COL_SWA_Q // GROUP_WIDTH)),
            kv(COL_SWA_K, prev), kv(COL_SWA_K, cur), kv(COL_SWA_K, nxt),
            kv(COL_SWA_V, prev), kv(COL_SWA_V, cur), kv(COL_SWA_V, nxt),
            pl.BlockSpec((ctx_len, LANES), lambda b, n: (b, COL_SWA_K // LANES)),
            pl.BlockSpec((ctx_len, LANES), lambda b, n: (b, COL_SWA_V // LANES)),
            pl.BlockSpec(memory_space=pltpu.SMEM),
            tab(GROUP_WIDTH, cur), tab(GROUP_WIDTH, cur), tab(LANES, prev), tab(LANES, prev),
            tab(LANES, cur), tab(LANES, cur), tab(LANES, nxt), tab(LANES, nxt),
        ],
        out_specs=pl.BlockSpec((blk, GROUP_WIDTH), lambda b, n: (cur(b, n), 0)),
        out_shape=jax.ShapeDtypeStruct((n_batch * seq, GROUP_WIDTH), F32),
        compiler_params=_params(2),
    )(px, px, px, px, px, px, px, pz, pz, sink, cq, sq, ck, sk, ck, sk, ck, sk)


def _ctx_attn_kernel(q_ref, k_ref, v_ref, sink_ref, o_ref, *, group, use_sink):
    dh = HEAD_DIM
    q = (q_ref[...] * (dh ** -0.5)).astype(BF16)
    k = k_ref[...].astype(BF16)
    v = v_ref[...].astype(BF16)
    outs = []
    for qh in range(q.shape[1] // dh):
        ks = slice((qh // group) * dh, (qh // group + 1) * dh)
        s = _dot_nt(q[:, qh * dh:(qh + 1) * dh], k[:, ks])
        m = jnp.max(s, axis=1, keepdims=True)
        if use_sink:
            m = jnp.maximum(m, sink_ref[qh])
        e = jnp.exp(s - m)
        den = jnp.sum(e, axis=1, keepdims=True)
        if use_sink:
            den = den + jnp.exp(sink_ref[qh] - m)
        outs.append(jnp.dot(e.astype(BF16), v[:, ks], preferred_element_type=F32) / den)
    o_ref[...] = jnp.concatenate(outs, axis=1)


def ctx_attention(pz, n_batch, ctx_len, cols, kv_width, sink):
    qc, kc, vc = cols
    use_sink = sink is not None
    if sink is None:
        sink = jnp.zeros((GROUP_WIDTH // HEAD_DIM,), F32)
    return pl.pallas_call(
        functools.partial(_ctx_attn_kernel, group=GROUP_WIDTH // kv_width, use_sink=use_sink),
        grid=(n_batch,),
        in_specs=[pl.BlockSpec((ctx_len, GROUP_WIDTH), lambda b: (b, qc // GROUP_WIDTH)),
                  pl.BlockSpec((ctx_len, kv_width), lambda b: (b, kc // kv_width)),
                  pl.BlockSpec((ctx_len, kv_width), lambda b: (b, vc // kv_width)),
                  pl.BlockSpec(memory_space=pltpu.SMEM)],
        out_specs=pl.BlockSpec((ctx_len, GROUP_WIDTH), lambda b: (b, 0)),
        out_shape=jax.ShapeDtypeStruct((n_batch * ctx_len, GROUP_WIDTH), F32),
        compiler_params=_params(1),
    )(pz, pz, pz, sink)


SCAN_ROWS = 256


def _split3(x):
    a = x.astype(BF16)
    r = x - a.astype(F32)
    b = r.astype(BF16)
    c = (r - b.astype(F32)).astype(BF16)
    return a, b, c


def _scan_kernel(*refs, heads, dk, dv, kind, rope, q_scale, k_scale):
    it = iter(refs)
    qf, kf, vf, qb, kb, vb = (next(it) for _ in range(6))
    if kind == "ret":
        lg = next(it)
    else:
        df, db, wup, bup = (next(it) for _ in range(4))
    if rope:
        cosf, sinf, cosb, sinb = (next(it) for _ in range(4))
    s0 = next(it)
    of, ob, sfin = next(it), next(it), next(it)
    st = next(it)

    s = pl.program_id(1)
    c = SCAN_CHUNK
    nch = SCAN_ROWS // c
    hk = heads * dk

    @pl.when(s == 0)
    def _():
        st[...] = s0[...]

    r_i = lax.broadcasted_iota(jnp.int32, (c, c), 0)
    c_i = lax.broadcasted_iota(jnp.int32, (c, c), 1)
    masks = (r_i >= c_i, c_i > r_i)

    if kind == "ret":
        pos = lax.broadcasted_iota(jnp.int32, (c, hk), 0).astype(F32)
        gcums = ((pos + 1.0) * lg[0:1, :], (float(c) - pos) * lg[1:2, :])
    else:
        rr = lax.broadcasted_iota(jnp.int32, (SCAN_ROWS, SCAN_ROWS), 0)
        cc = lax.broadcasted_iota(jnp.int32, (SCAN_ROWS, SCAN_ROWS), 1)
        same = (rr // c) == (cc // c)
        tris = (jnp.where(same & (rr >= cc), 1.0, 0.0).astype(BF16),
                jnp.where(same & (cc >= rr), 1.0, 0.0).astype(BF16))

        def gate_cum(d_ref, direction):
            pre = jnp.dot(d_ref[...].astype(BF16), wup[direction], preferred_element_type=F32) + bup[direction]
            g = -(jnp.maximum(-pre, 0.0) + jnp.log1p(jnp.exp(-jnp.abs(pre)))) / GLA_TAU
            return sum(jnp.dot(tris[direction], p, preferred_element_type=F32) for p in _split3(g))

        gcums = (gate_cum(df, 0), gate_cum(db, 1))

    def one(direction, q_ref, k_ref, v_ref, o_ref, cos_ref, sin_ref, ch):
        rows = pl.ds(ch * c, c)
        q = q_ref[rows, :]
        k = k_ref[rows, :]
        v = v_ref[rows, :]
        if rope:
            cs = jnp.concatenate([cos_ref[rows, :]] * heads, axis=1)
            sn = jnp.concatenate([sin_ref[rows, :]] * heads, axis=1)
            q = _rope_lanes(q, cs, sn, dk // 4)
            k = _rope_lanes(k, cs, sn, dk // 4)
        if q_scale != 1.0:
            q = q * q_scale
        if k_scale != 1.0:
            k = k * k_scale
        gcum = gcums[direction] if kind == "ret" else gcums[direction][ch * c:(ch + 1) * c, :]
        gtot = gcum[c - 1:c, :] if direction == 0 else gcum[0:1, :]
        q_rel = (q * jnp.exp(gcum - gtot)).astype(BF16)
        k_rel = (k * jnp.exp(gtot - gcum)).astype(BF16)
        q_dec = (q * jnp.exp(gcum)).astype(BF16)
        dec = jnp.exp(gtot)
        vb16 = v.astype(BF16)
        ks = [slice(hd * dk, (hd + 1) * dk) for hd in range(heads)]
        vs = [slice(hd * dv, (hd + 1) * dv) for hd in range(heads)]
        states = [st[direction, hd] for hd in range(heads)]
        a = [jnp.where(masks[direction], _dot_nt(q_rel[:, ks[hd]], k_rel[:, ks[hd]]), 0.0).astype(BF16)
             for hd in range(heads)]
        inter = [_dot_nt(q_dec[:, ks[hd]], states[hd].astype(BF16)) for hd in range(heads)]
        incs = [_dot_tn(vb16[:, vs[hd]], k_rel[:, ks[hd]]) for hd in range(heads)]
        outs = [jnp.dot(a[hd], vb16[:, vs[hd]], preferred_element_type=F32) + inter[hd] for hd in range(heads)]
        for hd in range(heads):
            st[direction, hd] = dec[:, ks[hd]] * states[hd] + incs[hd]
        o_ref[rows, :] = jnp.concatenate(outs, axis=1)

    for ch in range(nch):
        one(0, qf, kf, vf, of, cosf if rope else None, sinf if rope else None, ch)
        one(1, qb, kb, vb, ob, cosb if rope else None, sinb if rope else None, nch - 1 - ch)

    @pl.when(s == pl.num_programs(1) - 1)
    def _():
        sfin[...] = st[...]


def bidir_scan(p, n_batch, seq, cols, heads, dk, dv, kind, s0, *, lg=None, wup=None, bup=None,
               rope=None, q_scale=1.0, k_scale=1.0):
    t = SCAN_ROWS
    nblk = seq // t
    hk, hv = heads * dk, heads * dv
    qc, kc, vc = cols
    fwd = lambda w, off: pl.BlockSpec((t, w), lambda b, s: (b * nblk + s, off // w))
    bwd = lambda w, off: pl.BlockSpec((t, w), lambda b, s: (b * nblk + nblk - 1 - s, off // w))
    const = lambda shape: pl.BlockSpec(shape, lambda b, s: (0,) * len(shape))
    args = [p] * 6
    specs = [fwd(hk, qc), fwd(hk, kc), fwd(hv, vc), bwd(hk, qc), bwd(hk, kc), bwd(hv, vc)]
    if kind == "ret":
        args += [lg]
        specs += [const((2, hk))]
    else:
        args += [p, p, wup, bup]
        specs += [fwd(LANES, COL_GLA_D), bwd(LANES, COL_GLA_D), const(wup.shape), const(bup.shape)]
    if rope is not None:
        cos, sin = rope
        args += [cos, sin, cos, sin]
        specs += [pl.BlockSpec((t, dk), lambda b, s: (s, 0)), pl.BlockSpec((t, dk), lambda b, s: (s, 0)),
                  pl.BlockSpec((t, dk), lambda b, s: (nblk - 1 - s, 0)),
                  pl.BlockSpec((t, dk), lambda b, s: (nblk - 1 - s, 0))]
    args += [s0]
    state_spec = pl.BlockSpec((None, 2, heads, dv, dk), lambda b, s: (b, 0, 0, 0, 0))
    specs += [state_spec]
    n = n_batch * seq
    kern = functools.partial(_scan_kernel, heads=heads, dk=dk, dv=dv, kind=kind, rope=rope is not None,
                             q_scale=q_scale, k_scale=k_scale)
    return pl.pallas_call(
        kern,
        grid=(n_batch, nblk),
        in_specs=specs,
        out_specs=[pl.BlockSpec((t, hv), lambda b, s: (b * nblk + s, 0)),
                   pl.BlockSpec((t, hv), lambda b, s: (b * nblk + nblk - 1 - s, 0)),
                   state_spec],
        out_shape=[jax.ShapeDtypeStruct((n, hv), F32), jax.ShapeDtypeStruct((n, hv), F32),
                   jax.ShapeDtypeStruct((n_batch, 2, heads, dv, dk), F32)],
        scratch_shapes=[pltpu.VMEM((2, heads, dv, dk), F32)],
        compiler_params=_params(2),
    )(*args)


def _outproj_kernel(na_ref, rf_ref, rb_ref, rg_ref, gf_ref, gb_ref, gg_ref, sw_ref, gn_ref, w_ref, x_ref, mg_ref,
                    o_ref, *, head_w):
    ry = rf_ref[...] + rb_ref[...]
    gy = gf_ref[...] + gb_ref[...]
    r_out, g_out = [], []
    for hd in range(ry.shape[1] // head_w):
        sl = slice(hd * head_w, (hd + 1) * head_w)
        r = ry[:, sl]
        mu = jnp.mean(r, axis=-1, keepdims=True)
        var = jnp.mean(jnp.square(r - mu), axis=-1, keepdims=True)
        r_out.append((r - mu) * lax.rsqrt(var + EPS))
        gq = gy[:, sl]
        g_out.append(gq * lax.rsqrt(jnp.mean(gq * gq, axis=-1, keepdims=True) + EPS) * gn_ref[...])
    ret = jnp.concatenate(r_out, axis=1) * _silu(rg_ref[...])
    gla = jnp.concatenate(g_out, axis=1) * _silu(gg_ref[...])
    mix = jnp.concatenate([na_ref[...], ret, gla, sw_ref[...]], axis=1).astype(BF16)
    o_ref[...] = x_ref[...] + mg_ref[...] * jnp.dot(mix, w_ref[...], preferred_element_type=F32)


def outproj(na, rf, rb, gf, gb, sw, p, gla_norm_g, w_out, x2, mg, rows_per_mod):
    n, d = x2.shape
    gw = GROUP_WIDTH
    tm = 256
    m = mg.shape[0]
    row = lambda w_: pl.BlockSpec((tm, w_), lambda i: (i, 0))
    return pl.pallas_call(
        functools.partial(_outproj_kernel, head_w=RET_DV),
        grid=(n // tm,),
        in_specs=[row(gw), row(gw), row(gw), pl.BlockSpec((tm, gw), lambda i: (i, COL_RET_G // gw)),
                  row(gw), row(gw), pl.BlockSpec((tm, gw), lambda i: (i, COL_GLA_G // gw)), row(gw),
                  pl.BlockSpec((1, GLA_DV), lambda i: (0, 0)),
                  pl.BlockSpec((d, d), lambda i: (0, 0)), row(d),
                  pl.BlockSpec((None, 1, d), lambda i: (i // (rows_per_mod // tm), 0, 0))],
        out_specs=row(d),
        out_shape=jax.ShapeDtypeStruct((n, d), F32),
        compiler_params=_params(1),
    )(na, rf, rb, p, gf, gb, p, sw, gla_norm_g.reshape(1, GLA_DV), w_out, x2, mg.reshape(m, 1, d))


PEER_PAIRS = PEER_HEADS * PEER_TOPK
PEER_ROUTE_TOKENS = 128
PEER_EXPERT_TOKENS = 8
PEER_WAIT_GROUP = 4
PEER_SUB = 8
PEER_FOLD = D_MODEL // PEER_SUB
INV_SQRT2 = 0.7071067811865476


def _topk_cols(s, payload=None, order=None):
    row = lax.broadcasted_iota(jnp.int32, s.shape, 0) if order is None else order
    vals, idxs = [], []
    for _ in range(PEER_TOPK):
        m = jnp.max(s, axis=0, keepdims=True)
        am = jnp.min(jnp.where(s == m, row, jnp.iinfo(jnp.int32).max), axis=0, keepdims=True)
        sel = row == am
        vals.append(m)
        idxs.append(am if payload is None else jnp.max(jnp.where(sel, payload, -1), axis=0, keepdims=True))
        s = jnp.where(sel, -jnp.inf, s)
    return jnp.concatenate(vals, axis=0), jnp.concatenate(idxs, axis=0)


def _staircase_candidates(v0, i0, v1, i1):
    k = PEER_TOPK
    assert k == 16
    t = v0.shape[1]
    r8 = lax.broadcasted_iota(jnp.int32, (8, t), 0)
    r16 = lax.broadcasted_iota(jnp.int32, (k, t), 0)

    def piece(a_sl, b_sl):
        return v0[a_sl, :] + v1[b_sl, :], i0[a_sl, :] * PEER_N_KEYS + i1[b_sl, :]

    one = lambda j: slice(j, j + 1)
    lo = slice(0, 8)
    pieces = [
        (one(0), slice(0, k), None, r16),
        (one(1), lo, None, k + r8),
        (one(2), lo, r8 <= 4, 2 * k + r8),
        (one(3), lo, r8 <= 3, 3 * k + r8),
        (slice(8, k), one(0), None, (r8 + 8) * k),
        (lo, one(0), r8 >= 4, r8 * k),
        (lo, one(1), r8 >= 4, r8 * k + 1),
        (lo, one(2), r8 == 4, r8 * k + 2),
    ]
    sums, ids, orders = [], [], []
    for a_sl, b_sl, keep, order in pieces:
        s, e = piece(a_sl, b_sl)
        sums.append(s if keep is None else jnp.where(keep, s, NEG_INF))
        ids.append(e)
        orders.append(order)
    return jnp.concatenate(sums, axis=0), jnp.concatenate(ids, axis=0), jnp.concatenate(orders, axis=0)


def _route_head(hb, wq_ref, sk_ref, hd):
    half = PEER_DK // 2
    q = jnp.dot(hb, wq_ref[hd], preferred_element_type=F32)
    tops = []
    for p in range(2):
        qp = q[:, p * half:(p + 1) * half].astype(BF16)
        tops.append(_topk_cols(_dot_nt(sk_ref[p, hd], qp)))
    (v0, i0), (v1, i1) = tops
    best_s, best_e = _topk_cols(*_staircase_candidates(v0, i0, v1, i1))
    e = jnp.exp(best_s - best_s[0:1, :])
    return e / jnp.sum(e, axis=0, keepdims=True), best_e


def _peer_route_kernel(x_ref, g_ref, shift_ref, scale_ref, wq_ref, sk_ref, h_ref, idx_ref, gate_ref):
    x = x_ref[...]
    y = x * lax.rsqrt(jnp.mean(x * x, axis=-1, keepdims=True) + EPS)
    h = (y * g_ref[...]) * (1.0 + scale_ref[...]) + shift_ref[...]
    h_ref[...] = h
    hb = h.astype(BF16)

    for hd in range(PEER_HEADS):
        gate_ref[hd], idx_ref[hd] = _route_head(hb, wq_ref, sk_ref, hd)


def peer_route(x2, g, shift, scale, rows_per_mod, wq_h, sk):
    n, d = x2.shape
    t = PEER_ROUTE_TOKENS
    m = shift.shape[0]
    mod_map = lambda i: (i // (rows_per_mod // t), 0, 0)
    return pl.pallas_call(
        _peer_route_kernel,
        grid=(n // t,),
        in_specs=[
            pl.BlockSpec((t, d), lambda i: (i, 0)),
            pl.BlockSpec((1, d), lambda i: (0, 0)),
            pl.BlockSpec((None, 1, d), mod_map),
            pl.BlockSpec((None, 1, d), mod_map),
            pl.BlockSpec((PEER_HEADS, d, PEER_DK), lambda i: (0, 0, 0)),
            pl.BlockSpec((2, PEER_HEADS, PEER_N_KEYS, PEER_DK // 2), lambda i: (0, 0, 0, 0)),
        ],
        out_specs=[
            pl.BlockSpec((t, d), lambda i: (i, 0)),
            pl.BlockSpec((PEER_HEADS, PEER_TOPK, t), lambda i: (0, 0, i)),
            pl.BlockSpec((PEER_HEADS, PEER_TOPK, t), lambda i: (0, 0, i)),
        ],
        out_shape=[
            jax.ShapeDtypeStruct((n, d), F32),
            jax.ShapeDtypeStruct((PEER_HEADS, PEER_TOPK, n), jnp.int32),
            jax.ShapeDtypeStruct((PEER_HEADS, PEER_TOPK, n), F32),
        ],
        compiler_params=_params(1),
    )(x2, g.reshape(1, d), shift.reshape(m, 1, d), scale.reshape(m, 1, d), wq_h, sk)


def pack_experts(u, v):
    def bf16_bits(a):
        b = lax.bitcast_convert_type(a, jnp.uint32)
        rounded = (b + jnp.uint32(0x7FFF) + ((b >> 16) & jnp.uint32(1))) >> 16
        is_nan = (b & jnp.uint32(0x7FFFFFFF)) > jnp.uint32(0x7F800000)
        return jnp.where(is_nan, (b >> 16) | jnp.uint32(0x40), rounded)

    words = lax.bitcast_convert_type((bf16_bits(u) << 16) | bf16_bits(v), jnp.int32)
    return words.reshape(u.shape[0], PEER_SUB, PEER_FOLD)


def _fold_rows(hbuf, h_rows, tb):
    for s in range(PEER_SUB):
        for c in range(PEER_FOLD // LANES):
            lo = s * PEER_FOLD + c * LANES
            hbuf[c, pl.ds(s * tb, tb), :] = h_rows[:, lo:lo + LANES]


def _unfold_residual(o_ref, x_ref, og_ref, ybuf, tb):
    for s in range(PEER_SUB):
        for c in range(PEER_FOLD // LANES):
            sl = slice(s * PEER_FOLD + c * LANES, s * PEER_FOLD + (c + 1) * LANES)
            o_ref[:, sl] = x_ref[:, sl] + og_ref[:, sl] * ybuf[c, pl.ds(s, tb, stride=PEER_SUB), :]


def _expert_weights(j, buf, hbuf, pbuf, wrow, tb, gate_column):
    fold = PEER_FOLD
    hj = jnp.concatenate([hbuf[c, pl.ds(j, PEER_SUB, stride=tb), :] for c in range(fold // LANES)], axis=1)
    for p in range(PEER_PAIRS):
        u = lax.bitcast_convert_type(buf[j, p] & jnp.int32(-65536), F32)
        prod = u * hj
        pbuf[pl.ds(p * PEER_SUB, PEER_SUB), :] = prod[:, :LANES] + prod[:, LANES:]
    part = pbuf[pl.ds(0, PEER_PAIRS, stride=PEER_SUB), :]
    for s in range(1, PEER_SUB):
        part = part + pbuf[pl.ds(s, PEER_PAIRS, stride=PEER_SUB), :]
    sc = jnp.sum(part, axis=1, keepdims=True)
    act = 0.5 * sc * (1.0 + lax.erf(sc * INV_SQRT2))
    g = jnp.concatenate([gate_column(hd) for hd in range(PEER_HEADS)], axis=0)
    wrow[...] = jnp.broadcast_to(g * act, (PEER_PAIRS, LANES))


def _expert_mix(j, buf, wrow, ybuf, start_fetch):
    fold = PEER_FOLD
    accs = [jnp.zeros((PEER_SUB, fold), F32) for _ in range(4)]
    for p in range(PEER_PAIRS):
        if start_fetch is not None:
            start_fetch(p)
        v = lax.bitcast_convert_type(buf[j, p] << 16, F32)
        wp = jnp.broadcast_to(wrow[p:p + 1, :], (PEER_SUB, LANES))
        accs[p % 4] = accs[p % 4] + v * jnp.concatenate([wp, wp], axis=1)
    yj = (accs[0] + accs[1]) + (accs[2] + accs[3])
    for c in range(fold // LANES):
        ybuf[c, pl.ds(j * PEER_SUB, PEER_SUB), :] = yj[:, c * LANES:(c + 1) * LANES]


def _peer_expert_kernel(idx0_ref, idxn_ref, h_ref, gate_ref, x_ref, og_ref, uv_hbm, o_ref,
                        buf_even, buf_odd, sem, hbuf, pbuf, wbuf, ybuf, *, n_blocks):
    i = pl.program_id(0)
    n = n_blocks
    tb = PEER_EXPERT_TOKENS
    bufs = (buf_even, buf_odd)

    def slab_copy(idx_ref, j, r, parity):
        e = idx_ref[0, 0, j * PEER_PAIRS + r]
        return pltpu.make_async_copy(uv_hbm.at[e], bufs[parity].at[j, r], sem.at[parity, j])

    def wait_token(j, parity):
        pltpu.make_async_copy(uv_hbm.at[pl.ds(0, PEER_PAIRS)], bufs[parity].at[j], sem.at[parity, j]).wait()

    @pl.when(i == 0)
    def _():
        def prime(j, carry):
            for r in range(PEER_PAIRS):
                slab_copy(idx0_ref, j, r, 0).start(priority=r % 2)
            return carry
        lax.fori_loop(0, tb, prime, 0)

    lane = lax.broadcasted_iota(jnp.int32, (PEER_TOPK, PEER_ROUTE_TOKENS), 1)
    lane0 = (i % (PEER_ROUTE_TOKENS // tb)) * tb

    _fold_rows(hbuf, h_ref[...], tb)

    def fetcher(j, parity, prefetch):
        return (lambda r: slab_copy(idxn_ref, j, r, 1 - parity).start(priority=r % 2)) if prefetch else None

    def gate_column_of(j):
        return lambda hd: jnp.sum(jnp.where(lane == lane0 + j, gate_ref[hd], 0.0), axis=1, keepdims=True)

    def block(parity, prefetch):
        for j in range(tb):
            if j % PEER_WAIT_GROUP == 0:
                for jj in range(j, j + PEER_WAIT_GROUP):
                    wait_token(jj, parity)
            _expert_weights(j, bufs[parity], hbuf, pbuf, wbuf, tb, gate_column_of(j))
            _expert_mix(j, bufs[parity], wbuf, ybuf, fetcher(j, parity, prefetch))

    for parity in range(2):
        for prefetch in (True, False):
            @pl.when((i % 2 == parity) & ((i + 1 < n) == prefetch))
            def _(parity=parity, prefetch=prefetch):
                block(parity, prefetch)

    _unfold_residual(o_ref, x_ref, og_ref, ybuf, tb)


def peer_expert(h, idx, gate, x2, out_gate, rows_per_mod, uv):
    n, d = h.shape
    tb = PEER_EXPERT_TOKENS
    nb = n // tb
    m = out_gate.shape[0]
    rows = tb * PEER_PAIRS
    idx_rows = idx.reshape(PEER_PAIRS, n).T.reshape(nb, 1, rows)
    gate_blocks = PEER_ROUTE_TOKENS // tb
    return pl.pallas_call(
        functools.partial(_peer_expert_kernel, n_blocks=nb),
        grid=(nb,),
        in_specs=[
            pl.BlockSpec((1, 1, rows), lambda i: (0, 0, 0), memory_space=pltpu.SMEM),
            pl.BlockSpec((1, 1, rows), lambda i: (jnp.minimum(i + 1, nb - 1), 0, 0), memory_space=pltpu.SMEM),
            pl.BlockSpec((tb, d), lambda i: (i, 0)),
            pl.BlockSpec((PEER_HEADS, PEER_TOPK, PEER_ROUTE_TOKENS), lambda i: (0, 0, i // gate_blocks)),
            pl.BlockSpec((tb, d), lambda i: (i, 0)),
            pl.BlockSpec((None, 1, d), lambda i: (i // (rows_per_mod // tb), 0, 0)),
            pl.BlockSpec(memory_space=pl.ANY),
        ],
        out_specs=pl.BlockSpec((tb, d), lambda i: (i, 0)),
        out_shape=jax.ShapeDtypeStruct((n, d), F32),
        scratch_shapes=[pltpu.VMEM((tb, PEER_PAIRS, PEER_SUB, PEER_FOLD), jnp.int32),
                        pltpu.VMEM((tb, PEER_PAIRS, PEER_SUB, PEER_FOLD), jnp.int32),
                        pltpu.SemaphoreType.DMA((2, tb)),
                        pltpu.VMEM((PEER_FOLD // LANES, PEER_SUB * tb, LANES), F32),
                        pltpu.VMEM((PEER_PAIRS * PEER_SUB, LANES), F32),
                        pltpu.VMEM((PEER_PAIRS, LANES), F32),
                        pltpu.VMEM((PEER_FOLD // LANES, tb * PEER_SUB, LANES), F32)],
        compiler_params=pltpu.CompilerParams(dimension_semantics=("arbitrary",), vmem_limit_bytes=VMEM_LIMIT,
                                             disable_bounds_checks=True),
    )(idx_rows, idx_rows, h, gate, x2, out_gate.reshape(m, 1, d), uv)


def peer_residual(x2, g, shift, scale, out_gate, rows_per_mod, wq_h, sk, uv):
    h, idx, gate = peer_route(x2, g, shift, scale, rows_per_mod, wq_h, sk)
    return peer_expert(h, idx, gate, x2, out_gate, rows_per_mod, uv)


def _final_norm_kernel(x_ref, g_ref, o_ref):
    x = x_ref[...]
    y = x * lax.rsqrt(jnp.mean(x * x, axis=-1, keepdims=True) + EPS)
    o_ref[...] = y * g_ref[...]


def final_norm(x2, g):
    n, d = x2.shape
    tm = 512
    return pl.pallas_call(
        _final_norm_kernel,
        grid=(n // tm,),
        in_specs=[pl.BlockSpec((tm, d), lambda i: (i, 0)), pl.BlockSpec((1, d), lambda i: (0, 0))],
        out_specs=pl.BlockSpec((tm, d), lambda i: (i, 0)),
        out_shape=jax.ShapeDtypeStruct((n, d), x2.dtype),
        compiler_params=_params(1),
    )(x2, g.reshape(1, d))


def _mixers(p, pz, n_batch, seq, ctx_len, is_ctx, tables, prm):
    if is_ctx:
        na = ctx_attention(p, n_batch, seq, (COL_NA_Q, COL_NA_K, COL_NA_V), GROUP_WIDTH, None)
        sw = ctx_attention(p, n_batch, seq, (COL_SWA_Q, COL_SWA_K, COL_SWA_V), SWA_KV_HEADS * HEAD_DIM, prm["sink"])
    else:
        na = na_attention(p, pz, n_batch, seq, ctx_len, prm["na_bias"])
        sw = swa_attention(p, pz, n_batch, seq, ctx_len, prm["sink"], tables["swa_q"], tables["swa_k"])
    rf, rb, rs = bidir_scan(p, n_batch, seq, (COL_RET_Q, COL_RET_K, COL_RET_V), RET_HEADS, RET_DK, RET_DV, "ret",
                            prm["ret_s0"], lg=prm["ret_lg"], rope=None if is_ctx else tables["ret"],
                            k_scale=RET_DK ** -0.5)
    gf, gb, gs = bidir_scan(p, n_batch, seq, (COL_GLA_Q, COL_GLA_K, COL_GLA_V), GLA_HEADS, GLA_DK, GLA_DV, "gla",
                            prm["gla_s0"], wup=prm["gla_wup"], bup=prm["gla_bup"], q_scale=GLA_DK ** -0.5)
    return (na, rf, rb, gf, gb, sw), (rs, gs)


def kernel(x, c, ctx, c_ctx, w_ada, b_ada, norm_attn_g, norm_ffn_g, w_in, na_rpb, ret_log_gamma,
           gla_w_gate_up, gla_b_gate, gla_norm_g, swa_sink, w_out, peer_w_q, peer_sub_keys,
           peer_u, peer_v, final_g):
    bsz, slen, d = x.shape
    zlen = ctx.shape[1]
    x2 = x.reshape(bsz * slen, d)
    z2 = ctx.reshape(bsz * zlen, d)
    tables = {"ret": rope_lane_tables(slen, RET_DK, 1),
              "swa_q": rope_lane_tables(slen, HEAD_DIM, SWA_HEADS),
              "swa_k": rope_lane_tables(slen, HEAD_DIM, SWA_KV_HEADS)}
    c_rows = jnp.zeros((8, d), F32).at[:bsz].set(c).at[bsz].set(c_ctx)
    for layer in range(DEPTH):
        has_next = layer < DEPTH - 1
        mod = adaln(c_rows, w_ada[layer], b_ada[layer])
        mx = [mod[:bsz, k * d:(k + 1) * d] for k in range(6)]
        mz = [mod[bsz:bsz + 1, k * d:(k + 1) * d] for k in range(6)]

        wi = w_in[layer]
        wp = jnp.concatenate([wi[:, :REF_COL_GLA_D], wi[:, REF_COL_SWA_Q:], wi[:, REF_COL_GLA_D:REF_COL_SWA_Q],
                              jnp.zeros((d, PROJ_WIDTH - REF_D_IN), F32)], axis=1).astype(BF16)
        px = modproj(x2, norm_attn_g[layer], mx[0], mx[1], slen, wp)
        pz = modproj(z2, norm_attn_g[layer], mz[0], mz[1], bsz * zlen, wp)

        wup = (jnp.zeros((2, LANES, GLA_HEADS * GLA_DK), F32)
               .at[0, :GLA_RANK].set(gla_w_gate_up[layer, 0])
               .at[1, GLA_RANK:2 * GLA_RANK].set(gla_w_gate_up[layer, 1])).astype(BF16)
        prm = {"na_bias": na_band_bias(na_rpb[layer]), "sink": swa_sink[layer],
               "ret_lg": jnp.repeat(ret_log_gamma[layer], RET_DK, axis=1),
               "gla_wup": wup, "gla_bup": gla_b_gate[layer].reshape(2, 1, GLA_HEADS * GLA_DK),
               "ret_s0": jnp.zeros((bsz, 2, RET_HEADS, RET_DV, RET_DK), F32),
               "gla_s0": jnp.zeros((bsz, 2, GLA_HEADS, GLA_DV, GLA_DK), F32)}
        mix_z, (ret_s, gla_s) = _mixers(pz, pz, bsz, zlen, zlen, True, tables, prm)
        prm["ret_s0"], prm["gla_s0"] = ret_s, gla_s
        mix_x, _ = _mixers(px, pz, bsz, slen, zlen, False, tables, prm)

        wo = w_out[layer].astype(BF16)
        x2 = outproj(*mix_x, px, gla_norm_g[layer], wo, x2, mx[2], slen)

        uv = pack_experts(peer_u[layer], peer_v[layer])
        wq_h = peer_w_q[layer].reshape(d, PEER_HEADS, PEER_DK).transpose(1, 0, 2).astype(BF16)
        sk = peer_sub_keys[layer].astype(BF16)
        x2 = peer_residual(x2, norm_ffn_g[layer], mx[3], mx[4], mx[5], slen, wq_h, sk, uv)
        if has_next:
            z2 = outproj(*mix_z, pz, gla_norm_g[layer], wo, z2, mz[2], bsz * zlen)
            z2 = peer_residual(z2, norm_ffn_g[layer], mz[3], mz[4], mz[5], bsz * zlen, wq_h, sk, uv)
    return final_norm(x2, final_g).reshape(bsz, slen, d)
```

```python
import functools

import jax
import jax.numpy as jnp
import numpy as np
from jax import lax
from jax.experimental import pallas as pl
from jax.experimental.pallas import tpu as pltpu

D_MODEL = 2048
DEPTH = 2
GRID_W = 64
EPS = 1e-6
ROPE_BASE = 10000.0

GROUP_WIDTH = D_MODEL // 4
HEAD_DIM = 64
NA_HEADS = GROUP_WIDTH // HEAD_DIM
NA_ROWS = 8
NA_COLS = 16
RET_HEADS = 4
RET_DK = GROUP_WIDTH // RET_HEADS
RET_DV = GROUP_WIDTH // RET_HEADS
GLA_HEADS = 4
GLA_DV = GROUP_WIDTH // GLA_HEADS
GLA_DK = GLA_DV // 2
GLA_RANK = 16
GLA_TAU = 16.0
SWA_HEADS = GROUP_WIDTH // HEAD_DIM
SWA_KV_HEADS = SWA_HEADS // 4
SWA_WINDOW = 128
SWA_BLOCK = 128
SCAN_CHUNK = 64
PEER_HEADS = 8
PEER_N_KEYS = 128
PEER_N_EXPERTS = PEER_N_KEYS * PEER_N_KEYS
PEER_DK = 256
PEER_TOPK = 16

LANES = 128
VMEM_LIMIT = 48 * 1024 * 1024
BF16 = jnp.bfloat16
F32 = jnp.float32
NEG_INF = float("-inf")

COL_NA_Q, COL_NA_K, COL_NA_V = 0, 512, 1024
COL_RET_Q, COL_RET_K, COL_RET_V, COL_RET_G = 1536, 2048, 2560, 3072
COL_GLA_Q, COL_GLA_K, COL_GLA_V, COL_GLA_G = 3584, 3840, 4096, 4608
COL_SWA_Q, COL_SWA_K, COL_SWA_V = 5120, 5632, 5760
COL_GLA_D = 5888
REF_COL_GLA_D, REF_COL_SWA_Q, REF_D_IN = 5120, 5152, 5920
PROJ_WIDTH = 6144


def _silu(x):
    return x / (1.0 + jnp.exp(-x))


def _dot_nt(a, b):
    return lax.dot_general(a, b, (((1,), (1,)), ((), ())), preferred_element_type=F32)


def _dot_tn(a, b):
    return lax.dot_general(a, b, (((0,), (0,)), ((), ())), preferred_element_type=F32)


def _params(n_axes):
    return pltpu.CompilerParams(dimension_semantics=("arbitrary",) * n_axes, vmem_limit_bytes=VMEM_LIMIT)


def _rope_lanes(x, cs, sn, quarter):
    n = x.shape[-1]
    lane = lax.broadcasted_iota(jnp.int32, x.shape, x.ndim - 1)
    first = (lane % (2 * quarter)) < quarter
    swapped = jnp.where(first, pltpu.roll(x, n - quarter, x.ndim - 1), pltpu.roll(x, quarter, x.ndim - 1))
    return x * cs + swapped * sn


def rope_lane_tables(length, dh, copies):
    t = jnp.arange(length)
    pos = jnp.stack([t // GRID_W, t % GRID_W], axis=-1).astype(F32)
    quarter = dh // 4
    inv = ROPE_BASE ** (-jnp.arange(quarter, dtype=F32) / quarter)
    ang = pos[:, :, None] * inv
    cos, sin = jnp.cos(ang), jnp.sin(ang)
    cl = jnp.concatenate([cos[:, 0], cos[:, 0], cos[:, 1], cos[:, 1]], axis=-1)
    sl = jnp.concatenate([-sin[:, 0], sin[:, 0], -sin[:, 1], sin[:, 1]], axis=-1)
    return jnp.tile(cl, (1, copies)), jnp.tile(sl, (1, copies))


def _adaln_kernel(c_ref, w_ref, b_ref, o_ref):
    a = _silu(c_ref[...]).astype(BF16)
    o_ref[...] = jnp.dot(a, w_ref[...].astype(BF16), preferred_element_type=F32) + b_ref[...]


def adaln(c_rows, w, b):
    r, d = c_rows.shape
    m = w.shape[1]
    tn = 1024
    return pl.pallas_call(
        _adaln_kernel,
        grid=(m // tn,),
        in_specs=[pl.BlockSpec((r, d), lambda j: (0, 0)), pl.BlockSpec((d, tn), lambda j: (0, j)),
                  pl.BlockSpec((1, tn), lambda j: (0, j))],
        out_specs=pl.BlockSpec((r, tn), lambda j: (0, j)),
        out_shape=jax.ShapeDtypeStruct((r, m), F32),
        compiler_params=_params(1),
    )(c_rows, w, b.reshape(1, m))


def _modproj_kernel(x_ref, g_ref, shift_ref, scale_ref, w_ref, o_ref, hb_ref):
    @pl.when(pl.program_id(1) == 0)
    def _():
        x = x_ref[...]
        y = x * lax.rsqrt(jnp.mean(x * x, axis=-1, keepdims=True) + EPS)
        hb_ref[...] = ((y * g_ref[...]) * (1.0 + scale_ref[...]) + shift_ref[...]).astype(BF16)

    o_ref[...] = jnp.dot(hb_ref[...], w_ref[...], preferred_element_type=F32)


def modproj(x2, g, shift, scale, rows_per_mod, w):
    n, d = x2.shape
    wid = w.shape[1]
    tm = min(512, rows_per_mod)
    tn = 2048
    m = shift.shape[0]
    mod_map = lambda i, j: (i // (rows_per_mod // tm), 0, 0)
    return pl.pallas_call(
        _modproj_kernel,
        grid=(n // tm, wid // tn),
        in_specs=[pl.BlockSpec((tm, d), lambda i, j: (i, 0)), pl.BlockSpec((1, d), lambda i, j: (0, 0)),
                  pl.BlockSpec((None, 1, d), mod_map), pl.BlockSpec((None, 1, d), mod_map),
                  pl.BlockSpec((d, tn), lambda i, j: (0, j))],
        out_specs=pl.BlockSpec((tm, tn), lambda i, j: (i, j)),
        out_shape=jax.ShapeDtypeStruct((n, wid), F32),
        scratch_shapes=[pltpu.VMEM((tm, d), BF16)],
        compiler_params=_params(2),
    )(x2, g.reshape(1, d), shift.reshape(m, 1, d), scale.reshape(m, 1, d), w)


NA_QROWS = 8


def _na_kernel(q_ref, k_ref, v_ref, kz_ref, vz_ref, bias_ref, o_ref, *, rows):
    step = pl.program_id(2)
    dh = HEAD_DIM
    band = NA_ROWS * GRID_W
    kz = kz_ref[...].astype(BF16)
    vz = vz_ref[...].astype(BF16)
    heads = LANES // dh
    units = [(qr, hh) for qr in range(NA_QROWS) for hh in range(heads)]
    scores, vbands = {}, {}
    for qr in range(NA_QROWS):
        r = step * NA_QROWS + qr
        start = jnp.clip(r - NA_ROWS // 2, 0, rows - NA_ROWS)
        dr0 = start - r + NA_ROWS - 1
        tok0 = pl.multiple_of(start * GRID_W, GRID_W)
        kb = k_ref[pl.ds(tok0, band), :].astype(BF16)
        vbands[qr] = v_ref[pl.ds(tok0, band), :].astype(BF16)
        q = (q_ref[pl.ds(qr * GRID_W, GRID_W), :] * (dh ** -0.5)).astype(BF16)
        for hh in range(heads):
            sl = slice(hh * dh, (hh + 1) * dh)
            scores[qr, hh] = (_dot_nt(q[:, sl], kb[:, sl]) + bias_ref[hh, dr0],
                              _dot_nt(q[:, sl], kz[:, sl]))
    probs = {}
    for u in units:
        s_nb, s_cx = scores[u]
        m = jnp.maximum(jnp.max(s_nb, axis=1, keepdims=True), jnp.max(s_cx, axis=1, keepdims=True))
        p_nb = jnp.exp(s_nb - m)
        p_cx = jnp.exp(s_cx - m)
        den = jnp.sum(p_nb, axis=1, keepdims=True) + jnp.sum(p_cx, axis=1, keepdims=True)
        probs[u] = (p_nb.astype(BF16), p_cx.astype(BF16), den)
    for qr in range(NA_QROWS):
        outs = []
        for hh in range(heads):
            sl = slice(hh * dh, (hh + 1) * dh)
            p_nb, p_cx, den = probs[qr, hh]
            o = (jnp.dot(p_nb, vbands[qr][:, sl], preferred_element_type=F32)
                 + jnp.dot(p_cx, vz[:, sl], preferred_element_type=F32))
            outs.append(o / den)
        o_ref[pl.ds(qr * GRID_W, GRID_W), :] = jnp.concatenate(outs, axis=1)


def na_band_bias(rpb):
    col = jnp.arange(GRID_W)
    col_start = jnp.clip(col - NA_COLS // 2, 0, GRID_W - NA_COLS)
    col_ok = (col[None, :] >= col_start[:, None]) & (col[None, :] < col_start[:, None] + NA_COLS)
    d_col = jnp.clip(col[None, :] - col[:, None], -(NA_COLS - 1), NA_COLS - 1) + NA_COLS - 1
    d_row = jnp.arange(NA_ROWS)[:, None] + jnp.arange(NA_ROWS)[None, :]
    b = rpb.astype(F32)[:, d_row][..., d_col]
    b = jnp.where(col_ok[None, None, None], b, NEG_INF)
    return b.transpose(0, 1, 3, 2, 4).reshape(rpb.shape[0], NA_ROWS, GRID_W, NA_ROWS * GRID_W)


def na_attention(px, pz, n_batch, seq, ctx_len, bias):
    rows = seq // GRID_W
    tq = NA_QROWS * GRID_W
    nsteps = rows // NA_QROWS
    heads_per_blk = LANES // HEAD_DIM
    return pl.pallas_call(
        functools.partial(_na_kernel, rows=rows),
        grid=(n_batch, NA_HEADS // heads_per_blk, nsteps),
        in_specs=[
            pl.BlockSpec((tq, LANES), lambda b, hp, s: (b * nsteps + s, COL_NA_Q // LANES + hp)),
            pl.BlockSpec((seq, LANES), lambda b, hp, s: (b, COL_NA_K // LANES + hp)),
            pl.BlockSpec((seq, LANES), lambda b, hp, s: (b, COL_NA_V // LANES + hp)),
            pl.BlockSpec((ctx_len, LANES), lambda b, hp, s: (b, COL_NA_K // LANES + hp)),
            pl.BlockSpec((ctx_len, LANES), lambda b, hp, s: (b, COL_NA_V // LANES + hp)),
            pl.BlockSpec((heads_per_blk, NA_ROWS, GRID_W, NA_ROWS * GRID_W), lambda b, hp, s: (hp, 0, 0, 0)),
        ],
        out_specs=pl.BlockSpec((tq, LANES), lambda b, hp, s: (b * nsteps + s, hp)),
        out_shape=jax.ShapeDtypeStruct((n_batch * seq, GROUP_WIDTH), F32),
        compiler_params=_params(3),
    )(px, px, px, pz, pz, bias)


def _swa_kernel(q_ref, kp_ref, kc_ref, kn_ref, vp_ref, vc_ref, vn_ref, kz_ref, vz_ref, sink_ref,
                cq_ref, sq_ref, ckp_ref, skp_ref, ckc_ref, skc_ref, ckn_ref, skn_ref, o_ref):
    n = pl.program_id(1)
    nb = pl.num_programs(1)
    dh = HEAD_DIM
    blk = SWA_BLOCK
    quarter = dh // 4
    group = SWA_HEADS // SWA_KV_HEADS
    q = _rope_lanes(q_ref[...], cq_ref[...], sq_ref[...], quarter) * (dh ** -0.5)
    kp = _rope_lanes(kp_ref[...], ckp_ref[...], skp_ref[...], quarter).astype(BF16)
    kc = _rope_lanes(kc_ref[...], ckc_ref[...], skc_ref[...], quarter).astype(BF16)
    kn = _rope_lanes(kn_ref[...], ckn_ref[...], skn_ref[...], quarter).astype(BF16)
    kz = kz_ref[...].astype(BF16)
    vp, vc, vn, vz = (r[...].astype(BF16) for r in (vp_ref, vc_ref, vn_ref, vz_ref))
    qi = lax.broadcasted_iota(jnp.int32, (group * blk, blk), 0) % blk
    kj = lax.broadcasted_iota(jnp.int32, (group * blk, blk), 1)
    ok_p = (kj >= qi) & (n > 0)
    ok_n = (kj <= qi) & (n < nb - 1)
    outs = []
    for hk in range(SWA_KV_HEADS):
        ks = slice(hk * dh, (hk + 1) * dh)
        qs = jnp.concatenate([q[:, (hk * group + g) * dh:(hk * group + g + 1) * dh] for g in range(group)],
                             axis=0).astype(BF16)
        sink = jnp.concatenate([jnp.full((blk, 1), 1.0, F32) * sink_ref[hk * group + g] for g in range(group)],
                               axis=0)
        s_p = jnp.where(ok_p, _dot_nt(qs, kp[:, ks]), NEG_INF)
        s_c = _dot_nt(qs, kc[:, ks])
        s_n = jnp.where(ok_n, _dot_nt(qs, kn[:, ks]), NEG_INF)
        s_z = _dot_nt(qs, kz[:, ks])
        m = jnp.maximum(jnp.maximum(jnp.max(s_p, axis=1, keepdims=True), jnp.max(s_c, axis=1, keepdims=True)),
                        jnp.maximum(jnp.max(s_n, axis=1, keepdims=True), jnp.max(s_z, axis=1, keepdims=True)))
        m = jnp.maximum(m, sink)
        e_p, e_c, e_n, e_z = (jnp.exp(s - m) for s in (s_p, s_c, s_n, s_z))
        den = (jnp.sum(e_p, axis=1, keepdims=True) + jnp.sum(e_c, axis=1, keepdims=True)
               + jnp.sum(e_n, axis=1, keepdims=True) + jnp.sum(e_z, axis=1, keepdims=True) + jnp.exp(sink - m))
        o = (jnp.dot(e_p.astype(BF16), vp[:, ks], preferred_element_type=F32)
             + jnp.dot(e_c.astype(BF16), vc[:, ks], preferred_element_type=F32)
             + jnp.dot(e_n.astype(BF16), vn[:, ks], preferred_element_type=F32)
             + jnp.dot(e_z.astype(BF16), vz[:, ks], preferred_element_type=F32)) / den
        outs += [o[g * blk:(g + 1) * blk, :] for g in range(group)]
    o_ref[...] = jnp.concatenate(outs, axis=1)


def swa_attention(px, pz, n_batch, seq, ctx_len, sink, rope_q, rope_k):
    blk = SWA_BLOCK
    nb = seq // blk
    prev = lambda b, n: b * nb + jnp.maximum(n - 1, 0)
    cur = lambda b, n: b * nb + n
    nxt = lambda b, n: b * nb + jnp.minimum(n + 1, nb - 1)
    kv = lambda off, f: pl.BlockSpec((blk, LANES), lambda b, n: (f(b, n), off // LANES))
    tab = lambda w, f: pl.BlockSpec((blk, w), lambda b, n: (f(0, n), 0))
    cq, sq = rope_q
    ck, sk = rope_k
    return pl.pallas_call(
        _swa_kernel,
        grid=(n_batch, nb),
        in_specs=[
            pl.BlockSpec((blk, GROUP_WIDTH), lambda b, n: (cur(b, n), COL_SWA_Q // GROUP_WIDTH)),
            kv(COL_SWA_K, prev), kv(COL_SWA_K, cur), kv(COL_SWA_K, nxt),
            kv(COL_SWA_V, prev), kv(COL_SWA_V, cur), kv(COL_SWA_V, nxt),
            pl.BlockSpec((ctx_len, LANES), lambda b, n: (b, COL_SWA_K // LANES)),
            pl.BlockSpec((ctx_len, LANES), lambda b, n: (b, COL_SWA_V // LANES)),
            pl.BlockSpec(memory_space=pltpu.SMEM),
            tab(GROUP_WIDTH, cur), tab(GROUP_WIDTH, cur), tab(LANES, prev), tab(LANES, prev),
            tab(LANES, cur), tab(LANES, cur), tab(LANES, nxt), tab(LANES, nxt),
        ],
        out_specs=pl.BlockSpec((blk, GROUP_WIDTH), lambda b, n: (cur(b, n), 0)),
        out_shape=jax.ShapeDtypeStruct((n_batch * seq, GROUP_WIDTH), F32),
        compiler_params=_params(2),
    )(px, px, px, px, px, px, px, pz, pz, sink, cq, sq, ck, sk, ck, sk, ck, sk)


def _ctx_attn_kernel(q_ref, k_ref, v_ref, sink_ref, o_ref, *, group, use_sink):
    dh = HEAD_DIM
    q = (q_ref[...] * (dh ** -0.5)).astype(BF16)
    k = k_ref[...].astype(BF16)
    v = v_ref[...].astype(BF16)
    outs = []
    for qh in range(q.shape[1] // dh):
        ks = slice((qh // group) * dh, (qh // group + 1) * dh)
        s = _dot_nt(q[:, qh * dh:(qh + 1) * dh], k[:, ks])
        m = jnp.max(s, axis=1, keepdims=True)
        if use_sink:
            m = jnp.maximum(m, sink_ref[qh])
        e = jnp.exp(s - m)
        den = jnp.sum(e, axis=1, keepdims=True)
        if use_sink:
            den = den + jnp.exp(sink_ref[qh] - m)
        outs.append(jnp.dot(e.astype(BF16), v[:, ks], preferred_element_type=F32) / den)
    o_ref[...] = jnp.concatenate(outs, axis=1)


def ctx_attention(pz, n_batch, ctx_len, cols, kv_width, sink):
    qc, kc, vc = cols
    use_sink = sink is not None
    if sink is None:
        sink = jnp.zeros((GROUP_WIDTH // HEAD_DIM,), F32)
    return pl.pallas_call(
        functools.partial(_ctx_attn_kernel, group=GROUP_WIDTH // kv_width, use_sink=use_sink),
        grid=(n_batch,),
        in_specs=[pl.BlockSpec((ctx_len, GROUP_WIDTH), lambda b: (b, qc // GROUP_WIDTH)),
                  pl.BlockSpec((ctx_len, kv_width), lambda b: (b, kc // kv_width)),
                  pl.BlockSpec((ctx_len, kv_width), lambda b: (b, vc // kv_width)),
                  pl.BlockSpec(memory_space=pltpu.SMEM)],
        out_specs=pl.BlockSpec((ctx_len, GROUP_WIDTH), lambda b: (b, 0)),
        out_shape=jax.ShapeDtypeStruct((n_batch * ctx_len, GROUP_WIDTH), F32),
        compiler_params=_params(1),
    )(pz, pz, pz, sink)


SCAN_ROWS = 256


def _split3(x):
    a = x.astype(BF16)
    r = x - a.astype(F32)
    b = r.astype(BF16)
    c = (r - b.astype(F32)).astype(BF16)
    return a, b, c


def _scan_kernel(*refs, heads, dk, dv, kind, rope, q_scale, k_scale):
    it = iter(refs)
    qf, kf, vf, qb, kb, vb = (next(it) for _ in range(6))
    if kind == "ret":
        lg = next(it)
    else:
        df, db, wup, bup = (next(it) for _ in range(4))
    if rope:
        cosf, sinf, cosb, sinb = (next(it) for _ in range(4))
    s0 = next(it)
    of, ob, sfin = next(it), next(it), next(it)
    st = next(it)

    s = pl.program_id(1)
    c = SCAN_CHUNK
    nch = SCAN_ROWS // c
    hk = heads * dk

    @pl.when(s == 0)
    def _():
        st[...] = s0[...]

    r_i = lax.broadcasted_iota(jnp.int32, (c, c), 0)
    c_i = lax.broadcasted_iota(jnp.int32, (c, c), 1)
    masks = (r_i >= c_i, c_i > r_i)

    if kind == "ret":
        pos = lax.broadcasted_iota(jnp.int32, (c, hk), 0).astype(F32)
        gcums = ((pos + 1.0) * lg[0:1, :], (float(c) - pos) * lg[1:2, :])
    else:
        rr = lax.broadcasted_iota(jnp.int32, (SCAN_ROWS, SCAN_ROWS), 0)
        cc = lax.broadcasted_iota(jnp.int32, (SCAN_ROWS, SCAN_ROWS), 1)
        same = (rr // c) == (cc // c)
        tris = (jnp.where(same & (rr >= cc), 1.0, 0.0).astype(BF16),
                jnp.where(same & (cc >= rr), 1.0, 0.0).astype(BF16))

        def gate_cum(d_ref, direction):
            pre = jnp.dot(d_ref[...].astype(BF16), wup[direction], preferred_element_type=F32) + bup[direction]
            g = -(jnp.maximum(-pre, 0.0) + jnp.log1p(jnp.exp(-jnp.abs(pre)))) / GLA_TAU
            return sum(jnp.dot(tris[direction], p, preferred_element_type=F32) for p in _split3(g))

        gcums = (gate_cum(df, 0), gate_cum(db, 1))

    def one(direction, q_ref, k_ref, v_ref, o_ref, cos_ref, sin_ref, ch):
        rows = pl.ds(ch * c, c)
        q = q_ref[rows, :]
        k = k_ref[rows, :]
        v = v_ref[rows, :]
        if rope:
            cs = jnp.concatenate([cos_ref[rows, :]] * heads, axis=1)
            sn = jnp.concatenate([sin_ref[rows, :]] * heads, axis=1)
            q = _rope_lanes(q, cs, sn, dk // 4)
            k = _rope_lanes(k, cs, sn, dk // 4)
        if q_scale != 1.0:
            q = q * q_scale
        if k_scale != 1.0:
            k = k * k_scale
        gcum = gcums[direction] if kind == "ret" else gcums[direction][ch * c:(ch + 1) * c, :]
        gtot = gcum[c - 1:c, :] if direction == 0 else gcum[0:1, :]
        q_rel = (q * jnp.exp(gcum - gtot)).astype(BF16)
        k_rel = (k * jnp.exp(gtot - gcum)).astype(BF16)
        q_dec = (q * jnp.exp(gcum)).astype(BF16)
        dec = jnp.exp(gtot)
        vb16 = v.astype(BF16)
        ks = [slice(hd * dk, (hd + 1) * dk) for hd in range(heads)]
        vs = [slice(hd * dv, (hd + 1) * dv) for hd in range(heads)]
        states = [st[direction, hd] for hd in range(heads)]
        a = [jnp.where(masks[direction], _dot_nt(q_rel[:, ks[hd]], k_rel[:, ks[hd]]), 0.0).astype(BF16)
             for hd in range(heads)]
        inter = [_dot_nt(q_dec[:, ks[hd]], states[hd].astype(BF16)) for hd in range(heads)]
        incs = [_dot_tn(vb16[:, vs[hd]], k_rel[:, ks[hd]]) for hd in range(heads)]
        outs = [jnp.dot(a[hd], vb16[:, vs[hd]], preferred_element_type=F32) + inter[hd] for hd in range(heads)]
        for hd in range(heads):
            st[direction, hd] = dec[:, ks[hd]] * states[hd] + incs[hd]
        o_ref[rows, :] = jnp.concatenate(outs, axis=1)

    for ch in range(nch):
        one(0, qf, kf, vf, of, cosf if rope else None, sinf if rope else None, ch)
        one(1, qb, kb, vb, ob, cosb if rope else None, sinb if rope else None, nch - 1 - ch)

    @pl.when(s == pl.num_programs(1) - 1)
    def _():
        sfin[...] = st[...]


def bidir_scan(p, n_batch, seq, cols, heads, dk, dv, kind, s0, *, lg=None, wup=None, bup=None,
               rope=None, q_scale=1.0, k_scale=1.0):
    t = SCAN_ROWS
    nblk = seq // t
    hk, hv = heads * dk, heads * dv
    qc, kc, vc = cols
    fwd = lambda w, off: pl.BlockSpec((t, w), lambda b, s: (b * nblk + s, off // w))
    bwd = lambda w, off: pl.BlockSpec((t, w), lambda b, s: (b * nblk + nblk - 1 - s, off // w))
    const = lambda shape: pl.BlockSpec(shape, lambda b, s: (0,) * len(shape))
    args = [p] * 6
    specs = [fwd(hk, qc), fwd(hk, kc), fwd(hv, vc), bwd(hk, qc), bwd(hk, kc), bwd(hv, vc)]
    if kind == "ret":
        args += [lg]
        specs += [const((2, hk))]
    else:
        args += [p, p, wup, bup]
        specs += [fwd(LANES, COL_GLA_D), bwd(LANES, COL_GLA_D), const(wup.shape), const(bup.shape)]
    if rope is not None:
        cos, sin = rope
        args += [cos, sin, cos, sin]
        specs += [pl.BlockSpec((t, dk), lambda b, s: (s, 0)), pl.BlockSpec((t, dk), lambda b, s: (s, 0)),
                  pl.BlockSpec((t, dk), lambda b, s: (nblk - 1 - s, 0)),
                  pl.BlockSpec((t, dk), lambda b, s: (nblk - 1 - s, 0))]
    args += [s0]
    state_spec = pl.BlockSpec((None, 2, heads, dv, dk), lambda b, s: (b, 0, 0, 0, 0))
    specs += [state_spec]
    n = n_batch * seq
    kern = functools.partial(_scan_kernel, heads=heads, dk=dk, dv=dv, kind=kind, rope=rope is not None,
                             q_scale=q_scale, k_scale=k_scale)
    return pl.pallas_call(
        kern,
        grid=(n_batch, nblk),
        in_specs=specs,
        out_specs=[pl.BlockSpec((t, hv), lambda b, s: (b * nblk + s, 0)),
                   pl.BlockSpec((t, hv), lambda b, s: (b * nblk + nblk - 1 - s, 0)),
                   state_spec],
        out_shape=[jax.ShapeDtypeStruct((n, hv), F32), jax.ShapeDtypeStruct((n, hv), F32),
                   jax.ShapeDtypeStruct((n_batch, 2, heads, dv, dk), F32)],
        scratch_shapes=[pltpu.VMEM((2, heads, dv, dk), F32)],
        compiler_params=_params(2),
    )(*args)


def _outproj_kernel(na_ref, rf_ref, rb_ref, rg_ref, gf_ref, gb_ref, gg_ref, sw_ref, gn_ref, w_ref, x_ref, mg_ref,
                    o_ref, *, head_w):
    ry = rf_ref[...] + rb_ref[...]
    gy = gf_ref[...] + gb_ref[...]
    r_out, g_out = [], []
    for hd in range(ry.shape[1] // head_w):
        sl = slice(hd * head_w, (hd + 1) * head_w)
        r = ry[:, sl]
        mu = jnp.mean(r, axis=-1, keepdims=True)
        var = jnp.mean(jnp.square(r - mu), axis=-1, keepdims=True)
        r_out.append((r - mu) * lax.rsqrt(var + EPS))
        gq = gy[:, sl]
        g_out.append(gq * lax.rsqrt(jnp.mean(gq * gq, axis=-1, keepdims=True) + EPS) * gn_ref[...])
    ret = jnp.concatenate(r_out, axis=1) * _silu(rg_ref[...])
    gla = jnp.concatenate(g_out, axis=1) * _silu(gg_ref[...])
    mix = jnp.concatenate([na_ref[...], ret, gla, sw_ref[...]], axis=1).astype(BF16)
    o_ref[...] = x_ref[...] + mg_ref[...] * jnp.dot(mix, w_ref[...], preferred_element_type=F32)


def outproj(na, rf, rb, gf, gb, sw, p, gla_norm_g, w_out, x2, mg, rows_per_mod):
    n, d = x2.shape
    gw = GROUP_WIDTH
    tm = 256
    m = mg.shape[0]
    row = lambda w_: pl.BlockSpec((tm, w_), lambda i: (i, 0))
    return pl.pallas_call(
        functools.partial(_outproj_kernel, head_w=RET_DV),
        grid=(n // tm,),
        in_specs=[row(gw), row(gw), row(gw), pl.BlockSpec((tm, gw), lambda i: (i, COL_RET_G // gw)),
                  row(gw), row(gw), pl.BlockSpec((tm, gw), lambda i: (i, COL_GLA_G // gw)), row(gw),
                  pl.BlockSpec((1, GLA_DV), lambda i: (0, 0)),
                  pl.BlockSpec((d, d), lambda i: (0, 0)), row(d),
                  pl.BlockSpec((None, 1, d), lambda i: (i // (rows_per_mod // tm), 0, 0))],
        out_specs=row(d),
        out_shape=jax.ShapeDtypeStruct((n, d), F32),
        compiler_params=_params(1),
    )(na, rf, rb, p, gf, gb, p, sw, gla_norm_g.reshape(1, GLA_DV), w_out, x2, mg.reshape(m, 1, d))


PEER_PAIRS = PEER_HEADS * PEER_TOPK
PEER_ROUTE_TOKENS = 128
PEER_EXPERT_TOKENS = 16
PEER_WAIT_GROUP = 8
PEER_SUB = 8
PEER_FOLD = D_MODEL // PEER_SUB
INV_SQRT2 = 0.7071067811865476


def _topk_cols(s, payload=None, order=None):
    row = lax.broadcasted_iota(jnp.int32, s.shape, 0) if order is None else order
    vals, idxs = [], []
    for _ in range(PEER_TOPK):
        m = jnp.max(s, axis=0, keepdims=True)
        am = jnp.min(jnp.where(s == m, row, jnp.iinfo(jnp.int32).max), axis=0, keepdims=True)
        sel = row == am
        vals.append(m)
        idxs.append(am if payload is None else jnp.max(jnp.where(sel, payload, -1), axis=0, keepdims=True))
        s = jnp.where(sel, -jnp.inf, s)
    return jnp.concatenate(vals, axis=0), jnp.concatenate(idxs, axis=0)


def _staircase_candidates(v0, i0, v1, i1):
    k = PEER_TOPK
    assert k == 16
    t = v0.shape[1]
    r8 = lax.broadcasted_iota(jnp.int32, (8, t), 0)
    r16 = lax.broadcasted_iota(jnp.int32, (k, t), 0)

    def piece(a_sl, b_sl):
        return v0[a_sl, :] + v1[b_sl, :], i0[a_sl, :] * PEER_N_KEYS + i1[b_sl, :]

    one = lambda j: slice(j, j + 1)
    lo = slice(0, 8)
    pieces = [
        (one(0), slice(0, k), None, r16),
        (one(1), lo, None, k + r8),
        (one(2), lo, r8 <= 4, 2 * k + r8),
        (one(3), lo, r8 <= 3, 3 * k + r8),
        (slice(8, k), one(0), None, (r8 + 8) * k),
        (lo, one(0), r8 >= 4, r8 * k),
        (lo, one(1), r8 >= 4, r8 * k + 1),
        (lo, one(2), r8 == 4, r8 * k + 2),
    ]
    sums, ids, orders = [], [], []
    for a_sl, b_sl, keep, order in pieces:
        s, e = piece(a_sl, b_sl)
        sums.append(s if keep is None else jnp.where(keep, s, NEG_INF))
        ids.append(e)
        orders.append(order)
    return jnp.concatenate(sums, axis=0), jnp.concatenate(ids, axis=0), jnp.concatenate(orders, axis=0)


def _route_head(hb, wq_ref, sk_ref, hd):
    half = PEER_DK // 2
    q = jnp.dot(hb, wq_ref[hd], preferred_element_type=F32)
    tops = []
    for p in range(2):
        qp = q[:, p * half:(p + 1) * half].astype(BF16)
        tops.append(_topk_cols(_dot_nt(sk_ref[p, hd], qp)))
    (v0, i0), (v1, i1) = tops
    best_s, best_e = _topk_cols(*_staircase_candidates(v0, i0, v1, i1))
    e = jnp.exp(best_s - best_s[0:1, :])
    return e / jnp.sum(e, axis=0, keepdims=True), best_e


def _peer_route_kernel(x_ref, g_ref, shift_ref, scale_ref, wq_ref, sk_ref, h_ref, idx_ref, gate_ref):
    x = x_ref[...]
    y = x * lax.rsqrt(jnp.mean(x * x, axis=-1, keepdims=True) + EPS)
    h = (y * g_ref[...]) * (1.0 + scale_ref[...]) + shift_ref[...]
    h_ref[...] = h
    hb = h.astype(BF16)

    for hd in range(PEER_HEADS):
        gate_ref[hd], idx_ref[hd] = _route_head(hb, wq_ref, sk_ref, hd)


def peer_route(x2, g, shift, scale, rows_per_mod, wq_h, sk):
    n, d = x2.shape
    t = PEER_ROUTE_TOKENS
    m = shift.shape[0]
    mod_map = lambda i: (i // (rows_per_mod // t), 0, 0)
    return pl.pallas_call(
        _peer_route_kernel,
        grid=(n // t,),
        in_specs=[
            pl.BlockSpec((t, d), lambda i: (i, 0)),
            pl.BlockSpec((1, d), lambda i: (0, 0)),
            pl.BlockSpec((None, 1, d), mod_map),
            pl.BlockSpec((None, 1, d), mod_map),
            pl.BlockSpec((PEER_HEADS, d, PEER_DK), lambda i: (0, 0, 0)),
            pl.BlockSpec((2, PEER_HEADS, PEER_N_KEYS, PEER_DK // 2), lambda i: (0, 0, 0, 0)),
        ],
        out_specs=[
            pl.BlockSpec((t, d), lambda i: (i, 0)),
            pl.BlockSpec((PEER_HEADS, PEER_TOPK, t), lambda i: (0, 0, i)),
            pl.BlockSpec((PEER_HEADS, PEER_TOPK, t), lambda i: (0, 0, i)),
        ],
        out_shape=[
            jax.ShapeDtypeStruct((n, d), F32),
            jax.ShapeDtypeStruct((PEER_HEADS, PEER_TOPK, n), jnp.int32),
            jax.ShapeDtypeStruct((PEER_HEADS, PEER_TOPK, n), F32),
        ],
        compiler_params=_params(1),
    )(x2, g.reshape(1, d), shift.reshape(m, 1, d), scale.reshape(m, 1, d), wq_h, sk)


def pack_experts(u, v):
    def bf16_bits(a):
        b = lax.bitcast_convert_type(a, jnp.uint32)
        rounded = (b + jnp.uint32(0x7FFF) + ((b >> 16) & jnp.uint32(1))) >> 16
        is_nan = (b & jnp.uint32(0x7FFFFFFF)) > jnp.uint32(0x7F800000)
        return jnp.where(is_nan, (b >> 16) | jnp.uint32(0x40), rounded)

    words = lax.bitcast_convert_type((bf16_bits(u) << 16) | bf16_bits(v), jnp.int32)
    return words.reshape(u.shape[0], PEER_SUB, PEER_FOLD)


def _fold_rows(hbuf, h_rows, tb):
    for s in range(PEER_SUB):
        for c in range(PEER_FOLD // LANES):
            lo = s * PEER_FOLD + c * LANES
            hbuf[c, pl.ds(s * tb, tb), :] = h_rows[:, lo:lo + LANES]


def _unfold_residual(o_ref, x_ref, og_ref, ybuf, tb):
    for s in range(PEER_SUB):
        for c in range(PEER_FOLD // LANES):
            sl = slice(s * PEER_FOLD + c * LANES, s * PEER_FOLD + (c + 1) * LANES)
            o_ref[:, sl] = x_ref[:, sl] + og_ref[:, sl] * ybuf[c, pl.ds(s, tb, stride=PEER_SUB), :]


def _expert_weights(j, buf, hbuf, pbuf, wrow, tb, gate_column):
    fold = PEER_FOLD
    hj = jnp.concatenate([hbuf[c, pl.ds(j, PEER_SUB, stride=tb), :] for c in range(fold // LANES)], axis=1)
    for p in range(PEER_PAIRS):
        u = lax.bitcast_convert_type(buf[j, p] & jnp.int32(-65536), F32)
        prod = u * hj
        pbuf[pl.ds(p * PEER_SUB, PEER_SUB), :] = prod[:, :LANES] + prod[:, LANES:]
    part = pbuf[pl.ds(0, PEER_PAIRS, stride=PEER_SUB), :]
    for s in range(1, PEER_SUB):
        part = part + pbuf[pl.ds(s, PEER_PAIRS, stride=PEER_SUB), :]
    sc = jnp.sum(part, axis=1, keepdims=True)
    act = 0.5 * sc * (1.0 + lax.erf(sc * INV_SQRT2))
    g = jnp.concatenate([gate_column(hd) for hd in range(PEER_HEADS)], axis=0)
    wrow[...] = jnp.broadcast_to(g * act, (PEER_PAIRS, LANES))


def _expert_mix(j, buf, wrow, ybuf, start_fetch):
    fold = PEER_FOLD
    accs = [jnp.zeros((PEER_SUB, fold), F32) for _ in range(4)]
    for p in range(PEER_PAIRS):
        if start_fetch is not None:
            start_fetch(p)
        v = lax.bitcast_convert_type(buf[j, p] << 16, F32)
        wp = jnp.broadcast_to(wrow[p:p + 1, :], (PEER_SUB, LANES))
        accs[p % 4] = accs[p % 4] + v * jnp.concatenate([wp, wp], axis=1)
    yj = (accs[0] + accs[1]) + (accs[2] + accs[3])
    for c in range(fold // LANES):
        ybuf[c, pl.ds(j * PEER_SUB, PEER_SUB), :] = yj[:, c * LANES:(c + 1) * LANES]


def _peer_expert_kernel(idx0_ref, idxn_ref, h_ref, gate_ref, x_ref, og_ref, uv_hbm, o_ref,
                        buf_even, buf_odd, sem, hbuf, pbuf, wbuf, ybuf, *, n_blocks):
    i = pl.program_id(0)
    n = n_blocks
    tb = PEER_EXPERT_TOKENS
    bufs = (buf_even, buf_odd)

    def slab_copy(idx_ref, j, r, parity):
        e = idx_ref[0, 0, j * PEER_PAIRS + r]
        return pltpu.make_async_copy(uv_hbm.at[e], bufs[parity].at[j, r], sem.at[parity, j])

    def wait_token(j, parity):
        pltpu.make_async_copy(uv_hbm.at[pl.ds(0, PEER_PAIRS)], bufs[parity].at[j], sem.at[parity, j]).wait()

    @pl.when(i == 0)
    def _():
        def prime(j, carry):
            for r in range(PEER_PAIRS):
                slab_copy(idx0_ref, j, r, 0).start(priority=r % 2)
            return carry
        lax.fori_loop(0, tb, prime, 0)

    lane = lax.broadcasted_iota(jnp.int32, (PEER_TOPK, PEER_ROUTE_TOKENS), 1)
    lane0 = (i % (PEER_ROUTE_TOKENS // tb)) * tb

    _fold_rows(hbuf, h_ref[...], tb)

    def fetcher(j, parity, prefetch):
        return (lambda r: slab_copy(idxn_ref, j, r, 1 - parity).start(priority=r % 2)) if prefetch else None

    def gate_column_of(j):
        return lambda hd: jnp.sum(jnp.where(lane == lane0 + j, gate_ref[hd], 0.0), axis=1, keepdims=True)

    def block(parity, prefetch):
        for j in range(tb):
            if j % PEER_WAIT_GROUP == 0:
                for jj in range(j, j + PEER_WAIT_GROUP):
                    wait_token(jj, parity)
            _expert_weights(j, bufs[parity], hbuf, pbuf, wbuf, tb, gate_column_of(j))
            _expert_mix(j, bufs[parity], wbuf, ybuf, fetcher(j, parity, prefetch))

    for parity in range(2):
        for prefetch in (True, False):
            @pl.when((i % 2 == parity) & ((i + 1 < n) == prefetch))
            def _(parity=parity, prefetch=prefetch):
                block(parity, prefetch)

    _unfold_residual(o_ref, x_ref, og_ref, ybuf, tb)


def peer_expert(h, idx, gate, x2, out_gate, rows_per_mod, uv):
    n, d = h.shape
    tb = PEER_EXPERT_TOKENS
    nb = n // tb
    m = out_gate.shape[0]
    rows = tb * PEER_PAIRS
    idx_rows = idx.reshape(PEER_PAIRS, n).T.reshape(nb, 1, rows)
    gate_blocks = PEER_ROUTE_TOKENS // tb
    return pl.pallas_call(
        functools.partial(_peer_expert_kernel, n_blocks=nb),
        grid=(nb,),
        in_specs=[
            pl.BlockSpec((1, 1, rows), lambda i: (0, 0, 0), memory_space=pltpu.SMEM),
            pl.BlockSpec((1, 1, rows), lambda i: (jnp.minimum(i + 1, nb - 1), 0, 0), memory_space=pltpu.SMEM),
            pl.BlockSpec((tb, d), lambda i: (i, 0)),
            pl.BlockSpec((PEER_HEADS, PEER_TOPK, PEER_ROUTE_TOKENS), lambda i: (0, 0, i // gate_blocks)),
            pl.BlockSpec((tb, d), lambda i: (i, 0)),
            pl.BlockSpec((None, 1, d), lambda i: (i // (rows_per_mod // tb), 0, 0)),
            pl.BlockSpec(memory_space=pl.ANY),
        ],
        out_specs=pl.BlockSpec((tb, d), lambda i: (i, 0)),
        out_shape=jax.ShapeDtypeStruct((n, d), F32),
        scratch_shapes=[pltpu.VMEM((tb, PEER_PAIRS, PEER_SUB, PEER_FOLD), jnp.int32),
                        pltpu.VMEM((tb, PEER_PAIRS, PEER_SUB, PEER_FOLD), jnp.int32),
                        pltpu.SemaphoreType.DMA((2, tb)),
                        pltpu.VMEM((PEER_FOLD // LANES, PEER_SUB * tb, LANES), F32),
                        pltpu.VMEM((PEER_PAIRS * PEER_SUB, LANES), F32),
                        pltpu.VMEM((PEER_PAIRS, LANES), F32),
                        pltpu.VMEM((PEER_FOLD // LANES, tb * PEER_SUB, LANES), F32)],
        compiler_params=pltpu.CompilerParams(dimension_semantics=("arbitrary",), vmem_limit_bytes=VMEM_LIMIT,
                                             disable_bounds_checks=True),
    )(idx_rows, idx_rows, h, gate, x2, out_gate.reshape(m, 1, d), uv)


def peer_residual(x2, g, shift, scale, out_gate, rows_per_mod, wq_h, sk, uv):
    h, idx, gate = peer_route(x2, g, shift, scale, rows_per_mod, wq_h, sk)
    return peer_expert(h, idx, gate, x2, out_gate, rows_per_mod, uv)


def _final_norm_kernel(x_ref, g_ref, o_ref):
    x = x_ref[...]
    y = x * lax.rsqrt(jnp.mean(x * x, axis=-1, keepdims=True) + EPS)
    o_ref[...] = y * g_ref[...]


def final_norm(x2, g):
    n, d = x2.shape
    tm = 512
    return pl.pallas_call(
        _final_norm_kernel,
        grid=(n // tm,),
        in_specs=[pl.BlockSpec((tm, d), lambda i: (i, 0)), pl.BlockSpec((1, d), lambda i: (0, 0))],
        out_specs=pl.BlockSpec((tm, d), lambda i: (i, 0)),
        out_shape=jax.ShapeDtypeStruct((n, d), x2.dtype),
        compiler_params=_params(1),
    )(x2, g.reshape(1, d))


def _mixers(p, pz, n_batch, seq, ctx_len, is_ctx, tables, prm):
    if is_ctx:
        na = ctx_attention(p, n_batch, seq, (COL_NA_Q, COL_NA_K, COL_NA_V), GROUP_WIDTH, None)
        sw = ctx_attention(p, n_batch, seq, (COL_SWA_Q, COL_SWA_K, COL_SWA_V), SWA_KV_HEADS * HEAD_DIM, prm["sink"])
    else:
        na = na_attention(p, pz, n_batch, seq, ctx_len, prm["na_bias"])
        sw = swa_attention(p, pz, n_batch, seq, ctx_len, prm["sink"], tables["swa_q"], tables["swa_k"])
    rf, rb, rs = bidir_scan(p, n_batch, seq, (COL_RET_Q, COL_RET_K, COL_RET_V), RET_HEADS, RET_DK, RET_DV, "ret",
                            prm["ret_s0"], lg=prm["ret_lg"], rope=None if is_ctx else tables["ret"],
                            k_scale=RET_DK ** -0.5)
    gf, gb, gs = bidir_scan(p, n_batch, seq, (COL_GLA_Q, COL_GLA_K, COL_GLA_V), GLA_HEADS, GLA_DK, GLA_DV, "gla",
                            prm["gla_s0"], wup=prm["gla_wup"], bup=prm["gla_bup"], q_scale=GLA_DK ** -0.5)
    return (na, rf, rb, gf, gb, sw), (rs, gs)


def kernel(x, c, ctx, c_ctx, w_ada, b_ada, norm_attn_g, norm_ffn_g, w_in, na_rpb, ret_log_gamma,
           gla_w_gate_up, gla_b_gate, gla_norm_g, swa_sink, w_out, peer_w_q, peer_sub_keys,
           peer_u, peer_v, final_g):
    bsz, slen, d = x.shape
    zlen = ctx.shape[1]
    x2 = x.reshape(bsz * slen, d)
    z2 = ctx.reshape(bsz * zlen, d)
    tables = {"ret": rope_lane_tables(slen, RET_DK, 1),
              "swa_q": rope_lane_tables(slen, HEAD_DIM, SWA_HEADS),
              "swa_k": rope_lane_tables(slen, HEAD_DIM, SWA_KV_HEADS)}
    c_rows = jnp.zeros((8, d), F32).at[:bsz].set(c).at[bsz].set(c_ctx)
    for layer in range(DEPTH):
        has_next = layer < DEPTH - 1
        mod = adaln(c_rows, w_ada[layer], b_ada[layer])
        mx = [mod[:bsz, k * d:(k + 1) * d] for k in range(6)]
        mz = [mod[bsz:bsz + 1, k * d:(k + 1) * d] for k in range(6)]

        wi = w_in[layer]
        wp = jnp.concatenate([wi[:, :REF_COL_GLA_D], wi[:, REF_COL_SWA_Q:], wi[:, REF_COL_GLA_D:REF_COL_SWA_Q],
                              jnp.zeros((d, PROJ_WIDTH - REF_D_IN), F32)], axis=1).astype(BF16)
        px = modproj(x2, norm_attn_g[layer], mx[0], mx[1], slen, wp)
        pz = modproj(z2, norm_attn_g[layer], mz[0], mz[1], bsz * zlen, wp)

        wup = (jnp.zeros((2, LANES, GLA_HEADS * GLA_DK), F32)
               .at[0, :GLA_RANK].set(gla_w_gate_up[layer, 0])
               .at[1, GLA_RANK:2 * GLA_RANK].set(gla_w_gate_up[layer, 1])).astype(BF16)
        prm = {"na_bias": na_band_bias(na_rpb[layer]), "sink": swa_sink[layer],
               "ret_lg": jnp.repeat(ret_log_gamma[layer], RET_DK, axis=1),
               "gla_wup": wup, "gla_bup": gla_b_gate[layer].reshape(2, 1, GLA_HEADS * GLA_DK),
               "ret_s0": jnp.zeros((bsz, 2, RET_HEADS, RET_DV, RET_DK), F32),
               "gla_s0": jnp.zeros((bsz, 2, GLA_HEADS, GLA_DV, GLA_DK), F32)}
        mix_z, (ret_s, gla_s) = _mixers(pz, pz, bsz, zlen, zlen, True, tables, prm)
        prm["ret_s0"], prm["gla_s0"] = ret_s, gla_s
        mix_x, _ = _mixers(px, pz, bsz, slen, zlen, False, tables, prm)

        wo = w_out[layer].astype(BF16)
        x2 = outproj(*mix_x, px, gla_norm_g[layer], wo, x2, mx[2], slen)

        uv = pack_experts(peer_u[layer], peer_v[layer])
        wq_h = peer_w_q[layer].reshape(d, PEER_HEADS, PEER_DK).transpose(1, 0, 2).astype(BF16)
        sk = peer_sub_keys[layer].astype(BF16)
        x2 = peer_residual(x2, norm_ffn_g[layer], mx[3], mx[4], mx[5], slen, wq_h, sk, uv)
        if has_next:
            z2 = outproj(*mix_z, pz, gla_norm_g[layer], wo, z2, mz[2], bsz * zlen)
            z2 = peer_residual(z2, norm_ffn_g[layer], mz[3], mz[4], mz[5], bsz * zlen, wq_h, sk, uv)
    return final_norm(x2, final_g).reshape(bsz, slen, d)
```

```python
import functools

import jax
import jax.numpy as jnp
import numpy as np
from jax import lax
from jax.experimental import pallas as pl
from jax.experimental.pallas import tpu as pltpu

D_MODEL = 2048
DEPTH = 2
GRID_W = 64
EPS = 1e-6
ROPE_BASE = 10000.0

GROUP_WIDTH = D_MODEL // 4
HEAD_DIM = 64
NA_HEADS = GROUP_WIDTH // HEAD_DIM
NA_ROWS = 8
NA_COLS = 16
RET_HEADS = 4
RET_DK = GROUP_WIDTH // RET_HEADS
RET_DV = GROUP_WIDTH // RET_HEADS
GLA_HEADS = 4
GLA_DV = GROUP_WIDTH // GLA_HEADS
GLA_DK = GLA_DV // 2
GLA_RANK = 16
GLA_TAU = 16.0
SWA_HEADS = GROUP_WIDTH // HEAD_DIM
SWA_KV_HEADS = SWA_HEADS // 4
SWA_WINDOW = 128
SWA_BLOCK = 128
SCAN_CHUNK = 64
PEER_HEADS = 8
PEER_N_KEYS = 128
PEER_N_EXPERTS = PEER_N_KEYS * PEER_N_KEYS
PEER_DK = 256
PEER_TOPK = 16

LANES = 128
VMEM_LIMIT = 48 * 1024 * 1024
BF16 = jnp.bfloat16
F32 = jnp.float32
NEG_INF = float("-inf")

COL_NA_Q, COL_NA_K, COL_NA_V = 0, 512, 1024
COL_RET_Q, COL_RET_K, COL_RET_V, COL_RET_G = 1536, 2048, 2560, 3072
COL_GLA_Q, COL_GLA_K, COL_GLA_V, COL_GLA_G = 3584, 3840, 4096, 4608
COL_SWA_Q, COL_SWA_K, COL_SWA_V = 5120, 5632, 5760
COL_GLA_D = 5888
REF_COL_GLA_D, REF_COL_SWA_Q, REF_D_IN = 5120, 5152, 5920
PROJ_WIDTH = 6144


def _silu(x):
    return x / (1.0 + jnp.exp(-x))


def _dot_nt(a, b):
    return lax.dot_general(a, b, (((1,), (1,)), ((), ())), preferred_element_type=F32)


def _dot_tn(a, b):
    return lax.dot_general(a, b, (((0,), (0,)), ((), ())), preferred_element_type=F32)


def _params(n_axes):
    return pltpu.CompilerParams(dimension_semantics=("arbitrary",) * n_axes, vmem_limit_bytes=VMEM_LIMIT)


def _rope_lanes(x, cs, sn, quarter):
    n = x.shape[-1]
    lane = lax.broadcasted_iota(jnp.int32, x.shape, x.ndim - 1)
    first = (lane % (2 * quarter)) < quarter
    swapped = jnp.where(first, pltpu.roll(x, n - quarter, x.ndim - 1), pltpu.roll(x, quarter, x.ndim - 1))
    return x * cs + swapped * sn


def rope_lane_tables(length, dh, copies):
    t = jnp.arange(length)
    pos = jnp.stack([t // GRID_W, t % GRID_W], axis=-1).astype(F32)
    quarter = dh // 4
    inv = ROPE_BASE ** (-jnp.arange(quarter, dtype=F32) / quarter)
    ang = pos[:, :, None] * inv
    cos, sin = jnp.cos(ang), jnp.sin(ang)
    cl = jnp.concatenate([cos[:, 0], cos[:, 0], cos[:, 1], cos[:, 1]], axis=-1)
    sl = jnp.concatenate([-sin[:, 0], sin[:, 0], -sin[:, 1], sin[:, 1]], axis=-1)
    return jnp.tile(cl, (1, copies)), jnp.tile(sl, (1, copies))


def _adaln_kernel(c_ref, w_ref, b_ref, o_ref):
    a = _silu(c_ref[...]).astype(BF16)
    o_ref[...] = jnp.dot(a, w_ref[...].astype(BF16), preferred_element_type=F32) + b_ref[...]


def adaln(c_rows, w, b):
    r, d = c_rows.shape
    m = w.shape[1]
    tn = 1024
    return pl.pallas_call(
        _adaln_kernel,
        grid=(m // tn,),
        in_specs=[pl.BlockSpec((r, d), lambda j: (0, 0)), pl.BlockSpec((d, tn), lambda j: (0, j)),
                  pl.BlockSpec((1, tn), lambda j: (0, j))],
        out_specs=pl.BlockSpec((r, tn), lambda j: (0, j)),
        out_shape=jax.ShapeDtypeStruct((r, m), F32),
        compiler_params=_params(1),
    )(c_rows, w, b.reshape(1, m))


def _modproj_kernel(x_ref, g_ref, shift_ref, scale_ref, w_ref, o_ref, hb_ref):
    @pl.when(pl.program_id(1) == 0)
    def _():
        x = x_ref[...]
        y = x * lax.rsqrt(jnp.mean(x * x, axis=-1, keepdims=True) + EPS)
        hb_ref[...] = ((y * g_ref[...]) * (1.0 + scale_ref[...]) + shift_ref[...]).astype(BF16)

    o_ref[...] = jnp.dot(hb_ref[...], w_ref[...], preferred_element_type=F32)


def modproj(x2, g, shift, scale, rows_per_mod, w):
    n, d = x2.shape
    wid = w.shape[1]
    tm = min(512, rows_per_mod)
    tn = 2048
    m = shift.shape[0]
    mod_map = lambda i, j: (i // (rows_per_mod // tm), 0, 0)
    return pl.pallas_call(
        _modproj_kernel,
        grid=(n // tm, wid // tn),
        in_specs=[pl.BlockSpec((tm, d), lambda i, j: (i, 0)), pl.BlockSpec((1, d), lambda i, j: (0, 0)),
                  pl.BlockSpec((None, 1, d), mod_map), pl.BlockSpec((None, 1, d), mod_map),
                  pl.BlockSpec((d, tn), lambda i, j: (0, j))],
        out_specs=pl.BlockSpec((tm, tn), lambda i, j: (i, j)),
        out_shape=jax.ShapeDtypeStruct((n, wid), F32),
        scratch_shapes=[pltpu.VMEM((tm, d), BF16)],
        compiler_params=_params(2),
    )(x2, g.reshape(1, d), shift.reshape(m, 1, d), scale.reshape(m, 1, d), w)


NA_QROWS = 8


def _na_kernel(q_ref, k_ref, v_ref, kz_ref, vz_ref, bias_ref, o_ref, *, rows):
    step = pl.program_id(2)
    dh = HEAD_DIM
    band = NA_ROWS * GRID_W
    kz = kz_ref[...].astype(BF16)
    vz = vz_ref[...].astype(BF16)
    heads = LANES // dh
    units = [(qr, hh) for qr in range(NA_QROWS) for hh in range(heads)]
    scores, vbands = {}, {}
    for qr in range(NA_QROWS):
        r = step * NA_QROWS + qr
        start = jnp.clip(r - NA_ROWS // 2, 0, rows - NA_ROWS)
        dr0 = start - r + NA_ROWS - 1
        tok0 = pl.multiple_of(start * GRID_W, GRID_W)
        kb = k_ref[pl.ds(tok0, band), :].astype(BF16)
        vbands[qr] = v_ref[pl.ds(tok0, band), :].astype(BF16)
        q = (q_ref[pl.ds(qr * GRID_W, GRID_W), :] * (dh ** -0.5)).astype(BF16)
        for hh in range(heads):
            sl = slice(hh * dh, (hh + 1) * dh)
            scores[qr, hh] = (_dot_nt(q[:, sl], kb[:, sl]) + bias_ref[hh, dr0],
                              _dot_nt(q[:, sl], kz[:, sl]))
    probs = {}
    for u in units:
        s_nb, s_cx = scores[u]
        m = jnp.maximum(jnp.max(s_nb, axis=1, keepdims=True), jnp.max(s_cx, axis=1, keepdims=True))
        p_nb = jnp.exp(s_nb - m)
        p_cx = jnp.exp(s_cx - m)
        den = jnp.sum(p_nb, axis=1, keepdims=True) + jnp.sum(p_cx, axis=1, keepdims=True)
        probs[u] = (p_nb.astype(BF16), p_cx.astype(BF16), den)
    for qr in range(NA_QROWS):
        outs = []
        for hh in range(heads):
            sl = slice(hh * dh, (hh + 1) * dh)
            p_nb, p_cx, den = probs[qr, hh]
            o = (jnp.dot(p_nb, vbands[qr][:, sl], preferred_element_type=F32)
                 + jnp.dot(p_cx, vz[:, sl], preferred_element_type=F32))
            outs.append(o / den)
        o_ref[pl.ds(qr * GRID_W, GRID_W), :] = jnp.concatenate(outs, axis=1)


def na_band_bias(rpb):
    col = jnp.arange(GRID_W)
    col_start = jnp.clip(col - NA_COLS // 2, 0, GRID_W - NA_COLS)
    col_ok = (col[None, :] >= col_start[:, None]) & (col[None, :] < col_start[:, None] + NA_COLS)
    d_col = jnp.clip(col[None, :] - col[:, None], -(NA_COLS - 1), NA_COLS - 1) + NA_COLS - 1
    d_row = jnp.arange(NA_ROWS)[:, None] + jnp.arange(NA_ROWS)[None, :]
    b = rpb.astype(F32)[:, d_row][..., d_col]
    b = jnp.where(col_ok[None, None, None], b, NEG_INF)
    return b.transpose(0, 1, 3, 2, 4).reshape(rpb.shape[0], NA_ROWS, GRID_W, NA_ROWS * GRID_W)


def na_attention(px, pz, n_batch, seq, ctx_len, bias):
    rows = seq // GRID_W
    tq = NA_QROWS * GRID_W
    nsteps = rows // NA_QROWS
    heads_per_blk = LANES // HEAD_DIM
    return pl.pallas_call(
        functools.partial(_na_kernel, rows=rows),
        grid=(n_batch, NA_HEADS // heads_per_blk, nsteps),
        in_specs=[
            pl.BlockSpec((tq, LANES), lambda b, hp, s: (b * nsteps + s, COL_NA_Q // LANES + hp)),
            pl.BlockSpec((seq, LANES), lambda b, hp, s: (b, COL_NA_K // LANES + hp)),
            pl.BlockSpec((seq, LANES), lambda b, hp, s: (b, COL_NA_V // LANES + hp)),
            pl.BlockSpec((ctx_len, LANES), lambda b, hp, s: (b, COL_NA_K // LANES + hp)),
            pl.BlockSpec((ctx_len, LANES), lambda b, hp, s: (b, COL_NA_V // LANES + hp)),
            pl.BlockSpec((heads_per_blk, NA_ROWS, GRID_W, NA_ROWS * GRID_W), lambda b, hp, s: (hp, 0, 0, 0)),
        ],
        out_specs=pl.BlockSpec((tq, LANES), lambda b, hp, s: (b * nsteps + s, hp)),
        out_shape=jax.ShapeDtypeStruct((n_batch * seq, GROUP_WIDTH), F32),
        compiler_params=_params(3),
    )(px, px, px, pz, pz, bias)


def _swa_kernel(q_ref, kp_ref, kc_ref, kn_ref, vp_ref, vc_ref, vn_ref, kz_ref, vz_ref, sink_ref,
                cq_ref, sq_ref, ckp_ref, skp_ref, ckc_ref, skc_ref, ckn_ref, skn_ref, o_ref):
    n = pl.program_id(1)
    nb = pl.num_programs(1)
    dh = HEAD_DIM
    blk = SWA_BLOCK
    quarter = dh // 4
    group = SWA_HEADS // SWA_KV_HEADS
    q = _rope_lanes(q_ref[...], cq_ref[...], sq_ref[...], quarter) * (dh ** -0.5)
    kp = _rope_lanes(kp_ref[...], ckp_ref[...], skp_ref[...], quarter).astype(BF16)
    kc = _rope_lanes(kc_ref[...], ckc_ref[...], skc_ref[...], quarter).astype(BF16)
    kn = _rope_lanes(kn_ref[...], ckn_ref[...], skn_ref[...], quarter).astype(BF16)
    kz = kz_ref[...].astype(BF16)
    vp, vc, vn, vz = (r[...].astype(BF16) for r in (vp_ref, vc_ref, vn_ref, vz_ref))
    qi = lax.broadcasted_iota(jnp.int32, (group * blk, blk), 0) % blk
    kj = lax.broadcasted_iota(jnp.int32, (group * blk, blk), 1)
    ok_p = (kj >= qi) & (n > 0)
    ok_n = (kj <= qi) & (n < nb - 1)
    outs = []
    for hk in range(SWA_KV_HEADS):
        ks = slice(hk * dh, (hk + 1) * dh)
        qs = jnp.concatenate([q[:, (hk * group + g) * dh:(hk * group + g + 1) * dh] for g in range(group)],
                             axis=0).astype(BF16)
        sink = jnp.concatenate([jnp.full((blk, 1), 1.0, F32) * sink_ref[hk * group + g] for g in range(group)],
                               axis=0)
        s_p = jnp.where(ok_p, _dot_nt(qs, kp[:, ks]), NEG_INF)
        s_c = _dot_nt(qs, kc[:, ks])
        s_n = jnp.where(ok_n, _dot_nt(qs, kn[:, ks]), NEG_INF)
        s_z = _dot_nt(qs, kz[:, ks])
        m = jnp.maximum(jnp.maximum(jnp.max(s_p, axis=1, keepdims=True), jnp.max(s_c, axis=1, keepdims=True)),
                        jnp.maximum(jnp.max(s_n, axis=1, keepdims=True), jnp.max(s_z, axis=1, keepdims=True)))
        m = jnp.maximum(m, sink)
        e_p, e_c, e_n, e_z = (jnp.exp(s - m) for s in (s_p, s_c, s_n, s_z))
        den = (jnp.sum(e_p, axis=1, keepdims=True) + jnp.sum(e_c, axis=1, keepdims=True)
               + jnp.sum(e_n, axis=1, keepdims=True) + jnp.sum(e_z, axis=1, keepdims=True) + jnp.exp(sink - m))
        o = (jnp.dot(e_p.astype(BF16), vp[:, ks], preferred_element_type=F32)
             + jnp.dot(e_c.astype(BF16), vc[:, ks], preferred_element_type=F32)
             + jnp.dot(e_n.astype(BF16), vn[:, ks], preferred_element_type=F32)
             + jnp.dot(e_z.astype(BF16), vz[:, ks], preferred_element_type=F32)) / den
        outs += [o[g * blk:(g + 1) * blk, :] for g in range(group)]
    o_ref[...] = jnp.concatenate(outs, axis=1)


def swa_attention(px, pz, n_batch, seq, ctx_len, sink, rope_q, rope_k):
    blk = SWA_BLOCK
    nb = seq // blk
    prev = lambda b, n: b * nb + jnp.maximum(n - 1, 0)
    cur = lambda b, n: b * nb + n
    nxt = lambda b, n: b * nb + jnp.minimum(n + 1, nb - 1)
    kv = lambda off, f: pl.BlockSpec((blk, LANES), lambda b, n: (f(b, n), off // LANES))
    tab = lambda w, f: pl.BlockSpec((blk, w), lambda b, n: (f(0, n), 0))
    cq, sq = rope_q
    ck, sk = rope_k
    return pl.pallas_call(
        _swa_kernel,
        grid=(n_batch, nb),
        in_specs=[
            pl.BlockSpec((blk, GROUP_WIDTH), lambda b, n: (cur(b, n), COL_SWA_Q // GROUP_WIDTH)),
            kv(COL_SWA_K, prev), kv(COL_SWA_K, cur), kv(COL_SWA_K, nxt),
            kv(COL_SWA_V, prev), kv(COL_SWA_V, cur), kv(COL_SWA_V, nxt),
            pl.BlockSpec((ctx_len, LANES), lambda b, n: (b, COL_SWA_K // LANES)),
            pl.BlockSpec((ctx_len, LANES), lambda b, n: (b, COL_SWA_V // LANES)),
            pl.BlockSpec(memory_space=pltpu.SMEM),
            tab(GROUP_WIDTH, cur), tab(GROUP_WIDTH, cur), tab(LANES, prev), tab(LANES, prev),
            tab(LANES, cur), tab(LANES, cur), tab(LANES, nxt), tab(LANES, nxt),
        ],
        out_specs=pl.BlockSpec((blk, GROUP_WIDTH), lambda b, n: (cur(b, n), 0)),
        out_shape=jax.ShapeDtypeStruct((n_batch * seq, GROUP_WIDTH), F32),
        compiler_params=_params(2),
    )(px, px, px, px, px, px, px, pz, pz, sink, cq, sq, ck, sk, ck, sk, ck, sk)


def _ctx_attn_kernel(q_ref, k_ref, v_ref, sink_ref, o_ref, *, group, use_sink):
    dh = HEAD_DIM
    q = (q_ref[...] * (dh ** -0.5)).astype(BF16)
    k = k_ref[...].astype(BF16)
    v = v_ref[...].astype(BF16)
    outs = []
    for qh in range(q.shape[1] // dh):
        ks = slice((qh // group) * dh, (qh // group + 1) * dh)
        s = _dot_nt(q[:, qh * dh:(qh + 1) * dh], k[:, ks])
        m = jnp.max(s, axis=1, keepdims=True)
        if use_sink:
            m = jnp.maximum(m, sink_ref[qh])
        e = jnp.exp(s - m)
        den = jnp.sum(e, axis=1, keepdims=True)
        if use_sink:
            den = den + jnp.exp(sink_ref[qh] - m)
        outs.append(jnp.dot(e.astype(BF16), v[:, ks], preferred_element_type=F32) / den)
    o_ref[...] = jnp.concatenate(outs, axis=1)


def ctx_attention(pz, n_batch, ctx_len, cols, kv_width, sink):
    qc, kc, vc = cols
    use_sink = sink is not None
    if sink is None:
        sink = jnp.zeros((GROUP_WIDTH // HEAD_DIM,), F32)
    return pl.pallas_call(
        functools.partial(_ctx_attn_kernel, group=GROUP_WIDTH // kv_width, use_sink=use_sink),
        grid=(n_batch,),
        in_specs=[pl.BlockSpec((ctx_len, GROUP_WIDTH), lambda b: (b, qc // GROUP_WIDTH)),
                  pl.BlockSpec((ctx_len, kv_width), lambda b: (b, kc // kv_width)),
                  pl.BlockSpec((ctx_len, kv_width), lambda b: (b, vc // kv_width)),
                  pl.BlockSpec(memory_space=pltpu.SMEM)],
        out_specs=pl.BlockSpec((ctx_len, GROUP_WIDTH), lambda b: (b, 0)),
        out_shape=jax.ShapeDtypeStruct((n_batch * ctx_len, GROUP_WIDTH), F32),
        compiler_params=_params(1),
    )(pz, pz, pz, sink)


SCAN_ROWS = 256


def _split3(x):
    a = x.astype(BF16)
    r = x - a.astype(F32)
    b = r.astype(BF16)
    c = (r - b.astype(F32)).astype(BF16)
    return a, b, c


def _scan_kernel(*refs, heads, dk, dv, kind, rope, q_scale, k_scale):
    it = iter(refs)
    qf, kf, vf, qb, kb, vb = (next(it) for _ in range(6))
    if kind == "ret":
        lg = next(it)
    else:
        df, db, wup, bup = (next(it) for _ in range(4))
    if rope:
        cosf, sinf, cosb, sinb = (next(it) for _ in range(4))
    s0 = next(it)
    of, ob, sfin = next(it), next(it), next(it)
    st = next(it)

    s = pl.program_id(1)
    c = SCAN_CHUNK
    nch = SCAN_ROWS // c
    hk = heads * dk

    @pl.when(s == 0)
    def _():
        st[...] = s0[...]

    r_i = lax.broadcasted_iota(jnp.int32, (c, c), 0)
    c_i = lax.broadcasted_iota(jnp.int32, (c, c), 1)
    masks = (r_i >= c_i, c_i > r_i)

    if kind == "ret":
        pos = lax.broadcasted_iota(jnp.int32, (c, hk), 0).astype(F32)
        gcums = ((pos + 1.0) * lg[0:1, :], (float(c) - pos) * lg[1:2, :])
    else:
        rr = lax.broadcasted_iota(jnp.int32, (SCAN_ROWS, SCAN_ROWS), 0)
        cc = lax.broadcasted_iota(jnp.int32, (SCAN_ROWS, SCAN_ROWS), 1)
        same = (rr // c) == (cc // c)
        tris = (jnp.where(same & (rr >= cc), 1.0, 0.0).astype(BF16),
                jnp.where(same & (cc >= rr), 1.0, 0.0).astype(BF16))

        def gate_cum(d_ref, direction):
            pre = jnp.dot(d_ref[...].astype(BF16), wup[direction], preferred_element_type=F32) + bup[direction]
            g = -(jnp.maximum(-pre, 0.0) + jnp.log1p(jnp.exp(-jnp.abs(pre)))) / GLA_TAU
            return sum(jnp.dot(tris[direction], p, preferred_element_type=F32) for p in _split3(g))

        gcums = (gate_cum(df, 0), gate_cum(db, 1))

    def one(direction, q_ref, k_ref, v_ref, o_ref, cos_ref, sin_ref, ch):
        rows = pl.ds(ch * c, c)
        q = q_ref[rows, :]
        k = k_ref[rows, :]
        v = v_ref[rows, :]
        if rope:
            cs = jnp.concatenate([cos_ref[rows, :]] * heads, axis=1)
            sn = jnp.concatenate([sin_ref[rows, :]] * heads, axis=1)
            q = _rope_lanes(q, cs, sn, dk // 4)
            k = _rope_lanes(k, cs, sn, dk // 4)
        if q_scale != 1.0:
            q = q * q_scale
        if k_scale != 1.0:
            k = k * k_scale
        gcum = gcums[direction] if kind == "ret" else gcums[direction][ch * c:(ch + 1) * c, :]
        gtot = gcum[c - 1:c, :] if direction == 0 else gcum[0:1, :]
        q_rel = (q * jnp.exp(gcum - gtot)).astype(BF16)
        k_rel = (k * jnp.exp(gtot - gcum)).astype(BF16)
        q_dec = (q * jnp.exp(gcum)).astype(BF16)
        dec = jnp.exp(gtot)
        vb16 = v.astype(BF16)
        ks = [slice(hd * dk, (hd + 1) * dk) for hd in range(heads)]
        vs = [slice(hd * dv, (hd + 1) * dv) for hd in range(heads)]
        states = [st[direction, hd] for hd in range(heads)]
        a = [jnp.where(masks[direction], _dot_nt(q_rel[:, ks[hd]], k_rel[:, ks[hd]]), 0.0).astype(BF16)
             for hd in range(heads)]
        inter = [_dot_nt(q_dec[:, ks[hd]], states[hd].astype(BF16)) for hd in range(heads)]
        incs = [_dot_tn(vb16[:, vs[hd]], k_rel[:, ks[hd]]) for hd in range(heads)]
        outs = [jnp.dot(a[hd], vb16[:, vs[hd]], preferred_element_type=F32) + inter[hd] for hd in range(heads)]
        for hd in range(heads):
            st[direction, hd] = dec[:, ks[hd]] * states[hd] + incs[hd]
        o_ref[rows, :] = jnp.concatenate(outs, axis=1)

    for ch in range(nch):
        one(0, qf, kf, vf, of, cosf if rope else None, sinf if rope else None, ch)
        one(1, qb, kb, vb, ob, cosb if rope else None, sinb if rope else None, nch - 1 - ch)

    @pl.when(s == pl.num_programs(1) - 1)
    def _():
        sfin[...] = st[...]


def bidir_scan(p, n_batch, seq, cols, heads, dk, dv, kind, s0, *, lg=None, wup=None, bup=None,
               rope=None, q_scale=1.0, k_scale=1.0):
    t = SCAN_ROWS
    nblk = seq // t
    hk, hv = heads * dk, heads * dv
    qc, kc, vc = cols
    fwd = lambda w, off: pl.BlockSpec((t, w), lambda b, s: (b * nblk + s, off // w))
    bwd = lambda w, off: pl.BlockSpec((t, w), lambda b, s: (b * nblk + nblk - 1 - s, off // w))
    const = lambda shape: pl.BlockSpec(shape, lambda b, s: (0,) * len(shape))
    args = [p] * 6
    specs = [fwd(hk, qc), fwd(hk, kc), fwd(hv, vc), bwd(hk, qc), bwd(hk, kc), bwd(hv, vc)]
    if kind == "ret":
        args += [lg]
        specs += [const((2, hk))]
    else:
        args += [p, p, wup, bup]
        specs += [fwd(LANES, COL_GLA_D), bwd(LANES, COL_GLA_D), const(wup.shape), const(bup.shape)]
    if rope is not None:
        cos, sin = rope
        args += [cos, sin, cos, sin]
        specs += [pl.BlockSpec((t, dk), lambda b, s: (s, 0)), pl.BlockSpec((t, dk), lambda b, s: (s, 0)),
                  pl.BlockSpec((t, dk), lambda b, s: (nblk - 1 - s, 0)),
                  pl.BlockSpec((t, dk), lambda b, s: (nblk - 1 - s, 0))]
    args += [s0]
    state_spec = pl.BlockSpec((None, 2, heads, dv, dk), lambda b, s: (b, 0, 0, 0, 0))
    specs += [state_spec]
    n = n_batch * seq
    kern = functools.partial(_scan_kernel, heads=heads, dk=dk, dv=dv, kind=kind, rope=rope is not None,
                             q_scale=q_scale, k_scale=k_scale)
    return pl.pallas_call(
        kern,
        grid=(n_batch, nblk),
        in_specs=specs,
        out_specs=[pl.BlockSpec((t, hv), lambda b, s: (b * nblk + s, 0)),
                   pl.BlockSpec((t, hv), lambda b, s: (b * nblk + nblk - 1 - s, 0)),
                   state_spec],
        out_shape=[jax.ShapeDtypeStruct((n, hv), F32), jax.ShapeDtypeStruct((n, hv), F32),
                   jax.ShapeDtypeStruct((n_batch, 2, heads, dv, dk), F32)],
        scratch_shapes=[pltpu.VMEM((2, heads, dv, dk), F32)],
        compiler_params=_params(2),
    )(*args)


def _outproj_kernel(na_ref, rf_ref, rb_ref, rg_ref, gf_ref, gb_ref, gg_ref, sw_ref, gn_ref, w_ref, x_ref, mg_ref,
                    o_ref, *, head_w):
    ry = rf_ref[...] + rb_ref[...]
    gy = gf_ref[...] + gb_ref[...]
    r_out, g_out = [], []
    for hd in range(ry.shape[1] // head_w):
        sl = slice(hd * head_w, (hd + 1) * head_w)
        r = ry[:, sl]
        mu = jnp.mean(r, axis=-1, keepdims=True)
        var = jnp.mean(jnp.square(r - mu), axis=-1, keepdims=True)
        r_out.append((r - mu) * lax.rsqrt(var + EPS))
        gq = gy[:, sl]
        g_out.append(gq * lax.rsqrt(jnp.mean(gq * gq, axis=-1, keepdims=True) + EPS) * gn_ref[...])
    ret = jnp.concatenate(r_out, axis=1) * _silu(rg_ref[...])
    gla = jnp.concatenate(g_out, axis=1) * _silu(gg_ref[...])
    mix = jnp.concatenate([na_ref[...], ret, gla, sw_ref[...]], axis=1).astype(BF16)
    o_ref[...] = x_ref[...] + mg_ref[...] * jnp.dot(mix, w_ref[...], preferred_element_type=F32)


def outproj(na, rf, rb, gf, gb, sw, p, gla_norm_g, w_out, x2, mg, rows_per_mod):
    n, d = x2.shape
    gw = GROUP_WIDTH
    tm = 256
    m = mg.shape[0]
    row = lambda w_: pl.BlockSpec((tm, w_), lambda i: (i, 0))
    return pl.pallas_call(
        functools.partial(_outproj_kernel, head_w=RET_DV),
        grid=(n // tm,),
        in_specs=[row(gw), row(gw), row(gw), pl.BlockSpec((tm, gw), lambda i: (i, COL_RET_G // gw)),
                  row(gw), row(gw), pl.BlockSpec((tm, gw), lambda i: (i, COL_GLA_G // gw)), row(gw),
                  pl.BlockSpec((1, GLA_DV), lambda i: (0, 0)),
                  pl.BlockSpec((d, d), lambda i: (0, 0)), row(d),
                  pl.BlockSpec((None, 1, d), lambda i: (i // (rows_per_mod // tm), 0, 0))],
        out_specs=row(d),
        out_shape=jax.ShapeDtypeStruct((n, d), F32),
        compiler_params=_params(1),
    )(na, rf, rb, p, gf, gb, p, sw, gla_norm_g.reshape(1, GLA_DV), w_out, x2, mg.reshape(m, 1, d))


PEER_PAIRS = PEER_HEADS * PEER_TOPK
PEER_ROUTE_TOKENS = 128
PEER_EXPERT_TOKENS = 16
PEER_WAIT_GROUP = 16
PEER_SUB = 8
PEER_FOLD = D_MODEL // PEER_SUB
INV_SQRT2 = 0.7071067811865476


def _topk_cols(s, payload=None, order=None):
    row = lax.broadcasted_iota(jnp.int32, s.shape, 0) if order is None else order
    vals, idxs = [], []
    for _ in range(PEER_TOPK):
        m = jnp.max(s, axis=0, keepdims=True)
        am = jnp.min(jnp.where(s == m, row, jnp.iinfo(jnp.int32).max), axis=0, keepdims=True)
        sel = row == am
        vals.append(m)
        idxs.append(am if payload is None else jnp.max(jnp.where(sel, payload, -1), axis=0, keepdims=True))
        s = jnp.where(sel, -jnp.inf, s)
    return jnp.concatenate(vals, axis=0), jnp.concatenate(idxs, axis=0)


def _staircase_candidates(v0, i0, v1, i1):
    k = PEER_TOPK
    assert k == 16
    t = v0.shape[1]
    r8 = lax.broadcasted_iota(jnp.int32, (8, t), 0)
    r16 = lax.broadcasted_iota(jnp.int32, (k, t), 0)

    def piece(a_sl, b_sl):
        return v0[a_sl, :] + v1[b_sl, :], i0[a_sl, :] * PEER_N_KEYS + i1[b_sl, :]

    one = lambda j: slice(j, j + 1)
    lo = slice(0, 8)
    pieces = [
        (one(0), slice(0, k), None, r16),
        (one(1), lo, None, k + r8),
        (one(2), lo, r8 <= 4, 2 * k + r8),
        (one(3), lo, r8 <= 3, 3 * k + r8),
        (slice(8, k), one(0), None, (r8 + 8) * k),
        (lo, one(0), r8 >= 4, r8 * k),
        (lo, one(1), r8 >= 4, r8 * k + 1),
        (lo, one(2), r8 == 4, r8 * k + 2),
    ]
    sums, ids, orders = [], [], []
    for a_sl, b_sl, keep, order in pieces:
        s, e = piece(a_sl, b_sl)
        sums.append(s if keep is None else jnp.where(keep, s, NEG_INF))
        ids.append(e)
        orders.append(order)
    return jnp.concatenate(sums, axis=0), jnp.concatenate(ids, axis=0), jnp.concatenate(orders, axis=0)


def _route_head(hb, wq_ref, sk_ref, hd):
    half = PEER_DK // 2
    q = jnp.dot(hb, wq_ref[hd], preferred_element_type=F32)
    tops = []
    for p in range(2):
        qp = q[:, p * half:(p + 1) * half].astype(BF16)
        tops.append(_topk_cols(_dot_nt(sk_ref[p, hd], qp)))
    (v0, i0), (v1, i1) = tops
    best_s, best_e = _topk_cols(*_staircase_candidates(v0, i0, v1, i1))
    e = jnp.exp(best_s - best_s[0:1, :])
    return e / jnp.sum(e, axis=0, keepdims=True), best_e


def _peer_route_kernel(x_ref, g_ref, shift_ref, scale_ref, wq_ref, sk_ref, h_ref, idx_ref, gate_ref):
    x = x_ref[...]
    y = x * lax.rsqrt(jnp.mean(x * x, axis=-1, keepdims=True) + EPS)
    h = (y * g_ref[...]) * (1.0 + scale_ref[...]) + shift_ref[...]
    h_ref[...] = h
    hb = h.astype(BF16)

    for hd in range(PEER_HEADS):
        gate_ref[hd], idx_ref[hd] = _route_head(hb, wq_ref, sk_ref, hd)


def peer_route(x2, g, shift, scale, rows_per_mod, wq_h, sk):
    n, d = x2.shape
    t = PEER_ROUTE_TOKENS
    m = shift.shape[0]
    mod_map = lambda i: (i // (rows_per_mod // t), 0, 0)
    return pl.pallas_call(
        _peer_route_kernel,
        grid=(n // t,),
        in_specs=[
            pl.BlockSpec((t, d), lambda i: (i, 0)),
            pl.BlockSpec((1, d), lambda i: (0, 0)),
            pl.BlockSpec((None, 1, d), mod_map),
            pl.BlockSpec((None, 1, d), mod_map),
            pl.BlockSpec((PEER_HEADS, d, PEER_DK), lambda i: (0, 0, 0)),
            pl.BlockSpec((2, PEER_HEADS, PEER_N_KEYS, PEER_DK // 2), lambda i: (0, 0, 0, 0)),
        ],
        out_specs=[
            pl.BlockSpec((t, d), lambda i: (i, 0)),
            pl.BlockSpec((PEER_HEADS, PEER_TOPK, t), lambda i: (0, 0, i)),
            pl.BlockSpec((PEER_HEADS, PEER_TOPK, t), lambda i: (0, 0, i)),
        ],
        out_shape=[
            jax.ShapeDtypeStruct((n, d), F32),
            jax.ShapeDtypeStruct((PEER_HEADS, PEER_TOPK, n), jnp.int32),
            jax.ShapeDtypeStruct((PEER_HEADS, PEER_TOPK, n), F32),
        ],
        compiler_params=_params(1),
    )(x2, g.reshape(1, d), shift.reshape(m, 1, d), scale.reshape(m, 1, d), wq_h, sk)


def pack_experts(u, v):
    def bf16_bits(a):
        b = lax.bitcast_convert_type(a, jnp.uint32)
        rounded = (b + jnp.uint32(0x7FFF) + ((b >> 16) & jnp.uint32(1))) >> 16
        is_nan = (b & jnp.uint32(0x7FFFFFFF)) > jnp.uint32(0x7F800000)
        return jnp.where(is_nan, (b >> 16) | jnp.uint32(0x40), rounded)

    words = lax.bitcast_convert_type((bf16_bits(u) << 16) | bf16_bits(v), jnp.int32)
    return words.reshape(u.shape[0], PEER_SUB, PEER_FOLD)


def _fold_rows(hbuf, h_rows, tb):
    for s in range(PEER_SUB):
        for c in range(PEER_FOLD // LANES):
            lo = s * PEER_FOLD + c * LANES
            hbuf[c, pl.ds(s * tb, tb), :] = h_rows[:, lo:lo + LANES]


def _unfold_residual(o_ref, x_ref, og_ref, ybuf, tb):
    for s in range(PEER_SUB):
        for c in range(PEER_FOLD // LANES):
            sl = slice(s * PEER_FOLD + c * LANES, s * PEER_FOLD + (c + 1) * LANES)
            o_ref[:, sl] = x_ref[:, sl] + og_ref[:, sl] * ybuf[c, pl.ds(s, tb, stride=PEER_SUB), :]


def _expert_weights(j, buf, hbuf, pbuf, wrow, tb, gate_column):
    fold = PEER_FOLD
    hj = jnp.concatenate([hbuf[c, pl.ds(j, PEER_SUB, stride=tb), :] for c in range(fold // LANES)], axis=1)
    for p in range(PEER_PAIRS):
        u = lax.bitcast_convert_type(buf[j, p] & jnp.int32(-65536), F32)
        prod = u * hj
        pbuf[pl.ds(p * PEER_SUB, PEER_SUB), :] = prod[:, :LANES] + prod[:, LANES:]
    part = pbuf[pl.ds(0, PEER_PAIRS, stride=PEER_SUB), :]
    for s in range(1, PEER_SUB):
        part = part + pbuf[pl.ds(s, PEER_PAIRS, stride=PEER_SUB), :]
    sc = jnp.sum(part, axis=1, keepdims=True)
    act = 0.5 * sc * (1.0 + lax.erf(sc * INV_SQRT2))
    g = jnp.concatenate([gate_column(hd) for hd in range(PEER_HEADS)], axis=0)
    wrow[...] = jnp.broadcast_to(g * act, (PEER_PAIRS, LANES))


def _expert_mix(j, buf, wrow, ybuf, start_fetch):
    fold = PEER_FOLD
    accs = [jnp.zeros((PEER_SUB, fold), F32) for _ in range(4)]
    for p in range(PEER_PAIRS):
        if start_fetch is not None:
            start_fetch(p)
        v = lax.bitcast_convert_type(buf[j, p] << 16, F32)
        wp = jnp.broadcast_to(wrow[p:p + 1, :], (PEER_SUB, LANES))
        accs[p % 4] = accs[p % 4] + v * jnp.concatenate([wp, wp], axis=1)
    yj = (accs[0] + accs[1]) + (accs[2] + accs[3])
    for c in range(fold // LANES):
        ybuf[c, pl.ds(j * PEER_SUB, PEER_SUB), :] = yj[:, c * LANES:(c + 1) * LANES]


def _peer_expert_kernel(idx0_ref, idxn_ref, h_ref, gate_ref, x_ref, og_ref, uv_hbm, o_ref,
                        buf_even, buf_odd, sem, hbuf, pbuf, wbuf, ybuf, *, n_blocks):
    i = pl.program_id(0)
    n = n_blocks
    tb = PEER_EXPERT_TOKENS
    bufs = (buf_even, buf_odd)

    def slab_copy(idx_ref, j, r, parity):
        e = idx_ref[0, 0, j * PEER_PAIRS + r]
        return pltpu.make_async_copy(uv_hbm.at[e], bufs[parity].at[j, r], sem.at[parity, j])

    def wait_token(j, parity):
        pltpu.make_async_copy(uv_hbm.at[pl.ds(0, PEER_PAIRS)], bufs[parity].at[j], sem.at[parity, j]).wait()

    @pl.when(i == 0)
    def _():
        def prime(j, carry):
            for r in range(PEER_PAIRS):
                slab_copy(idx0_ref, j, r, 0).start(priority=r % 2)
            return carry
        lax.fori_loop(0, tb, prime, 0)

    lane = lax.broadcasted_iota(jnp.int32, (PEER_TOPK, PEER_ROUTE_TOKENS), 1)
    lane0 = (i % (PEER_ROUTE_TOKENS // tb)) * tb

    _fold_rows(hbuf, h_ref[...], tb)

    def fetcher(j, parity, prefetch):
        return (lambda r: slab_copy(idxn_ref, j, r, 1 - parity).start(priority=r % 2)) if prefetch else None

    def gate_column_of(j):
        return lambda hd: jnp.sum(jnp.where(lane == lane0 + j, gate_ref[hd], 0.0), axis=1, keepdims=True)

    def block(parity, prefetch):
        for j in range(tb):
            if j % PEER_WAIT_GROUP == 0:
                for jj in range(j, j + PEER_WAIT_GROUP):
                    wait_token(jj, parity)
            _expert_weights(j, bufs[parity], hbuf, pbuf, wbuf, tb, gate_column_of(j))
            _expert_mix(j, bufs[parity], wbuf, ybuf, fetcher(j, parity, prefetch))

    for parity in range(2):
        for prefetch in (True, False):
            @pl.when((i % 2 == parity) & ((i + 1 < n) == prefetch))
            def _(parity=parity, prefetch=prefetch):
                block(parity, prefetch)

    _unfold_residual(o_ref, x_ref, og_ref, ybuf, tb)


def peer_expert(h, idx, gate, x2, out_gate, rows_per_mod, uv):
    n, d = h.shape
    tb = PEER_EXPERT_TOKENS
    nb = n // tb
    m = out_gate.shape[0]
    rows = tb * PEER_PAIRS
    idx_rows = idx.reshape(PEER_PAIRS, n).T.reshape(nb, 1, rows)
    gate_blocks = PEER_ROUTE_TOKENS // tb
    return pl.pallas_call(
        functools.partial(_peer_expert_kernel, n_blocks=nb),
        grid=(nb,),
        in_specs=[
            pl.BlockSpec((1, 1, rows), lambda i: (0, 0, 0), memory_space=pltpu.SMEM),
            pl.BlockSpec((1, 1, rows), lambda i: (jnp.minimum(i + 1, nb - 1), 0, 0), memory_space=pltpu.SMEM),
            pl.BlockSpec((tb, d), lambda i: (i, 0)),
            pl.BlockSpec((PEER_HEADS, PEER_TOPK, PEER_ROUTE_TOKENS), lambda i: (0, 0, i // gate_blocks)),
            pl.BlockSpec((tb, d), lambda i: (i, 0)),
            pl.BlockSpec((None, 1, d), lambda i: (i // (rows_per_mod // tb), 0, 0)),
            pl.BlockSpec(memory_space=pl.ANY),
        ],
        out_specs=pl.BlockSpec((tb, d), lambda i: (i, 0)),
        out_shape=jax.ShapeDtypeStruct((n, d), F32),
        scratch_shapes=[pltpu.VMEM((tb, PEER_PAIRS, PEER_SUB, PEER_FOLD), jnp.int32),
                        pltpu.VMEM((tb, PEER_PAIRS, PEER_SUB, PEER_FOLD), jnp.int32),
                        pltpu.SemaphoreType.DMA((2, tb)),
                        pltpu.VMEM((PEER_FOLD // LANES, PEER_SUB * tb, LANES), F32),
                        pltpu.VMEM((PEER_PAIRS * PEER_SUB, LANES), F32),
                        pltpu.VMEM((PEER_PAIRS, LANES), F32),
                        pltpu.VMEM((PEER_FOLD // LANES, tb * PEER_SUB, LANES), F32)],
        compiler_params=pltpu.CompilerParams(dimension_semantics=("arbitrary",), vmem_limit_bytes=VMEM_LIMIT,
                                             disable_bounds_checks=True),
    )(idx_rows, idx_rows, h, gate, x2, out_gate.reshape(m, 1, d), uv)


def peer_residual(x2, g, shift, scale, out_gate, rows_per_mod, wq_h, sk, uv):
    h, idx, gate = peer_route(x2, g, shift, scale, rows_per_mod, wq_h, sk)
    return peer_expert(h, idx, gate, x2, out_gate, rows_per_mod, uv)


def _final_norm_kernel(x_ref, g_ref, o_ref):
    x = x_ref[...]
    y = x * lax.rsqrt(jnp.mean(x * x, axis=-1, keepdims=True) + EPS)
    o_ref[...] = y * g_ref[...]


def final_norm(x2, g):
    n, d = x2.shape
    tm = 512
    return pl.pallas_call(
        _final_norm_kernel,
        grid=(n // tm,),
        in_specs=[pl.BlockSpec((tm, d), lambda i: (i, 0)), pl.BlockSpec((1, d), lambda i: (0, 0))],
        out_specs=pl.BlockSpec((tm, d), lambda i: (i, 0)),
        out_shape=jax.ShapeDtypeStruct((n, d), x2.dtype),
        compiler_params=_params(1),
    )(x2, g.reshape(1, d))


def _mixers(p, pz, n_batch, seq, ctx_len, is_ctx, tables, prm):
    if is_ctx:
        na = ctx_attention(p, n_batch, seq, (COL_NA_Q, COL_NA_K, COL_NA_V), GROUP_WIDTH, None)
        sw = ctx_attention(p, n_batch, seq, (COL_SWA_Q, COL_SWA_K, COL_SWA_V), SWA_KV_HEADS * HEAD_DIM, prm["sink"])
    else:
        na = na_attention(p, pz, n_batch, seq, ctx_len, prm["na_bias"])
        sw = swa_attention(p, pz, n_batch, seq, ctx_len, prm["sink"], tables["swa_q"], tables["swa_k"])
    rf, rb, rs = bidir_scan(p, n_batch, seq, (COL_RET_Q, COL_RET_K, COL_RET_V), RET_HEADS, RET_DK, RET_DV, "ret",
                            prm["ret_s0"], lg=prm["ret_lg"], rope=None if is_ctx else tables["ret"],
                            k_scale=RET_DK ** -0.5)
    gf, gb, gs = bidir_scan(p, n_batch, seq, (COL_GLA_Q, COL_GLA_K, COL_GLA_V), GLA_HEADS, GLA_DK, GLA_DV, "gla",
                            prm["gla_s0"], wup=prm["gla_wup"], bup=prm["gla_bup"], q_scale=GLA_DK ** -0.5)
    return (na, rf, rb, gf, gb, sw), (rs, gs)


def kernel(x, c, ctx, c_ctx, w_ada, b_ada, norm_attn_g, norm_ffn_g, w_in, na_rpb, ret_log_gamma,
           gla_w_gate_up, gla_b_gate, gla_norm_g, swa_sink, w_out, peer_w_q, peer_sub_keys,
           peer_u, peer_v, final_g):
    bsz, slen, d = x.shape
    zlen = ctx.shape[1]
    x2 = x.reshape(bsz * slen, d)
    z2 = ctx.reshape(bsz * zlen, d)
    tables = {"ret": rope_lane_tables(slen, RET_DK, 1),
              "swa_q": rope_lane_tables(slen, HEAD_DIM, SWA_HEADS),
              "swa_k": rope_lane_tables(slen, HEAD_DIM, SWA_KV_HEADS)}
    c_rows = jnp.zeros((8, d), F32).at[:bsz].set(c).at[bsz].set(c_ctx)
    for layer in range(DEPTH):
        has_next = layer < DEPTH - 1
        mod = adaln(c_rows, w_ada[layer], b_ada[layer])
        mx = [mod[:bsz, k * d:(k + 1) * d] for k in range(6)]
        mz = [mod[bsz:bsz + 1, k * d:(k + 1) * d] for k in range(6)]

        wi = w_in[layer]
        wp = jnp.concatenate([wi[:, :REF_COL_GLA_D], wi[:, REF_COL_SWA_Q:], wi[:, REF_COL_GLA_D:REF_COL_SWA_Q],
                              jnp.zeros((d, PROJ_WIDTH - REF_D_IN), F32)], axis=1).astype(BF16)
        px = modproj(x2, norm_attn_g[layer], mx[0], mx[1], slen, wp)
        pz = modproj(z2, norm_attn_g[layer], mz[0], mz[1], bsz * zlen, wp)

        wup = (jnp.zeros((2, LANES, GLA_HEADS * GLA_DK), F32)
               .at[0, :GLA_RANK].set(gla_w_gate_up[layer, 0])
               .at[1, GLA_RANK:2 * GLA_RANK].set(gla_w_gate_up[layer, 1])).astype(BF16)
        prm = {"na_bias": na_band_bias(na_rpb[layer]), "sink": swa_sink[layer],
               "ret_lg": jnp.repeat(ret_log_gamma[layer], RET_DK, axis=1),
               "gla_wup": wup, "gla_bup": gla_b_gate[layer].reshape(2, 1, GLA_HEADS * GLA_DK),
               "ret_s0": jnp.zeros((bsz, 2, RET_HEADS, RET_DV, RET_DK), F32),
               "gla_s0": jnp.zeros((bsz, 2, GLA_HEADS, GLA_DV, GLA_DK), F32)}
        mix_z, (ret_s, gla_s) = _mixers(pz, pz, bsz, zlen, zlen, True, tables, prm)
        prm["ret_s0"], prm["gla_s0"] = ret_s, gla_s
        mix_x, _ = _mixers(px, pz, bsz, slen, zlen, False, tables, prm)

        wo = w_out[layer].astype(BF16)
        x2 = outproj(*mix_x, px, gla_norm_g[layer], wo, x2, mx[2], slen)

        uv = pack_experts(peer_u[layer], peer_v[layer])
        wq_h = peer_w_q[layer].reshape(d, PEER_HEADS, PEER_DK).transpose(1, 0, 2).astype(BF16)
        sk = peer_sub_keys[layer].astype(BF16)
        x2 = peer_residual(x2, norm_ffn_g[layer], mx[3], mx[4], mx[5], slen, wq_h, sk, uv)
        if has_next:
            z2 = outproj(*mix_z, pz, gla_norm_g[layer], wo, z2, mz[2], bsz * zlen)
            z2 = peer_residual(z2, norm_ffn_g[layer], mz[3], mz[4], mz[5], bsz * zlen, wq_h, sk, uv)
    return final_norm(x2, final_g).reshape(bsz, slen, d)
```
